```python
import math
import jax
import jax.numpy as jnp
from jax import lax
import numpy as np

D_MODEL = 2048
BATCH = 2
SEQ = 4096
DEPTH = 2

GRID_W = 64
CTX_LEN = 256
HEAD_DIM = 128
GDN_HEADS = 8
NA_HEADS = 8
GDN_WIDTH = GDN_HEADS * HEAD_DIM
NA_WIDTH = NA_HEADS * HEAD_DIM
GDN_CONV = 5
GDN_CHUNK = 64
NA_KH = 8
NA_KW = 16
SC_CONV = 3
FFN_HIDDEN = -(-8 * D_MODEL // (3 * 256)) * 256
ROPE_THETA = 10000.0
NORM_EPS = 1e-6
EVEN_SEGS = (GDN_WIDTH, GDN_WIDTH, NA_WIDTH, NA_WIDTH, 2 * GDN_HEADS, 2 * GDN_HEADS, GDN_WIDTH, NA_WIDTH, GDN_WIDTH)
N_CTX_SEGS = 6

kernel_name = 'hybrid_gdn_natten_shortconv_dit'


def rmsnorm(x, g):
    xf = x.astype(jnp.float32)
    y = xf * lax.rsqrt(jnp.mean(xf * xf, axis=-1, keepdims=True) + NORM_EPS)
    return (y * g.astype(jnp.float32)).astype(x.dtype)


def modulate(x, g, shift, scale):
    return rmsnorm(x, g) * (1 + scale[..., None, :]) + shift[..., None, :]


def l2norm(x):
    xf = x.astype(jnp.float32)
    return xf * lax.rsqrt(jnp.sum(xf * xf, axis=-1, keepdims=True) + NORM_EPS)


def heads(t):
    return t.reshape(t.shape[0], t.shape[1], -1, HEAD_DIM)


def to_bhtd(t):
    return None if t is None else jnp.swapaxes(t, 1, 2)


def split_segments(p, sizes):
    cuts = [int(s) for s in np.cumsum(sizes)[:-1]]
    return jnp.split(p, cuts, axis=-1)


def depthwise_conv_centred(x, w):
    k = w.shape[0]
    return lax.conv_general_dilated(
        x, w[:, None, :].astype(x.dtype), window_strides=(1,), padding=[(k // 2, k // 2)],
        dimension_numbers=('NWC', 'WIO', 'NWC'), feature_group_count=x.shape[-1])


def axial_rope(x):
    t = jnp.arange(x.shape[1])
    row = (t // GRID_W).astype(jnp.float32)
    col = (t % GRID_W).astype(jnp.float32)
    n_freq = HEAD_DIM // 4
    inv_freq = ROPE_THETA ** (-jnp.arange(n_freq, dtype=jnp.float32) / n_freq)
    ang = jnp.concatenate([row[:, None] * inv_freq, col[:, None] * inv_freq], axis=-1)[:, None, :]
    cos, sin = jnp.cos(ang), jnp.sin(ang)
    x1, x2 = jnp.split(x, 2, axis=-1)
    return jnp.concatenate([x1 * cos - x2 * sin, x1 * sin + x2 * cos], axis=-1)


def gated_delta_chunked(q, k, v, g, beta, s0):
    f32 = jnp.float32
    k, v, g, beta = (t.astype(f32) for t in (k, v, g, beta))
    bn, hn, tn, _ = k.shape
    n, cs = tn // GDN_CHUNK, GDN_CHUNK

    def chunks(t):
        return t.reshape(t.shape[:2] + (n, cs) + t.shape[3:])

    k, v, g, beta = chunks(k), chunks(v), chunks(g), chunks(beta)
    gc = jnp.cumsum(g, axis=-1)
    i = jnp.arange(cs)
    lower_incl = i[:, None] >= i[None, :]
    decay = jnp.exp(jnp.where(lower_incl, gc[..., :, None] - gc[..., None, :], -jnp.inf))
    kb = k * beta[..., None]
    m = jnp.where(i[:, None] > i[None, :], jnp.einsum('bhnid,bhnjd->bhnij', kb, k) * decay, 0.0)
    eye = jnp.eye(cs, dtype=f32)
    t_inv = lax.linalg.triangular_solve(eye + m, jnp.broadcast_to(eye, m.shape),
                                        left_side=True, lower=True, unit_diagonal=True)
    u = jnp.einsum('bhnij,bhnjd->bhnid', t_inv, v * beta[..., None])
    w = jnp.einsum('bhnij,bhnjd->bhnid', t_inv, kb * jnp.exp(gc)[..., None])
    g_last = gc[..., -1]
    k_dec = k * jnp.exp(g_last[..., None] - gc)[..., None]

    def to_scan(t):
        return jnp.moveaxis(t, 2, 0)

    xs = [to_scan(w), to_scan(u), to_scan(k_dec), to_scan(g_last)]
    if q is not None:
        q = chunks(q.astype(f32))
        a_intra = jnp.where(lower_incl, jnp.einsum('bhnid,bhnjd->bhnij', q, k) * decay, 0.0)
        xs += [to_scan(q * jnp.exp(gc)[..., None]), to_scan(a_intra)]

    def step(s, xc):
        w_c, u_c, kd_c, gl_c = xc[:4]
        v_new = u_c - jnp.einsum('bhik,bhkv->bhiv', w_c, s)
        s_next = s * jnp.exp(gl_c)[..., None, None] + jnp.einsum('bhik,bhiv->bhkv', kd_c, v_new)
        if len(xc) == 4:
            return s_next, None
        qd_c, a_c = xc[4:]
        o_c = jnp.einsum('bhik,bhkv->bhiv', qd_c, s) + jnp.einsum('bhij,bhjv->bhiv', a_c, v_new)
        return s_next, o_c

    s_fin, o = lax.scan(step, s0, tuple(xs))
    if q is None:
        return None, s_fin
    return jnp.moveaxis(o, 0, 2).reshape(bn, hn, tn, -1), s_fin


def flip_time(t, d):
    return t if (t is None or d == 0) else jnp.flip(t, axis=2)


def gdn_bidirectional(q, k, v, g, beta, init):
    out, finals = None, []
    for d in range(2):
        o_d, s_d = gated_delta_chunked(flip_time(q, d), flip_time(k, d), flip_time(v, d),
                                       flip_time(g[d], d), flip_time(beta[d], d), init[d])
        if o_d is not None:
            o_d = flip_time(o_d, d)
            out = o_d if out is None else out + o_d
        finals.append(s_d)
    return out, finals


def gdn_gates(a, b, a_log, dt_bias):
    shp = a.shape[:2] + (2, GDN_HEADS)
    a = a.astype(jnp.float32).reshape(shp)
    b = b.astype(jnp.float32).reshape(shp)
    g = -jnp.exp(a_log.astype(jnp.float32)) * jax.nn.softplus(a + dt_bias.astype(jnp.float32))
    beta = jax.nn.sigmoid(b)
    return jnp.transpose(g, (2, 0, 3, 1)), jnp.transpose(beta, (2, 0, 3, 1))


def gdn_out_norm(o, z, g_norm):
    o = jnp.swapaxes(o, 1, 2)
    y = rmsnorm(o, g_norm) * jax.nn.silu(heads(z).astype(jnp.float32))
    return y.reshape(y.shape[0], y.shape[1], GDN_WIDTH).astype(z.dtype)


def short_conv_silu(parts, conv_w):
    xcat = jnp.concatenate(parts, axis=-1)
    y = jax.nn.silu(depthwise_conv_centred(xcat, conv_w[:, :xcat.shape[-1]]))
    return jnp.split(y, len(parts), axis=-1)


def neighbourhood_attention(q, k, v, k_ctx, v_ctx, rpb):
    bn, tn, hn, dh = q.shape
    rows = tn // GRID_W
    kh = min(NA_KH, rows)
    r = jnp.arange(rows)
    row_idx = jnp.clip(r - kh // 2, 0, rows - kh)[:, None] + jnp.arange(kh)[None, :]
    col = jnp.arange(GRID_W)
    col_start = jnp.clip(col - NA_KW // 2, 0, GRID_W - NA_KW)
    in_win = (col[None, :] >= col_start[:, None]) & (col[None, :] < col_start[:, None] + NA_KW)
    dr = row_idx - r[:, None] + (NA_KH - 1)
    dc = jnp.clip(col[None, :] - col[:, None], -(NA_KW - 1), NA_KW - 1) + (NA_KW - 1)
    bias = rpb.astype(jnp.float32)[:, dr[:, None, :, None], dc[None, :, None, :]]
    scale = dh ** -0.5
    qg = q.reshape(bn, rows, GRID_W, hn, dh)
    kg = k.reshape(bn, rows, GRID_W, hn, dh)[:, row_idx]
    vg = v.reshape(bn, rows, GRID_W, hn, dh)[:, row_idx]
    s_loc = jnp.einsum('brqhd,brjkhd->bhrqjk', qg, kg).astype(jnp.float32) * scale + bias[None]
    s_loc = jnp.where(in_win[:, None, :], s_loc, -jnp.inf)
    n_loc = kh * GRID_W
    s_loc = s_loc.reshape(bn, hn, rows, GRID_W, n_loc)
    s_ctx = jnp.einsum('brqhd,bchd->bhrqc', qg, k_ctx).astype(jnp.float32) * scale
    p = jax.nn.softmax(jnp.concatenate([s_loc, s_ctx], axis=-1), axis=-1)
    p_loc = p[..., :n_loc].reshape(bn, hn, rows, GRID_W, kh, GRID_W).astype(v.dtype)
    p_ctx = p[..., n_loc:].astype(v.dtype)
    o = jnp.einsum('bhrqjk,brjkhd->brqhd', p_loc, vg) + jnp.einsum('bhrqc,bchd->brqhd', p_ctx, v_ctx)
    return o.reshape(bn, tn, hn * dh)


def context_attention(q, k, v):
    s = jnp.einsum('bqhd,bkhd->bhqk', q, k).astype(jnp.float32) * HEAD_DIM ** -0.5
    p = jax.nn.softmax(s, axis=-1).astype(v.dtype)
    o = jnp.einsum('bhqk,bkhd->bqhd', p, v)
    return o.reshape(o.shape[0], o.shape[1], -1)


def even_mixer(h, hc, w_in, conv_w, a_log, dt_bias, gdn_norm, rpb, w_out, want_ctx):
    f32 = jnp.float32
    bn = h.shape[0]
    n_seg_c = len(EVEN_SEGS) if want_ctx else N_CTX_SEGS
    pc = hc @ w_in[:, :sum(EVEN_SEGS[:n_seg_c])]
    segs_c = split_segments(pc, EVEN_SEGS[:n_seg_c])
    ka_c, va_c, kb_c, vb_c, a_c, b_c = segs_c[:6]
    conv_c = short_conv_silu([ka_c, va_c] + ([segs_c[6]] if want_ctx else []), conv_w)
    k_c = l2norm(heads(conv_c[0]))
    v_c = heads(conv_c[1]).astype(f32)
    q_c = l2norm(heads(conv_c[2])) * HEAD_DIM ** -0.5 if want_ctx else None
    g_c, beta_c = gdn_gates(a_c, b_c, a_log, dt_bias)
    zeros = jnp.zeros((bn, GDN_HEADS, HEAD_DIM, HEAD_DIM), f32)
    o_c, s_ctx_fin = gdn_bidirectional(to_bhtd(q_c), to_bhtd(k_c), to_bhtd(v_c), g_c, beta_c, (zeros, zeros))
    k_ctx_na, v_ctx_na = heads(kb_c), heads(vb_c)
    ka, va, kb, vb, a, b, qa, qb, za = split_segments(h @ w_in, EVEN_SEGS)
    ka, va, qa = short_conv_silu([ka, va, qa], conv_w)
    k_l = axial_rope(l2norm(heads(ka)))
    q_l = axial_rope(l2norm(heads(qa))) * HEAD_DIM ** -0.5
    g_l, beta_l = gdn_gates(a, b, a_log, dt_bias)
    o_l, _ = gdn_bidirectional(to_bhtd(q_l), to_bhtd(k_l), to_bhtd(heads(va)), g_l, beta_l, s_ctx_fin)
    y_gdn = gdn_out_norm(o_l, za, gdn_norm).astype(h.dtype)
    y_na = neighbourhood_attention(heads(qb), heads(kb), heads(vb), k_ctx_na, v_ctx_na, rpb)
    y = jnp.concatenate([y_gdn, y_na], axis=-1) @ w_out
    if not want_ctx:
        return y, None
    yc_gdn = gdn_out_norm(o_c, segs_c[8], gdn_norm).astype(hc.dtype)
    yc_na = context_attention(heads(segs_c[7]), k_ctx_na, v_ctx_na)
    yc = jnp.concatenate([yc_gdn, yc_na], axis=-1) @ w_out
    return y, yc


def shortconv_mixer(h, w_in, conv_w, w_out):
    gate_b, gate_c, val = jnp.split(h @ w_in, 3, axis=-1)
    return (gate_b * depthwise_conv_centred(gate_c * val, conv_w)) @ w_out


def swiglu(h, w_gate, w_up, w_down):
    return (jax.nn.silu(h @ w_gate) * (h @ w_up)) @ w_down


def setup_inputs(seed: int = 0) -> dict:
    key = jax.random.key(seed)
    ks = iter(jax.random.split(key, 32))
    n_even, n_odd = (DEPTH + 1) // 2, DEPTH // 2
    f32 = jnp.float32
    d = D_MODEL

    def normal(shape, scale):
        return jax.random.normal(next(ks), shape, f32) * scale

    def gain(shape):
        return 1.0 + normal(shape, 0.02)

    dt = jnp.exp(jax.random.uniform(next(ks), (n_even, 2, GDN_HEADS), f32,
                                    minval=math.log(1e-3), maxval=math.log(1e-1)))
    a_log = jnp.log(jax.random.uniform(next(ks), (n_even, 2, GDN_HEADS), f32, minval=1.0, maxval=16.0))
    mix_w = GDN_WIDTH + NA_WIDTH
    return {
        'x': normal((BATCH, SEQ, d), 1.0),
        'c': normal((BATCH, d), 1.0),
        'ctx': normal((BATCH, CTX_LEN, d), 1.0),
        'c_ctx': normal((d,), 1.0),
        'ada_w': normal((DEPTH, d, 6 * d), 0.5 * d ** -0.5),
        'ada_b': normal((DEPTH, 6 * d), 0.02),
        'norm_mix': gain((DEPTH, d)),
        'norm_ffn': gain((DEPTH, d)),
        'ffn_w_gate': normal((DEPTH, d, FFN_HIDDEN), d ** -0.5),
        'ffn_w_up': normal((DEPTH, d, FFN_HIDDEN), d ** -0.5),
        'ffn_w_down': normal((DEPTH, FFN_HIDDEN, d), FFN_HIDDEN ** -0.5),
        'final_norm': gain((d,)),
        'ev_w_in': normal((n_even, d, sum(EVEN_SEGS)), d ** -0.5),
        'ev_conv': normal((n_even, GDN_CONV, 3 * GDN_WIDTH), GDN_CONV ** -0.5),
        'ev_a_log': a_log,
        'ev_dt_bias': dt + jnp.log(-jnp.expm1(-dt)),
        'ev_gdn_norm': gain((n_even, HEAD_DIM)),
        'ev_rpb': normal((n_even, NA_HEADS, 2 * NA_KH - 1, 2 * NA_KW - 1), 0.1),
        'ev_w_out': normal((n_even, mix_w, d), mix_w ** -0.5),
        'od_w_in': normal((n_odd, d, 3 * d), d ** -0.5),
        'od_conv': normal((n_odd, SC_CONV, d), SC_CONV ** -0.5),
        'od_w_out': normal((n_odd, d, d), d ** -0.5),
    }


def reference(x, c, ctx, c_ctx, ada_w, ada_b, norm_mix, norm_ffn, ffn_w_gate, ffn_w_up, ffn_w_down,
              final_norm, ev_w_in, ev_conv, ev_a_log, ev_dt_bias, ev_gdn_norm, ev_rpb, ev_w_out,
              od_w_in, od_conv, od_w_out):
    x_lat, x_ctx = x, ctx
    for l in range(DEPTH):
        ctx_read_later = any(j % 2 == 0 for j in range(l + 1, DEPTH))
        need_ctx = (l % 2 == 0) or ctx_read_later
        sh1, sc1, g1, sh2, sc2, g2 = jnp.split(jax.nn.silu(c) @ ada_w[l] + ada_b[l], 6, axis=-1)
        h = modulate(x_lat, norm_mix[l], sh1, sc1)
        hc = None
        if need_ctx:
            csh1, csc1, cg1, csh2, csc2, cg2 = jnp.split(jax.nn.silu(c_ctx) @ ada_w[l] + ada_b[l], 6, axis=-1)
            hc = modulate(x_ctx, norm_mix[l], csh1, csc1)
        if l % 2 == 0:
            e = l // 2
            y, yc = even_mixer(h, hc, ev_w_in[e], ev_conv[e], ev_a_log[e], ev_dt_bias[e],
                               ev_gdn_norm[e], ev_rpb[e], ev_w_out[e], ctx_read_later)
        else:
            j_odd = l // 2
            y = shortconv_mixer(h, od_w_in[j_odd], od_conv[j_odd], od_w_out[j_odd])
            yc = shortconv_mixer(hc, od_w_in[j_odd], od_conv[j_odd], od_w_out[j_odd]) if ctx_read_later else None
        x_lat = x_lat + g1[:, None, :] * y
        x_lat = x_lat + g2[:, None, :] * swiglu(modulate(x_lat, norm_ffn[l], sh2, sc2),
                                                ffn_w_gate[l], ffn_w_up[l], ffn_w_down[l])
        if ctx_read_later:
            x_ctx = x_ctx + cg1 * yc
            x_ctx = x_ctx + cg2 * swiglu(modulate(x_ctx, norm_ffn[l], csh2, csc2),
                                         ffn_w_gate[l], ffn_w_up[l], ffn_w_down[l])
    return rmsnorm(x_lat, final_norm)
```

```python
import functools
import math

import jax
import jax.numpy as jnp
import numpy as np
from jax import lax
from jax.experimental import pallas as pl
from jax.experimental.pallas import tpu as pltpu

F32 = jnp.float32
BF16 = jnp.bfloat16
HIGHEST = lax.Precision.HIGHEST

LANES = 128
SUBLANES = 8
VMEM_LIMIT = 56 * 1024 * 1024

GRID_W = 64
HEAD_DIM = 128
GDN_HEADS = 8
NA_HEADS = 8
GDN_CONV = 5
CHUNK = 64
NA_KH = 8
NA_KW = 16
SC_CONV = 3
ROPE_THETA = 10000.0
NORM_EPS = 1e-6
NEG_BIG = -1e30

NT_DIMS = (((1,), (1,)), ((), ()))


def _cparams(sem):
    return pltpu.CompilerParams(dimension_semantics=sem, vmem_limit_bytes=VMEM_LIMIT)


def _sigmoid(x):
    return 1.0 / (1.0 + jnp.exp(-x))


def _silu(x):
    return x * _sigmoid(x)


def _softplus(x):
    return jnp.maximum(x, 0.0) + jnp.log(1.0 + jnp.exp(-jnp.abs(x)))


def _mm(a, b):
    return jnp.dot(a.astype(BF16), b.astype(BF16), preferred_element_type=F32)


def _mm_nt(a, b):
    return lax.dot_general(a.astype(BF16), b.astype(BF16), NT_DIMS, preferred_element_type=F32)


def _mm_exact(a, b):
    return jnp.dot(a, b, precision=HIGHEST, preferred_element_type=F32)


def _ada_kernel(cv_ref, w_ref, b_ref, o_ref):
    s = _silu(cv_ref[...])
    o_ref[...] = _mm_exact(s, w_ref[...]) + b_ref[...]


def ada_modulation(cv, ada_w, ada_b, tn=1024):
    depth, d, n = ada_w.shape
    return pl.pallas_call(
        _ada_kernel,
        grid=(depth, n // tn),
        in_specs=[
            pl.BlockSpec((SUBLANES, d), lambda l, j: (0, 0)),
            pl.BlockSpec((None, d, tn), lambda l, j: (l, 0, j)),
            pl.BlockSpec((None, 1, tn), lambda l, j: (l, 0, j)),
        ],
        out_specs=pl.BlockSpec((None, SUBLANES, tn), lambda l, j: (l, 0, j)),
        out_shape=jax.ShapeDtypeStruct((depth, SUBLANES, n), F32),
        compiler_params=_cparams(("parallel", "parallel")),
        name="ada",
    )(cv, ada_w, ada_b.reshape(depth, 1, n))


def _modulated_norm(x, g, sc, sh):
    y = x * lax.rsqrt(jnp.mean(x * x, axis=-1, keepdims=True) + NORM_EPS)
    return (y * g) * (1.0 + sc) + sh


def _proj_kernel(*refs, with_gates):
    if with_gates:
        x_ref, g_ref, sc_ref, sh_ref, w_ref, wab_ref, gp_ref, o_ref, gate_ref, hs_ref = refs
    else:
        x_ref, g_ref, sc_ref, sh_ref, w_ref, o_ref, hs_ref = refs

    @pl.when(pl.program_id(2) == 0)
    def _():
        hb = _modulated_norm(x_ref[...], g_ref[...], sc_ref[...], sh_ref[...]).astype(BF16)
        hs_ref[...] = hb
        if with_gates:
            a = jnp.dot(hb, wab_ref[...], preferred_element_type=F32)
            neg_decay_rate = -jnp.exp(gp_ref[0:1, :])
            g = neg_decay_rate * _softplus(a + gp_ref[1:2, :])
            lane = lax.broadcasted_iota(jnp.int32, a.shape, 1)
            gate_ref[...] = jnp.where(lane < 2 * GDN_HEADS, g, _sigmoid(a))

    o_ref[...] = jnp.dot(hs_ref[...], w_ref[...], preferred_element_type=F32)


def proj(x, g, sc, sh, w, n_out, wab=None, gparams=None, tm=512, tn=512):
    bn, t, d = x.shape
    tm = min(tm, t)
    with_gates = wab is not None
    vec = pl.BlockSpec((None, 1, d), lambda b, i, j: (b, 0, 0))
    in_specs = [
        pl.BlockSpec((None, tm, d), lambda b, i, j: (b, i, 0)),
        pl.BlockSpec((1, d), lambda b, i, j: (0, 0)),
        vec, vec,
        pl.BlockSpec((d, tn), lambda b, i, j: (0, j)),
    ]
    args = [x, g.reshape(1, d), sc, sh, w]
    out_specs = [pl.BlockSpec((None, tm, tn), lambda b, i, j: (b, i, j))]
    out_shape = [jax.ShapeDtypeStruct((bn, t, n_out), F32)]
    if with_gates:
        in_specs += [pl.BlockSpec((d, LANES), lambda b, i, j: (0, 0)),
                     pl.BlockSpec((SUBLANES, LANES), lambda b, i, j: (0, 0))]
        args += [wab, gparams]
        out_specs.append(pl.BlockSpec((None, tm, LANES), lambda b, i, j: (b, i, 0)))
        out_shape.append(jax.ShapeDtypeStruct((bn, t, LANES), F32))
    res = pl.pallas_call(
        functools.partial(_proj_kernel, with_gates=with_gates),
        grid=(bn, t // tm, n_out // tn),
        in_specs=in_specs,
        out_specs=out_specs,
        out_shape=out_shape,
        scratch_shapes=[pltpu.VMEM((tm, d), BF16)],
        compiler_params=_cparams(("parallel", "parallel", "arbitrary")),
        name="proj",
    )(*args)
    return res if with_gates else res[0]


CONV_HALO = SUBLANES


def _stage_padded(pad_ref, load_rows, t, rb):
    zeros = jnp.zeros((CONV_HALO, LANES), F32)
    pad_ref[0:CONV_HALO, :] = zeros
    pad_ref[CONV_HALO + t:2 * CONV_HALO + t, :] = zeros

    def body(i, _):
        r0 = pl.multiple_of(i * rb, rb)
        pad_ref[pl.ds(CONV_HALO + r0, rb), :] = load_rows(r0)
        return 0

    lax.fori_loop(0, t // rb, body, 0)


def _conv_rows(pad_ref, w_ref, r0, rb, taps):
    n = rb + 2 * CONV_HALO
    xs = pad_ref[pl.ds(r0, n), :]
    acc = None
    for j in range(taps):
        shift = (taps // 2 - j) % n
        xj = xs if shift == 0 else pltpu.roll(xs, shift, axis=0)
        term = xj[CONV_HALO:CONV_HALO + rb, :] * w_ref[j:j + 1, :]
        acc = term if acc is None else acc + term
    return acc


def _l2norm(x):
    return x * lax.rsqrt(jnp.sum(x * x, axis=-1, keepdims=True) + NORM_EPS)


def _gdn_kernel(*refs, t, with_q):
    nc = t // CHUNK
    rb = min(128, t)
    if with_q:
        (ka_ref, va_ref, qa_ref, za_ref, wk_ref, wv_ref, wq_ref, grow_ref, cos_ref, sin_ref, s0_ref,
         gn_ref, y_ref,
         pad_ref, k_ref, v_ref, q_ref, o_ref, gc_ref, wq_s, a_s, kt_s, u_s, gl_s) = refs
    else:
        (ka_ref, va_ref, wk_ref, wv_ref, grow_ref, sfin_ref,
         pad_ref, k_ref, v_ref, gc_ref, wq_s, kt_s, u_s, gl_s) = refs

    def conv_all(src_ref, w_ref, finish, dst_ref):
        _stage_padded(pad_ref, lambda r0: src_ref[pl.ds(r0, rb), :], t, rb)

        def body(i, _):
            r0 = pl.multiple_of(i * rb, rb)
            y = _silu(_conv_rows(pad_ref, w_ref, r0, rb, GDN_CONV))
            dst_ref[pl.ds(r0, rb), :] = finish(y, r0)
            return 0

        lax.fori_loop(0, t // rb, body, 0)

    def rope(x, r0):
        return (x * cos_ref[pl.ds(r0, rb), :]
                + pltpu.roll(x, HEAD_DIM // 2, axis=1) * sin_ref[pl.ds(r0, rb), :])

    if with_q:
        conv_all(ka_ref, wk_ref, lambda y, r0: rope(_l2norm(y), r0), k_ref)
        conv_all(qa_ref, wq_ref, lambda y, r0: rope(_l2norm(y), r0) * HEAD_DIM ** -0.5, q_ref)
    else:
        conv_all(ka_ref, wk_ref, lambda y, r0: _l2norm(y), k_ref)
    conv_all(va_ref, wv_ref, lambda y, r0: y, v_ref)

    ii = lax.broadcasted_iota(jnp.int32, (CHUNK, CHUNK), 0)
    jj = lax.broadcasted_iota(jnp.int32, (CHUNK, CHUNK), 1)
    gc_ref[0] = _mm_exact(grow_ref[0], (ii <= jj).astype(F32))
    gc_ref[1] = _mm_exact(grow_ref[1], (ii >= jj).astype(F32))

    eye = ii == jj
    ones_cl = jnp.ones((CHUNK, LANES), F32)
    ident = (lax.broadcasted_iota(jnp.int32, (HEAD_DIM, HEAD_DIM), 0)
             == lax.broadcasted_iota(jnp.int32, (HEAD_DIM, HEAD_DIM), 1)).astype(BF16)

    def prep(c, _):
        r0 = pl.multiple_of(c * CHUNK, CHUNK)
        kc = k_ref[pl.ds(r0, CHUNK), :]
        vc = v_ref[pl.ds(r0, CHUNK), :]
        kcb = kc.astype(BF16)
        if with_q:
            qc = q_ref[pl.ds(r0, CHUNK), :]
            both = _mm_nt(jnp.concatenate([kc, qc], axis=0), kcb)
            kk, qk = both[:CHUNK], both[CHUNK:]
        else:
            kk = _mm_nt(kcb, kcb)
        for d in range(2):
            incl = (ii >= jj) if d == 0 else (ii <= jj)
            strict = (ii > jj) if d == 0 else (ii < jj)
            gc_r = jnp.broadcast_to(gc_ref[d, pl.ds(c, 1), :], (CHUNK, CHUNK))
            be_r = jnp.broadcast_to(grow_ref[2 + d, pl.ds(c, 1), :], (CHUNK, CHUNK))
            gc_c = _mm_exact(jnp.where(eye, gc_r, 0.0), ones_cl)
            be_c = _mm_exact(jnp.where(eye, be_r, 0.0), ones_cl)
            diff = gc_c[:, :CHUNK] - gc_r
            decay = jnp.where(incl, jnp.exp(jnp.where(incl, diff, 0.0)), 0.0)
            m = jnp.where(strict, kk * be_c[:, :CHUNK] * decay, 0.0)
            p = jnp.where(eye, 1.0, -m)
            sq = _mm(m, m)
            for _ in range(4):
                p = p + _mm(p, sq)
                sq = _mm(sq, sq)
            p = p + _mm(p, sq)
            u = _mm(p * be_r, vc)
            w = _mm(p * (be_r * jnp.exp(gc_r)), kc)
            gl = gc_c[CHUNK - 1:CHUNK, :] if d == 0 else gc_c[0:1, :]
            kd = kc * jnp.exp(gl - gc_c)
            kt_s[d, c] = _mm_nt(ident, kd).astype(BF16)
            u_s[d, c] = u
            gl_s[d, c] = jnp.broadcast_to(jnp.exp(gl), (SUBLANES, LANES))
            if with_q:
                qd = qc * jnp.exp(gc_c)
                wq_s[d, c] = jnp.concatenate([w, qd], axis=0).astype(BF16)
                a_s[d, c] = (qk * decay).astype(BF16)
            else:
                wq_s[d, c] = w.astype(BF16)
        return 0

    lax.fori_loop(0, nc, prep, 0)

    if with_q:
        o_ref[...] = jnp.zeros_like(o_ref)

    def scan(s, carry):
        new = []
        for d in range(2):
            st = carry[d]
            c = s if d == 0 else nc - 1 - s
            r = jnp.dot(wq_s[d, c], st.astype(BF16), preferred_element_type=F32)
            vb = (u_s[d, c] - r[:CHUNK]).astype(BF16)
            if with_q:
                r0 = pl.multiple_of(c * CHUNK, CHUNK)
                o = r[CHUNK:] + jnp.dot(a_s[d, c], vb, preferred_element_type=F32)
                o_ref[pl.ds(r0, CHUNK), :] += o
            new.append(st * gl_s[d, c][0:1, :]
                       + jnp.dot(kt_s[d, c], vb, preferred_element_type=F32))
        return tuple(new)

    if with_q:
        init = (s0_ref[0], s0_ref[1])
    else:
        zero = jnp.zeros((HEAD_DIM, HEAD_DIM), F32)
        init = (zero, zero)
    s_f, s_b = lax.fori_loop(0, nc, scan, init)

    if not with_q:
        sfin_ref[0] = s_f
        sfin_ref[1] = s_b
        return

    def finish(i, _):
        r0 = pl.multiple_of(i * rb, rb)
        o = o_ref[pl.ds(r0, rb), :]
        y = o * lax.rsqrt(jnp.mean(o * o, axis=-1, keepdims=True) + NORM_EPS)
        y_ref[pl.ds(r0, rb), :] = ((y * gn_ref[...]) * _silu(za_ref[pl.ds(r0, rb), :])).astype(y_ref.dtype)
        return 0

    lax.fori_loop(0, t // rb, finish, 0)


def _single(block_shape, index_map):
    return pl.BlockSpec(block_shape, index_map, pipeline_mode=pl.Buffered(1))


def gdn_latent(p, conv_w, grow, cosf, sinf, s0, gnorm, col_k, col_v, col_q, col_z):
    bn, t, _ = p.shape
    nc = t // CHUNK
    hd = HEAD_DIM
    seq = lambda col: _single((None, t, hd), lambda b, h, col=col: (b, 0, col + h))
    cw = lambda col: pl.BlockSpec((GDN_CONV, hd), lambda b, h, col=col: (0, col + h))
    return pl.pallas_call(
        functools.partial(_gdn_kernel, t=t, with_q=True),
        grid=(bn, GDN_HEADS),
        in_specs=[
            seq(col_k), seq(col_v), seq(col_q), seq(col_z),
            cw(0), cw(GDN_HEADS), cw(2 * GDN_HEADS),
            pl.BlockSpec((None, None, 4, nc, CHUNK), lambda b, h: (b, h, 0, 0, 0)),
            _single((t, hd), lambda b, h: (0, 0)),
            _single((t, hd), lambda b, h: (0, 0)),
            pl.BlockSpec((None, None, 2, hd, hd), lambda b, h: (b, h, 0, 0, 0)),
            pl.BlockSpec((1, hd), lambda b, h: (0, 0)),
        ],
        out_specs=pl.BlockSpec((None, t, hd), lambda b, h: (b, 0, h)),
        out_shape=jax.ShapeDtypeStruct((bn, t, GDN_HEADS * hd), BF16),
        scratch_shapes=[
            pltpu.VMEM((t + 2 * CONV_HALO, hd), F32),
            pltpu.VMEM((t, hd), F32),
            pltpu.VMEM((t, hd), F32),
            pltpu.VMEM((t, hd), F32),
            pltpu.VMEM((t, hd), F32),
            pltpu.VMEM((2, nc, CHUNK), F32),
            pltpu.VMEM((2, nc, 2 * CHUNK, hd), BF16),
            pltpu.VMEM((2, nc, CHUNK, CHUNK), BF16),
            pltpu.VMEM((2, nc, hd, CHUNK), BF16),
            pltpu.VMEM((2, nc, CHUNK, hd), F32),
            pltpu.VMEM((2, nc, SUBLANES, LANES), F32),
        ],
        compiler_params=_cparams(("parallel", "parallel")),
        name="gdn_latent",
    )(p, p, p, p, conv_w, conv_w, conv_w, grow, cosf, sinf, s0, gnorm.reshape(1, hd))


def gdn_context(pc, conv_w, grow, col_k, col_v):
    bn, t, _ = pc.shape
    nc = t // CHUNK
    ncp = grow.shape[3]
    hd = HEAD_DIM
    seq = lambda col: pl.BlockSpec((None, t, hd), lambda b, h, col=col: (b, 0, col + h))
    cw = lambda col: pl.BlockSpec((GDN_CONV, hd), lambda b, h, col=col: (0, col + h))
    return pl.pallas_call(
        functools.partial(_gdn_kernel, t=t, with_q=False),
        grid=(bn, GDN_HEADS),
        in_specs=[
            seq(col_k), seq(col_v), cw(0), cw(GDN_HEADS),
            pl.BlockSpec((None, None, 4, ncp, CHUNK), lambda b, h: (b, h, 0, 0, 0)),
        ],
        out_specs=pl.BlockSpec((None, None, 2, hd, hd), lambda b, h: (b, h, 0, 0, 0)),
        out_shape=jax.ShapeDtypeStruct((bn, GDN_HEADS, 2, hd, hd), F32),
        scratch_shapes=[
            pltpu.VMEM((t + 2 * CONV_HALO, hd), F32),
            pltpu.VMEM((t, hd), F32),
            pltpu.VMEM((t, hd), F32),
            pltpu.VMEM((2, ncp, CHUNK), F32),
            pltpu.VMEM((2, nc, CHUNK, hd), BF16),
            pltpu.VMEM((2, nc, hd, CHUNK), BF16),
            pltpu.VMEM((2, nc, CHUNK, hd), F32),
            pltpu.VMEM((2, nc, SUBLANES, LANES), F32),
        ],
        compiler_params=_cparams(("parallel", "parallel")),
        name="gdn_context",
    )(pc, pc, conv_w, conv_w, grow)


def gate_rows(gates, t):
    bn = gates.shape[0]
    nc = t // CHUNK
    g = gates[:, :, :4 * GDN_HEADS].reshape(bn, nc, CHUNK, 4, GDN_HEADS)
    g = jnp.transpose(g, (0, 4, 3, 1, 2))
    if nc < SUBLANES:
        g = jnp.pad(g, ((0, 0), (0, 0), (0, 0), (0, SUBLANES - nc), (0, 0)))
    return g


def _natten_kernel(q_ref, k_ref, v_ref, kc_ref, vc_ref, bias_ref, o_ref, kb_s, vb_s, *, rows):
    t = rows * GRID_W
    rb = 512

    def cast(i, _):
        r0 = pl.multiple_of(i * rb, rb)
        kb_s[pl.ds(r0, rb), :] = k_ref[pl.ds(r0, rb), :].astype(BF16)
        vb_s[pl.ds(r0, rb), :] = v_ref[pl.ds(r0, rb), :].astype(BF16)
        return 0

    lax.fori_loop(0, t // rb, cast, 0)
    kcb = kc_ref[...].astype(BF16)
    vcb = vc_ref[...].astype(BF16)
    scale = HEAD_DIM ** -0.5
    win = NA_KH * GRID_W

    def body(r, _):
        rs = jnp.clip(r - NA_KH // 2, 0, rows - NA_KH)
        q0 = pl.multiple_of(r * GRID_W, GRID_W)
        k0 = pl.multiple_of(rs * GRID_W, GRID_W)
        q = q_ref[pl.ds(q0, GRID_W), :].astype(BF16)
        kw = kb_s[pl.ds(k0, win), :]
        vw = vb_s[pl.ds(k0, win), :]
        s_loc = lax.dot_general(q, kw, NT_DIMS, preferred_element_type=F32) * scale + bias_ref[r - rs]
        s_ctx = lax.dot_general(q, kcb, NT_DIMS, preferred_element_type=F32) * scale
        m = jnp.maximum(jnp.max(s_loc, axis=-1, keepdims=True), jnp.max(s_ctx, axis=-1, keepdims=True))
        p_loc = jnp.exp(s_loc - m)
        p_ctx = jnp.exp(s_ctx - m)
        l = jnp.sum(p_loc, axis=-1, keepdims=True) + jnp.sum(p_ctx, axis=-1, keepdims=True)
        o = (jnp.dot(p_loc.astype(BF16), vw, preferred_element_type=F32)
             + jnp.dot(p_ctx.astype(BF16), vcb, preferred_element_type=F32))
        o_ref[pl.ds(q0, GRID_W), :] = (o / l).astype(o_ref.dtype)
        return 0

    lax.fori_loop(0, rows, body, 0)


def natten(p, pc, bias, col_q, col_k, col_v, ctx_col_k, ctx_col_v):
    bn, t, _ = p.shape
    lc = pc.shape[1]
    hd = HEAD_DIM
    rows = t // GRID_W
    seq = lambda col: pl.BlockSpec((None, t, hd), lambda b, h, col=col: (b, 0, col + h))
    cseq = lambda col: pl.BlockSpec((None, lc, hd), lambda b, h, col=col: (b, 0, col + h))
    return pl.pallas_call(
        functools.partial(_natten_kernel, rows=rows),
        grid=(bn, NA_HEADS),
        in_specs=[
            seq(col_q), seq(col_k), seq(col_v), cseq(ctx_col_k), cseq(ctx_col_v),
            pl.BlockSpec((None, NA_KH, GRID_W, NA_KH * GRID_W), lambda b, h: (h, 0, 0, 0)),
        ],
        out_specs=pl.BlockSpec((None, t, hd), lambda b, h: (b, 0, h)),
        out_shape=jax.ShapeDtypeStruct((bn, t, NA_HEADS * hd), BF16),
        scratch_shapes=[pltpu.VMEM((t, hd), BF16), pltpu.VMEM((t, hd), BF16)],
        compiler_params=_cparams(("parallel", "parallel")),
        name="natten",
    )(p, p, p, pc, pc, bias)


def natten_bias(rpb, rows):
    kh = min(NA_KH, rows)
    cls = np.arange(NA_KH)[:, None]
    j = np.arange(kh)[None, :]
    dr = j - cls + (NA_KH - 1)
    col = np.arange(GRID_W)
    col_start = np.clip(col - NA_KW // 2, 0, GRID_W - NA_KW)
    in_win = (col[None, :] >= col_start[:, None]) & (col[None, :] < col_start[:, None] + NA_KW)
    dc = np.clip(col[None, :] - col[:, None], -(NA_KW - 1), NA_KW - 1) + (NA_KW - 1)
    b = rpb.astype(F32)[:, dr[:, None, :, None], dc[None, :, None, :]]
    b = jnp.where(in_win[None, None, :, None, :], b, NEG_BIG)
    return b.reshape(rpb.shape[0], NA_KH, GRID_W, kh * GRID_W)


def _zmix_kernel(gb_ref, gc_ref, val_ref, w_ref, z_ref, pad_ref, *, t):
    rb = min(256, t)
    _stage_padded(pad_ref, lambda r0: gc_ref[pl.ds(r0, rb), :] * val_ref[pl.ds(r0, rb), :], t, rb)

    def body(i, _):
        r0 = pl.multiple_of(i * rb, rb)
        y = _conv_rows(pad_ref, w_ref, r0, rb, SC_CONV)
        z_ref[pl.ds(r0, rb), :] = (gb_ref[pl.ds(r0, rb), :] * y).astype(z_ref.dtype)
        return 0

    lax.fori_loop(0, t // rb, body, 0)


def zmix(p, conv_w):
    bn, t, n3 = p.shape
    d = n3 // 3
    nb = d // LANES
    seq = lambda off: pl.BlockSpec((None, t, LANES), lambda b, j, off=off: (b, 0, off + j))
    return pl.pallas_call(
        functools.partial(_zmix_kernel, t=t),
        grid=(bn, nb),
        in_specs=[seq(0), seq(nb), seq(2 * nb), pl.BlockSpec((SC_CONV, LANES), lambda b, j: (0, j))],
        out_specs=pl.BlockSpec((None, t, LANES), lambda b, j: (b, 0, j)),
        out_shape=jax.ShapeDtypeStruct((bn, t, d), BF16),
        scratch_shapes=[pltpu.VMEM((t + 2 * CONV_HALO, LANES), F32)],
        compiler_params=_cparams(("parallel", "parallel")),
        name="zmix",
    )(p, p, p, conv_w)


def _outproj_kernel(*refs, n_in):
    x_ref, gate_ref = refs[0], refs[1]
    a_refs = refs[2:2 + n_in]
    w_refs = refs[2 + n_in:2 + 2 * n_in]
    o_ref = refs[2 + 2 * n_in]
    y = None
    for a_ref, w_ref in zip(a_refs, w_refs):
        part = jnp.dot(a_ref[...], w_ref[...], preferred_element_type=F32)
        y = part if y is None else y + part
    o_ref[...] = x_ref[...] + gate_ref[...] * y


def outproj(x, gate, acts, weights, tm=512, tn=1024):
    bn, t, d = x.shape
    tn = min(tn, d)
    n_in = len(acts)
    in_specs = [pl.BlockSpec((None, tm, tn), lambda b, i, j: (b, i, j)),
                pl.BlockSpec((None, 1, tn), lambda b, i, j: (b, 0, j))]
    in_specs += [pl.BlockSpec((None, tm, a.shape[-1]), lambda b, i, j: (b, i, 0)) for a in acts]
    in_specs += [pl.BlockSpec((w.shape[0], tn), lambda b, i, j: (0, j)) for w in weights]
    return pl.pallas_call(
        functools.partial(_outproj_kernel, n_in=n_in),
        grid=(bn, t // tm, d // tn),
        in_specs=in_specs,
        out_specs=pl.BlockSpec((None, tm, tn), lambda b, i, j: (b, i, j)),
        out_shape=jax.ShapeDtypeStruct((bn, t, d), F32),
        compiler_params=_cparams(("parallel", "parallel", "parallel")),
        name="outproj",
    )(x, gate, *acts, *weights)


def _ffn_kernel(x_ref, g_ref, sc_ref, sh_ref, gate_ref, fn_ref, wg_ref, wu_ref, wd_ref, o_ref,
                hs_ref, acc_ref, *, final_norm):
    j = pl.program_id(2)

    @pl.when(j == 0)
    def _():
        hs_ref[...] = _modulated_norm(x_ref[...], g_ref[...], sc_ref[...], sh_ref[...]).astype(BF16)
        acc_ref[...] = jnp.zeros_like(acc_ref)

    h = hs_ref[...]
    gate = jnp.dot(h, wg_ref[...], preferred_element_type=F32)
    up = jnp.dot(h, wu_ref[...], preferred_element_type=F32)
    a = (_silu(gate) * up).astype(BF16)
    acc_ref[...] += jnp.dot(a, wd_ref[...], preferred_element_type=F32)

    @pl.when(j == pl.num_programs(2) - 1)
    def _():
        y = x_ref[...] + gate_ref[...] * acc_ref[...]
        if final_norm:
            y = (y * lax.rsqrt(jnp.mean(y * y, axis=-1, keepdims=True) + NORM_EPS)) * fn_ref[...]
        o_ref[...] = y


def ffn(x, g, sc, sh, gate, fnorm, w_gate, w_up, w_down, final_norm, tm=512, tf=512):
    bn, t, d = x.shape
    f = w_gate.shape[1]
    vec = pl.BlockSpec((None, 1, d), lambda b, i, j: (b, 0, 0))
    one = pl.BlockSpec((1, d), lambda b, i, j: (0, 0))
    return pl.pallas_call(
        functools.partial(_ffn_kernel, final_norm=final_norm),
        grid=(bn, t // tm, f // tf),
        in_specs=[
            pl.BlockSpec((None, tm, d), lambda b, i, j: (b, i, 0)),
            one, vec, vec, vec, one,
            pl.BlockSpec((d, tf), lambda b, i, j: (0, j)),
            pl.BlockSpec((d, tf), lambda b, i, j: (0, j)),
            pl.BlockSpec((tf, d), lambda b, i, j: (j, 0)),
        ],
        out_specs=pl.BlockSpec((None, tm, d), lambda b, i, j: (b, i, 0)),
        out_shape=jax.ShapeDtypeStruct((bn, t, d), F32),
        scratch_shapes=[pltpu.VMEM((tm, d), BF16), pltpu.VMEM((tm, d), F32)],
        compiler_params=_cparams(("parallel", "parallel", "arbitrary")),
        name="ffn",
    )(x, g.reshape(1, d), sc, sh, gate, fnorm.reshape(1, d), w_gate, w_up, w_down)


def rope_tables(t):
    pos = np.arange(t)
    row = (pos // GRID_W).astype(np.float32)
    col = (pos % GRID_W).astype(np.float32)
    n_freq = HEAD_DIM // 4
    inv_freq = jnp.asarray(ROPE_THETA, F32) ** (-jnp.arange(n_freq, dtype=F32) / n_freq)
    ang = jnp.concatenate([jnp.asarray(row)[:, None] * inv_freq, jnp.asarray(col)[:, None] * inv_freq], axis=-1)
    cos, sin = jnp.cos(ang), jnp.sin(ang)
    return jnp.concatenate([cos, cos], axis=-1), jnp.concatenate([-sin, sin], axis=-1)


def kernel(x, c, ctx, c_ctx, ada_w, ada_b, norm_mix, norm_ffn, ffn_w_gate, ffn_w_up, ffn_w_down, final_norm,
           ev_w_in, ev_conv, ev_a_log, ev_dt_bias, ev_gdn_norm, ev_rpb, ev_w_out, od_w_in, od_conv, od_w_out):
    bn, t, d = x.shape
    depth = ada_w.shape[0]
    assert depth <= 2
    gw = GDN_HEADS * HEAD_DIM
    nw = NA_HEADS * HEAD_DIM
    nh = GDN_HEADS

    cv = jnp.zeros((SUBLANES, d), F32).at[:bn].set(c).at[bn].set(c_ctx)
    mods = ada_modulation(cv, ada_w, ada_b)

    def mod_vecs(l, rows):
        m = mods[l, rows].reshape(-1, 6, d)
        return [m[:, k][:, None, :] for k in range(6)]

    x_lat = x
    for l in range(depth):
        sh1, sc1, g1, sh2, sc2, g2 = mod_vecs(l, slice(0, bn))
        if l % 2 == 0:
            e = l // 2
            csh1, csc1 = [jnp.broadcast_to(v, (bn, 1, d)) for v in mod_vecs(l, slice(bn, bn + 1))[:2]]
            w_in = ev_w_in[e]
            n_gate = 4 * nh
            seg_gate = 2 * gw + 2 * nw
            w_main = jnp.concatenate([w_in[:, :seg_gate], w_in[:, seg_gate + n_gate:]], axis=1).astype(BF16)
            w_ab = jnp.pad(w_in[:, seg_gate:seg_gate + n_gate], ((0, 0), (0, LANES - n_gate))).astype(BF16)
            gparams = jnp.zeros((SUBLANES, LANES), F32)
            gparams = gparams.at[0, :2 * nh].set(ev_a_log[e].reshape(-1)).at[1, :2 * nh].set(ev_dt_bias[e].reshape(-1))
            col_ka, col_va, col_kb, col_vb, col_qa, col_qb, col_za = [k * nh for k in range(7)]

            pc, gates_c = proj(ctx, norm_mix[l], csc1, csh1, w_main, seg_gate, w_ab, gparams, tm=ctx.shape[1])
            s_ctx = gdn_context(pc, ev_conv[e], gate_rows(gates_c, ctx.shape[1]), col_ka, col_va)

            p, gates = proj(x_lat, norm_mix[l], sc1, sh1, w_main, w_main.shape[1], w_ab, gparams)
            cosf, sinf = rope_tables(t)
            y_gdn = gdn_latent(p, ev_conv[e], gate_rows(gates, t), cosf, sinf, s_ctx, ev_gdn_norm[e],
                               col_ka, col_va, col_qa, col_za)
            y_na = natten(p, pc, natten_bias(ev_rpb[e], t // GRID_W), col_qb, col_kb, col_vb, col_kb, col_vb)
            w_out = ev_w_out[e].astype(BF16)
            x_lat = outproj(x_lat, g1, [y_gdn, y_na], [w_out[:gw], w_out[gw:]])
        else:
            j_odd = l // 2
            p = proj(x_lat, norm_mix[l], sc1, sh1, od_w_in[j_odd].astype(BF16), 3 * d)
            z = zmix(p, od_conv[j_odd])
            x_lat = outproj(x_lat, g1, [z], [od_w_out[j_odd].astype(BF16)])
        x_lat = ffn(x_lat, norm_ffn[l], sc2, sh2, g2, final_norm,
                    ffn_w_gate[l].astype(BF16), ffn_w_up[l].astype(BF16), ffn_w_down[l].astype(BF16),
                    final_norm=(l == depth - 1))
    return x_lat
```

```python
import functools
import math

import jax
import jax.numpy as jnp
import numpy as np
from jax import lax
from jax.experimental import pallas as pl
from jax.experimental.pallas import tpu as pltpu

F32 = jnp.float32
BF16 = jnp.bfloat16
HIGHEST = lax.Precision.HIGHEST

LANES = 128
SUBLANES = 8
VMEM_LIMIT = 56 * 1024 * 1024

GRID_W = 64
HEAD_DIM = 128
GDN_HEADS = 8
NA_HEADS = 8
GDN_CONV = 5
CHUNK = 64
GROUP = 2 * CHUNK
PREP_GROUPS_PER_ITER = 4
NA_KH = 8
NA_KW = 16
SC_CONV = 3
ROPE_THETA = 10000.0
NORM_EPS = 1e-6
NEG_BIG = -1e30

NT_DIMS = (((1,), (1,)), ((), ()))


def _cparams(sem):
    return pltpu.CompilerParams(dimension_semantics=sem, vmem_limit_bytes=VMEM_LIMIT)


def _sigmoid(x):
    return 1.0 / (1.0 + jnp.exp(-x))


def _silu(x):
    return x * _sigmoid(x)


def _softplus(x):
    return jnp.maximum(x, 0.0) + jnp.log(1.0 + jnp.exp(-jnp.abs(x)))


def _mm(a, b):
    return jnp.dot(a.astype(BF16), b.astype(BF16), preferred_element_type=F32)


def _mm_nt(a, b):
    return lax.dot_general(a.astype(BF16), b.astype(BF16), NT_DIMS, preferred_element_type=F32)


def _mm_exact(a, b):
    return jnp.dot(a, b, precision=HIGHEST, preferred_element_type=F32)


def _ada_kernel(cv_ref, w_ref, b_ref, o_ref):
    s = _silu(cv_ref[...])
    o_ref[...] = _mm_exact(s, w_ref[...]) + b_ref[...]


def ada_modulation(cv, ada_w, ada_b, tn=1024):
    depth, d, n = ada_w.shape
    return pl.pallas_call(
        _ada_kernel,
        grid=(depth, n // tn),
        in_specs=[
            pl.BlockSpec((SUBLANES, d), lambda l, j: (0, 0)),
            pl.BlockSpec((None, d, tn), lambda l, j: (l, 0, j)),
            pl.BlockSpec((None, 1, tn), lambda l, j: (l, 0, j)),
        ],
        out_specs=pl.BlockSpec((None, SUBLANES, tn), lambda l, j: (l, 0, j)),
        out_shape=jax.ShapeDtypeStruct((depth, SUBLANES, n), F32),
        compiler_params=_cparams(("parallel", "parallel")),
        name="ada",
    )(cv, ada_w, ada_b.reshape(depth, 1, n))


def _modulated_norm(x, g, sc, sh):
    y = x * lax.rsqrt(jnp.mean(x * x, axis=-1, keepdims=True) + NORM_EPS)
    return (y * g) * (1.0 + sc) + sh


def _proj_kernel(*refs, with_gates):
    if with_gates:
        x_ref, g_ref, sc_ref, sh_ref, w_ref, wab_ref, gp_ref, o_ref, gate_ref, hs_ref = refs
    else:
        x_ref, g_ref, sc_ref, sh_ref, w_ref, o_ref, hs_ref = refs

    @pl.when(pl.program_id(2) == 0)
    def _():
        hb = _modulated_norm(x_ref[...], g_ref[...], sc_ref[...], sh_ref[...]).astype(BF16)
        hs_ref[...] = hb
        if with_gates:
            a = jnp.dot(hb, wab_ref[...], preferred_element_type=F32)
            neg_decay_rate = -jnp.exp(gp_ref[0:1, :])
            g = neg_decay_rate * _softplus(a + gp_ref[1:2, :])
            lane = lax.broadcasted_iota(jnp.int32, a.shape, 1)
            gate_ref[...] = jnp.where(lane < 2 * GDN_HEADS, g, _sigmoid(a))

    o_ref[...] = jnp.dot(hs_ref[...], w_ref[...], preferred_element_type=F32)


def proj(x, g, sc, sh, w, n_out, wab=None, gparams=None, tm=512, tn=512):
    bn, t, d = x.shape
    tm = min(tm, t)
    with_gates = wab is not None
    vec = pl.BlockSpec((None, 1, d), lambda b, i, j: (b, 0, 0))
    in_specs = [
        pl.BlockSpec((None, tm, d), lambda b, i, j: (b, i, 0)),
        pl.BlockSpec((1, d), lambda b, i, j: (0, 0)),
        vec, vec,
        pl.BlockSpec((d, tn), lambda b, i, j: (0, j)),
    ]
    args = [x, g.reshape(1, d), sc, sh, w]
    out_specs = [pl.BlockSpec((None, tm, tn), lambda b, i, j: (b, i, j))]
    out_shape = [jax.ShapeDtypeStruct((bn, t, n_out), F32)]
    if with_gates:
        in_specs += [pl.BlockSpec((d, LANES), lambda b, i, j: (0, 0)),
                     pl.BlockSpec((SUBLANES, LANES), lambda b, i, j: (0, 0))]
        args += [wab, gparams]
        out_specs.append(pl.BlockSpec((None, tm, LANES), lambda b, i, j: (b, i, 0)))
        out_shape.append(jax.ShapeDtypeStruct((bn, t, LANES), F32))
    res = pl.pallas_call(
        functools.partial(_proj_kernel, with_gates=with_gates),
        grid=(bn, t // tm, n_out // tn),
        in_specs=in_specs,
        out_specs=out_specs,
        out_shape=out_shape,
        scratch_shapes=[pltpu.VMEM((tm, d), BF16)],
        compiler_params=_cparams(("parallel", "parallel", "arbitrary")),
        name="proj",
    )(*args)
    return res if with_gates else res[0]


CONV_HALO = SUBLANES


def _stage_padded(pad_ref, load_rows, t, rb):
    zeros = jnp.zeros((CONV_HALO, LANES), F32)
    pad_ref[0:CONV_HALO, :] = zeros
    pad_ref[CONV_HALO + t:2 * CONV_HALO + t, :] = zeros

    def body(i, _):
        r0 = pl.multiple_of(i * rb, rb)
        pad_ref[pl.ds(CONV_HALO + r0, rb), :] = load_rows(r0)
        return 0

    lax.fori_loop(0, t // rb, body, 0)


def _conv_rows(pad_ref, w_ref, r0, rb, taps):
    n = rb + 2 * CONV_HALO
    xs = pad_ref[pl.ds(r0, n), :]
    acc = None
    for j in range(taps):
        shift = (taps // 2 - j) % n
        xj = xs if shift == 0 else pltpu.roll(xs, shift, axis=0)
        term = xj[CONV_HALO:CONV_HALO + rb, :] * w_ref[j:j + 1, :]
        acc = term if acc is None else acc + term
    return acc


def _l2norm(x):
    return x * lax.rsqrt(jnp.sum(x * x, axis=-1, keepdims=True) + NORM_EPS)


def _gdn_kernel(*refs, t, with_q):
    nc = t // CHUNK
    rb = min(128, t)
    if with_q:
        (ka_ref, va_ref, qa_ref, za_ref, wk_ref, wv_ref, wq_ref, grow_ref, cos_ref, sin_ref, s0_ref,
         gn_ref, y_ref,
         pad_ref, k_ref, v_ref, q_ref, o_ref, gc_ref, wq_s, a_s, kt_s, u_s, gl_s) = refs
    else:
        (ka_ref, va_ref, wk_ref, wv_ref, grow_ref, sfin_ref,
         pad_ref, k_ref, v_ref, gc_ref, wq_s, kt_s, u_s, gl_s) = refs

    def conv_all(src_ref, w_ref, finish, dst_ref):
        _stage_padded(pad_ref, lambda r0: src_ref[pl.ds(r0, rb), :], t, rb)

        def body(i, _):
            r0 = pl.multiple_of(i * rb, rb)
            y = _silu(_conv_rows(pad_ref, w_ref, r0, rb, GDN_CONV))
            dst_ref[pl.ds(r0, rb), :] = finish(y, r0)
            return 0

        lax.fori_loop(0, t // rb, body, 0)

    def rope(x, r0):
        return (x * cos_ref[pl.ds(r0, rb), :]
                + pltpu.roll(x, HEAD_DIM // 2, axis=1) * sin_ref[pl.ds(r0, rb), :])

    if with_q:
        conv_all(ka_ref, wk_ref, lambda y, r0: rope(_l2norm(y), r0), k_ref)
        conv_all(qa_ref, wq_ref, lambda y, r0: rope(_l2norm(y), r0) * HEAD_DIM ** -0.5, q_ref)
    else:
        conv_all(ka_ref, wk_ref, lambda y, r0: _l2norm(y), k_ref)
    conv_all(va_ref, wv_ref, lambda y, r0: y, v_ref)

    ii = lax.broadcasted_iota(jnp.int32, (GROUP, GROUP), 0)
    jj = lax.broadcasted_iota(jnp.int32, (GROUP, GROUP), 1)
    same = (ii // CHUNK) == (jj // CHUNK)
    lower = same & (ii >= jj)
    upper = same & (ii <= jj)
    eye = ii == jj
    gc_ref[0] = _mm_exact(grow_ref[0], upper.astype(F32))
    gc_ref[1] = _mm_exact(grow_ref[1], lower.astype(F32))

    def prep_groups(groups):
        chains = [(g, d) for g in range(len(groups)) for d in range(2)]
        kgs, vgs, qgs, kks, qks = {}, {}, {}, {}, {}
        for g in range(len(groups)):
            r0 = pl.multiple_of(groups[g] * GROUP, GROUP)
            kgs[g] = k_ref[pl.ds(r0, GROUP), :]
            vgs[g] = v_ref[pl.ds(r0, GROUP), :]
            kgb = kgs[g].astype(BF16)
            if with_q:
                qgs[g] = q_ref[pl.ds(r0, GROUP), :]
                both = _mm_nt(jnp.concatenate([kgs[g], qgs[g]], axis=0), kgb)
                kks[g], qks[g] = both[:GROUP], both[GROUP:]
            else:
                kks[g] = _mm_nt(kgb, kgb)
        gc_r, be_r, gc_c, decay, p, sq = {}, {}, {}, {}, {}, {}
        for ch in chains:
            g, d = ch
            incl = lower if d == 0 else upper
            gc_r[ch] = jnp.broadcast_to(gc_ref[d, pl.ds(groups[g], 1), :], (GROUP, GROUP))
            be_r[ch] = jnp.broadcast_to(grow_ref[2 + d, pl.ds(groups[g], 1), :], (GROUP, GROUP))
            gc_c[ch] = gc_r[ch].T
            decay[ch] = jnp.where(incl, jnp.exp(jnp.where(incl, gc_c[ch] - gc_r[ch], 0.0)), 0.0)
            m = jnp.where(incl & jnp.logical_not(eye), kks[g] * be_r[ch].T * decay[ch], 0.0)
            p[ch] = jnp.where(eye, 1.0, -m)
            sq[ch] = _mm(m, m)
        for _ in range(4):
            for ch in chains:
                p[ch] = p[ch] + _mm(p[ch], sq[ch])
                sq[ch] = _mm(sq[ch], sq[ch])
        for ch in chains:
            p[ch] = p[ch] + _mm(p[ch], sq[ch])
        u, w = {}, {}
        for ch in chains:
            g, d = ch
            u[ch] = _mm(p[ch] * be_r[ch], vgs[g])
            w[ch] = _mm(p[ch] * (be_r[ch] * jnp.exp(gc_r[ch])), kgs[g])
        for ch in chains:
            g, d = ch
            last = [CHUNK - 1, GROUP - 1] if d == 0 else [0, CHUNK]
            gl = [gc_c[ch][i:i + 1, :] for i in last]
            gl_c = jnp.concatenate([jnp.broadcast_to(x, (CHUNK, LANES)) for x in gl], axis=0)
            kdt = (kgs[g] * jnp.exp(gl_c - gc_c[ch])).T
            if with_q:
                qd = qgs[g] * jnp.exp(gc_c[ch])
                a = qks[g] * decay[ch]
            for half in range(2):
                c = 2 * groups[g] + half
                rs = slice(half * CHUNK, (half + 1) * CHUNK)
                kt_s[d, c] = kdt[:, rs].astype(BF16)
                u_s[d, c] = u[ch][rs]
                gl_s[d, c] = jnp.broadcast_to(jnp.exp(gl[half]), (SUBLANES, LANES))
                if with_q:
                    wq_s[d, c] = jnp.concatenate([w[ch][rs], qd[rs]], axis=0).astype(BF16)
                    a_s[d, c] = a[rs, rs].astype(BF16)
                else:
                    wq_s[d, c] = w[ch][rs].astype(BF16)

    n_groups = t // GROUP
    per_iter = min(PREP_GROUPS_PER_ITER, n_groups)

    def prep(i, _):
        prep_groups([i * per_iter + k for k in range(per_iter)])
        return 0

    lax.fori_loop(0, n_groups // per_iter, prep, 0)

    if with_q:
        o_ref[...] = jnp.zeros_like(o_ref)

    def scan(s, carry):
        new = []
        for d in range(2):
            st = carry[d]
            c = s if d == 0 else nc - 1 - s
            r = jnp.dot(wq_s[d, c], st.astype(BF16), preferred_element_type=F32)
            vb = (u_s[d, c] - r[:CHUNK]).astype(BF16)
            if with_q:
                r0 = pl.multiple_of(c * CHUNK, CHUNK)
                o = r[CHUNK:] + jnp.dot(a_s[d, c], vb, preferred_element_type=F32)
                o_ref[pl.ds(r0, CHUNK), :] += o
            new.append(st * gl_s[d, c][0:1, :]
                       + jnp.dot(kt_s[d, c], vb, preferred_element_type=F32))
        return tuple(new)

    if with_q:
        init = (s0_ref[0], s0_ref[1])
    else:
        zero = jnp.zeros((HEAD_DIM, HEAD_DIM), F32)
        init = (zero, zero)
    s_f, s_b = lax.fori_loop(0, nc, scan, init)

    if not with_q:
        sfin_ref[0] = s_f
        sfin_ref[1] = s_b
        return

    def finish(i, _):
        r0 = pl.multiple_of(i * rb, rb)
        o = o_ref[pl.ds(r0, rb), :]
        y = o * lax.rsqrt(jnp.mean(o * o, axis=-1, keepdims=True) + NORM_EPS)
        y_ref[pl.ds(r0, rb), :] = ((y * gn_ref[...]) * _silu(za_ref[pl.ds(r0, rb), :])).astype(y_ref.dtype)
        return 0

    lax.fori_loop(0, t // rb, finish, 0)


def _single(block_shape, index_map):
    return pl.BlockSpec(block_shape, index_map, pipeline_mode=pl.Buffered(1))


def gdn_latent(p, conv_w, grow, cosf, sinf, s0, gnorm, col_k, col_v, col_q, col_z):
    bn, t, _ = p.shape
    nc = t // CHUNK
    ngp = grow.shape[3]
    hd = HEAD_DIM
    seq = lambda col: _single((None, t, hd), lambda b, h, col=col: (b, 0, col + h))
    cw = lambda col: pl.BlockSpec((GDN_CONV, hd), lambda b, h, col=col: (0, col + h))
    return pl.pallas_call(
        functools.partial(_gdn_kernel, t=t, with_q=True),
        grid=(bn, GDN_HEADS),
        in_specs=[
            seq(col_k), seq(col_v), seq(col_q), seq(col_z),
            cw(0), cw(GDN_HEADS), cw(2 * GDN_HEADS),
            pl.BlockSpec((None, None, 4, ngp, GROUP), lambda b, h: (b, h, 0, 0, 0)),
            _single((t, hd), lambda b, h: (0, 0)),
            _single((t, hd), lambda b, h: (0, 0)),
            pl.BlockSpec((None, None, 2, hd, hd), lambda b, h: (b, h, 0, 0, 0)),
            pl.BlockSpec((1, hd), lambda b, h: (0, 0)),
        ],
        out_specs=pl.BlockSpec((None, t, hd), lambda b, h: (b, 0, h)),
        out_shape=jax.ShapeDtypeStruct((bn, t, GDN_HEADS * hd), BF16),
        scratch_shapes=[
            pltpu.VMEM((t + 2 * CONV_HALO, hd), F32),
            pltpu.VMEM((t, hd), F32),
            pltpu.VMEM((t, hd), F32),
            pltpu.VMEM((t, hd), F32),
            pltpu.VMEM((t, hd), F32),
            pltpu.VMEM((2, ngp, GROUP), F32),
            pltpu.VMEM((2, nc, 2 * CHUNK, hd), BF16),
            pltpu.VMEM((2, nc, CHUNK, CHUNK), BF16),
            pltpu.VMEM((2, nc, hd, CHUNK), BF16),
            pltpu.VMEM((2, nc, CHUNK, hd), F32),
            pltpu.VMEM((2, nc, SUBLANES, LANES), F32),
        ],
        compiler_params=_cparams(("parallel", "parallel")),
        name="gdn_latent",
    )(p, p, p, p, conv_w, conv_w, conv_w, grow, cosf, sinf, s0, gnorm.reshape(1, hd))


def gdn_context(pc, conv_w, grow, col_k, col_v):
    bn, t, _ = pc.shape
    nc = t // CHUNK
    ngp = grow.shape[3]
    hd = HEAD_DIM
    seq = lambda col: pl.BlockSpec((None, t, hd), lambda b, h, col=col: (b, 0, col + h))
    cw = lambda col: pl.BlockSpec((GDN_CONV, hd), lambda b, h, col=col: (0, col + h))
    return pl.pallas_call(
        functools.partial(_gdn_kernel, t=t, with_q=False),
        grid=(bn, GDN_HEADS),
        in_specs=[
            seq(col_k), seq(col_v), cw(0), cw(GDN_HEADS),
            pl.BlockSpec((None, None, 4, ngp, GROUP), lambda b, h: (b, h, 0, 0, 0)),
        ],
        out_specs=pl.BlockSpec((None, None, 2, hd, hd), lambda b, h: (b, h, 0, 0, 0)),
        out_shape=jax.ShapeDtypeStruct((bn, GDN_HEADS, 2, hd, hd), F32),
        scratch_shapes=[
            pltpu.VMEM((t + 2 * CONV_HALO, hd), F32),
            pltpu.VMEM((t, hd), F32),
            pltpu.VMEM((t, hd), F32),
            pltpu.VMEM((2, ngp, GROUP), F32),
            pltpu.VMEM((2, nc, CHUNK, hd), BF16),
            pltpu.VMEM((2, nc, hd, CHUNK), BF16),
            pltpu.VMEM((2, nc, CHUNK, hd), F32),
            pltpu.VMEM((2, nc, SUBLANES, LANES), F32),
        ],
        compiler_params=_cparams(("parallel", "parallel")),
        name="gdn_context",
    )(pc, pc, conv_w, conv_w, grow)


def gate_rows(gates, t):
    bn = gates.shape[0]
    ng = t // GROUP
    g = gates[:, :, :4 * GDN_HEADS].reshape(bn, ng, GROUP, 4, GDN_HEADS)
    g = jnp.transpose(g, (0, 4, 3, 1, 2))
    if ng < SUBLANES:
        g = jnp.pad(g, ((0, 0), (0, 0), (0, 0), (0, SUBLANES - ng), (0, 0)))
    return g


def _natten_kernel(q_ref, k_ref, v_ref, kc_ref, vc_ref, bias_ref, o_ref, kb_s, vb_s, *, rows):
    t = rows * GRID_W
    rb = 512

    def cast(i, _):
        r0 = pl.multiple_of(i * rb, rb)
        kb_s[pl.ds(r0, rb), :] = k_ref[pl.ds(r0, rb), :].astype(BF16)
        vb_s[pl.ds(r0, rb), :] = v_ref[pl.ds(r0, rb), :].astype(BF16)
        return 0

    lax.fori_loop(0, t // rb, cast, 0)
    kcb = kc_ref[...].astype(BF16)
    vcb = vc_ref[...].astype(BF16)
    scale = HEAD_DIM ** -0.5
    win = NA_KH * GRID_W

    def body(r, _):
        rs = jnp.clip(r - NA_KH // 2, 0, rows - NA_KH)
        q0 = pl.multiple_of(r * GRID_W, GRID_W)
        k0 = pl.multiple_of(rs * GRID_W, GRID_W)
        q = q_ref[pl.ds(q0, GRID_W), :].astype(BF16)
        kw = kb_s[pl.ds(k0, win), :]
        vw = vb_s[pl.ds(k0, win), :]
        s_loc = lax.dot_general(q, kw, NT_DIMS, preferred_element_type=F32) * scale + bias_ref[r - rs]
        s_ctx = lax.dot_general(q, kcb, NT_DIMS, preferred_element_type=F32) * scale
        m = jnp.maximum(jnp.max(s_loc, axis=-1, keepdims=True), jnp.max(s_ctx, axis=-1, keepdims=True))
        p_loc = jnp.exp(s_loc - m)
        p_ctx = jnp.exp(s_ctx - m)
        l = jnp.sum(p_loc, axis=-1, keepdims=True) + jnp.sum(p_ctx, axis=-1, keepdims=True)
        o = (jnp.dot(p_loc.astype(BF16), vw, preferred_element_type=F32)
             + jnp.dot(p_ctx.astype(BF16), vcb, preferred_element_type=F32))
        o_ref[pl.ds(q0, GRID_W), :] = (o / l).astype(o_ref.dtype)
        return 0

    lax.fori_loop(0, rows, body, 0)


def natten(p, pc, bias, col_q, col_k, col_v, ctx_col_k, ctx_col_v):
    bn, t, _ = p.shape
    lc = pc.shape[1]
    hd = HEAD_DIM
    rows = t // GRID_W
    seq = lambda col: pl.BlockSpec((None, t, hd), lambda b, h, col=col: (b, 0, col + h))
    cseq = lambda col: pl.BlockSpec((None, lc, hd), lambda b, h, col=col: (b, 0, col + h))
    return pl.pallas_call(
        functools.partial(_natten_kernel, rows=rows),
        grid=(bn, NA_HEADS),
        in_specs=[
            seq(col_q), seq(col_k), seq(col_v), cseq(ctx_col_k), cseq(ctx_col_v),
            pl.BlockSpec((None, NA_KH, GRID_W, NA_KH * GRID_W), lambda b, h: (h, 0, 0, 0)),
        ],
        out_specs=pl.BlockSpec((None, t, hd), lambda b, h: (b, 0, h)),
        out_shape=jax.ShapeDtypeStruct((bn, t, NA_HEADS * hd), BF16),
        scratch_shapes=[pltpu.VMEM((t, hd), BF16), pltpu.VMEM((t, hd), BF16)],
        compiler_params=_cparams(("parallel", "parallel")),
        name="natten",
    )(p, p, p, pc, pc, bias)


def natten_bias(rpb, rows):
    kh = min(NA_KH, rows)
    hn, n_dr, n_dc = rpb.shape
    col = np.arange(GRID_W)
    col_start = np.clip(col - NA_KW // 2, 0, GRID_W - NA_KW)
    in_win = (col[None, :] >= col_start[:, None]) & (col[None, :] < col_start[:, None] + NA_KW)
    dc = np.clip(col[None, :] - col[:, None], -(NA_KW - 1), NA_KW - 1) + (NA_KW - 1)
    onehot = (np.arange(n_dc)[:, None] == dc.reshape(-1)[None, :]).astype(np.float32)
    by_dr = jnp.dot(rpb.astype(F32).reshape(hn * n_dr, n_dc), jnp.asarray(onehot), precision=HIGHEST)
    by_dr = jnp.where(in_win[None, None], by_dr.reshape(hn, n_dr, GRID_W, GRID_W), NEG_BIG)
    b = jnp.stack([by_dr[:, NA_KH - 1 - cls:NA_KH - 1 - cls + kh] for cls in range(NA_KH)], axis=1)
    return jnp.transpose(b, (0, 1, 3, 2, 4)).reshape(hn, NA_KH, GRID_W, kh * GRID_W)


def _zmix_kernel(gb_ref, gc_ref, val_ref, w_ref, z_ref, pad_ref, *, t):
    rb = min(256, t)
    _stage_padded(pad_ref, lambda r0: gc_ref[pl.ds(r0, rb), :] * val_ref[pl.ds(r0, rb), :], t, rb)

    def body(i, _):
        r0 = pl.multiple_of(i * rb, rb)
        y = _conv_rows(pad_ref, w_ref, r0, rb, SC_CONV)
        z_ref[pl.ds(r0, rb), :] = (gb_ref[pl.ds(r0, rb), :] * y).astype(z_ref.dtype)
        return 0

    lax.fori_loop(0, t // rb, body, 0)


def zmix(p, conv_w):
    bn, t, n3 = p.shape
    d = n3 // 3
    nb = d // LANES
    seq = lambda off: pl.BlockSpec((None, t, LANES), lambda b, j, off=off: (b, 0, off + j))
    return pl.pallas_call(
        functools.partial(_zmix_kernel, t=t),
        grid=(bn, nb),
        in_specs=[seq(0), seq(nb), seq(2 * nb), pl.BlockSpec((SC_CONV, LANES), lambda b, j: (0, j))],
        out_specs=pl.BlockSpec((None, t, LANES), lambda b, j: (b, 0, j)),
        out_shape=jax.ShapeDtypeStruct((bn, t, d), BF16),
        scratch_shapes=[pltpu.VMEM((t + 2 * CONV_HALO, LANES), F32)],
        compiler_params=_cparams(("parallel", "parallel")),
        name="zmix",
    )(p, p, p, conv_w)


def _outproj_kernel(*refs, n_in):
    x_ref, gate_ref = refs[0], refs[1]
    a_refs = refs[2:2 + n_in]
    w_refs = refs[2 + n_in:2 + 2 * n_in]
    o_ref = refs[2 + 2 * n_in]
    y = None
    for a_ref, w_ref in zip(a_refs, w_refs):
        part = jnp.dot(a_ref[...], w_ref[...], preferred_element_type=F32)
        y = part if y is None else y + part
    o_ref[...] = x_ref[...] + gate_ref[...] * y


def outproj(x, gate, acts, weights, tm=512, tn=1024):
    bn, t, d = x.shape
    tn = min(tn, d)
    n_in = len(acts)
    in_specs = [pl.BlockSpec((None, tm, tn), lambda b, i, j: (b, i, j)),
                pl.BlockSpec((None, 1, tn), lambda b, i, j: (b, 0, j))]
    in_specs += [pl.BlockSpec((None, tm, a.shape[-1]), lambda b, i, j: (b, i, 0)) for a in acts]
    in_specs += [pl.BlockSpec((w.shape[0], tn), lambda b, i, j: (0, j)) for w in weights]
    return pl.pallas_call(
        functools.partial(_outproj_kernel, n_in=n_in),
        grid=(bn, t // tm, d // tn),
        in_specs=in_specs,
        out_specs=pl.BlockSpec((None, tm, tn), lambda b, i, j: (b, i, j)),
        out_shape=jax.ShapeDtypeStruct((bn, t, d), F32),
        compiler_params=_cparams(("parallel", "parallel", "parallel")),
        name="outproj",
    )(x, gate, *acts, *weights)


def _ffn_kernel(x_ref, g_ref, sc_ref, sh_ref, gate_ref, fn_ref, wg_ref, wu_ref, wd_ref, o_ref,
                hs_ref, acc_ref, *, final_norm):
    j = pl.program_id(2)

    @pl.when(j == 0)
    def _():
        hs_ref[...] = _modulated_norm(x_ref[...], g_ref[...], sc_ref[...], sh_ref[...]).astype(BF16)
        acc_ref[...] = jnp.zeros_like(acc_ref)

    h = hs_ref[...]
    gate = jnp.dot(h, wg_ref[...], preferred_element_type=F32)
    up = jnp.dot(h, wu_ref[...], preferred_element_type=F32)
    a = (_silu(gate) * up).astype(BF16)
    acc_ref[...] += jnp.dot(a, wd_ref[...], preferred_element_type=F32)

    @pl.when(j == pl.num_programs(2) - 1)
    def _():
        y = x_ref[...] + gate_ref[...] * acc_ref[...]
        if final_norm:
            y = (y * lax.rsqrt(jnp.mean(y * y, axis=-1, keepdims=True) + NORM_EPS)) * fn_ref[...]
        o_ref[...] = y


def ffn(x, g, sc, sh, gate, fnorm, w_gate, w_up, w_down, final_norm, tm=512, tf=512):
    bn, t, d = x.shape
    f = w_gate.shape[1]
    vec = pl.BlockSpec((None, 1, d), lambda b, i, j: (b, 0, 0))
    one = pl.BlockSpec((1, d), lambda b, i, j: (0, 0))
    return pl.pallas_call(
        functools.partial(_ffn_kernel, final_norm=final_norm),
        grid=(bn, t // tm, f // tf),
        in_specs=[
            pl.BlockSpec((None, tm, d), lambda b, i, j: (b, i, 0)),
            one, vec, vec, vec, one,
            pl.BlockSpec((d, tf), lambda b, i, j: (0, j)),
            pl.BlockSpec((d, tf), lambda b, i, j: (0, j)),
            pl.BlockSpec((tf, d), lambda b, i, j: (j, 0)),
        ],
        out_specs=pl.BlockSpec((None, tm, d), lambda b, i, j: (b, i, 0)),
        out_shape=jax.ShapeDtypeStruct((bn, t, d), F32),
        scratch_shapes=[pltpu.VMEM((tm, d), BF16), pltpu.VMEM((tm, d), F32)],
        compiler_params=_cparams(("parallel", "parallel", "arbitrary")),
        name="ffn",
    )(x, g.reshape(1, d), sc, sh, gate, fnorm.reshape(1, d), w_gate, w_up, w_down)


def rope_tables(t):
    pos = np.arange(t)
    row = (pos // GRID_W).astype(np.float32)
    col = (pos % GRID_W).astype(np.float32)
    n_freq = HEAD_DIM // 4
    inv_freq = jnp.asarray(ROPE_THETA, F32) ** (-jnp.arange(n_freq, dtype=F32) / n_freq)
    ang = jnp.concatenate([jnp.asarray(row)[:, None] * inv_freq, jnp.asarray(col)[:, None] * inv_freq], axis=-1)
    cos, sin = jnp.cos(ang), jnp.sin(ang)
    return jnp.concatenate([cos, cos], axis=-1), jnp.concatenate([-sin, sin], axis=-1)


def kernel(x, c, ctx, c_ctx, ada_w, ada_b, norm_mix, norm_ffn, ffn_w_gate, ffn_w_up, ffn_w_down, final_norm,
           ev_w_in, ev_conv, ev_a_log, ev_dt_bias, ev_gdn_norm, ev_rpb, ev_w_out, od_w_in, od_conv, od_w_out):
    bn, t, d = x.shape
    depth = ada_w.shape[0]
    assert depth <= 2
    gw = GDN_HEADS * HEAD_DIM
    nw = NA_HEADS * HEAD_DIM
    nh = GDN_HEADS

    cv = jnp.zeros((SUBLANES, d), F32).at[:bn].set(c).at[bn].set(c_ctx)
    mods = ada_modulation(cv, ada_w, ada_b)

    def mod_vecs(l, rows):
        m = mods[l, rows].reshape(-1, 6, d)
        return [m[:, k][:, None, :] for k in range(6)]

    x_lat = x
    for l in range(depth):
        sh1, sc1, g1, sh2, sc2, g2 = mod_vecs(l, slice(0, bn))
        if l % 2 == 0:
            e = l // 2
            csh1, csc1 = [jnp.broadcast_to(v, (bn, 1, d)) for v in mod_vecs(l, slice(bn, bn + 1))[:2]]
            w_in = ev_w_in[e]
            n_gate = 4 * nh
            seg_gate = 2 * gw + 2 * nw
            w_main = jnp.concatenate([w_in[:, :seg_gate], w_in[:, seg_gate + n_gate:]], axis=1).astype(BF16)
            w_ab = jnp.pad(w_in[:, seg_gate:seg_gate + n_gate], ((0, 0), (0, LANES - n_gate))).astype(BF16)
            gparams = jnp.zeros((SUBLANES, LANES), F32)
            gparams = gparams.at[0, :2 * nh].set(ev_a_log[e].reshape(-1)).at[1, :2 * nh].set(ev_dt_bias[e].reshape(-1))
            col_ka, col_va, col_kb, col_vb, col_qa, col_qb, col_za = [k * nh for k in range(7)]

            pc, gates_c = proj(ctx, norm_mix[l], csc1, csh1, w_main, seg_gate, w_ab, gparams, tm=ctx.shape[1])
            s_ctx = gdn_context(pc, ev_conv[e], gate_rows(gates_c, ctx.shape[1]), col_ka, col_va)

            p, gates = proj(x_lat, norm_mix[l], sc1, sh1, w_main, w_main.shape[1], w_ab, gparams)
            cosf, sinf = rope_tables(t)
            y_gdn = gdn_latent(p, ev_conv[e], gate_rows(gates, t), cosf, sinf, s_ctx, ev_gdn_norm[e],
                               col_ka, col_va, col_qa, col_za)
            y_na = natten(p, pc, natten_bias(ev_rpb[e], t // GRID_W), col_qb, col_kb, col_vb, col_kb, col_vb)
            w_out = ev_w_out[e].astype(BF16)
            x_lat = outproj(x_lat, g1, [y_gdn, y_na], [w_out[:gw], w_out[gw:]])
        else:
            j_odd = l // 2
            p = proj(x_lat, norm_mix[l], sc1, sh1, od_w_in[j_odd].astype(BF16), 3 * d)
            z = zmix(p, od_conv[j_odd])
            x_lat = outproj(x_lat, g1, [z], [od_w_out[j_odd].astype(BF16)])
        x_lat = ffn(x_lat, norm_ffn[l], sc2, sh2, g2, final_norm,
                    ffn_w_gate[l].astype(BF16), ffn_w_up[l].astype(BF16), ffn_w_down[l].astype(BF16),
                    final_norm=(l == depth - 1))
    return x_lat
```

```python
import functools
import math

import jax
import jax.numpy as jnp
import numpy as np
from jax import lax
from jax.experimental import pallas as pl
from jax.experimental.pallas import tpu as pltpu

F32 = jnp.float32
BF16 = jnp.bfloat16
HIGHEST = lax.Precision.HIGHEST

LANES = 128
SUBLANES = 8
VMEM_LIMIT = 56 * 1024 * 1024

GRID_W = 64
HEAD_DIM = 128
GDN_HEADS = 8
NA_HEADS = 8
GDN_CONV = 5
CHUNK = 64
GROUP = 2 * CHUNK
PREP_GROUPS_PER_ITER = 4
GDN_ROW_BLOCK = 512
NA_ROWS_PER_ITER = 4
NA_KH = 8
NA_KW = 16
SC_CONV = 3
ROPE_THETA = 10000.0
NORM_EPS = 1e-6
NEG_BIG = -1e30

NT_DIMS = (((1,), (1,)), ((), ()))


def _cparams(sem):
    return pltpu.CompilerParams(dimension_semantics=sem, vmem_limit_bytes=VMEM_LIMIT)


def _sigmoid(x):
    return 1.0 / (1.0 + jnp.exp(-x))


def _silu(x):
    return x * _sigmoid(x)


def _softplus(x):
    return jnp.maximum(x, 0.0) + jnp.log(1.0 + jnp.exp(-jnp.abs(x)))


def _mm(a, b):
    return jnp.dot(a.astype(BF16), b.astype(BF16), preferred_element_type=F32)


def _mm_nt(a, b):
    return lax.dot_general(a.astype(BF16), b.astype(BF16), NT_DIMS, preferred_element_type=F32)


def _mm_exact(a, b):
    return jnp.dot(a, b, precision=HIGHEST, preferred_element_type=F32)


def _ada_kernel(cv_ref, w_ref, b_ref, o_ref):
    s = _silu(cv_ref[...])
    o_ref[...] = _mm_exact(s, w_ref[...]) + b_ref[...]


def ada_modulation(cv, ada_w, ada_b, tn=1024):
    depth, d, n = ada_w.shape
    return pl.pallas_call(
        _ada_kernel,
        grid=(depth, n // tn),
        in_specs=[
            pl.BlockSpec((SUBLANES, d), lambda l, j: (0, 0)),
            pl.BlockSpec((None, d, tn), lambda l, j: (l, 0, j)),
            pl.BlockSpec((None, 1, tn), lambda l, j: (l, 0, j)),
        ],
        out_specs=pl.BlockSpec((None, SUBLANES, tn), lambda l, j: (l, 0, j)),
        out_shape=jax.ShapeDtypeStruct((depth, SUBLANES, n), F32),
        compiler_params=_cparams(("parallel", "parallel")),
        name="ada",
    )(cv, ada_w, ada_b.reshape(depth, 1, n))


def _modulated_norm(x, g, sc, sh):
    y = x * lax.rsqrt(jnp.mean(x * x, axis=-1, keepdims=True) + NORM_EPS)
    return (y * g) * (1.0 + sc) + sh


def _proj_kernel(*refs, with_gates):
    if with_gates:
        x_ref, g_ref, sc_ref, sh_ref, w_ref, wab_ref, gp_ref, o_ref, gate_ref, hs_ref = refs
    else:
        x_ref, g_ref, sc_ref, sh_ref, w_ref, o_ref, hs_ref = refs

    @pl.when(pl.program_id(2) == 0)
    def _():
        hb = _modulated_norm(x_ref[...], g_ref[...], sc_ref[...], sh_ref[...]).astype(BF16)
        hs_ref[...] = hb
        if with_gates:
            a = jnp.dot(hb, wab_ref[...], preferred_element_type=F32)
            neg_decay_rate = -jnp.exp(gp_ref[0:1, :])
            g = neg_decay_rate * _softplus(a + gp_ref[1:2, :])
            lane = lax.broadcasted_iota(jnp.int32, a.shape, 1)
            gate_ref[...] = jnp.where(lane < 2 * GDN_HEADS, g, _sigmoid(a))

    o_ref[...] = jnp.dot(hs_ref[...], w_ref[...], preferred_element_type=F32).astype(o_ref.dtype)


def proj(x, g, sc, sh, w, n_out, wab=None, gparams=None, tm=1024, tn=512):
    bn, t, d = x.shape
    tm = min(tm, t)
    with_gates = wab is not None
    vec = pl.BlockSpec((None, 1, d), lambda b, i, j: (b, 0, 0))
    in_specs = [
        pl.BlockSpec((None, tm, d), lambda b, i, j: (b, i, 0)),
        pl.BlockSpec((1, d), lambda b, i, j: (0, 0)),
        vec, vec,
        pl.BlockSpec((d, tn), lambda b, i, j: (0, j)),
    ]
    args = [x, g.reshape(1, d), sc, sh, w]
    out_specs = [pl.BlockSpec((None, tm, tn), lambda b, i, j: (b, i, j))]
    out_shape = [jax.ShapeDtypeStruct((bn, t, n_out), BF16)]
    if with_gates:
        in_specs += [pl.BlockSpec((d, LANES), lambda b, i, j: (0, 0)),
                     pl.BlockSpec((SUBLANES, LANES), lambda b, i, j: (0, 0))]
        args += [wab, gparams]
        out_specs.append(pl.BlockSpec((None, tm, LANES), lambda b, i, j: (b, i, 0)))
        out_shape.append(jax.ShapeDtypeStruct((bn, t, LANES), F32))
    res = pl.pallas_call(
        functools.partial(_proj_kernel, with_gates=with_gates),
        grid=(bn, t // tm, n_out // tn),
        in_specs=in_specs,
        out_specs=out_specs,
        out_shape=out_shape,
        scratch_shapes=[pltpu.VMEM((tm, d), BF16)],
        compiler_params=_cparams(("parallel", "parallel", "arbitrary")),
        name="proj",
    )(*args)
    return res if with_gates else res[0]


CONV_HALO = SUBLANES


def _stage_padded(pad_ref, load_rows, t, rb):
    zeros = jnp.zeros((CONV_HALO, LANES), F32)
    pad_ref[0:CONV_HALO, :] = zeros
    pad_ref[CONV_HALO + t:2 * CONV_HALO + t, :] = zeros

    def body(i, _):
        r0 = pl.multiple_of(i * rb, rb)
        pad_ref[pl.ds(CONV_HALO + r0, rb), :] = load_rows(r0)
        return 0

    lax.fori_loop(0, t // rb, body, 0)


def _conv_rows(pad_ref, w_ref, r0, rb, taps):
    acc = None
    for j in range(taps):
        xj = pad_ref[pl.ds(r0 + (CONV_HALO + j - taps // 2), rb), :]
        term = xj * w_ref[j:j + 1, :]
        acc = term if acc is None else acc + term
    return acc


def _l2norm(x):
    return x * lax.rsqrt(jnp.sum(x * x, axis=-1, keepdims=True) + NORM_EPS)


def _gdn_kernel(*refs, t, with_q):
    nc = t // CHUNK
    rb = min(GDN_ROW_BLOCK, t)
    if with_q:
        (ka_ref, va_ref, qa_ref, za_ref, wk_ref, wv_ref, wq_ref, grow_ref, cos_ref, sin_ref, s0_ref,
         gn_ref, y_ref,
         pad_ref, k_ref, v_ref, q_ref, o_ref, gc_ref, kwq_s, h_s, au_s, gl_s) = refs
    else:
        (ka_ref, va_ref, wk_ref, wv_ref, grow_ref, sfin_ref,
         pad_ref, k_ref, v_ref, gc_ref, kwq_s, h_s, gl_s) = refs

    def conv_all(src_ref, w_ref, finish, dst_ref):
        _stage_padded(pad_ref, lambda r0: src_ref[pl.ds(r0, rb), :].astype(F32), t, rb)

        def body(i, _):
            r0 = pl.multiple_of(i * rb, rb)
            y = _silu(_conv_rows(pad_ref, w_ref, r0, rb, GDN_CONV))
            dst_ref[pl.ds(r0, rb), :] = finish(y, r0)
            return 0

        lax.fori_loop(0, t // rb, body, 0)

    def rope(x, r0):
        return (x * cos_ref[pl.ds(r0, rb), :]
                + pltpu.roll(x, HEAD_DIM // 2, axis=1) * sin_ref[pl.ds(r0, rb), :])

    if with_q:
        conv_all(ka_ref, wk_ref, lambda y, r0: rope(_l2norm(y), r0), k_ref)
        conv_all(qa_ref, wq_ref, lambda y, r0: rope(_l2norm(y), r0) * HEAD_DIM ** -0.5, q_ref)
    else:
        conv_all(ka_ref, wk_ref, lambda y, r0: _l2norm(y), k_ref)
    conv_all(va_ref, wv_ref, lambda y, r0: y, v_ref)

    ii = lax.broadcasted_iota(jnp.int32, (GROUP, GROUP), 0)
    jj = lax.broadcasted_iota(jnp.int32, (GROUP, GROUP), 1)
    same = (ii // CHUNK) == (jj // CHUNK)
    lower = same & (ii >= jj)
    upper = same & (ii <= jj)
    eye = ii == jj
    gc_ref[0] = _mm_exact(grow_ref[0], upper.astype(F32))
    gc_ref[1] = _mm_exact(grow_ref[1], lower.astype(F32))

    def prep_groups(groups):
        chains = [(g, d) for g in range(len(groups)) for d in range(2)]
        kgs, vgs, qgs, kks, qks = {}, {}, {}, {}, {}
        for g in range(len(groups)):
            r0 = pl.multiple_of(groups[g] * GROUP, GROUP)
            kgs[g] = k_ref[pl.ds(r0, GROUP), :]
            vgs[g] = v_ref[pl.ds(r0, GROUP), :]
            kgb = kgs[g].astype(BF16)
            if with_q:
                qgs[g] = q_ref[pl.ds(r0, GROUP), :]
                both = _mm_nt(jnp.concatenate([kgs[g], qgs[g]], axis=0), kgb)
                kks[g], qks[g] = both[:GROUP], both[GROUP:]
            else:
                kks[g] = _mm_nt(kgb, kgb)
        gc_r, be_r, gc_c, decay, p, sq = {}, {}, {}, {}, {}, {}
        for ch in chains:
            g, d = ch
            incl = lower if d == 0 else upper
            gc_r[ch] = jnp.broadcast_to(gc_ref[d, pl.ds(groups[g], 1), :], (GROUP, GROUP))
            be_r[ch] = jnp.broadcast_to(grow_ref[2 + d, pl.ds(groups[g], 1), :], (GROUP, GROUP))
            gc_c[ch] = gc_r[ch].T
            decay[ch] = jnp.where(incl, jnp.exp(jnp.where(incl, gc_c[ch] - gc_r[ch], 0.0)), 0.0)
            m = jnp.where(incl & jnp.logical_not(eye), kks[g] * be_r[ch].T * decay[ch], 0.0)
            p[ch] = jnp.where(eye, 1.0, -m)
            sq[ch] = _mm(m, m)
        for _ in range(4):
            for ch in chains:
                p[ch] = p[ch] + _mm(p[ch], sq[ch])
                sq[ch] = _mm(sq[ch], sq[ch])
        for ch in chains:
            p[ch] = p[ch] + _mm(p[ch], sq[ch])
        uw = {}
        for ch in chains:
            g, d = ch
            rhs = jnp.concatenate([vgs[g], kgs[g] * jnp.exp(gc_c[ch])], axis=1)
            uw[ch] = _mm(p[ch] * be_r[ch], rhs)
        lhs, gls, qds = {}, {}, {}
        for ch in chains:
            g, d = ch
            last = [CHUNK - 1, GROUP - 1] if d == 0 else [0, CHUNK]
            gls[ch] = [gc_c[ch][i:i + 1, :] for i in last]
            gl_c = jnp.concatenate([jnp.broadcast_to(x, (CHUNK, LANES)) for x in gls[ch]], axis=0)
            kdt = (kgs[g] * jnp.exp(gl_c - gc_c[ch])).T
            parts = [jnp.where(jj < CHUNK, kdt, 0.0), jnp.where(jj >= CHUNK, kdt, 0.0)]
            if with_q:
                qds[ch] = qgs[g] * jnp.exp(gc_c[ch])
                parts = [qks[g] * decay[ch]] + parts
            lhs[ch] = jnp.concatenate(parts, axis=0)
        prod = {ch: _mm(lhs[ch], uw[ch]) for ch in chains}
        for ch in chains:
            g, d = ch
            r = prod[ch]
            off = GROUP if with_q else 0
            for half in range(2):
                c = 2 * groups[g] + half
                blk = r[off + half * HEAD_DIM:off + (half + 1) * HEAD_DIM]
                h_s[d, c] = blk[:, :HEAD_DIM]
                gl_s[d, c] = jnp.broadcast_to(jnp.exp(gls[ch][half]), (SUBLANES, LANES))
                if with_q:
                    rs = slice(half * CHUNK, (half + 1) * CHUNK)
                    au_s[d, c] = r[rs, :HEAD_DIM]
                    qw = qds[ch][rs] - r[rs, HEAD_DIM:]
                    kwq_s[d, c] = jnp.concatenate([blk[:, HEAD_DIM:], qw], axis=0).astype(BF16)
                else:
                    kwq_s[d, c] = blk[:, HEAD_DIM:].astype(BF16)

    n_groups = t // GROUP
    per_iter = min(PREP_GROUPS_PER_ITER, n_groups)

    def prep(i, _):
        prep_groups([i * per_iter + k for k in range(per_iter)])
        return 0

    lax.fori_loop(0, n_groups // per_iter, prep, 0)

    if with_q:
        o_ref[...] = jnp.zeros_like(o_ref)

    def scan(s, carry):
        cs = (s, nc - 1 - s)
        rr = [jnp.dot(kwq_s[d, cs[d]], carry[d].astype(BF16), preferred_element_type=F32) for d in range(2)]
        new = []
        for d in range(2):
            c = cs[d]
            new.append(carry[d] * gl_s[d, c][0:1, :] + h_s[d, c] - rr[d][:HEAD_DIM])
            if with_q:
                r0 = pl.multiple_of(c * CHUNK, CHUNK)
                o_ref[pl.ds(r0, CHUNK), :] += rr[d][HEAD_DIM:] + au_s[d, c]
        return tuple(new)

    if with_q:
        init = (s0_ref[0], s0_ref[1])
    else:
        zero = jnp.zeros((HEAD_DIM, HEAD_DIM), F32)
        init = (zero, zero)
    s_f, s_b = lax.fori_loop(0, nc, scan, init)

    if not with_q:
        sfin_ref[0] = s_f
        sfin_ref[1] = s_b
        return

    def finish(i, _):
        r0 = pl.multiple_of(i * rb, rb)
        o = o_ref[pl.ds(r0, rb), :]
        y = o * lax.rsqrt(jnp.mean(o * o, axis=-1, keepdims=True) + NORM_EPS)
        y_ref[pl.ds(r0, rb), :] = ((y * gn_ref[...]) * _silu(za_ref[pl.ds(r0, rb), :].astype(F32))).astype(y_ref.dtype)
        return 0

    lax.fori_loop(0, t // rb, finish, 0)


def _single(block_shape, index_map):
    return pl.BlockSpec(block_shape, index_map, pipeline_mode=pl.Buffered(1))


def gdn_latent(p, conv_w, grow, cosf, sinf, s0, gnorm, col_k, col_v, col_q, col_z):
    bn, t, _ = p.shape
    nc = t // CHUNK
    ngp = grow.shape[3]
    hd = HEAD_DIM
    seq = lambda col: _single((None, t, hd), lambda b, h, col=col: (b, 0, col + h))
    cw = lambda col: pl.BlockSpec((GDN_CONV, hd), lambda b, h, col=col: (0, col + h))
    return pl.pallas_call(
        functools.partial(_gdn_kernel, t=t, with_q=True),
        grid=(bn, GDN_HEADS),
        in_specs=[
            seq(col_k), seq(col_v), seq(col_q), seq(col_z),
            cw(0), cw(GDN_HEADS), cw(2 * GDN_HEADS),
            pl.BlockSpec((None, None, 4, ngp, GROUP), lambda b, h: (b, h, 0, 0, 0)),
            _single((t, hd), lambda b, h: (0, 0)),
            _single((t, hd), lambda b, h: (0, 0)),
            pl.BlockSpec((None, None, 2, hd, hd), lambda b, h: (b, h, 0, 0, 0)),
            pl.BlockSpec((1, hd), lambda b, h: (0, 0)),
        ],
        out_specs=pl.BlockSpec((None, t, hd), lambda b, h: (b, 0, h)),
        out_shape=jax.ShapeDtypeStruct((bn, t, GDN_HEADS * hd), BF16),
        scratch_shapes=[
            pltpu.VMEM((t + 2 * CONV_HALO, hd), F32),
            pltpu.VMEM((t, hd), F32),
            pltpu.VMEM((t, hd), F32),
            pltpu.VMEM((t, hd), F32),
            pltpu.VMEM((t, hd), F32),
            pltpu.VMEM((2, ngp, GROUP), F32),
            pltpu.VMEM((2, nc, hd + CHUNK, hd), BF16),
            pltpu.VMEM((2, nc, hd, hd), F32),
            pltpu.VMEM((2, nc, CHUNK, hd), F32),
            pltpu.VMEM((2, nc, SUBLANES, LANES), F32),
        ],
        compiler_params=_cparams(("parallel", "parallel")),
        name="gdn_latent",
    )(p, p, p, p, conv_w, conv_w, conv_w, grow, cosf, sinf, s0, gnorm.reshape(1, hd))


def gdn_context(pc, conv_w, grow, col_k, col_v):
    bn, t, _ = pc.shape
    nc = t // CHUNK
    ngp = grow.shape[3]
    hd = HEAD_DIM
    seq = lambda col: pl.BlockSpec((None, t, hd), lambda b, h, col=col: (b, 0, col + h))
    cw = lambda col: pl.BlockSpec((GDN_CONV, hd), lambda b, h, col=col: (0, col + h))
    return pl.pallas_call(
        functools.partial(_gdn_kernel, t=t, with_q=False),
        grid=(bn, GDN_HEADS),
        in_specs=[
            seq(col_k), seq(col_v), cw(0), cw(GDN_HEADS),
            pl.BlockSpec((None, None, 4, ngp, GROUP), lambda b, h: (b, h, 0, 0, 0)),
        ],
        out_specs=pl.BlockSpec((None, None, 2, hd, hd), lambda b, h: (b, h, 0, 0, 0)),
        out_shape=jax.ShapeDtypeStruct((bn, GDN_HEADS, 2, hd, hd), F32),
        scratch_shapes=[
            pltpu.VMEM((t + 2 * CONV_HALO, hd), F32),
            pltpu.VMEM((t, hd), F32),
            pltpu.VMEM((t, hd), F32),
            pltpu.VMEM((2, ngp, GROUP), F32),
            pltpu.VMEM((2, nc, hd, hd), BF16),
            pltpu.VMEM((2, nc, hd, hd), F32),
            pltpu.VMEM((2, nc, SUBLANES, LANES), F32),
        ],
        compiler_params=_cparams(("parallel", "parallel")),
        name="gdn_context",
    )(pc, pc, conv_w, conv_w, grow)


def gate_rows(gates, t):
    bn = gates.shape[0]
    ng = t // GROUP
    g = gates[:, :, :4 * GDN_HEADS].reshape(bn, ng, GROUP, 4, GDN_HEADS)
    g = jnp.transpose(g, (0, 4, 3, 1, 2))
    if ng < SUBLANES:
        g = jnp.pad(g, ((0, 0), (0, 0), (0, 0), (0, SUBLANES - ng), (0, 0)))
    return g


def _natten_kernel(q_ref, k_ref, v_ref, kc_ref, vc_ref, bias_ref, o_ref, *, rows):
    kcb = kc_ref[...]
    vcb = vc_ref[...]
    scale = HEAD_DIM ** -0.5
    win = NA_KH * GRID_W
    per_iter = min(NA_ROWS_PER_ITER, rows)

    def body(it, _):
        rws = [it * per_iter + k for k in range(per_iter)]
        q0s, k0s, s_loc, s_ctx = [], [], [], []
        for r in rws:
            rs = jnp.clip(r - NA_KH // 2, 0, rows - NA_KH)
            q0s.append(pl.multiple_of(r * GRID_W, GRID_W))
            k0s.append(pl.multiple_of(rs * GRID_W, GRID_W))
            q = q_ref[pl.ds(q0s[-1], GRID_W), :]
            kw = k_ref[pl.ds(k0s[-1], win), :]
            s_loc.append(lax.dot_general(q, kw, NT_DIMS, preferred_element_type=F32) * scale + bias_ref[r - rs])
            s_ctx.append(lax.dot_general(q, kcb, NT_DIMS, preferred_element_type=F32) * scale)
        p_loc, p_ctx, den = [], [], []
        for a, b in zip(s_loc, s_ctx):
            m = jnp.maximum(jnp.max(a, axis=-1, keepdims=True), jnp.max(b, axis=-1, keepdims=True))
            p_loc.append(jnp.exp(a - m))
            p_ctx.append(jnp.exp(b - m))
            den.append(jnp.sum(p_loc[-1], axis=-1, keepdims=True) + jnp.sum(p_ctx[-1], axis=-1, keepdims=True))
        outs = []
        for k in range(per_iter):
            vw = v_ref[pl.ds(k0s[k], win), :]
            outs.append(jnp.dot(p_loc[k].astype(BF16), vw, preferred_element_type=F32)
                        + jnp.dot(p_ctx[k].astype(BF16), vcb, preferred_element_type=F32))
        for k in range(per_iter):
            o_ref[pl.ds(q0s[k], GRID_W), :] = (outs[k] / den[k]).astype(o_ref.dtype)
        return 0

    lax.fori_loop(0, rows // per_iter, body, 0)


def natten(p, pc, bias, col_q, col_k, col_v, ctx_col_k, ctx_col_v):
    bn, t, _ = p.shape
    lc = pc.shape[1]
    hd = HEAD_DIM
    rows = t // GRID_W
    seq = lambda col: pl.BlockSpec((None, t, hd), lambda b, h, col=col: (b, 0, col + h))
    cseq = lambda col: pl.BlockSpec((None, lc, hd), lambda b, h, col=col: (b, 0, col + h))
    return pl.pallas_call(
        functools.partial(_natten_kernel, rows=rows),
        grid=(bn, NA_HEADS),
        in_specs=[
            seq(col_q), seq(col_k), seq(col_v), cseq(ctx_col_k), cseq(ctx_col_v),
            pl.BlockSpec((None, NA_KH, GRID_W, NA_KH * GRID_W), lambda b, h: (h, 0, 0, 0)),
        ],
        out_specs=pl.BlockSpec((None, t, hd), lambda b, h: (b, 0, h)),
        out_shape=jax.ShapeDtypeStruct((bn, t, NA_HEADS * hd), BF16),
        compiler_params=_cparams(("parallel", "parallel")),
        name="natten",
    )(p, p, p, pc, pc, bias)


def natten_bias(rpb, rows):
    kh = min(NA_KH, rows)
    hn, n_dr, n_dc = rpb.shape
    col = np.arange(GRID_W)
    col_start = np.clip(col - NA_KW // 2, 0, GRID_W - NA_KW)
    in_win = (col[None, :] >= col_start[:, None]) & (col[None, :] < col_start[:, None] + NA_KW)
    dc = np.clip(col[None, :] - col[:, None], -(NA_KW - 1), NA_KW - 1) + (NA_KW - 1)
    onehot = (np.arange(n_dc)[:, None] == dc.reshape(-1)[None, :]).astype(np.float32)
    by_dr = jnp.dot(rpb.astype(F32).reshape(hn * n_dr, n_dc), jnp.asarray(onehot), precision=HIGHEST)
    by_dr = jnp.where(in_win[None, None], by_dr.reshape(hn, n_dr, GRID_W, GRID_W), NEG_BIG)
    b = jnp.stack([by_dr[:, NA_KH - 1 - cls:NA_KH - 1 - cls + kh] for cls in range(NA_KH)], axis=1)
    return jnp.transpose(b, (0, 1, 3, 2, 4)).reshape(hn, NA_KH, GRID_W, kh * GRID_W)


def _zmix_kernel(gb_ref, gc_ref, val_ref, w_ref, z_ref, pad_ref, *, t):
    rb = min(256, t)
    _stage_padded(pad_ref, lambda r0: gc_ref[pl.ds(r0, rb), :].astype(F32) * val_ref[pl.ds(r0, rb), :].astype(F32),
                  t, rb)

    def body(i, _):
        r0 = pl.multiple_of(i * rb, rb)
        y = _conv_rows(pad_ref, w_ref, r0, rb, SC_CONV)
        z_ref[pl.ds(r0, rb), :] = (gb_ref[pl.ds(r0, rb), :].astype(F32) * y).astype(z_ref.dtype)
        return 0

    lax.fori_loop(0, t // rb, body, 0)


def zmix(p, conv_w):
    bn, t, n3 = p.shape
    d = n3 // 3
    nb = d // LANES
    seq = lambda off: pl.BlockSpec((None, t, LANES), lambda b, j, off=off: (b, 0, off + j))
    return pl.pallas_call(
        functools.partial(_zmix_kernel, t=t),
        grid=(bn, nb),
        in_specs=[seq(0), seq(nb), seq(2 * nb), pl.BlockSpec((SC_CONV, LANES), lambda b, j: (0, j))],
        out_specs=pl.BlockSpec((None, t, LANES), lambda b, j: (b, 0, j)),
        out_shape=jax.ShapeDtypeStruct((bn, t, d), BF16),
        scratch_shapes=[pltpu.VMEM((t + 2 * CONV_HALO, LANES), F32)],
        compiler_params=_cparams(("parallel", "parallel")),
        name="zmix",
    )(p, p, p, conv_w)


def _outproj_kernel(*refs, n_in):
    x_ref, gate_ref = refs[0], refs[1]
    a_refs = refs[2:2 + n_in]
    w_refs = refs[2 + n_in:2 + 2 * n_in]
    o_ref = refs[2 + 2 * n_in]
    y = None
    for a_ref, w_ref in zip(a_refs, w_refs):
        part = jnp.dot(a_ref[...], w_ref[...], preferred_element_type=F32)
        y = part if y is None else y + part
    o_ref[...] = x_ref[...] + gate_ref[...] * y


def outproj(x, gate, acts, weights, tm=512, tn=2048):
    bn, t, d = x.shape
    tn = min(tn, d)
    n_in = len(acts)
    in_specs = [pl.BlockSpec((None, tm, tn), lambda b, i, j: (b, i, j)),
                pl.BlockSpec((None, 1, tn), lambda b, i, j: (b, 0, j))]
    in_specs += [pl.BlockSpec((None, tm, a.shape[-1]), lambda b, i, j: (b, i, 0)) for a in acts]
    in_specs += [pl.BlockSpec((w.shape[0], tn), lambda b, i, j: (0, j)) for w in weights]
    return pl.pallas_call(
        functools.partial(_outproj_kernel, n_in=n_in),
        grid=(bn, t // tm, d // tn),
        in_specs=in_specs,
        out_specs=pl.BlockSpec((None, tm, tn), lambda b, i, j: (b, i, j)),
        out_shape=jax.ShapeDtypeStruct((bn, t, d), F32),
        compiler_params=_cparams(("parallel", "parallel", "parallel")),
        name="outproj",
    )(x, gate, *acts, *weights)


def _ffn_kernel(x_ref, g_ref, sc_ref, sh_ref, gate_ref, fn_ref, wg_ref, wu_ref, wd_ref, o_ref,
                hs_ref, acc_ref, *, final_norm):
    j = pl.program_id(2)

    @pl.when(j == 0)
    def _():
        hs_ref[...] = _modulated_norm(x_ref[...], g_ref[...], sc_ref[...], sh_ref[...]).astype(BF16)
        acc_ref[...] = jnp.zeros_like(acc_ref)

    h = hs_ref[...]
    gate = jnp.dot(h, wg_ref[...], preferred_element_type=F32)
    up = jnp.dot(h, wu_ref[...], preferred_element_type=F32)
    a = (_silu(gate) * up).astype(BF16)
    acc_ref[...] += jnp.dot(a, wd_ref[...], preferred_element_type=F32)

    @pl.when(j == pl.num_programs(2) - 1)
    def _():
        y = x_ref[...] + gate_ref[...] * acc_ref[...]
        if final_norm:
            y = (y * lax.rsqrt(jnp.mean(y * y, axis=-1, keepdims=True) + NORM_EPS)) * fn_ref[...]
        o_ref[...] = y


def ffn(x, g, sc, sh, gate, fnorm, w_gate, w_up, w_down, final_norm, tm=512, tf=512):
    bn, t, d = x.shape
    f = w_gate.shape[1]
    vec = pl.BlockSpec((None, 1, d), lambda b, i, j: (b, 0, 0))
    one = pl.BlockSpec((1, d), lambda b, i, j: (0, 0))
    return pl.pallas_call(
        functools.partial(_ffn_kernel, final_norm=final_norm),
        grid=(bn, t // tm, f // tf),
        in_specs=[
            pl.BlockSpec((None, tm, d), lambda b, i, j: (b, i, 0)),
            one, vec, vec, vec, one,
            pl.BlockSpec((d, tf), lambda b, i, j: (0, j)),
            pl.BlockSpec((d, tf), lambda b, i, j: (0, j)),
            pl.BlockSpec((tf, d), lambda b, i, j: (j, 0)),
        ],
        out_specs=pl.BlockSpec((None, tm, d), lambda b, i, j: (b, i, 0)),
        out_shape=jax.ShapeDtypeStruct((bn, t, d), F32),
        scratch_shapes=[pltpu.VMEM((tm, d), BF16), pltpu.VMEM((tm, d), F32)],
        compiler_params=_cparams(("parallel", "parallel", "arbitrary")),
        name="ffn",
    )(x, g.reshape(1, d), sc, sh, gate, fnorm.reshape(1, d), w_gate, w_up, w_down)


def rope_tables(t):
    pos = np.arange(t)
    row = (pos // GRID_W).astype(np.float32)
    col = (pos % GRID_W).astype(np.float32)
    n_freq = HEAD_DIM // 4
    inv_freq = jnp.asarray(ROPE_THETA, F32) ** (-jnp.arange(n_freq, dtype=F32) / n_freq)
    ang = jnp.concatenate([jnp.asarray(row)[:, None] * inv_freq, jnp.asarray(col)[:, None] * inv_freq], axis=-1)
    cos, sin = jnp.cos(ang), jnp.sin(ang)
    return jnp.concatenate([cos, cos], axis=-1), jnp.concatenate([-sin, sin], axis=-1)


def kernel(x, c, ctx, c_ctx, ada_w, ada_b, norm_mix, norm_ffn, ffn_w_gate, ffn_w_up, ffn_w_down, final_norm,
           ev_w_in, ev_conv, ev_a_log, ev_dt_bias, ev_gdn_norm, ev_rpb, ev_w_out, od_w_in, od_conv, od_w_out):
    bn, t, d = x.shape
    depth = ada_w.shape[0]
    assert depth <= 2
    gw = GDN_HEADS * HEAD_DIM
    nw = NA_HEADS * HEAD_DIM
    nh = GDN_HEADS

    cv = jnp.zeros((SUBLANES, d), F32).at[:bn].set(c).at[bn].set(c_ctx)
    mods = ada_modulation(cv, ada_w, ada_b)

    def mod_vecs(l, rows):
        m = mods[l, rows].reshape(-1, 6, d)
        return [m[:, k][:, None, :] for k in range(6)]

    x_lat = x
    for l in range(depth):
        sh1, sc1, g1, sh2, sc2, g2 = mod_vecs(l, slice(0, bn))
        if l % 2 == 0:
            e = l // 2
            csh1, csc1 = [jnp.broadcast_to(v, (bn, 1, d)) for v in mod_vecs(l, slice(bn, bn + 1))[:2]]
            w_in = ev_w_in[e]
            n_gate = 4 * nh
            seg_gate = 2 * gw + 2 * nw
            w_main = jnp.concatenate([w_in[:, :seg_gate], w_in[:, seg_gate + n_gate:]], axis=1).astype(BF16)
            w_ab = jnp.pad(w_in[:, seg_gate:seg_gate + n_gate], ((0, 0), (0, LANES - n_gate))).astype(BF16)
            gparams = jnp.zeros((SUBLANES, LANES), F32)
            gparams = gparams.at[0, :2 * nh].set(ev_a_log[e].reshape(-1)).at[1, :2 * nh].set(ev_dt_bias[e].reshape(-1))
            col_ka, col_va, col_kb, col_vb, col_qa, col_qb, col_za = [k * nh for k in range(7)]

            pc, gates_c = proj(ctx, norm_mix[l], csc1, csh1, w_main, seg_gate, w_ab, gparams, tm=ctx.shape[1])
            s_ctx = gdn_context(pc, ev_conv[e], gate_rows(gates_c, ctx.shape[1]), col_ka, col_va)

            p, gates = proj(x_lat, norm_mix[l], sc1, sh1, w_main, w_main.shape[1], w_ab, gparams)
            cosf, sinf = rope_tables(t)
            y_gdn = gdn_latent(p, ev_conv[e], gate_rows(gates, t), cosf, sinf, s_ctx, ev_gdn_norm[e],
                               col_ka, col_va, col_qa, col_za)
            y_na = natten(p, pc, natten_bias(ev_rpb[e], t // GRID_W), col_qb, col_kb, col_vb, col_kb, col_vb)
            w_out = ev_w_out[e].astype(BF16)
            x_lat = outproj(x_lat, g1, [y_gdn, y_na], [w_out[:gw], w_out[gw:]])
        else:
            j_odd = l // 2
            p = proj(x_lat, norm_mix[l], sc1, sh1, od_w_in[j_odd].astype(BF16), 3 * d)
            z = zmix(p, od_conv[j_odd])
            x_lat = outproj(x_lat, g1, [z], [od_w_out[j_odd].astype(BF16)])
        x_lat = ffn(x_lat, norm_ffn[l], sc2, sh2, g2, final_norm,
                    ffn_w_gate[l].astype(BF16), ffn_w_up[l].astype(BF16), ffn_w_down[l].astype(BF16),
                    final_norm=(l == depth - 1))
    return x_lat
```

```python
import functools
import math

import jax
import jax.numpy as jnp
import numpy as np
from jax import lax
from jax.experimental import pallas as pl
from jax.experimental.pallas import tpu as pltpu

F32 = jnp.float32
BF16 = jnp.bfloat16
HIGHEST = lax.Precision.HIGHEST

LANES = 128
SUBLANES = 8
VMEM_LIMIT = 56 * 1024 * 1024

GRID_W = 64
HEAD_DIM = 128
GDN_HEADS = 8
NA_HEADS = 8
GDN_CONV = 5
CHUNK = 64
GROUP = 2 * CHUNK
PREP_GROUPS_PER_ITER = 4
GDN_ROW_BLOCK = 512
NA_ROWS_PER_ITER = 4
NA_KH = 8
NA_KW = 16
SC_CONV = 3
ROPE_THETA = 10000.0
NORM_EPS = 1e-6
NEG_BIG = -1e30

NT_DIMS = (((1,), (1,)), ((), ()))


def _cparams(sem):
    return pltpu.CompilerParams(dimension_semantics=sem, vmem_limit_bytes=VMEM_LIMIT)


def _sigmoid(x):
    return 1.0 / (1.0 + jnp.exp(-x))


def _silu(x):
    return x * _sigmoid(x)


def _softplus(x):
    return jnp.maximum(x, 0.0) + jnp.log(1.0 + jnp.exp(-jnp.abs(x)))


def _mm(a, b):
    return jnp.dot(a.astype(BF16), b.astype(BF16), preferred_element_type=F32)


def _mm_nt(a, b):
    return lax.dot_general(a.astype(BF16), b.astype(BF16), NT_DIMS, preferred_element_type=F32)


def _mm_exact(a, b):
    return jnp.dot(a, b, precision=HIGHEST, preferred_element_type=F32)


def _ada_kernel(cv_ref, w_ref, b_ref, o_ref):
    s = _silu(cv_ref[...])
    o_ref[...] = _mm(s, w_ref[...]) + b_ref[...]


def ada_modulation(cv, ada_w, ada_b, tn=1024):
    depth, d, n = ada_w.shape
    return pl.pallas_call(
        _ada_kernel,
        grid=(depth, n // tn),
        in_specs=[
            pl.BlockSpec((SUBLANES, d), lambda l, j: (0, 0)),
            pl.BlockSpec((None, d, tn), lambda l, j: (l, 0, j)),
            pl.BlockSpec((None, 1, tn), lambda l, j: (l, 0, j)),
        ],
        out_specs=pl.BlockSpec((None, SUBLANES, tn), lambda l, j: (l, 0, j)),
        out_shape=jax.ShapeDtypeStruct((depth, SUBLANES, n), F32),
        compiler_params=_cparams(("parallel", "parallel")),
        name="ada",
    )(cv, ada_w, ada_b.reshape(depth, 1, n))


def _modulated_norm(x, g, sc, sh):
    y = x * lax.rsqrt(jnp.mean(x * x, axis=-1, keepdims=True) + NORM_EPS)
    return (y * g) * (1.0 + sc) + sh


NORM_ROW_BLOCK = 128


def _modulated_norm_rows(x_ref, g_ref, sc_ref, sh_ref, hs_ref):
    rb = min(NORM_ROW_BLOCK, x_ref.shape[0])

    def body(i, _):
        rows = pl.ds(pl.multiple_of(i * rb, rb), rb)
        hs_ref[rows, :] = _modulated_norm(x_ref[rows, :], g_ref[...], sc_ref[...], sh_ref[...]).astype(BF16)
        return 0

    lax.fori_loop(0, x_ref.shape[0] // rb, body, 0)


def _proj_kernel(*refs, with_gates, with_tail, head_blocks):
    x_ref, g_ref, sc_ref, sh_ref, w_ref = refs[:5]
    refs = refs[5:]
    wt_ref = None
    if with_tail:
        wt_ref, refs = refs[0], refs[1:]
    if with_gates:
        wab_ref, gp_ref, o_ref, gate_ref, hs_ref, wb_ref = refs
    else:
        o_ref, hs_ref, wb_ref = refs
    j = pl.program_id(2)

    @pl.when(j == 0)
    def _():
        if with_gates:
            hb = _modulated_norm(x_ref[...], g_ref[...], sc_ref[...], sh_ref[...]).astype(BF16)
            hs_ref[...] = hb
            a = jnp.dot(hb, wab_ref[...], preferred_element_type=F32)
            neg_decay_rate = -jnp.exp(gp_ref[0:1, :])
            g = neg_decay_rate * _softplus(a + gp_ref[1:2, :])
            lane = lax.broadcasted_iota(jnp.int32, a.shape, 1)
            gate_ref[...] = jnp.where(lane < 2 * GDN_HEADS, g, _sigmoid(a))
        else:
            _modulated_norm_rows(x_ref, g_ref, sc_ref, sh_ref, hs_ref)

    def emit(wref):
        wb_ref[...] = wref[...].astype(BF16)
        o_ref[...] = jnp.dot(hs_ref[...], wb_ref[...], preferred_element_type=F32).astype(o_ref.dtype)

    pl.when(j < head_blocks)(lambda: emit(w_ref))
    if with_tail:
        pl.when(j >= head_blocks)(lambda: emit(wt_ref))


def proj(x, g, sc, sh, w, layer, head_cols, w_tail, n_out, wab=None, gparams=None, tm=1024, tn=512):
    bn, t, d = x.shape
    tm = min(tm, t)
    head_blocks = head_cols // tn
    with_gates = wab is not None
    vec = pl.BlockSpec((None, 1, d), lambda b, i, j: (b, 0, 0))
    in_specs = [
        pl.BlockSpec((None, tm, d), lambda b, i, j: (b, i, 0)),
        pl.BlockSpec((1, d), lambda b, i, j: (0, 0)),
        vec, vec,
        pl.BlockSpec((None, d, tn), lambda b, i, j: (layer, 0, jnp.minimum(j, head_blocks - 1))),
    ]
    args = [x, g.reshape(1, d), sc, sh, w]
    with_tail = w_tail is not None
    if with_tail:
        in_specs.append(pl.BlockSpec((d, tn), lambda b, i, j: (0, jnp.maximum(j - head_blocks, 0))))
        args.append(w_tail)
    out_specs = [pl.BlockSpec((None, tm, tn), lambda b, i, j: (b, i, j))]
    out_shape = [jax.ShapeDtypeStruct((bn, t, n_out), BF16)]
    if with_gates:
        in_specs += [pl.BlockSpec((d, LANES), lambda b, i, j: (0, 0)),
                     pl.BlockSpec((SUBLANES, LANES), lambda b, i, j: (0, 0))]
        args += [wab, gparams]
        out_specs.append(pl.BlockSpec((None, tm, LANES), lambda b, i, j: (b, i, 0)))
        out_shape.append(jax.ShapeDtypeStruct((bn, t, LANES), F32))
    res = pl.pallas_call(
        functools.partial(_proj_kernel, with_gates=with_gates, with_tail=with_tail, head_blocks=head_blocks),
        grid=(bn, t // tm, n_out // tn),
        in_specs=in_specs,
        out_specs=out_specs,
        out_shape=out_shape,
        scratch_shapes=[pltpu.VMEM((tm, d), BF16), pltpu.VMEM((d, tn), BF16)],
        compiler_params=_cparams(("parallel", "parallel", "arbitrary")),
        name="proj",
    )(*args)
    return res if with_gates else res[0]


CONV_HALO = SUBLANES


def _stage_padded(pad_ref, load_rows, t, rb):
    zeros = jnp.zeros((CONV_HALO, LANES), F32)
    pad_ref[0:CONV_HALO, :] = zeros
    pad_ref[CONV_HALO + t:2 * CONV_HALO + t, :] = zeros

    def body(i, _):
        r0 = pl.multiple_of(i * rb, rb)
        pad_ref[pl.ds(CONV_HALO + r0, rb), :] = load_rows(r0)
        return 0

    lax.fori_loop(0, t // rb, body, 0)


def _conv_rows(pad_ref, w_ref, r0, rb, taps):
    acc = None
    for j in range(taps):
        xj = pad_ref[pl.ds(r0 + (CONV_HALO + j - taps // 2), rb), :]
        term = xj * w_ref[j:j + 1, :]
        acc = term if acc is None else acc + term
    return acc


def _l2norm(x):
    return x * lax.rsqrt(jnp.sum(x * x, axis=-1, keepdims=True) + NORM_EPS)


def _gdn_kernel(*refs, t, with_q):
    nc = t // CHUNK
    rb = min(GDN_ROW_BLOCK, t)
    if with_q:
        (ka_ref, va_ref, qa_ref, za_ref, wk_ref, wv_ref, wq_ref, grow_ref, cos_ref, sin_ref, s0_ref,
         gn_ref, y_ref,
         pad_ref, k_ref, v_ref, q_ref, o_ref, gc_ref, kwq_s, h_s, au_s, gl_s) = refs
    else:
        (ka_ref, va_ref, wk_ref, wv_ref, grow_ref, sfin_ref,
         pad_ref, k_ref, v_ref, gc_ref, kwq_s, h_s, gl_s) = refs

    def conv_all(src_ref, w_ref, finish, dst_ref):
        _stage_padded(pad_ref, lambda r0: src_ref[pl.ds(r0, rb), :].astype(F32), t, rb)

        def body(i, _):
            r0 = pl.multiple_of(i * rb, rb)
            y = _silu(_conv_rows(pad_ref, w_ref, r0, rb, GDN_CONV))
            dst_ref[pl.ds(r0, rb), :] = finish(y, r0)
            return 0

        lax.fori_loop(0, t // rb, body, 0)

    def rope(x, r0):
        return (x * cos_ref[pl.ds(r0, rb), :]
                + pltpu.roll(x, HEAD_DIM // 2, axis=1) * sin_ref[pl.ds(r0, rb), :])

    if with_q:
        conv_all(ka_ref, wk_ref, lambda y, r0: rope(_l2norm(y), r0), k_ref)
        conv_all(qa_ref, wq_ref, lambda y, r0: rope(_l2norm(y), r0) * HEAD_DIM ** -0.5, q_ref)
    else:
        conv_all(ka_ref, wk_ref, lambda y, r0: _l2norm(y), k_ref)
    conv_all(va_ref, wv_ref, lambda y, r0: y, v_ref)

    ii = lax.broadcasted_iota(jnp.int32, (GROUP, GROUP), 0)
    jj = lax.broadcasted_iota(jnp.int32, (GROUP, GROUP), 1)
    same = (ii // CHUNK) == (jj // CHUNK)
    lower = same & (ii >= jj)
    upper = same & (ii <= jj)
    eye = ii == jj
    gc_ref[0] = _mm_exact(grow_ref[0], upper.astype(F32))
    gc_ref[1] = _mm_exact(grow_ref[1], lower.astype(F32))

    def prep_groups(groups):
        chains = [(g, d) for g in range(len(groups)) for d in range(2)]
        kgs, vgs, qgs, kks, qks = {}, {}, {}, {}, {}
        for g in range(len(groups)):
            r0 = pl.multiple_of(groups[g] * GROUP, GROUP)
            kgs[g] = k_ref[pl.ds(r0, GROUP), :]
            vgs[g] = v_ref[pl.ds(r0, GROUP), :]
            kgb = kgs[g].astype(BF16)
            if with_q:
                qgs[g] = q_ref[pl.ds(r0, GROUP), :]
                both = _mm_nt(jnp.concatenate([kgs[g], qgs[g]], axis=0), kgb)
                kks[g], qks[g] = both[:GROUP], both[GROUP:]
            else:
                kks[g] = _mm_nt(kgb, kgb)
        gc_r, be_r, gc_c, decay, p, sq = {}, {}, {}, {}, {}, {}
        for ch in chains:
            g, d = ch
            incl = lower if d == 0 else upper
            gc_r[ch] = jnp.broadcast_to(gc_ref[d, pl.ds(groups[g], 1), :], (GROUP, GROUP))
            be_r[ch] = jnp.broadcast_to(grow_ref[2 + d, pl.ds(groups[g], 1), :], (GROUP, GROUP))
            gc_c[ch] = gc_r[ch].T
            decay[ch] = jnp.where(incl, jnp.exp(jnp.where(incl, gc_c[ch] - gc_r[ch], 0.0)), 0.0)
            m = jnp.where(incl & jnp.logical_not(eye), kks[g] * be_r[ch].T * decay[ch], 0.0)
            p[ch] = jnp.where(eye, 1.0, -m)
            sq[ch] = _mm(m, m)
        for _ in range(4):
            for ch in chains:
                p[ch] = p[ch] + _mm(p[ch], sq[ch])
                sq[ch] = _mm(sq[ch], sq[ch])
        for ch in chains:
            p[ch] = p[ch] + _mm(p[ch], sq[ch])
        uw = {}
        for ch in chains:
            g, d = ch
            rhs = jnp.concatenate([vgs[g], kgs[g] * jnp.exp(gc_c[ch])], axis=1)
            uw[ch] = _mm(p[ch] * be_r[ch], rhs)
        lhs, gls, qds = {}, {}, {}
        for ch in chains:
            g, d = ch
            last = [CHUNK - 1, GROUP - 1] if d == 0 else [0, CHUNK]
            gls[ch] = [gc_c[ch][i:i + 1, :] for i in last]
            gl_c = jnp.concatenate([jnp.broadcast_to(x, (CHUNK, LANES)) for x in gls[ch]], axis=0)
            kdt = (kgs[g] * jnp.exp(gl_c - gc_c[ch])).T
            parts = [jnp.where(jj < CHUNK, kdt, 0.0), jnp.where(jj >= CHUNK, kdt, 0.0)]
            if with_q:
                qds[ch] = qgs[g] * jnp.exp(gc_c[ch])
                parts = [qks[g] * decay[ch]] + parts
            lhs[ch] = jnp.concatenate(parts, axis=0)
        prod = {ch: _mm(lhs[ch], uw[ch]) for ch in chains}
        for ch in chains:
            g, d = ch
            r = prod[ch]
            off = GROUP if with_q else 0
            for half in range(2):
                c = 2 * groups[g] + half
                blk = r[off + half * HEAD_DIM:off + (half + 1) * HEAD_DIM]
                h_s[d, c] = blk[:, :HEAD_DIM]
                gl_s[d, c] = jnp.broadcast_to(jnp.exp(gls[ch][half]), (SUBLANES, LANES))
                if with_q:
                    rs = slice(half * CHUNK, (half + 1) * CHUNK)
                    au_s[d, c] = r[rs, :HEAD_DIM]
                    qw = qds[ch][rs] - r[rs, HEAD_DIM:]
                    kwq_s[d, c] = jnp.concatenate([blk[:, HEAD_DIM:], qw], axis=0).astype(BF16)
                else:
                    kwq_s[d, c] = blk[:, HEAD_DIM:].astype(BF16)

    n_groups = t // GROUP
    per_iter = min(PREP_GROUPS_PER_ITER, n_groups)

    def prep(i, _):
        prep_groups([i * per_iter + k for k in range(per_iter)])
        return 0

    lax.fori_loop(0, n_groups // per_iter, prep, 0)

    if with_q:
        o_ref[...] = jnp.zeros_like(o_ref)

    def scan(s, carry):
        cs = (s, nc - 1 - s)
        rr = [jnp.dot(kwq_s[d, cs[d]], carry[d].astype(BF16), preferred_element_type=F32) for d in range(2)]
        new = []
        for d in range(2):
            c = cs[d]
            new.append(carry[d] * gl_s[d, c][0:1, :] + h_s[d, c] - rr[d][:HEAD_DIM])
            if with_q:
                r0 = pl.multiple_of(c * CHUNK, CHUNK)
                o_ref[pl.ds(r0, CHUNK), :] += rr[d][HEAD_DIM:] + au_s[d, c]
        return tuple(new)

    if with_q:
        init = (s0_ref[0], s0_ref[1])
    else:
        zero = jnp.zeros((HEAD_DIM, HEAD_DIM), F32)
        init = (zero, zero)
    s_f, s_b = lax.fori_loop(0, nc, scan, init)

    if not with_q:
        sfin_ref[0] = s_f
        sfin_ref[1] = s_b
        return

    def finish(i, _):
        r0 = pl.multiple_of(i * rb, rb)
        o = o_ref[pl.ds(r0, rb), :]
        y = o * lax.rsqrt(jnp.mean(o * o, axis=-1, keepdims=True) + NORM_EPS)
        y_ref[pl.ds(r0, rb), :] = ((y * gn_ref[...]) * _silu(za_ref[pl.ds(r0, rb), :].astype(F32))).astype(y_ref.dtype)
        return 0

    lax.fori_loop(0, t // rb, finish, 0)


def _single(block_shape, index_map):
    return pl.BlockSpec(block_shape, index_map, pipeline_mode=pl.Buffered(1))


def gdn_latent(p, conv_w, grow, cosf, sinf, s0, gnorm, col_k, col_v, col_q, col_z):
    bn, t, _ = p.shape
    nc = t // CHUNK
    ngp = grow.shape[3]
    hd = HEAD_DIM
    seq = lambda col: _single((None, t, hd), lambda b, h, col=col: (b, 0, col + h))
    cw = lambda col: pl.BlockSpec((GDN_CONV, hd), lambda b, h, col=col: (0, col + h))
    return pl.pallas_call(
        functools.partial(_gdn_kernel, t=t, with_q=True),
        grid=(bn, GDN_HEADS),
        in_specs=[
            seq(col_k), seq(col_v), seq(col_q), seq(col_z),
            cw(0), cw(GDN_HEADS), cw(2 * GDN_HEADS),
            pl.BlockSpec((None, None, 4, ngp, GROUP), lambda b, h: (b, h, 0, 0, 0)),
            _single((t, hd), lambda b, h: (0, 0)),
            _single((t, hd), lambda b, h: (0, 0)),
            pl.BlockSpec((None, None, 2, hd, hd), lambda b, h: (b, h, 0, 0, 0)),
            pl.BlockSpec((1, hd), lambda b, h: (0, 0)),
        ],
        out_specs=pl.BlockSpec((None, t, hd), lambda b, h: (b, 0, h)),
        out_shape=jax.ShapeDtypeStruct((bn, t, GDN_HEADS * hd), BF16),
        scratch_shapes=[
            pltpu.VMEM((t + 2 * CONV_HALO, hd), F32),
            pltpu.VMEM((t, hd), F32),
            pltpu.VMEM((t, hd), F32),
            pltpu.VMEM((t, hd), F32),
            pltpu.VMEM((t, hd), F32),
            pltpu.VMEM((2, ngp, GROUP), F32),
            pltpu.VMEM((2, nc, hd + CHUNK, hd), BF16),
            pltpu.VMEM((2, nc, hd, hd), F32),
            pltpu.VMEM((2, nc, CHUNK, hd), F32),
            pltpu.VMEM((2, nc, SUBLANES, LANES), F32),
        ],
        compiler_params=_cparams(("parallel", "parallel")),
        name="gdn_latent",
    )(p, p, p, p, conv_w, conv_w, conv_w, grow, cosf, sinf, s0, gnorm.reshape(1, hd))


def gdn_context(pc, conv_w, grow, col_k, col_v):
    bn, t, _ = pc.shape
    nc = t // CHUNK
    ngp = grow.shape[3]
    hd = HEAD_DIM
    seq = lambda col: pl.BlockSpec((None, t, hd), lambda b, h, col=col: (b, 0, col + h))
    cw = lambda col: pl.BlockSpec((GDN_CONV, hd), lambda b, h, col=col: (0, col + h))
    return pl.pallas_call(
        functools.partial(_gdn_kernel, t=t, with_q=False),
        grid=(bn, GDN_HEADS),
        in_specs=[
            seq(col_k), seq(col_v), cw(0), cw(GDN_HEADS),
            pl.BlockSpec((None, None, 4, ngp, GROUP), lambda b, h: (b, h, 0, 0, 0)),
        ],
        out_specs=pl.BlockSpec((None, None, 2, hd, hd), lambda b, h: (b, h, 0, 0, 0)),
        out_shape=jax.ShapeDtypeStruct((bn, GDN_HEADS, 2, hd, hd), F32),
        scratch_shapes=[
            pltpu.VMEM((t + 2 * CONV_HALO, hd), F32),
            pltpu.VMEM((t, hd), F32),
            pltpu.VMEM((t, hd), F32),
            pltpu.VMEM((2, ngp, GROUP), F32),
            pltpu.VMEM((2, nc, hd, hd), BF16),
            pltpu.VMEM((2, nc, hd, hd), F32),
            pltpu.VMEM((2, nc, SUBLANES, LANES), F32),
        ],
        compiler_params=_cparams(("parallel", "parallel")),
        name="gdn_context",
    )(pc, pc, conv_w, conv_w, grow)


def gate_rows(gates, t):
    bn = gates.shape[0]
    ng = t // GROUP
    g = gates[:, :, :4 * GDN_HEADS].reshape(bn, ng, GROUP, 4, GDN_HEADS)
    g = jnp.transpose(g, (0, 4, 3, 1, 2))
    if ng < SUBLANES:
        g = jnp.pad(g, ((0, 0), (0, 0), (0, 0), (0, SUBLANES - ng), (0, 0)))
    return g


def _natten_kernel(q_ref, k_ref, v_ref, kc_ref, vc_ref, bias_ref, o_ref, *, rows):
    kcb = kc_ref[...]
    vcb = vc_ref[...]
    scale = HEAD_DIM ** -0.5
    win = NA_KH * GRID_W
    per_iter = min(NA_ROWS_PER_ITER, rows)

    def body(it, _):
        rws = [it * per_iter + k for k in range(per_iter)]
        q0s, k0s, s_loc, s_ctx = [], [], [], []
        for r in rws:
            rs = jnp.clip(r - NA_KH // 2, 0, rows - NA_KH)
            q0s.append(pl.multiple_of(r * GRID_W, GRID_W))
            k0s.append(pl.multiple_of(rs * GRID_W, GRID_W))
            q = q_ref[pl.ds(q0s[-1], GRID_W), :]
            kw = k_ref[pl.ds(k0s[-1], win), :]
            s_loc.append(lax.dot_general(q, kw, NT_DIMS, preferred_element_type=F32) * scale + bias_ref[r - rs])
            s_ctx.append(lax.dot_general(q, kcb, NT_DIMS, preferred_element_type=F32) * scale)
        p_loc, p_ctx, den = [], [], []
        for a, b in zip(s_loc, s_ctx):
            m = jnp.maximum(jnp.max(a, axis=-1, keepdims=True), jnp.max(b, axis=-1, keepdims=True))
            p_loc.append(jnp.exp(a - m))
            p_ctx.append(jnp.exp(b - m))
            den.append(jnp.sum(p_loc[-1], axis=-1, keepdims=True) + jnp.sum(p_ctx[-1], axis=-1, keepdims=True))
        outs = []
        for k in range(per_iter):
            vw = v_ref[pl.ds(k0s[k], win), :]
            outs.append(jnp.dot(p_loc[k].astype(BF16), vw, preferred_element_type=F32)
                        + jnp.dot(p_ctx[k].astype(BF16), vcb, preferred_element_type=F32))
        for k in range(per_iter):
            o_ref[pl.ds(q0s[k], GRID_W), :] = (outs[k] / den[k]).astype(o_ref.dtype)
        return 0

    lax.fori_loop(0, rows // per_iter, body, 0)


def natten(p, pc, bias, col_q, col_k, col_v, ctx_col_k, ctx_col_v):
    bn, t, _ = p.shape
    lc = pc.shape[1]
    hd = HEAD_DIM
    rows = t // GRID_W
    seq = lambda col: pl.BlockSpec((None, t, hd), lambda b, h, col=col: (b, 0, col + h))
    cseq = lambda col: pl.BlockSpec((None, lc, hd), lambda b, h, col=col: (b, 0, col + h))
    return pl.pallas_call(
        functools.partial(_natten_kernel, rows=rows),
        grid=(bn, NA_HEADS),
        in_specs=[
            seq(col_q), seq(col_k), seq(col_v), cseq(ctx_col_k), cseq(ctx_col_v),
            pl.BlockSpec((None, NA_KH, GRID_W, NA_KH * GRID_W), lambda b, h: (h, 0, 0, 0)),
        ],
        out_specs=pl.BlockSpec((None, t, hd), lambda b, h: (b, 0, h)),
        out_shape=jax.ShapeDtypeStruct((bn, t, NA_HEADS * hd), BF16),
        compiler_params=_cparams(("parallel", "parallel")),
        name="natten",
    )(p, p, p, pc, pc, bias)


def natten_bias(rpb, rows):
    kh = min(NA_KH, rows)
    hn, n_dr, n_dc = rpb.shape
    col = np.arange(GRID_W)
    col_start = np.clip(col - NA_KW // 2, 0, GRID_W - NA_KW)
    in_win = (col[None, :] >= col_start[:, None]) & (col[None, :] < col_start[:, None] + NA_KW)
    dc = np.clip(col[None, :] - col[:, None], -(NA_KW - 1), NA_KW - 1) + (NA_KW - 1)
    onehot = (np.arange(n_dc)[:, None] == dc.reshape(-1)[None, :]).astype(np.float32)
    by_dr = jnp.dot(rpb.astype(F32).reshape(hn * n_dr, n_dc), jnp.asarray(onehot), precision=HIGHEST)
    by_dr = jnp.where(in_win[None, None], by_dr.reshape(hn, n_dr, GRID_W, GRID_W), NEG_BIG)
    b = jnp.stack([by_dr[:, NA_KH - 1 - cls:NA_KH - 1 - cls + kh] for cls in range(NA_KH)], axis=1)
    return jnp.transpose(b, (0, 1, 3, 2, 4)).reshape(hn, NA_KH, GRID_W, kh * GRID_W)


def _zmix_kernel(gb_ref, gc_ref, val_ref, w_ref, z_ref, pad_ref, *, t):
    rb = min(256, t)
    _stage_padded(pad_ref, lambda r0: gc_ref[pl.ds(r0, rb), :].astype(F32) * val_ref[pl.ds(r0, rb), :].astype(F32),
                  t, rb)

    def body(i, _):
        r0 = pl.multiple_of(i * rb, rb)
        y = _conv_rows(pad_ref, w_ref, r0, rb, SC_CONV)
        z_ref[pl.ds(r0, rb), :] = (gb_ref[pl.ds(r0, rb), :].astype(F32) * y).astype(z_ref.dtype)
        return 0

    lax.fori_loop(0, t // rb, body, 0)


def zmix(p, conv_w):
    bn, t, n3 = p.shape
    d = n3 // 3
    nb = d // LANES
    seq = lambda off: pl.BlockSpec((None, t, LANES), lambda b, j, off=off: (b, 0, off + j))
    return pl.pallas_call(
        functools.partial(_zmix_kernel, t=t),
        grid=(bn, nb),
        in_specs=[seq(0), seq(nb), seq(2 * nb), pl.BlockSpec((SC_CONV, LANES), lambda b, j: (0, j))],
        out_specs=pl.BlockSpec((None, t, LANES), lambda b, j: (b, 0, j)),
        out_shape=jax.ShapeDtypeStruct((bn, t, d), BF16),
        scratch_shapes=[pltpu.VMEM((t + 2 * CONV_HALO, LANES), F32)],
        compiler_params=_cparams(("parallel", "parallel")),
        name="zmix",
    )(p, p, p, conv_w)


def _outproj_kernel(*refs, n_in):
    x_ref, gate_ref = refs[0], refs[1]
    a_refs = refs[2:2 + n_in]
    w_refs = refs[2 + n_in:2 + 2 * n_in]
    o_ref = refs[2 + 2 * n_in]
    y = None
    for a_ref, w_ref in zip(a_refs, w_refs):
        part = jnp.dot(a_ref[...], w_ref[...], preferred_element_type=F32)
        y = part if y is None else y + part
    o_ref[...] = x_ref[...] + gate_ref[...] * y


def outproj(x, gate, acts, weights, tm=512, tn=2048):
    bn, t, d = x.shape
    tn = min(tn, d)
    n_in = len(acts)
    in_specs = [pl.BlockSpec((None, tm, tn), lambda b, i, j: (b, i, j)),
                pl.BlockSpec((None, 1, tn), lambda b, i, j: (b, 0, j))]
    in_specs += [pl.BlockSpec((None, tm, a.shape[-1]), lambda b, i, j: (b, i, 0)) for a in acts]
    in_specs += [pl.BlockSpec((w.shape[0], tn), lambda b, i, j: (0, j)) for w in weights]
    return pl.pallas_call(
        functools.partial(_outproj_kernel, n_in=n_in),
        grid=(bn, t // tm, d // tn),
        in_specs=in_specs,
        out_specs=pl.BlockSpec((None, tm, tn), lambda b, i, j: (b, i, j)),
        out_shape=jax.ShapeDtypeStruct((bn, t, d), F32),
        compiler_params=_cparams(("parallel", "parallel", "parallel")),
        name="outproj",
    )(x, gate, *acts, *weights)


def _ffn_kernel(x_ref, g_ref, sc_ref, sh_ref, gate_ref, fn_ref, wg_ref, wu_ref, wd_ref, o_ref,
                hs_ref, wgb_ref, wub_ref, wdb_ref, *, final_norm):
    j = pl.program_id(2)

    @pl.when(j == 0)
    def _():
        _modulated_norm_rows(x_ref, g_ref, sc_ref, sh_ref, hs_ref)
        o_ref[...] = jnp.zeros_like(o_ref)

    wgb_ref[...] = wg_ref[...].astype(BF16)
    wub_ref[...] = wu_ref[...].astype(BF16)
    wdb_ref[...] = wd_ref[...].astype(BF16)
    h = hs_ref[...]
    gate = jnp.dot(h, wgb_ref[...], preferred_element_type=F32)
    up = jnp.dot(h, wub_ref[...], preferred_element_type=F32)
    a = (_silu(gate) * up).astype(BF16)
    o_ref[...] += jnp.dot(a, wdb_ref[...], preferred_element_type=F32)

    @pl.when(j == pl.num_programs(2) - 1)
    def _():
        rb = min(NORM_ROW_BLOCK, o_ref.shape[0])

        def body(i, _):
            rows = pl.ds(pl.multiple_of(i * rb, rb), rb)
            y = x_ref[rows, :] + gate_ref[...] * o_ref[rows, :]
            if final_norm:
                y = (y * lax.rsqrt(jnp.mean(y * y, axis=-1, keepdims=True) + NORM_EPS)) * fn_ref[...]
            o_ref[rows, :] = y
            return 0

        lax.fori_loop(0, o_ref.shape[0] // rb, body, 0)


def ffn(x, g, sc, sh, gate, fnorm, w_gate, w_up, w_down, layer, final_norm, tm=1024, tf=256):
    bn, t, d = x.shape
    tm = min(tm, t)
    f = w_gate.shape[2]
    vec = pl.BlockSpec((None, 1, d), lambda b, i, j: (b, 0, 0))
    one = pl.BlockSpec((1, d), lambda b, i, j: (0, 0))
    return pl.pallas_call(
        functools.partial(_ffn_kernel, final_norm=final_norm),
        grid=(bn, t // tm, f // tf),
        in_specs=[
            _single((None, tm, d), lambda b, i, j: (b, i, 0)),
            one, vec, vec, vec, one,
            pl.BlockSpec((None, d, tf), lambda b, i, j: (layer, 0, j)),
            pl.BlockSpec((None, d, tf), lambda b, i, j: (layer, 0, j)),
            pl.BlockSpec((None, tf, d), lambda b, i, j: (layer, j, 0)),
        ],
        out_specs=pl.BlockSpec((None, tm, d), lambda b, i, j: (b, i, 0)),
        out_shape=jax.ShapeDtypeStruct((bn, t, d), F32),
        scratch_shapes=[pltpu.VMEM((tm, d), BF16), pltpu.VMEM((d, tf), BF16), pltpu.VMEM((d, tf), BF16),
                        pltpu.VMEM((tf, d), BF16)],
        compiler_params=_cparams(("parallel", "parallel", "arbitrary")),
        name="ffn",
    )(x, g.reshape(1, d), sc, sh, gate, fnorm.reshape(1, d), w_gate, w_up, w_down)


def rope_tables(t):
    pos = np.arange(t)
    row = (pos // GRID_W).astype(np.float32)
    col = (pos % GRID_W).astype(np.float32)
    n_freq = HEAD_DIM // 4
    inv_freq = jnp.asarray(ROPE_THETA, F32) ** (-jnp.arange(n_freq, dtype=F32) / n_freq)
    ang = jnp.concatenate([jnp.asarray(row)[:, None] * inv_freq, jnp.asarray(col)[:, None] * inv_freq], axis=-1)
    cos, sin = jnp.cos(ang), jnp.sin(ang)
    return jnp.concatenate([cos, cos], axis=-1), jnp.concatenate([-sin, sin], axis=-1)


def kernel(x, c, ctx, c_ctx, ada_w, ada_b, norm_mix, norm_ffn, ffn_w_gate, ffn_w_up, ffn_w_down, final_norm,
           ev_w_in, ev_conv, ev_a_log, ev_dt_bias, ev_gdn_norm, ev_rpb, ev_w_out, od_w_in, od_conv, od_w_out):
    bn, t, d = x.shape
    depth = ada_w.shape[0]
    assert depth <= 2
    gw = GDN_HEADS * HEAD_DIM
    nw = NA_HEADS * HEAD_DIM
    nh = GDN_HEADS

    cv = jnp.zeros((SUBLANES, d), F32).at[:bn].set(c).at[bn].set(c_ctx)
    mods = ada_modulation(cv, ada_w, ada_b)

    def mod_vecs(l, rows):
        m = mods[l, rows].reshape(-1, 6, d)
        return [m[:, k][:, None, :] for k in range(6)]

    x_lat = x
    for l in range(depth):
        sh1, sc1, g1, sh2, sc2, g2 = mod_vecs(l, slice(0, bn))
        if l % 2 == 0:
            e = l // 2
            csh1, csc1 = [jnp.broadcast_to(v, (bn, 1, d)) for v in mod_vecs(l, slice(bn, bn + 1))[:2]]
            w_in = ev_w_in[e]
            n_gate = 4 * nh
            seg_gate = 2 * gw + 2 * nw
            w_tail = w_in[:, seg_gate + n_gate:]
            n_main = seg_gate + w_tail.shape[1]
            w_ab =jnp.pad(w_in[:, seg_gate:seg_gate + n_gate], ((0, 0), (0, LANES - n_gate))).astype(BF16)
            gparams = jnp.zeros((SUBLANES, LANES), F32)
            gparams = gparams.at[0, :2 * nh].set(ev_a_log[e].reshape(-1)).at[1, :2 * nh].set(ev_dt_bias[e].reshape(-1))
            col_ka, col_va, col_kb, col_vb, col_qa, col_qb, col_za = [k * nh for k in range(7)]

            pc, gates_c = proj(ctx, norm_mix[l], csc1, csh1, ev_w_in, e, seg_gate, None, seg_gate, w_ab, gparams)
            s_ctx = gdn_context(pc, ev_conv[e], gate_rows(gates_c, ctx.shape[1]), col_ka, col_va)

            p, gates = proj(x_lat, norm_mix[l], sc1, sh1, ev_w_in, e, seg_gate, w_tail, n_main, w_ab, gparams)
            cosf, sinf = rope_tables(t)
            y_gdn = gdn_latent(p, ev_conv[e], gate_rows(gates, t), cosf, sinf, s_ctx, ev_gdn_norm[e],
                               col_ka, col_va, col_qa, col_za)
            y_na = natten(p, pc, natten_bias(ev_rpb[e], t // GRID_W), col_qb, col_kb, col_vb, col_kb, col_vb)
            w_out = ev_w_out[e].astype(BF16)
            x_lat = outproj(x_lat, g1, [y_gdn, y_na], [w_out[:gw], w_out[gw:]])
        else:
            j_odd = l // 2
            p = proj(x_lat, norm_mix[l], sc1, sh1, od_w_in, j_odd, 3 * d, None, 3 * d)
            z = zmix(p, od_conv[j_odd])
            x_lat = outproj(x_lat, g1, [z], [od_w_out[j_odd].astype(BF16)])
        x_lat = ffn(x_lat, norm_ffn[l], sc2, sh2, g2, final_norm,
                    ffn_w_gate, ffn_w_up, ffn_w_down, layer=l, final_norm=(l == depth - 1))
    return x_lat
```

```python
import functools
import math

import jax
import jax.numpy as jnp
import numpy as np
from jax import lax
from jax.experimental import pallas as pl
from jax.experimental.pallas import tpu as pltpu

F32 = jnp.float32
BF16 = jnp.bfloat16
HIGHEST = lax.Precision.HIGHEST

LANES = 128
SUBLANES = 8
VMEM_LIMIT = 56 * 1024 * 1024

GRID_W = 64
HEAD_DIM = 128
GDN_HEADS = 8
NA_HEADS = 8
GDN_CONV = 5
CHUNK = 64
GROUP = 2 * CHUNK
PREP_GROUPS_PER_ITER = 4
GDN_ROW_BLOCK = 512
NA_ROWS_PER_ITER = 8
NA_KH = 8
NA_KW = 16
SC_CONV = 3
ROPE_THETA = 10000.0
NORM_EPS = 1e-6
NEG_BIG = -1e30

NT_DIMS = (((1,), (1,)), ((), ()))


def _cparams(sem):
    return pltpu.CompilerParams(dimension_semantics=sem, vmem_limit_bytes=VMEM_LIMIT)


def _sigmoid(x):
    return 1.0 / (1.0 + jnp.exp(-x))


def _silu(x):
    return x * _sigmoid(x)


def _softplus(x):
    return jnp.maximum(x, 0.0) + jnp.log(1.0 + jnp.exp(-jnp.abs(x)))


def _mm(a, b):
    return jnp.dot(a.astype(BF16), b.astype(BF16), preferred_element_type=F32)


def _mm_nt(a, b):
    return lax.dot_general(a.astype(BF16), b.astype(BF16), NT_DIMS, preferred_element_type=F32)


def _mm_exact(a, b):
    return jnp.dot(a, b, precision=HIGHEST, preferred_element_type=F32)


class CastJob:
    def __init__(self, src, in_block, in_index, out_shape, out_block, out_index):
        self.src = src
        self.in_spec = pl.BlockSpec(in_block, in_index)
        self.out_spec = pl.BlockSpec(out_block, out_index)
        self.out_shape = jax.ShapeDtypeStruct(out_shape, BF16)


def _hosting_casts(body, n_in, n_out, n_cast):
    def kernel(*refs):
        ins, rest = refs[:n_in], refs[n_in:]
        cast_in, rest = rest[:n_cast], rest[n_cast:]
        outs, rest = rest[:n_out], rest[n_out:]
        cast_out, scratch = rest[:n_cast], rest[n_cast:]
        for src, dst in zip(cast_in, cast_out):
            dst[...] = src[...].astype(BF16)
        body(*ins, *outs, *scratch)

    return kernel


def row_slab_job(w, layer, n_steps, step_of, cols=None):
    rows = w.shape[1]
    cols = w.shape[2] if cols is None else cols
    slab = rows // n_steps
    assert slab * n_steps == rows and slab % (2 * SUBLANES) == 0
    return CastJob(w, (None, slab, cols), lambda *idx: (layer, step_of(*idx), 0),
                   (rows, cols), (slab, cols), lambda *idx: (step_of(*idx), 0))


def _call_hosting(body, casts, grid, in_specs, args, out_specs, out_shapes, **kwargs):
    n_in, n_out = len(in_specs), len(out_specs)
    casts = list(casts)
    res = pl.pallas_call(
        _hosting_casts(body, n_in, n_out, len(casts)),
        grid=grid,
        in_specs=list(in_specs) + [c.in_spec for c in casts],
        out_specs=list(out_specs) + [c.out_spec for c in casts],
        out_shape=list(out_shapes) + [c.out_shape for c in casts],
        **kwargs,
    )(*args, *[c.src for c in casts])
    return res[:n_out], res[n_out:]


def _ada_kernel(cv_ref, w_ref, b_ref, o_ref):
    s = _silu(cv_ref[...])
    o_ref[...] = _mm(s, w_ref[...]) + b_ref[...]


def ada_modulation(cv, ada_w, ada_b, tn=1024):
    depth, d, n = ada_w.shape
    return pl.pallas_call(
        _ada_kernel,
        grid=(depth, n // tn),
        in_specs=[
            pl.BlockSpec((SUBLANES, d), lambda l, j: (0, 0)),
            pl.BlockSpec((None, d, tn), lambda l, j: (l, 0, j)),
            pl.BlockSpec((None, 1, tn), lambda l, j: (l, 0, j)),
        ],
        out_specs=pl.BlockSpec((None, SUBLANES, tn), lambda l, j: (l, 0, j)),
        out_shape=jax.ShapeDtypeStruct((depth, SUBLANES, n), F32),
        compiler_params=_cparams(("parallel", "parallel")),
        name="ada",
    )(cv, ada_w, ada_b.reshape(depth, 1, n))


def _modulated_norm(x, g, sc, sh):
    y = x * lax.rsqrt(jnp.mean(x * x, axis=-1, keepdims=True) + NORM_EPS)
    return (y * g) * (1.0 + sc) + sh


NORM_ROW_BLOCK = 128


def _modulated_norm_rows(x_ref, g_ref, sc_ref, sh_ref, hs_ref):
    rb = min(NORM_ROW_BLOCK, x_ref.shape[0])

    def body(i, _):
        rows = pl.ds(pl.multiple_of(i * rb, rb), rb)
        hs_ref[rows, :] = _modulated_norm(x_ref[rows, :], g_ref[...], sc_ref[...], sh_ref[...]).astype(BF16)
        return 0

    lax.fori_loop(0, x_ref.shape[0] // rb, body, 0)


def _proj_kernel(*refs, with_gates, with_tail, head_blocks):
    x_ref, g_ref, sc_ref, sh_ref, w_ref = refs[:5]
    refs = refs[5:]
    wt_ref = None
    if with_tail:
        wt_ref, refs = refs[0], refs[1:]
    if with_gates:
        wab_ref, gp_ref, o_ref, gate_ref, hs_ref, wb_ref = refs
    else:
        o_ref, hs_ref, wb_ref = refs
    j = pl.program_id(2)

    @pl.when(j == 0)
    def _():
        if with_gates:
            hb = _modulated_norm(x_ref[...], g_ref[...], sc_ref[...], sh_ref[...]).astype(BF16)
            hs_ref[...] = hb
            a = jnp.dot(hb, wab_ref[...], preferred_element_type=F32)
            neg_decay_rate = -jnp.exp(gp_ref[0:1, :])
            g = neg_decay_rate * _softplus(a + gp_ref[1:2, :])
            lane = lax.broadcasted_iota(jnp.int32, a.shape, 1)
            gate_ref[...] = jnp.where(lane < 2 * GDN_HEADS, g, _sigmoid(a))
        else:
            _modulated_norm_rows(x_ref, g_ref, sc_ref, sh_ref, hs_ref)

    def emit(wref):
        if wref.dtype != BF16:
            wb_ref[...] = wref[...].astype(BF16)
            wref = wb_ref
        o_ref[...] = jnp.dot(hs_ref[...], wref[...], preferred_element_type=F32).astype(o_ref.dtype)

    pl.when(j < head_blocks)(lambda: emit(w_ref))
    if with_tail:
        pl.when(j >= head_blocks)(lambda: emit(wt_ref))


def proj(x, g, sc, sh, w, layer, head_cols, w_tail, n_out, wab=None, gparams=None, tm=1024, tn=512):
    bn, t, d = x.shape
    tm = min(tm, t)
    head_blocks = head_cols // tn
    with_gates = wab is not None
    vec = pl.BlockSpec((None, 1, d), lambda b, i, j: (b, 0, 0))
    in_specs = [
        pl.BlockSpec((None, tm, d), lambda b, i, j: (b, i, 0)),
        pl.BlockSpec((1, d), lambda b, i, j: (0, 0)),
        vec, vec,
        pl.BlockSpec((None, d, tn), lambda b, i, j: (layer, 0, jnp.minimum(j, head_blocks - 1))),
    ]
    args = [x, g.reshape(1, d), sc, sh, w]
    with_tail = w_tail is not None
    if with_tail:
        in_specs.append(pl.BlockSpec((d, tn), lambda b, i, j: (0, jnp.maximum(j - head_blocks, 0))))
        args.append(w_tail)
    out_specs = [pl.BlockSpec((None, tm, tn), lambda b, i, j: (b, i, j))]
    out_shape = [jax.ShapeDtypeStruct((bn, t, n_out), BF16)]
    if with_gates:
        in_specs += [pl.BlockSpec((d, LANES), lambda b, i, j: (0, 0)),
                     pl.BlockSpec((SUBLANES, LANES), lambda b, i, j: (0, 0))]
        args += [wab, gparams]
        out_specs.append(pl.BlockSpec((None, tm, LANES), lambda b, i, j: (b, i, 0)))
        out_shape.append(jax.ShapeDtypeStruct((bn, t, LANES), F32))
    res = pl.pallas_call(
        functools.partial(_proj_kernel, with_gates=with_gates, with_tail=with_tail, head_blocks=head_blocks),
        grid=(bn, t // tm, n_out // tn),
        in_specs=in_specs,
        out_specs=out_specs,
        out_shape=out_shape,
        scratch_shapes=[pltpu.VMEM((tm, d), BF16), pltpu.VMEM((d, tn), BF16)],
        compiler_params=_cparams(("parallel", "parallel", "arbitrary")),
        name="proj",
    )(*args)
    return res if with_gates else res[0]


CONV_HALO = SUBLANES


def _stage_padded(pad_ref, load_rows, t, rb):
    zeros = jnp.zeros((CONV_HALO, LANES), F32)
    pad_ref[0:CONV_HALO, :] = zeros
    pad_ref[CONV_HALO + t:2 * CONV_HALO + t, :] = zeros

    def body(i, _):
        r0 = pl.multiple_of(i * rb, rb)
        pad_ref[pl.ds(CONV_HALO + r0, rb), :] = load_rows(r0)
        return 0

    lax.fori_loop(0, t // rb, body, 0)


def _conv_rows(pad_ref, w_ref, r0, rb, taps):
    acc = None
    for j in range(taps):
        xj = pad_ref[pl.ds(r0 + (CONV_HALO + j - taps // 2), rb), :]
        term = xj * w_ref[j:j + 1, :]
        acc = term if acc is None else acc + term
    return acc


def _l2norm(x):
    return x * lax.rsqrt(jnp.sum(x * x, axis=-1, keepdims=True) + NORM_EPS)


def _gdn_kernel(*refs, t, with_q):
    nc = t // CHUNK
    rb = min(GDN_ROW_BLOCK, t)
    if with_q:
        (ka_ref, va_ref, qa_ref, za_ref, wk_ref, wv_ref, wq_ref, grow_ref, cos_ref, sin_ref, s0_ref,
         gn_ref, y_ref,
         pad_ref, k_ref, v_ref, q_ref, o_ref, gc_ref, kwq_s, h_s, au_s, gl_s) = refs
    else:
        (ka_ref, va_ref, wk_ref, wv_ref, grow_ref, sfin_ref,
         pad_ref, k_ref, v_ref, gc_ref, kwq_s, h_s, gl_s) = refs

    def conv_all(src_ref, w_ref, finish, dst_ref):
        _stage_padded(pad_ref, lambda r0: src_ref[pl.ds(r0, rb), :].astype(F32), t, rb)

        def body(i, _):
            r0 = pl.multiple_of(i * rb, rb)
            y = _silu(_conv_rows(pad_ref, w_ref, r0, rb, GDN_CONV))
            dst_ref[pl.ds(r0, rb), :] = finish(y, r0)
            return 0

        lax.fori_loop(0, t // rb, body, 0)

    def rope(x, r0):
        return (x * cos_ref[pl.ds(r0, rb), :]
                + pltpu.roll(x, HEAD_DIM // 2, axis=1) * sin_ref[pl.ds(r0, rb), :])

    if with_q:
        conv_all(ka_ref, wk_ref, lambda y, r0: rope(_l2norm(y), r0), k_ref)
        conv_all(qa_ref, wq_ref, lambda y, r0: rope(_l2norm(y), r0) * HEAD_DIM ** -0.5, q_ref)
    else:
        conv_all(ka_ref, wk_ref, lambda y, r0: _l2norm(y), k_ref)
    conv_all(va_ref, wv_ref, lambda y, r0: y, v_ref)

    ii = lax.broadcasted_iota(jnp.int32, (GROUP, GROUP), 0)
    jj = lax.broadcasted_iota(jnp.int32, (GROUP, GROUP), 1)
    same = (ii // CHUNK) == (jj // CHUNK)
    lower = same & (ii >= jj)
    upper = same & (ii <= jj)
    eye = ii == jj
    gc_ref[0] = _mm_exact(grow_ref[0], upper.astype(F32))
    gc_ref[1] = _mm_exact(grow_ref[1], lower.astype(F32))

    def prep_groups(groups):
        chains = [(g, d) for g in range(len(groups)) for d in range(2)]
        kgs, vgs, qgs, kks, qks = {}, {}, {}, {}, {}
        for g in range(len(groups)):
            r0 = pl.multiple_of(groups[g] * GROUP, GROUP)
            kgs[g] = k_ref[pl.ds(r0, GROUP), :]
            vgs[g] = v_ref[pl.ds(r0, GROUP), :]
            kgb = kgs[g].astype(BF16)
            if with_q:
                qgs[g] = q_ref[pl.ds(r0, GROUP), :]
                both = _mm_nt(jnp.concatenate([kgs[g], qgs[g]], axis=0), kgb)
                kks[g], qks[g] = both[:GROUP], both[GROUP:]
            else:
                kks[g] = _mm_nt(kgb, kgb)
        gc_r, be_r, gc_c, decay, p, sq = {}, {}, {}, {}, {}, {}
        for ch in chains:
            g, d = ch
            incl = lower if d == 0 else upper
            gc_r[ch] = jnp.broadcast_to(gc_ref[d, pl.ds(groups[g], 1), :], (GROUP, GROUP))
            be_r[ch] = jnp.broadcast_to(grow_ref[2 + d, pl.ds(groups[g], 1), :], (GROUP, GROUP))
            gc_c[ch] = gc_r[ch].T
            decay[ch] = jnp.where(incl, jnp.exp(jnp.where(incl, gc_c[ch] - gc_r[ch], 0.0)), 0.0)
            m = jnp.where(incl & jnp.logical_not(eye), kks[g] * be_r[ch].T * decay[ch], 0.0)
            p[ch] = jnp.where(eye, 1.0, -m)
            sq[ch] = _mm(m, m)
        for _ in range(4):
            for ch in chains:
                p[ch] = p[ch] + _mm(p[ch], sq[ch])
                sq[ch] = _mm(sq[ch], sq[ch])
        for ch in chains:
            p[ch] = p[ch] + _mm(p[ch], sq[ch])
        uw = {}
        for ch in chains:
            g, d = ch
            rhs = jnp.concatenate([vgs[g], kgs[g] * jnp.exp(gc_c[ch])], axis=1)
            uw[ch] = _mm(p[ch] * be_r[ch], rhs)
        lhs, gls, qds = {}, {}, {}
        for ch in chains:
            g, d = ch
            last = [CHUNK - 1, GROUP - 1] if d == 0 else [0, CHUNK]
            gls[ch] = [gc_c[ch][i:i + 1, :] for i in last]
            gl_c = jnp.concatenate([jnp.broadcast_to(x, (CHUNK, LANES)) for x in gls[ch]], axis=0)
            kdt = (kgs[g] * jnp.exp(gl_c - gc_c[ch])).T
            parts = [jnp.where(jj < CHUNK, kdt, 0.0), jnp.where(jj >= CHUNK, kdt, 0.0)]
            if with_q:
                qds[ch] = qgs[g] * jnp.exp(gc_c[ch])
                parts = [qks[g] * decay[ch]] + parts
            lhs[ch] = jnp.concatenate(parts, axis=0)
        prod = {ch: _mm(lhs[ch], uw[ch]) for ch in chains}
        for ch in chains:
            g, d = ch
            r = prod[ch]
            off = GROUP if with_q else 0
            for half in range(2):
                c = 2 * groups[g] + half
                blk = r[off + half * HEAD_DIM:off + (half + 1) * HEAD_DIM]
                h_s[d, c] = blk[:, :HEAD_DIM]
                gl_s[d, c] = jnp.broadcast_to(jnp.exp(gls[ch][half]), (SUBLANES, LANES))
                if with_q:
                    rs = slice(half * CHUNK, (half + 1) * CHUNK)
                    au_s[d, c] = r[rs, :HEAD_DIM]
                    qw = qds[ch][rs] - r[rs, HEAD_DIM:]
                    kwq_s[d, c] = jnp.concatenate([blk[:, HEAD_DIM:], qw], axis=0).astype(BF16)
                else:
                    kwq_s[d, c] = blk[:, HEAD_DIM:].astype(BF16)

    n_groups = t // GROUP
    per_iter = min(PREP_GROUPS_PER_ITER, n_groups)

    def prep(i, _):
        prep_groups([i * per_iter + k for k in range(per_iter)])
        return 0

    lax.fori_loop(0, n_groups // per_iter, prep, 0)

    if with_q:
        o_ref[...] = jnp.zeros_like(o_ref)

    def scan(s, carry):
        cs = (s, nc - 1 - s)
        rr = [jnp.dot(kwq_s[d, cs[d]], carry[d].astype(BF16), preferred_element_type=F32) for d in range(2)]
        new = []
        for d in range(2):
            c = cs[d]
            new.append(carry[d] * gl_s[d, c][0:1, :] + h_s[d, c] - rr[d][:HEAD_DIM])
            if with_q:
                r0 = pl.multiple_of(c * CHUNK, CHUNK)
                o_ref[pl.ds(r0, CHUNK), :] += rr[d][HEAD_DIM:] + au_s[d, c]
        return tuple(new)

    if with_q:
        init = (s0_ref[0], s0_ref[1])
    else:
        zero = jnp.zeros((HEAD_DIM, HEAD_DIM), F32)
        init = (zero, zero)
    s_f, s_b = lax.fori_loop(0, nc, scan, init)

    if not with_q:
        sfin_ref[0] = s_f
        sfin_ref[1] = s_b
        return

    def finish(i, _):
        r0 = pl.multiple_of(i * rb, rb)
        o = o_ref[pl.ds(r0, rb), :]
        y = o * lax.rsqrt(jnp.mean(o * o, axis=-1, keepdims=True) + NORM_EPS)
        y_ref[pl.ds(r0, rb), :] = ((y * gn_ref[...]) * _silu(za_ref[pl.ds(r0, rb), :].astype(F32))).astype(y_ref.dtype)
        return 0

    lax.fori_loop(0, t // rb, finish, 0)


def _single(block_shape, index_map):
    return pl.BlockSpec(block_shape, index_map, pipeline_mode=pl.Buffered(1))


def gdn_latent(p, conv_w, grow, cosf, sinf, s0, gnorm, col_k, col_v, col_q, col_z):
    bn, t, _ = p.shape
    nc = t // CHUNK
    ngp = grow.shape[3]
    hd = HEAD_DIM
    seq = lambda col: _single((None, t, hd), lambda b, h, col=col: (b, 0, col + h))
    cw = lambda col: pl.BlockSpec((GDN_CONV, hd), lambda b, h, col=col: (0, col + h))
    return pl.pallas_call(
        functools.partial(_gdn_kernel, t=t, with_q=True),
        grid=(bn, GDN_HEADS),
        in_specs=[
            seq(col_k), seq(col_v), seq(col_q), seq(col_z),
            cw(0), cw(GDN_HEADS), cw(2 * GDN_HEADS),
            pl.BlockSpec((None, None, 4, ngp, GROUP), lambda b, h: (b, h, 0, 0, 0)),
            _single((t, hd), lambda b, h: (0, 0)),
            _single((t, hd), lambda b, h: (0, 0)),
            pl.BlockSpec((None, None, 2, hd, hd), lambda b, h: (b, h, 0, 0, 0)),
            pl.BlockSpec((1, hd), lambda b, h: (0, 0)),
        ],
        out_specs=pl.BlockSpec((None, t, hd), lambda b, h: (b, 0, h)),
        out_shape=jax.ShapeDtypeStruct((bn, t, GDN_HEADS * hd), BF16),
        scratch_shapes=[
            pltpu.VMEM((t + 2 * CONV_HALO, hd), F32),
            pltpu.VMEM((t, hd), F32),
            pltpu.VMEM((t, hd), F32),
            pltpu.VMEM((t, hd), F32),
            pltpu.VMEM((t, hd), F32),
            pltpu.VMEM((2, ngp, GROUP), F32),
            pltpu.VMEM((2, nc, hd + CHUNK, hd), BF16),
            pltpu.VMEM((2, nc, hd, hd), F32),
            pltpu.VMEM((2, nc, CHUNK, hd), F32),
            pltpu.VMEM((2, nc, SUBLANES, LANES), F32),
        ],
        compiler_params=_cparams(("parallel", "parallel")),
        name="gdn_latent",
    )(p, p, p, p, conv_w, conv_w, conv_w, grow, cosf, sinf, s0, gnorm.reshape(1, hd))


def gdn_context(pc, conv_w, grow, col_k, col_v, casts=()):
    bn, t, _ = pc.shape
    nc = t // CHUNK
    ngp = grow.shape[3]
    hd = HEAD_DIM
    seq = lambda col: pl.BlockSpec((None, t, hd), lambda b, h, col=col: (b, 0, col + h))
    cw = lambda col: pl.BlockSpec((GDN_CONV, hd), lambda b, h, col=col: (0, col + h))
    (states,), rounded = _call_hosting(
        lambda *refs: _gdn_kernel(*refs, t=t, with_q=False), casts,
        grid=(bn, GDN_HEADS),
        in_specs=[
            seq(col_k), seq(col_v), cw(0), cw(GDN_HEADS),
            pl.BlockSpec((None, None, 4, ngp, GROUP), lambda b, h: (b, h, 0, 0, 0)),
        ],
        args=(pc, pc, conv_w, conv_w, grow),
        out_specs=[pl.BlockSpec((None, None, 2, hd, hd), lambda b, h: (b, h, 0, 0, 0))],
        out_shapes=[jax.ShapeDtypeStruct((bn, GDN_HEADS, 2, hd, hd), F32)],
        scratch_shapes=[
            pltpu.VMEM((t + 2 * CONV_HALO, hd), F32),
            pltpu.VMEM((t, hd), F32),
            pltpu.VMEM((t, hd), F32),
            pltpu.VMEM((2, ngp, GROUP), F32),
            pltpu.VMEM((2, nc, hd, hd), BF16),
            pltpu.VMEM((2, nc, hd, hd), F32),
            pltpu.VMEM((2, nc, SUBLANES, LANES), F32),
        ],
        compiler_params=_cparams(("parallel", "parallel")),
        name="gdn_context",
    )
    return states, rounded


def gate_rows(gates, t):
    bn = gates.shape[0]
    ng = t // GROUP
    g = gates[:, :, :4 * GDN_HEADS].reshape(bn, ng, GROUP, 4, GDN_HEADS)
    g = jnp.transpose(g, (0, 4, 3, 1, 2))
    if ng < SUBLANES:
        g = jnp.pad(g, ((0, 0), (0, 0), (0, 0), (0, SUBLANES - ng), (0, 0)))
    return g


def _natten_kernel(q_ref, k_ref, v_ref, kc_ref, vc_ref, bias_ref, o_ref, *, rows):
    kcb = kc_ref[...]
    vcb = vc_ref[...]
    scale = HEAD_DIM ** -0.5
    win = NA_KH * GRID_W
    per_iter = min(NA_ROWS_PER_ITER, rows)

    def body(it, _):
        rws = [it * per_iter + k for k in range(per_iter)]
        q0s, k0s, s_loc, s_ctx = [], [], [], []
        for r in rws:
            rs = jnp.clip(r - NA_KH // 2, 0, rows - NA_KH)
            q0s.append(pl.multiple_of(r * GRID_W, GRID_W))
            k0s.append(pl.multiple_of(rs * GRID_W, GRID_W))
            q = q_ref[pl.ds(q0s[-1], GRID_W), :]
            kw = k_ref[pl.ds(k0s[-1], win), :]
            s_loc.append(lax.dot_general(q, kw, NT_DIMS, preferred_element_type=F32) * scale + bias_ref[r - rs])
            s_ctx.append(lax.dot_general(q, kcb, NT_DIMS, preferred_element_type=F32) * scale)
        p_loc, p_ctx, den = [], [], []
        for a, b in zip(s_loc, s_ctx):
            m = jnp.maximum(jnp.max(a, axis=-1, keepdims=True), jnp.max(b, axis=-1, keepdims=True))
            p_loc.append(jnp.exp(a - m))
            p_ctx.append(jnp.exp(b - m))
            den.append(jnp.sum(p_loc[-1], axis=-1, keepdims=True) + jnp.sum(p_ctx[-1], axis=-1, keepdims=True))
        outs = []
        for k in range(per_iter):
            vw = v_ref[pl.ds(k0s[k], win), :]
            outs.append(jnp.dot(p_loc[k].astype(BF16), vw, preferred_element_type=F32)
                        + jnp.dot(p_ctx[k].astype(BF16), vcb, preferred_element_type=F32))
        for k in range(per_iter):
            o_ref[pl.ds(q0s[k], GRID_W), :] = (outs[k] / den[k]).astype(o_ref.dtype)
        return 0

    lax.fori_loop(0, rows // per_iter, body, 0)


def natten(p, pc, bias, col_q, col_k, col_v, ctx_col_k, ctx_col_v, casts=()):
    bn, t, _ = p.shape
    lc = pc.shape[1]
    hd = HEAD_DIM
    rows = t // GRID_W
    seq = lambda col: pl.BlockSpec((None, t, hd), lambda b, h, col=col: (b, 0, col + h))
    cseq = lambda col: pl.BlockSpec((None, lc, hd), lambda b, h, col=col: (b, 0, col + h))
    (y,), rounded = _call_hosting(
        lambda *refs: _natten_kernel(*refs, rows=rows), casts,
        grid=(bn, NA_HEADS),
        in_specs=[
            seq(col_q), seq(col_k), seq(col_v), cseq(ctx_col_k), cseq(ctx_col_v),
            pl.BlockSpec((None, NA_KH, GRID_W, NA_KH * GRID_W), lambda b, h: (h, 0, 0, 0)),
        ],
        args=(p, p, p, pc, pc, bias),
        out_specs=[pl.BlockSpec((None, t, hd), lambda b, h: (b, 0, h))],
        out_shapes=[jax.ShapeDtypeStruct((bn, t, NA_HEADS * hd), BF16)],
        compiler_params=_cparams(("parallel", "parallel")),
        name="natten",
    )
    return y, rounded


def natten_bias(rpb, rows):
    kh = min(NA_KH, rows)
    hn, n_dr, n_dc = rpb.shape
    col = np.arange(GRID_W)
    col_start = np.clip(col - NA_KW // 2, 0, GRID_W - NA_KW)
    in_win = (col[None, :] >= col_start[:, None]) & (col[None, :] < col_start[:, None] + NA_KW)
    dc = np.clip(col[None, :] - col[:, None], -(NA_KW - 1), NA_KW - 1) + (NA_KW - 1)
    onehot = (np.arange(n_dc)[:, None] == dc.reshape(-1)[None, :]).astype(np.float32)
    by_dr = jnp.dot(rpb.astype(F32).reshape(hn * n_dr, n_dc), jnp.asarray(onehot), precision=HIGHEST)
    by_dr = jnp.where(in_win[None, None], by_dr.reshape(hn, n_dr, GRID_W, GRID_W), NEG_BIG)
    b = jnp.stack([by_dr[:, NA_KH - 1 - cls:NA_KH - 1 - cls + kh] for cls in range(NA_KH)], axis=1)
    return jnp.transpose(b, (0, 1, 3, 2, 4)).reshape(hn, NA_KH, GRID_W, kh * GRID_W)


def _zmix_kernel(gb_ref, gc_ref, val_ref, w_ref, z_ref, pad_ref, *, t):
    rb = min(256, t)
    _stage_padded(pad_ref, lambda r0: gc_ref[pl.ds(r0, rb), :].astype(F32) * val_ref[pl.ds(r0, rb), :].astype(F32),
                  t, rb)

    def body(i, _):
        r0 = pl.multiple_of(i * rb, rb)
        y = _conv_rows(pad_ref, w_ref, r0, rb, SC_CONV)
        z_ref[pl.ds(r0, rb), :] = (gb_ref[pl.ds(r0, rb), :].astype(F32) * y).astype(z_ref.dtype)
        return 0

    lax.fori_loop(0, t // rb, body, 0)


def zmix(p, conv_w):
    bn, t, n3 = p.shape
    d = n3 // 3
    nb = d // LANES
    seq = lambda off: pl.BlockSpec((None, t, LANES), lambda b, j, off=off: (b, 0, off + j))
    return pl.pallas_call(
        functools.partial(_zmix_kernel, t=t),
        grid=(bn, nb),
        in_specs=[seq(0), seq(nb), seq(2 * nb), pl.BlockSpec((SC_CONV, LANES), lambda b, j: (0, j))],
        out_specs=pl.BlockSpec((None, t, LANES), lambda b, j: (b, 0, j)),
        out_shape=jax.ShapeDtypeStruct((bn, t, d), BF16),
        scratch_shapes=[pltpu.VMEM((t + 2 * CONV_HALO, LANES), F32)],
        compiler_params=_cparams(("parallel", "parallel")),
        name="zmix",
    )(p, p, p, conv_w)


def _outproj_kernel(*refs, n_in):
    x_ref, gate_ref = refs[0], refs[1]
    a_refs = refs[2:2 + n_in]
    w_refs = refs[2 + n_in:2 + 2 * n_in]
    o_ref = refs[2 + 2 * n_in]
    y = None
    for a_ref, w_ref in zip(a_refs, w_refs):
        part = jnp.dot(a_ref[...], w_ref[...], preferred_element_type=F32)
        y = part if y is None else y + part
    o_ref[...] = x_ref[...] + gate_ref[...] * y


def outproj(x, gate, acts, weight, tm=512, tn=2048):
    bn, t, d = x.shape
    tn = min(tn, d)
    n_in = len(acts)
    kdim = acts[0].shape[-1]
    assert all(a.shape[-1] == kdim for a in acts) and weight.shape[0] == n_in * kdim
    weights = [weight] * n_in
    in_specs = [pl.BlockSpec((None, tm, tn), lambda b, i, j: (b, i, j)),
                pl.BlockSpec((None, 1, tn), lambda b, i, j: (b, 0, j))]
    in_specs += [pl.BlockSpec((None, tm, kdim), lambda b, i, j: (b, i, 0)) for a in acts]
    in_specs += [pl.BlockSpec((kdim, tn), lambda b, i, j, k=k: (k, j)) for k in range(n_in)]
    return pl.pallas_call(
        functools.partial(_outproj_kernel, n_in=n_in),
        grid=(bn, t // tm, d // tn),
        in_specs=in_specs,
        out_specs=pl.BlockSpec((None, tm, tn), lambda b, i, j: (b, i, j)),
        out_shape=jax.ShapeDtypeStruct((bn, t, d), F32),
        compiler_params=_cparams(("parallel", "parallel", "parallel")),
        name="outproj",
    )(x, gate, *acts, *weights)


def _ffn_kernel(x_ref, g_ref, sc_ref, sh_ref, gate_ref, fn_ref, wg_ref, wu_ref, wd_ref, o_ref,
                hs_ref, *, final_norm):
    j = pl.program_id(2)

    @pl.when(j == 0)
    def _():
        _modulated_norm_rows(x_ref, g_ref, sc_ref, sh_ref, hs_ref)
        o_ref[...] = jnp.zeros_like(o_ref)

    h = hs_ref[...]
    gate = jnp.dot(h, wg_ref[...], preferred_element_type=F32)
    up = jnp.dot(h, wu_ref[...], preferred_element_type=F32)
    a = (_silu(gate) * up).astype(BF16)
    o_ref[...] += jnp.dot(a, wd_ref[...], preferred_element_type=F32)

    @pl.when(j == pl.num_programs(2) - 1)
    def _():
        rb = min(NORM_ROW_BLOCK, o_ref.shape[0])

        def body(i, _):
            rows = pl.ds(pl.multiple_of(i * rb, rb), rb)
            y = x_ref[rows, :] + gate_ref[...] * o_ref[rows, :]
            if final_norm:
                y = (y * lax.rsqrt(jnp.mean(y * y, axis=-1, keepdims=True) + NORM_EPS)) * fn_ref[...]
            o_ref[rows, :] = y
            return 0

        lax.fori_loop(0, o_ref.shape[0] // rb, body, 0)


def ffn_grid(x, f, tm=1024, tf=512):
    bn, t, _ = x.shape
    tm = min(tm, t)
    return tm, tf, (bn, t // tm, f // tf)


def ffn_weight_jobs(w_gate, w_up, w_down, layer, grid, tf):
    bn, ni, nj = grid
    d = w_gate.shape[1]
    slab = d // (bn * ni)
    assert slab * bn * ni == d and slab % LANES == 0
    tile = lambda b, i: b * ni + i
    up_job = lambda w: CastJob(w, (None, slab, tf), lambda b, i, j: (layer, tile(b, i), j),
                               w.shape[1:], (slab, tf), lambda b, i, j: (tile(b, i), j))
    down_job = CastJob(w_down, (None, tf, slab), lambda b, i, j: (layer, j, tile(b, i)),
                       w_down.shape[1:], (tf, slab), lambda b, i, j: (j, tile(b, i)))
    return [up_job(w_gate), up_job(w_up), down_job]


def ffn(x, g, sc, sh, gate, fnorm, w_gate, w_up, w_down, final_norm, casts=()):
    bn, t, d = x.shape
    f = w_gate.shape[1]
    tm, tf, grid = ffn_grid(x, f)
    vec = pl.BlockSpec((None, 1, d), lambda b, i, j: (b, 0, 0))
    one = pl.BlockSpec((1, d), lambda b, i, j: (0, 0))
    (y,), rounded = _call_hosting(
        functools.partial(_ffn_kernel, final_norm=final_norm), casts,
        grid=grid,
        in_specs=[
            _single((None, tm, d), lambda b, i, j: (b, i, 0)),
            one, vec, vec, vec, one,
            pl.BlockSpec((d, tf), lambda b, i, j: (0, j)),
            pl.BlockSpec((d, tf), lambda b, i, j: (0, j)),
            pl.BlockSpec((tf, d), lambda b, i, j: (j, 0)),
        ],
        args=(x, g.reshape(1, d), sc, sh, gate, fnorm.reshape(1, d), w_gate, w_up, w_down),
        out_specs=[pl.BlockSpec((None, tm, d), lambda b, i, j: (b, i, 0))],
        out_shapes=[jax.ShapeDtypeStruct((bn, t, d), F32)],
        scratch_shapes=[pltpu.VMEM((tm, d), BF16)],
        compiler_params=_cparams(("parallel", "parallel", "arbitrary")),
        name="ffn",
    )
    return y, rounded


def rope_tables(t):
    pos = np.arange(t)
    row = (pos // GRID_W).astype(np.float32)
    col = (pos % GRID_W).astype(np.float32)
    n_freq = HEAD_DIM // 4
    inv_freq = jnp.asarray(ROPE_THETA, F32) ** (-jnp.arange(n_freq, dtype=F32) / n_freq)
    ang = jnp.concatenate([jnp.asarray(row)[:, None] * inv_freq, jnp.asarray(col)[:, None] * inv_freq], axis=-1)
    cos, sin = jnp.cos(ang), jnp.sin(ang)
    return jnp.concatenate([cos, cos], axis=-1), jnp.concatenate([-sin, sin], axis=-1)


def kernel(x, c, ctx, c_ctx, ada_w, ada_b, norm_mix, norm_ffn, ffn_w_gate, ffn_w_up, ffn_w_down, final_norm,
           ev_w_in, ev_conv, ev_a_log, ev_dt_bias, ev_gdn_norm, ev_rpb, ev_w_out, od_w_in, od_conv, od_w_out):
    bn, t, d = x.shape
    depth = ada_w.shape[0]
    assert depth == 2
    gw = GDN_HEADS * HEAD_DIM
    nw = NA_HEADS * HEAD_DIM
    nh = GDN_HEADS

    cv = jnp.zeros((SUBLANES, d), F32).at[:bn].set(c).at[bn].set(c_ctx)
    mods = ada_modulation(cv, ada_w, ada_b)

    def mod_vecs(l, rows):
        m = mods[l, rows].reshape(-1, 6, d)
        return [m[:, k][:, None, :] for k in range(6)]

    sh1, sc1, g1, sh2, sc2, g2 = mod_vecs(0, slice(0, bn))
    csh1, csc1 = [jnp.broadcast_to(v, (bn, 1, d)) for v in mod_vecs(0, slice(bn, bn + 1))[:2]]
    w_in = ev_w_in[0]
    n_gate = 4 * nh
    seg_gate = 2 * gw + 2 * nw
    w_tail = w_in[:, seg_gate + n_gate:].astype(BF16)
    n_main = seg_gate + w_tail.shape[1]
    w_ab = jnp.pad(w_in[:, seg_gate:seg_gate + n_gate], ((0, 0), (0, LANES - n_gate))).astype(BF16)
    gparams = jnp.zeros((SUBLANES, LANES), F32)
    gparams = gparams.at[0, :2 * nh].set(ev_a_log[0].reshape(-1)).at[1, :2 * nh].set(ev_dt_bias[0].reshape(-1))
    col_ka, col_va, col_kb, col_vb, col_qa, col_qb, col_za = [k * nh for k in range(7)]

    head_step = lambda b, h: b * nh + h
    n_head_steps = bn * nh
    pc, gates_c = proj(ctx, norm_mix[0], csc1, csh1, ev_w_in, 0, seg_gate, None, seg_gate, w_ab, gparams)
    s_ctx, (w_head, w_out0, w_in1, w_out1) = gdn_context(
        pc, ev_conv[0], gate_rows(gates_c, ctx.shape[1]), col_ka, col_va,
        casts=[row_slab_job(ev_w_in, 0, n_head_steps, head_step, cols=seg_gate),
               row_slab_job(ev_w_out, 0, n_head_steps, head_step),
               row_slab_job(od_w_in, 0, n_head_steps, head_step),
               row_slab_job(od_w_out, 0, n_head_steps, head_step)])

    p, gates = proj(x, norm_mix[0], sc1, sh1, w_head[None], 0, seg_gate, w_tail, n_main, w_ab, gparams)
    cosf, sinf = rope_tables(t)
    y_gdn = gdn_latent(p, ev_conv[0], gate_rows(gates, t), cosf, sinf, s_ctx, ev_gdn_norm[0],
                       col_ka, col_va, col_qa, col_za)
    y_na, ffn0 = natten(p, pc, natten_bias(ev_rpb[0], t // GRID_W), col_qb, col_kb, col_vb, col_kb, col_vb,
                        casts=[row_slab_job(w, 0, n_head_steps, head_step)
                               for w in (ffn_w_gate, ffn_w_up, ffn_w_down)])
    x_lat = outproj(x, g1, [y_gdn, y_na], w_out0)
    _, tf, grid = ffn_grid(x_lat, ffn_w_gate.shape[2])
    x_lat, ffn1 = ffn(x_lat, norm_ffn[0], sc2, sh2, g2, final_norm, *ffn0, final_norm=False,
                      casts=ffn_weight_jobs(ffn_w_gate, ffn_w_up, ffn_w_down, 1, grid, tf))

    sh1, sc1, g1, sh2, sc2, g2 = mod_vecs(1, slice(0, bn))
    p = proj(x_lat, norm_mix[1], sc1, sh1, w_in1[None], 0, 3 * d, None, 3 * d)
    z = zmix(p, od_conv[0])
    x_lat = outproj(x_lat, g1, [z], w_out1)
    x_lat, _ = ffn(x_lat, norm_ffn[1], sc2, sh2, g2, final_norm, *ffn1, final_norm=True)
    return x_lat
```

```python
import functools
import math

import jax
import jax.numpy as jnp
import numpy as np
from jax import lax
from jax.experimental import pallas as pl
from jax.experimental.pallas import tpu as pltpu

F32 = jnp.float32
BF16 = jnp.bfloat16
HIGHEST = lax.Precision.HIGHEST

LANES = 128
SUBLANES = 8
VMEM_LIMIT = 56 * 1024 * 1024

GRID_W = 64
HEAD_DIM = 128
GDN_HEADS = 8
NA_HEADS = 8
GDN_CONV = 5
CHUNK = 64
INVERSE_LEVELS = 6
GROUP = 2 * CHUNK
PREP_GROUPS_PER_ITER = 4
GDN_ROW_BLOCK = 512
NA_ROWS_PER_ITER = 8
NA_KH = 8
NA_KW = 16
SC_CONV = 3
ROPE_THETA = 10000.0
NORM_EPS = 1e-6
NEG_BIG = -1e30

NT_DIMS = (((1,), (1,)), ((), ()))


def _cparams(sem):
    return pltpu.CompilerParams(dimension_semantics=sem, vmem_limit_bytes=VMEM_LIMIT)


def _sigmoid(x):
    return 1.0 / (1.0 + jnp.exp(-x))


def _silu(x):
    return x * _sigmoid(x)


def _softplus(x):
    return jnp.maximum(x, 0.0) + jnp.log(1.0 + jnp.exp(-jnp.abs(x)))


def _mm(a, b):
    return jnp.dot(a.astype(BF16), b.astype(BF16), preferred_element_type=F32)


def _mm_nt(a, b):
    return lax.dot_general(a.astype(BF16), b.astype(BF16), NT_DIMS, preferred_element_type=F32)


def _mm_exact(a, b):
    return jnp.dot(a, b, precision=HIGHEST, preferred_element_type=F32)


class CastJob:
    def __init__(self, src, in_block, in_index, out_shape, out_block, out_index):
        self.src = src
        self.in_spec = pl.BlockSpec(in_block, in_index)
        self.out_spec = pl.BlockSpec(out_block, out_index)
        self.out_shape = jax.ShapeDtypeStruct(out_shape, BF16)


def _hosting_casts(body, n_in, n_out, n_cast):
    def kernel(*refs):
        ins, rest = refs[:n_in], refs[n_in:]
        cast_in, rest = rest[:n_cast], rest[n_cast:]
        outs, rest = rest[:n_out], rest[n_out:]
        cast_out, scratch = rest[:n_cast], rest[n_cast:]
        for src, dst in zip(cast_in, cast_out):
            dst[...] = src[...].astype(BF16)
        body(*ins, *outs, *scratch)

    return kernel


def row_slab_job(w, layer, n_steps, step_of, cols=None):
    rows = w.shape[1]
    cols = w.shape[2] if cols is None else cols
    slab = rows // n_steps
    assert slab * n_steps == rows and slab % (2 * SUBLANES) == 0
    return CastJob(w, (None, slab, cols), lambda *idx: (layer, step_of(*idx), 0),
                   (rows, cols), (slab, cols), lambda *idx: (step_of(*idx), 0))


def _call_hosting(body, casts, grid, in_specs, args, out_specs, out_shapes, **kwargs):
    n_in, n_out = len(in_specs), len(out_specs)
    casts = list(casts)
    res = pl.pallas_call(
        _hosting_casts(body, n_in, n_out, len(casts)),
        grid=grid,
        in_specs=list(in_specs) + [c.in_spec for c in casts],
        out_specs=list(out_specs) + [c.out_spec for c in casts],
        out_shape=list(out_shapes) + [c.out_shape for c in casts],
        **kwargs,
    )(*args, *[c.src for c in casts])
    return res[:n_out], res[n_out:]


def _ada_kernel(cv_ref, w_ref, b_ref, o_ref):
    s = _silu(cv_ref[...])
    o_ref[...] = _mm(s, w_ref[...]) + b_ref[...]


def ada_modulation(cv, ada_w, ada_b, tn=1024):
    depth, d, n = ada_w.shape
    return pl.pallas_call(
        _ada_kernel,
        grid=(depth, n // tn),
        in_specs=[
            pl.BlockSpec((SUBLANES, d), lambda l, j: (0, 0)),
            pl.BlockSpec((None, d, tn), lambda l, j: (l, 0, j)),
            pl.BlockSpec((None, 1, tn), lambda l, j: (l, 0, j)),
        ],
        out_specs=pl.BlockSpec((None, SUBLANES, tn), lambda l, j: (l, 0, j)),
        out_shape=jax.ShapeDtypeStruct((depth, SUBLANES, n), F32),
        compiler_params=_cparams(("parallel", "parallel")),
        name="ada",
    )(cv, ada_w, ada_b.reshape(depth, 1, n))


def _modulated_norm(x, g, sc, sh):
    y = x * lax.rsqrt(jnp.mean(x * x, axis=-1, keepdims=True) + NORM_EPS)
    return (y * g) * (1.0 + sc) + sh


NORM_ROW_BLOCK = 128


def _modulated_norm_rows(x_ref, g_ref, sc_ref, sh_ref, hs_ref):
    rb = min(NORM_ROW_BLOCK, x_ref.shape[0])

    def body(i, _):
        rows = pl.ds(pl.multiple_of(i * rb, rb), rb)
        hs_ref[rows, :] = _modulated_norm(x_ref[rows, :], g_ref[...], sc_ref[...], sh_ref[...]).astype(BF16)
        return 0

    lax.fori_loop(0, x_ref.shape[0] // rb, body, 0)


def _proj_kernel(*refs, with_gates, with_tail, head_blocks):
    x_ref, g_ref, sc_ref, sh_ref, w_ref = refs[:5]
    refs = refs[5:]
    wt_ref = None
    if with_tail:
        wt_ref, refs = refs[0], refs[1:]
    if with_gates:
        wab_ref, gp_ref, o_ref, gate_ref, hs_ref, wb_ref = refs
    else:
        o_ref, hs_ref, wb_ref = refs
    j = pl.program_id(2)

    @pl.when(j == 0)
    def _():
        if with_gates:
            hb = _modulated_norm(x_ref[...], g_ref[...], sc_ref[...], sh_ref[...]).astype(BF16)
            hs_ref[...] = hb
            a = jnp.dot(hb, wab_ref[...], preferred_element_type=F32)
            neg_decay_rate = -jnp.exp(gp_ref[0:1, :])
            g = neg_decay_rate * _softplus(a + gp_ref[1:2, :])
            lane = lax.broadcasted_iota(jnp.int32, a.shape, 1)
            gate_ref[...] = jnp.where(lane < 2 * GDN_HEADS, g, _sigmoid(a))
        else:
            _modulated_norm_rows(x_ref, g_ref, sc_ref, sh_ref, hs_ref)

    def emit(wref):
        if wref.dtype != BF16:
            wb_ref[...] = wref[...].astype(BF16)
            wref = wb_ref
        o_ref[...] = jnp.dot(hs_ref[...], wref[...], preferred_element_type=F32).astype(o_ref.dtype)

    pl.when(j < head_blocks)(lambda: emit(w_ref))
    if with_tail:
        pl.when(j >= head_blocks)(lambda: emit(wt_ref))


def proj(x, g, sc, sh, w, layer, head_cols, w_tail, n_out, wab=None, gparams=None, tm=1024, tn=512):
    bn, t, d = x.shape
    tm = min(tm, t)
    head_blocks = head_cols // tn
    with_gates = wab is not None
    vec = pl.BlockSpec((None, 1, d), lambda b, i, j: (b, 0, 0))
    in_specs = [
        pl.BlockSpec((None, tm, d), lambda b, i, j: (b, i, 0)),
        pl.BlockSpec((1, d), lambda b, i, j: (0, 0)),
        vec, vec,
        pl.BlockSpec((None, d, tn), lambda b, i, j: (layer, 0, jnp.minimum(j, head_blocks - 1))),
    ]
    args = [x, g.reshape(1, d), sc, sh, w]
    with_tail = w_tail is not None
    if with_tail:
        in_specs.append(pl.BlockSpec((d, tn), lambda b, i, j: (0, jnp.maximum(j - head_blocks, 0))))
        args.append(w_tail)
    out_specs = [pl.BlockSpec((None, tm, tn), lambda b, i, j: (b, i, j))]
    out_shape = [jax.ShapeDtypeStruct((bn, t, n_out), BF16)]
    if with_gates:
        in_specs += [pl.BlockSpec((d, LANES), lambda b, i, j: (0, 0)),
                     pl.BlockSpec((SUBLANES, LANES), lambda b, i, j: (0, 0))]
        args += [wab, gparams]
        out_specs.append(pl.BlockSpec((None, tm, LANES), lambda b, i, j: (b, i, 0)))
        out_shape.append(jax.ShapeDtypeStruct((bn, t, LANES), F32))
    res = pl.pallas_call(
        functools.partial(_proj_kernel, with_gates=with_gates, with_tail=with_tail, head_blocks=head_blocks),
        grid=(bn, t // tm, n_out // tn),
        in_specs=in_specs,
        out_specs=out_specs,
        out_shape=out_shape,
        scratch_shapes=[pltpu.VMEM((tm, d), BF16), pltpu.VMEM((d, tn), BF16)],
        compiler_params=_cparams(("parallel", "parallel", "arbitrary")),
        name="proj",
    )(*args)
    return res if with_gates else res[0]


CONV_HALO = SUBLANES


def _stage_padded(pad_ref, load_rows, t, rb):
    zeros = jnp.zeros((CONV_HALO, LANES), F32)
    pad_ref[0:CONV_HALO, :] = zeros
    pad_ref[CONV_HALO + t:2 * CONV_HALO + t, :] = zeros

    def body(i, _):
        r0 = pl.multiple_of(i * rb, rb)
        pad_ref[pl.ds(CONV_HALO + r0, rb), :] = load_rows(r0)
        return 0

    lax.fori_loop(0, t // rb, body, 0)


def _conv_rows(pad_ref, w_ref, r0, rb, taps):
    acc = None
    for j in range(taps):
        xj = pad_ref[pl.ds(r0 + (CONV_HALO + j - taps // 2), rb), :]
        term = xj * w_ref[j:j + 1, :]
        acc = term if acc is None else acc + term
    return acc


def _l2norm(x):
    return x * lax.rsqrt(jnp.sum(x * x, axis=-1, keepdims=True) + NORM_EPS)


def _gdn_kernel(*refs, t, with_q):
    nc = t // CHUNK
    rb = min(GDN_ROW_BLOCK, t)
    if with_q:
        (ka_ref, va_ref, qa_ref, za_ref, wk_ref, wv_ref, wq_ref, grow_ref, cos_ref, sin_ref, s0_ref,
         gn_ref, y_ref,
         pad_ref, k_ref, v_ref, q_ref, o_ref, gc_ref, kwq_s, h_s, au_s, gl_s) = refs
    else:
        (ka_ref, va_ref, wk_ref, wv_ref, grow_ref, sfin_ref,
         pad_ref, k_ref, v_ref, gc_ref, kwq_s, h_s, gl_s) = refs

    def conv_all(src_ref, w_ref, finish, dst_ref):
        _stage_padded(pad_ref, lambda r0: src_ref[pl.ds(r0, rb), :].astype(F32), t, rb)

        def body(i, _):
            r0 = pl.multiple_of(i * rb, rb)
            y = _silu(_conv_rows(pad_ref, w_ref, r0, rb, GDN_CONV))
            dst_ref[pl.ds(r0, rb), :] = finish(y, r0)
            return 0

        lax.fori_loop(0, t // rb, body, 0)

    def rope(x, r0):
        return (x * cos_ref[pl.ds(r0, rb), :]
                + pltpu.roll(x, HEAD_DIM // 2, axis=1) * sin_ref[pl.ds(r0, rb), :])

    if with_q:
        conv_all(ka_ref, wk_ref, lambda y, r0: rope(_l2norm(y), r0), k_ref)
        conv_all(qa_ref, wq_ref, lambda y, r0: rope(_l2norm(y), r0) * HEAD_DIM ** -0.5, q_ref)
    else:
        conv_all(ka_ref, wk_ref, lambda y, r0: _l2norm(y), k_ref)
    conv_all(va_ref, wv_ref, lambda y, r0: y, v_ref)

    ii = lax.broadcasted_iota(jnp.int32, (GROUP, GROUP), 0)
    jj = lax.broadcasted_iota(jnp.int32, (GROUP, GROUP), 1)
    same = (ii // CHUNK) == (jj // CHUNK)
    lower = same & (ii >= jj)
    upper = same & (ii <= jj)
    eye = ii == jj
    gc_ref[0] = _mm_exact(grow_ref[0], upper.astype(F32))
    gc_ref[1] = _mm_exact(grow_ref[1], lower.astype(F32))

    def prep_groups(groups):
        chains = [(g, d) for g in range(len(groups)) for d in range(2)]
        kgs, vgs, qgs, kks, qks = {}, {}, {}, {}, {}
        for g in range(len(groups)):
            r0 = pl.multiple_of(groups[g] * GROUP, GROUP)
            kgs[g] = k_ref[pl.ds(r0, GROUP), :]
            vgs[g] = v_ref[pl.ds(r0, GROUP), :]
            kgb = kgs[g].astype(BF16)
            if with_q:
                qgs[g] = q_ref[pl.ds(r0, GROUP), :]
                both = _mm_nt(jnp.concatenate([kgs[g], qgs[g]], axis=0), kgb)
                kks[g], qks[g] = both[:GROUP], both[GROUP:]
            else:
                kks[g] = _mm_nt(kgb, kgb)
        gc_r, be_r, gc_c, decay, p, sq = {}, {}, {}, {}, {}, {}
        for ch in chains:
            g, d = ch
            incl = lower if d == 0 else upper
            gc_r[ch] = jnp.broadcast_to(gc_ref[d, pl.ds(groups[g], 1), :], (GROUP, GROUP))
            be_r[ch] = jnp.broadcast_to(grow_ref[2 + d, pl.ds(groups[g], 1), :], (GROUP, GROUP))
            gc_c[ch] = gc_r[ch].T
            decay[ch] = jnp.where(incl, jnp.exp(jnp.where(incl, gc_c[ch] - gc_r[ch], 0.0)), 0.0)
            m = jnp.where(incl & jnp.logical_not(eye), kks[g] * be_r[ch].T * decay[ch], 0.0)
            p[ch] = jnp.where(eye, 1.0, -m)
            sq[ch] = _mm(m, m)
        for _ in range(INVERSE_LEVELS - 2):
            for ch in chains:
                p[ch] = p[ch] + _mm(p[ch], sq[ch])
                sq[ch] = _mm(sq[ch], sq[ch])
        for ch in chains:
            p[ch] = p[ch] + _mm(p[ch], sq[ch])
        uw = {}
        for ch in chains:
            g, d = ch
            rhs = jnp.concatenate([vgs[g], kgs[g] * jnp.exp(gc_c[ch])], axis=1)
            uw[ch] = _mm(p[ch] * be_r[ch], rhs)
        lhs, gls, qds = {}, {}, {}
        for ch in chains:
            g, d = ch
            last = [CHUNK - 1, GROUP - 1] if d == 0 else [0, CHUNK]
            gls[ch] = [gc_c[ch][i:i + 1, :] for i in last]
            gl_c = jnp.concatenate([jnp.broadcast_to(x, (CHUNK, LANES)) for x in gls[ch]], axis=0)
            kdt = (kgs[g] * jnp.exp(gl_c - gc_c[ch])).T
            parts = [jnp.where(jj < CHUNK, kdt, 0.0), jnp.where(jj >= CHUNK, kdt, 0.0)]
            if with_q:
                qds[ch] = qgs[g] * jnp.exp(gc_c[ch])
                parts = [qks[g] * decay[ch]] + parts
            lhs[ch] = jnp.concatenate(parts, axis=0)
        prod = {ch: _mm(lhs[ch], uw[ch]) for ch in chains}
        for ch in chains:
            g, d = ch
            r = prod[ch]
            off = GROUP if with_q else 0
            for half in range(2):
                c = 2 * groups[g] + half
                blk = r[off + half * HEAD_DIM:off + (half + 1) * HEAD_DIM]
                h_s[d, c] = blk[:, :HEAD_DIM]
                gl_s[d, c] = jnp.broadcast_to(jnp.exp(gls[ch][half]), (SUBLANES, LANES))
                if with_q:
                    rs = slice(half * CHUNK, (half + 1) * CHUNK)
                    au_s[d, c] = r[rs, :HEAD_DIM]
                    qw = qds[ch][rs] - r[rs, HEAD_DIM:]
                    kwq_s[d, c] = jnp.concatenate([blk[:, HEAD_DIM:], qw], axis=0).astype(BF16)
                else:
                    kwq_s[d, c] = blk[:, HEAD_DIM:].astype(BF16)

    n_groups = t // GROUP
    per_iter = min(PREP_GROUPS_PER_ITER, n_groups)

    def prep(i, _):
        prep_groups([i * per_iter + k for k in range(per_iter)])
        return 0

    lax.fori_loop(0, n_groups // per_iter, prep, 0)

    if with_q:
        o_ref[...] = jnp.zeros_like(o_ref)

    def scan(s, carry):
        cs = (s, nc - 1 - s)
        rr = [jnp.dot(kwq_s[d, cs[d]], carry[d].astype(BF16), preferred_element_type=F32) for d in range(2)]
        new = []
        for d in range(2):
            c = cs[d]
            new.append(carry[d] * gl_s[d, c][0:1, :] + h_s[d, c] - rr[d][:HEAD_DIM])
            if with_q:
                r0 = pl.multiple_of(c * CHUNK, CHUNK)
                o_ref[pl.ds(r0, CHUNK), :] += rr[d][HEAD_DIM:] + au_s[d, c]
        return tuple(new)

    if with_q:
        init = (s0_ref[0], s0_ref[1])
    else:
        zero = jnp.zeros((HEAD_DIM, HEAD_DIM), F32)
        init = (zero, zero)
    s_f, s_b = lax.fori_loop(0, nc, scan, init)

    if not with_q:
        sfin_ref[0] = s_f
        sfin_ref[1] = s_b
        return

    def finish(i, _):
        r0 = pl.multiple_of(i * rb, rb)
        o = o_ref[pl.ds(r0, rb), :]
        y = o * lax.rsqrt(jnp.mean(o * o, axis=-1, keepdims=True) + NORM_EPS)
        y_ref[pl.ds(r0, rb), :] = ((y * gn_ref[...]) * _silu(za_ref[pl.ds(r0, rb), :].astype(F32))).astype(y_ref.dtype)
        return 0

    lax.fori_loop(0, t // rb, finish, 0)


def _single(block_shape, index_map):
    return pl.BlockSpec(block_shape, index_map, pipeline_mode=pl.Buffered(1))


def gdn_latent(p, conv_w, grow, cosf, sinf, s0, gnorm, col_k, col_v, col_q, col_z):
    bn, t, _ = p.shape
    nc = t // CHUNK
    ngp = grow.shape[3]
    hd = HEAD_DIM
    seq = lambda col: _single((None, t, hd), lambda b, h, col=col: (b, 0, col + h))
    cw = lambda col: pl.BlockSpec((GDN_CONV, hd), lambda b, h, col=col: (0, col + h))
    return pl.pallas_call(
        functools.partial(_gdn_kernel, t=t, with_q=True),
        grid=(bn, GDN_HEADS),
        in_specs=[
            seq(col_k), seq(col_v), seq(col_q), seq(col_z),
            cw(0), cw(GDN_HEADS), cw(2 * GDN_HEADS),
            pl.BlockSpec((None, None, 4, ngp, GROUP), lambda b, h: (b, h, 0, 0, 0)),
            _single((t, hd), lambda b, h: (0, 0)),
            _single((t, hd), lambda b, h: (0, 0)),
            pl.BlockSpec((None, None, 2, hd, hd), lambda b, h: (b, h, 0, 0, 0)),
            pl.BlockSpec((1, hd), lambda b, h: (0, 0)),
        ],
        out_specs=pl.BlockSpec((None, t, hd), lambda b, h: (b, 0, h)),
        out_shape=jax.ShapeDtypeStruct((bn, t, GDN_HEADS * hd), BF16),
        scratch_shapes=[
            pltpu.VMEM((t + 2 * CONV_HALO, hd), F32),
            pltpu.VMEM((t, hd), F32),
            pltpu.VMEM((t, hd), F32),
            pltpu.VMEM((t, hd), F32),
            pltpu.VMEM((t, hd), F32),
            pltpu.VMEM((2, ngp, GROUP), F32),
            pltpu.VMEM((2, nc, hd + CHUNK, hd), BF16),
            pltpu.VMEM((2, nc, hd, hd), F32),
            pltpu.VMEM((2, nc, CHUNK, hd), F32),
            pltpu.VMEM((2, nc, SUBLANES, LANES), F32),
        ],
        compiler_params=_cparams(("parallel", "parallel")),
        name="gdn_latent",
    )(p, p, p, p, conv_w, conv_w, conv_w, grow, cosf, sinf, s0, gnorm.reshape(1, hd))


def gdn_context(pc, conv_w, grow, col_k, col_v, casts=()):
    bn, t, _ = pc.shape
    nc = t // CHUNK
    ngp = grow.shape[3]
    hd = HEAD_DIM
    seq = lambda col: pl.BlockSpec((None, t, hd), lambda b, h, col=col: (b, 0, col + h))
    cw = lambda col: pl.BlockSpec((GDN_CONV, hd), lambda b, h, col=col: (0, col + h))
    (states,), rounded = _call_hosting(
        lambda *refs: _gdn_kernel(*refs, t=t, with_q=False), casts,
        grid=(bn, GDN_HEADS),
        in_specs=[
            seq(col_k), seq(col_v), cw(0), cw(GDN_HEADS),
            pl.BlockSpec((None, None, 4, ngp, GROUP), lambda b, h: (b, h, 0, 0, 0)),
        ],
        args=(pc, pc, conv_w, conv_w, grow),
        out_specs=[pl.BlockSpec((None, None, 2, hd, hd), lambda b, h: (b, h, 0, 0, 0))],
        out_shapes=[jax.ShapeDtypeStruct((bn, GDN_HEADS, 2, hd, hd), F32)],
        scratch_shapes=[
            pltpu.VMEM((t + 2 * CONV_HALO, hd), F32),
            pltpu.VMEM((t, hd), F32),
            pltpu.VMEM((t, hd), F32),
            pltpu.VMEM((2, ngp, GROUP), F32),
            pltpu.VMEM((2, nc, hd, hd), BF16),
            pltpu.VMEM((2, nc, hd, hd), F32),
            pltpu.VMEM((2, nc, SUBLANES, LANES), F32),
        ],
        compiler_params=_cparams(("parallel", "parallel")),
        name="gdn_context",
    )
    return states, rounded


def gate_rows(gates, t):
    bn = gates.shape[0]
    ng = t // GROUP
    g = gates[:, :, :4 * GDN_HEADS].reshape(bn, ng, GROUP, 4, GDN_HEADS)
    g = jnp.transpose(g, (0, 4, 3, 1, 2))
    if ng < SUBLANES:
        g = jnp.pad(g, ((0, 0), (0, 0), (0, 0), (0, SUBLANES - ng), (0, 0)))
    return g


def _natten_kernel(q_ref, k_ref, v_ref, kc_ref, vc_ref, bias_ref, o_ref, *, rows):
    kcb = kc_ref[...]
    vcb = vc_ref[...]
    scale = HEAD_DIM ** -0.5
    win = NA_KH * GRID_W
    per_iter = min(NA_ROWS_PER_ITER, rows)

    def body(it, _):
        rws = [it * per_iter + k for k in range(per_iter)]
        q0s, k0s, s_loc, s_ctx = [], [], [], []
        for r in rws:
            rs = jnp.clip(r - NA_KH // 2, 0, rows - NA_KH)
            q0s.append(pl.multiple_of(r * GRID_W, GRID_W))
            k0s.append(pl.multiple_of(rs * GRID_W, GRID_W))
            q = q_ref[pl.ds(q0s[-1], GRID_W), :]
            kw = k_ref[pl.ds(k0s[-1], win), :]
            s_loc.append(lax.dot_general(q, kw, NT_DIMS, preferred_element_type=F32) * scale + bias_ref[r - rs])
            s_ctx.append(lax.dot_general(q, kcb, NT_DIMS, preferred_element_type=F32) * scale)
        p_loc, p_ctx, den = [], [], []
        for a, b in zip(s_loc, s_ctx):
            m = jnp.maximum(jnp.max(a, axis=-1, keepdims=True), jnp.max(b, axis=-1, keepdims=True))
            p_loc.append(jnp.exp(a - m))
            p_ctx.append(jnp.exp(b - m))
            den.append(jnp.sum(p_loc[-1], axis=-1, keepdims=True) + jnp.sum(p_ctx[-1], axis=-1, keepdims=True))
        outs = []
        for k in range(per_iter):
            vw = v_ref[pl.ds(k0s[k], win), :]
            outs.append(jnp.dot(p_loc[k].astype(BF16), vw, preferred_element_type=F32)
                        + jnp.dot(p_ctx[k].astype(BF16), vcb, preferred_element_type=F32))
        for k in range(per_iter):
            o_ref[pl.ds(q0s[k], GRID_W), :] = (outs[k] / den[k]).astype(o_ref.dtype)
        return 0

    lax.fori_loop(0, rows // per_iter, body, 0)


def natten(p, pc, bias, col_q, col_k, col_v, ctx_col_k, ctx_col_v, casts=()):
    bn, t, _ = p.shape
    lc = pc.shape[1]
    hd = HEAD_DIM
    rows = t // GRID_W
    seq = lambda col: pl.BlockSpec((None, t, hd), lambda b, h, col=col: (b, 0, col + h))
    cseq = lambda col: pl.BlockSpec((None, lc, hd), lambda b, h, col=col: (b, 0, col + h))
    (y,), rounded = _call_hosting(
        lambda *refs: _natten_kernel(*refs, rows=rows), casts,
        grid=(bn, NA_HEADS),
        in_specs=[
            seq(col_q), seq(col_k), seq(col_v), cseq(ctx_col_k), cseq(ctx_col_v),
            pl.BlockSpec((None, NA_KH, GRID_W, NA_KH * GRID_W), lambda b, h: (h, 0, 0, 0)),
        ],
        args=(p, p, p, pc, pc, bias),
        out_specs=[pl.BlockSpec((None, t, hd), lambda b, h: (b, 0, h))],
        out_shapes=[jax.ShapeDtypeStruct((bn, t, NA_HEADS * hd), BF16)],
        compiler_params=_cparams(("parallel", "parallel")),
        name="natten",
    )
    return y, rounded


def natten_bias(rpb, rows):
    kh = min(NA_KH, rows)
    hn, n_dr, n_dc = rpb.shape
    col = np.arange(GRID_W)
    col_start = np.clip(col - NA_KW // 2, 0, GRID_W - NA_KW)
    in_win = (col[None, :] >= col_start[:, None]) & (col[None, :] < col_start[:, None] + NA_KW)
    dc = np.clip(col[None, :] - col[:, None], -(NA_KW - 1), NA_KW - 1) + (NA_KW - 1)
    onehot = (np.arange(n_dc)[:, None] == dc.reshape(-1)[None, :]).astype(np.float32)
    by_dr = jnp.dot(rpb.astype(F32).reshape(hn * n_dr, n_dc), jnp.asarray(onehot), precision=HIGHEST)
    by_dr = jnp.where(in_win[None, None], by_dr.reshape(hn, n_dr, GRID_W, GRID_W), NEG_BIG)
    b = jnp.stack([by_dr[:, NA_KH - 1 - cls:NA_KH - 1 - cls + kh] for cls in range(NA_KH)], axis=1)
    return jnp.transpose(b, (0, 1, 3, 2, 4)).reshape(hn, NA_KH, GRID_W, kh * GRID_W)


def _zmix_kernel(gb_ref, gc_ref, val_ref, w_ref, z_ref, pad_ref, *, t):
    rb = min(256, t)
    _stage_padded(pad_ref, lambda r0: gc_ref[pl.ds(r0, rb), :].astype(F32) * val_ref[pl.ds(r0, rb), :].astype(F32),
                  t, rb)

    def body(i, _):
        r0 = pl.multiple_of(i * rb, rb)
        y = _conv_rows(pad_ref, w_ref, r0, rb, SC_CONV)
        z_ref[pl.ds(r0, rb), :] = (gb_ref[pl.ds(r0, rb), :].astype(F32) * y).astype(z_ref.dtype)
        return 0

    lax.fori_loop(0, t // rb, body, 0)


def zmix(p, conv_w):
    bn, t, n3 = p.shape
    d = n3 // 3
    nb = d // LANES
    seq = lambda off: pl.BlockSpec((None, t, LANES), lambda b, j, off=off: (b, 0, off + j))
    return pl.pallas_call(
        functools.partial(_zmix_kernel, t=t),
        grid=(bn, nb),
        in_specs=[seq(0), seq(nb), seq(2 * nb), pl.BlockSpec((SC_CONV, LANES), lambda b, j: (0, j))],
        out_specs=pl.BlockSpec((None, t, LANES), lambda b, j: (b, 0, j)),
        out_shape=jax.ShapeDtypeStruct((bn, t, d), BF16),
        scratch_shapes=[pltpu.VMEM((t + 2 * CONV_HALO, LANES), F32)],
        compiler_params=_cparams(("parallel", "parallel")),
        name="zmix",
    )(p, p, p, conv_w)


def _outproj_kernel(*refs, n_in):
    x_ref, gate_ref, g_ref, sc_ref, sh_ref = refs[:5]
    a_refs = refs[5:5 + n_in]
    w_refs = refs[5 + n_in:5 + 2 * n_in]
    o_ref, hs_ref = refs[5 + 2 * n_in:]
    y = None
    for a_ref, w_ref in zip(a_refs, w_refs):
        part = jnp.dot(a_ref[...], w_ref[...], preferred_element_type=F32)
        y = part if y is None else y + part
    x1 = x_ref[...] + gate_ref[...] * y
    o_ref[...] = x1
    hs_ref[...] = _modulated_norm(x1, g_ref[...], sc_ref[...], sh_ref[...]).astype(BF16)


def outproj(x, gate, acts, weight, g_next, sc_next, sh_next, tm=512):
    bn, t, d = x.shape
    n_in = len(acts)
    kdim = acts[0].shape[-1]
    assert all(a.shape[-1] == kdim for a in acts) and weight.shape[0] == n_in * kdim
    weights = [weight] * n_in
    row = pl.BlockSpec((None, tm, d), lambda b, i: (b, i, 0))
    vec = pl.BlockSpec((None, 1, d), lambda b, i: (b, 0, 0))
    in_specs = [row, vec, pl.BlockSpec((1, d), lambda b, i: (0, 0)), vec, vec]
    in_specs += [pl.BlockSpec((None, tm, kdim), lambda b, i: (b, i, 0)) for a in acts]
    in_specs += [pl.BlockSpec((kdim, d), lambda b, i, k=k: (k, 0)) for k in range(n_in)]
    return pl.pallas_call(
        functools.partial(_outproj_kernel, n_in=n_in),
        grid=(bn, t // tm),
        in_specs=in_specs,
        out_specs=[row, row],
        out_shape=[jax.ShapeDtypeStruct((bn, t, d), F32), jax.ShapeDtypeStruct((bn, t, d), BF16)],
        compiler_params=_cparams(("parallel", "parallel")),
        name="outproj",
    )(x, gate, g_next.reshape(1, d), sc_next, sh_next, *acts, *weights)


def _ffn_kernel(x_ref, hs_ref, gate_ref, fn_ref, wg_ref, wu_ref, wd_ref, o_ref, *, final_norm):
    j = pl.program_id(2)

    @pl.when(j == 0)
    def _():
        o_ref[...] = jnp.zeros_like(o_ref)

    h = hs_ref[...]
    gate = jnp.dot(h, wg_ref[...], preferred_element_type=F32)
    up = jnp.dot(h, wu_ref[...], preferred_element_type=F32)
    a = (_silu(gate) * up).astype(BF16)
    o_ref[...] += jnp.dot(a, wd_ref[...], preferred_element_type=F32)

    @pl.when(j == pl.num_programs(2) - 1)
    def _():
        rb = min(NORM_ROW_BLOCK, o_ref.shape[0])

        def body(i, _):
            rows = pl.ds(pl.multiple_of(i * rb, rb), rb)
            y = x_ref[rows, :] + gate_ref[...] * o_ref[rows, :]
            if final_norm:
                y = (y * lax.rsqrt(jnp.mean(y * y, axis=-1, keepdims=True) + NORM_EPS)) * fn_ref[...]
            o_ref[rows, :] = y
            return 0

        lax.fori_loop(0, o_ref.shape[0] // rb, body, 0)


def ffn_grid(x, f, tm=1024, tf=512):
    bn, t, _ = x.shape
    tm = min(tm, t)
    return tm, tf, (bn, t // tm, f // tf)


def ffn_weight_jobs(w_gate, w_up, w_down, layer, grid, tf):
    bn, ni, nj = grid
    d = w_gate.shape[1]
    slab = d // (bn * ni)
    assert slab * bn * ni == d and slab % LANES == 0
    tile = lambda b, i: b * ni + i
    up_job = lambda w: CastJob(w, (None, slab, tf), lambda b, i, j: (layer, tile(b, i), j),
                               w.shape[1:], (slab, tf), lambda b, i, j: (tile(b, i), j))
    down_job = CastJob(w_down, (None, tf, slab), lambda b, i, j: (layer, j, tile(b, i)),
                       w_down.shape[1:], (tf, slab), lambda b, i, j: (j, tile(b, i)))
    return [up_job(w_gate), up_job(w_up), down_job]


def ffn(x, hs, gate, fnorm, w_gate, w_up, w_down, final_norm, casts=()):
    bn, t, d = x.shape
    f = w_gate.shape[1]
    tm, tf, grid = ffn_grid(x, f)
    vec = pl.BlockSpec((None, 1, d), lambda b, i, j: (b, 0, 0))
    one = pl.BlockSpec((1, d), lambda b, i, j: (0, 0))
    (y,), rounded = _call_hosting(
        functools.partial(_ffn_kernel, final_norm=final_norm), casts,
        grid=grid,
        in_specs=[
            _single((None, tm, d), lambda b, i, j: (b, i, 0)),
            pl.BlockSpec((None, tm, d), lambda b, i, j: (b, i, 0)),
            vec, one,
            pl.BlockSpec((d, tf), lambda b, i, j: (0, j)),
            pl.BlockSpec((d, tf), lambda b, i, j: (0, j)),
            pl.BlockSpec((tf, d), lambda b, i, j: (j, 0)),
        ],
        args=(x, hs, gate, fnorm.reshape(1, d), w_gate, w_up, w_down),
        out_specs=[pl.BlockSpec((None, tm, d), lambda b, i, j: (b, i, 0))],
        out_shapes=[jax.ShapeDtypeStruct((bn, t, d), F32)],
        compiler_params=_cparams(("parallel", "parallel", "arbitrary")),
        name="ffn",
    )
    return y, rounded


def rope_tables(t):
    pos = np.arange(t)
    row = (pos // GRID_W).astype(np.float32)
    col = (pos % GRID_W).astype(np.float32)
    n_freq = HEAD_DIM // 4
    inv_freq = jnp.asarray(ROPE_THETA, F32) ** (-jnp.arange(n_freq, dtype=F32) / n_freq)
    ang = jnp.concatenate([jnp.asarray(row)[:, None] * inv_freq, jnp.asarray(col)[:, None] * inv_freq], axis=-1)
    cos, sin = jnp.cos(ang), jnp.sin(ang)
    return jnp.concatenate([cos, cos], axis=-1), jnp.concatenate([-sin, sin], axis=-1)


def kernel(x, c, ctx, c_ctx, ada_w, ada_b, norm_mix, norm_ffn, ffn_w_gate, ffn_w_up, ffn_w_down, final_norm,
           ev_w_in, ev_conv, ev_a_log, ev_dt_bias, ev_gdn_norm, ev_rpb, ev_w_out, od_w_in, od_conv, od_w_out):
    bn, t, d = x.shape
    depth = ada_w.shape[0]
    assert depth == 2
    gw = GDN_HEADS * HEAD_DIM
    nw = NA_HEADS * HEAD_DIM
    nh = GDN_HEADS

    cv = jnp.zeros((SUBLANES, d), F32).at[:bn].set(c).at[bn].set(c_ctx)
    mods = ada_modulation(cv, ada_w, ada_b)

    def mod_vecs(l, rows):
        m = mods[l, rows].reshape(-1, 6, d)
        return [m[:, k][:, None, :] for k in range(6)]

    sh1, sc1, g1, sh2, sc2, g2 = mod_vecs(0, slice(0, bn))
    csh1, csc1 = [jnp.broadcast_to(v, (bn, 1, d)) for v in mod_vecs(0, slice(bn, bn + 1))[:2]]
    w_in = ev_w_in[0]
    n_gate = 4 * nh
    seg_gate = 2 * gw + 2 * nw
    w_head = w_in[:, :seg_gate].astype(BF16)[None]
    w_tail = w_in[:, seg_gate + n_gate:].astype(BF16)
    n_main = seg_gate + w_tail.shape[1]
    w_ab = jnp.pad(w_in[:, seg_gate:seg_gate + n_gate], ((0, 0), (0, LANES - n_gate))).astype(BF16)
    gparams = jnp.zeros((SUBLANES, LANES), F32)
    gparams = gparams.at[0, :2 * nh].set(ev_a_log[0].reshape(-1)).at[1, :2 * nh].set(ev_dt_bias[0].reshape(-1))
    col_ka, col_va, col_kb, col_vb, col_qa, col_qb, col_za = [k * nh for k in range(7)]

    head_step = lambda b, h: b * nh + h
    n_head_steps = bn * nh
    pc, gates_c = proj(ctx, norm_mix[0], csc1, csh1, w_head, 0, seg_gate, None, seg_gate, w_ab, gparams)
    s_ctx, (w_out0, w_in1, w_out1) = gdn_context(
        pc, ev_conv[0], gate_rows(gates_c, ctx.shape[1]), col_ka, col_va,
        casts=[row_slab_job(ev_w_out, 0, n_head_steps, head_step),
               row_slab_job(od_w_in, 0, n_head_steps, head_step),
               row_slab_job(od_w_out, 0, n_head_steps, head_step)])

    p, gates = proj(x, norm_mix[0], sc1, sh1, w_head, 0, seg_gate, w_tail, n_main, w_ab, gparams)
    cosf, sinf = rope_tables(t)
    y_gdn = gdn_latent(p, ev_conv[0], gate_rows(gates, t), cosf, sinf, s_ctx, ev_gdn_norm[0],
                       col_ka, col_va, col_qa, col_za)
    y_na, ffn0 = natten(p, pc, natten_bias(ev_rpb[0], t // GRID_W), col_qb, col_kb, col_vb, col_kb, col_vb,
                        casts=[row_slab_job(w, 0, n_head_steps, head_step)
                               for w in (ffn_w_gate, ffn_w_up, ffn_w_down)])
    x_lat, hs = outproj(x, g1, [y_gdn, y_na], w_out0, norm_ffn[0], sc2, sh2)
    _, tf, grid = ffn_grid(x_lat, ffn_w_gate.shape[2])
    x_lat, ffn1 = ffn(x_lat, hs, g2, final_norm, *ffn0, final_norm=False,
                      casts=ffn_weight_jobs(ffn_w_gate, ffn_w_up, ffn_w_down, 1, grid, tf))

    sh1, sc1, g1, sh2, sc2, g2 = mod_vecs(1, slice(0, bn))
    p = proj(x_lat, norm_mix[1], sc1, sh1, w_in1[None], 0, 3 * d, None, 3 * d)
    z = zmix(p, od_conv[0])
    x_lat, hs = outproj(x_lat, g1, [z], w_out1, norm_ffn[1], sc2, sh2)
    x_lat, _ = ffn(x_lat, hs, g2, final_norm, *ffn1, final_norm=True)
    return x_lat
```

```python
import functools
import math

import jax
import jax.numpy as jnp
import numpy as np
from jax import lax
from jax.experimental import pallas as pl
from jax.experimental.pallas import tpu as pltpu

F32 = jnp.float32
BF16 = jnp.bfloat16
HIGHEST = lax.Precision.HIGHEST

LANES = 128
SUBLANES = 8
VMEM_LIMIT = 56 * 1024 * 1024

GRID_W = 64
HEAD_DIM = 128
GDN_HEADS = 8
NA_HEADS = 8
GDN_CONV = 5
CHUNK = 64
INVERSE_LEVELS = 6
GROUP = 2 * CHUNK
PREP_GROUPS_PER_ITER = 4
GDN_ROW_BLOCK = 512
NA_ROWS_PER_ITER = 8
NA_KH = 8
NA_KW = 16
SC_CONV = 3
ROPE_THETA = 10000.0
NORM_EPS = 1e-6
NEG_BIG = -1e30

NT_DIMS = (((1,), (1,)), ((), ()))


def _cparams(sem):
    return pltpu.CompilerParams(dimension_semantics=sem, vmem_limit_bytes=VMEM_LIMIT)


def _sigmoid(x):
    return 1.0 / (1.0 + jnp.exp(-x))


def _silu(x):
    return x * _sigmoid(x)


def _softplus(x):
    return jnp.maximum(x, 0.0) + jnp.log(1.0 + jnp.exp(-jnp.abs(x)))


def _mm(a, b):
    return jnp.dot(a.astype(BF16), b.astype(BF16), preferred_element_type=F32)


def _mm_nt(a, b):
    return lax.dot_general(a.astype(BF16), b.astype(BF16), NT_DIMS, preferred_element_type=F32)


def _mm_exact(a, b):
    return jnp.dot(a, b, precision=HIGHEST, preferred_element_type=F32)


class CastJob:
    def __init__(self, src, in_block, in_index, out_shape, out_block, out_index):
        self.src = src
        self.in_spec = pl.BlockSpec(in_block, in_index)
        self.out_spec = pl.BlockSpec(out_block, out_index)
        self.out_shape = jax.ShapeDtypeStruct(out_shape, BF16)


def _hosting_casts(body, n_in, n_out, n_cast):
    def kernel(*refs):
        ins, rest = refs[:n_in], refs[n_in:]
        cast_in, rest = rest[:n_cast], rest[n_cast:]
        outs, rest = rest[:n_out], rest[n_out:]
        cast_out, scratch = rest[:n_cast], rest[n_cast:]
        for src, dst in zip(cast_in, cast_out):
            dst[...] = src[...].astype(BF16)
        body(*ins, *outs, *scratch)

    return kernel


def row_slab_job(w, layer, n_steps, step_of, cols=None):
    rows = w.shape[1]
    cols = w.shape[2] if cols is None else cols
    slab = rows // n_steps
    assert slab * n_steps == rows and slab % (2 * SUBLANES) == 0
    return CastJob(w, (None, slab, cols), lambda *idx: (layer, step_of(*idx), 0),
                   (rows, cols), (slab, cols), lambda *idx: (step_of(*idx), 0))


def _call_hosting(body, casts, grid, in_specs, args, out_specs, out_shapes, **kwargs):
    n_in, n_out = len(in_specs), len(out_specs)
    casts = list(casts)
    res = pl.pallas_call(
        _hosting_casts(body, n_in, n_out, len(casts)),
        grid=grid,
        in_specs=list(in_specs) + [c.in_spec for c in casts],
        out_specs=list(out_specs) + [c.out_spec for c in casts],
        out_shape=list(out_shapes) + [c.out_shape for c in casts],
        **kwargs,
    )(*args, *[c.src for c in casts])
    return res[:n_out], res[n_out:]


def _ada_kernel(cv_ref, w_ref, b_ref, o_ref):
    s = _silu(cv_ref[...])
    o_ref[...] = _mm(s, w_ref[...]) + b_ref[...]


def ada_modulation(cv, ada_w, ada_b, tn=1024):
    depth, d, n = ada_w.shape
    return pl.pallas_call(
        _ada_kernel,
        grid=(depth, n // tn),
        in_specs=[
            pl.BlockSpec((SUBLANES, d), lambda l, j: (0, 0)),
            pl.BlockSpec((None, d, tn), lambda l, j: (l, 0, j)),
            pl.BlockSpec((None, 1, tn), lambda l, j: (l, 0, j)),
        ],
        out_specs=pl.BlockSpec((None, SUBLANES, tn), lambda l, j: (l, 0, j)),
        out_shape=jax.ShapeDtypeStruct((depth, SUBLANES, n), F32),
        compiler_params=_cparams(("parallel", "parallel")),
        name="ada",
    )(cv, ada_w, ada_b.reshape(depth, 1, n))


def _modulated_norm(x, g, sc, sh):
    y = x * lax.rsqrt(jnp.mean(x * x, axis=-1, keepdims=True) + NORM_EPS)
    return (y * g) * (1.0 + sc) + sh


NORM_ROW_BLOCK = 128


def _modulated_norm_rows(x_ref, g_ref, sc_ref, sh_ref, hs_ref):
    rb = min(NORM_ROW_BLOCK, x_ref.shape[0])

    def body(i, _):
        rows = pl.ds(pl.multiple_of(i * rb, rb), rb)
        hs_ref[rows, :] = _modulated_norm(x_ref[rows, :], g_ref[...], sc_ref[...], sh_ref[...]).astype(BF16)
        return 0

    lax.fori_loop(0, x_ref.shape[0] // rb, body, 0)


def _proj_kernel(*refs, with_gates, with_tail, head_blocks):
    x_ref, g_ref, sc_ref, sh_ref, w_ref = refs[:5]
    refs = refs[5:]
    wt_ref = None
    if with_tail:
        wt_ref, refs = refs[0], refs[1:]
    if with_gates:
        wab_ref, gp_ref, o_ref, gate_ref, hs_ref, wb_ref = refs
    else:
        o_ref, hs_ref, wb_ref = refs
    j = pl.program_id(2)

    @pl.when(j == 0)
    def _():
        if with_gates:
            hb = _modulated_norm(x_ref[...], g_ref[...], sc_ref[...], sh_ref[...]).astype(BF16)
            hs_ref[...] = hb
            a = jnp.dot(hb, wab_ref[...], preferred_element_type=F32)
            neg_decay_rate = -jnp.exp(gp_ref[0:1, :])
            g = neg_decay_rate * _softplus(a + gp_ref[1:2, :])
            lane = lax.broadcasted_iota(jnp.int32, a.shape, 1)
            gate_ref[...] = jnp.where(lane < 2 * GDN_HEADS, g, _sigmoid(a))
        else:
            _modulated_norm_rows(x_ref, g_ref, sc_ref, sh_ref, hs_ref)

    def emit(wref):
        if wref.dtype != BF16:
            wb_ref[...] = wref[...].astype(BF16)
            wref = wb_ref
        o_ref[...] = jnp.dot(hs_ref[...], wref[...], preferred_element_type=F32).astype(o_ref.dtype)

    pl.when(j < head_blocks)(lambda: emit(w_ref))
    if with_tail:
        pl.when(j >= head_blocks)(lambda: emit(wt_ref))


def proj(x, g, sc, sh, w, layer, head_cols, w_tail, n_out, wab=None, gparams=None, tm=1024, tn=512):
    bn, t, d = x.shape
    tm = min(tm, t)
    head_blocks = head_cols // tn
    with_gates = wab is not None
    vec = pl.BlockSpec((None, 1, d), lambda b, i, j: (b, 0, 0))
    in_specs = [
        pl.BlockSpec((None, tm, d), lambda b, i, j: (b, i, 0)),
        pl.BlockSpec((1, d), lambda b, i, j: (0, 0)),
        vec, vec,
        pl.BlockSpec((None, d, tn), lambda b, i, j: (layer, 0, jnp.minimum(j, head_blocks - 1))),
    ]
    args = [x, g.reshape(1, d), sc, sh, w]
    with_tail = w_tail is not None
    if with_tail:
        in_specs.append(pl.BlockSpec((d, tn), lambda b, i, j: (0, jnp.maximum(j - head_blocks, 0))))
        args.append(w_tail)
    out_specs = [pl.BlockSpec((None, tm, tn), lambda b, i, j: (b, i, j))]
    out_shape = [jax.ShapeDtypeStruct((bn, t, n_out), BF16)]
    if with_gates:
        in_specs += [pl.BlockSpec((d, LANES), lambda b, i, j: (0, 0)),
                     pl.BlockSpec((SUBLANES, LANES), lambda b, i, j: (0, 0))]
        args += [wab, gparams]
        out_specs.append(pl.BlockSpec((None, tm, LANES), lambda b, i, j: (b, i, 0)))
        out_shape.append(jax.ShapeDtypeStruct((bn, t, LANES), F32))
    res = pl.pallas_call(
        functools.partial(_proj_kernel, with_gates=with_gates, with_tail=with_tail, head_blocks=head_blocks),
        grid=(bn, t // tm, n_out // tn),
        in_specs=in_specs,
        out_specs=out_specs,
        out_shape=out_shape,
        scratch_shapes=[pltpu.VMEM((tm, d), BF16), pltpu.VMEM((d, tn), BF16)],
        compiler_params=_cparams(("parallel", "parallel", "arbitrary")),
        name="proj",
    )(*args)
    return res if with_gates else res[0]


CONV_HALO = SUBLANES


def _stage_padded(pad_ref, load_rows, t, rb):
    zeros = jnp.zeros((CONV_HALO, LANES), F32)
    pad_ref[0:CONV_HALO, :] = zeros
    pad_ref[CONV_HALO + t:2 * CONV_HALO + t, :] = zeros

    def body(i, _):
        r0 = pl.multiple_of(i * rb, rb)
        pad_ref[pl.ds(CONV_HALO + r0, rb), :] = load_rows(r0)
        return 0

    lax.fori_loop(0, t // rb, body, 0)


def _conv_rows(pad_ref, w_ref, r0, rb, taps):
    acc = None
    for j in range(taps):
        xj = pad_ref[pl.ds(r0 + (CONV_HALO + j - taps // 2), rb), :]
        term = xj * w_ref[j:j + 1, :]
        acc = term if acc is None else acc + term
    return acc


def _l2norm(x):
    return x * lax.rsqrt(jnp.sum(x * x, axis=-1, keepdims=True) + NORM_EPS)


def _gdn_kernel(*refs, t, with_q):
    nc = t // CHUNK
    rb = min(GDN_ROW_BLOCK, t)
    if with_q:
        (ka_ref, va_ref, qa_ref, za_ref, wk_ref, wv_ref, wq_ref, grow_ref, cos_ref, sin_ref, s0_ref,
         gn_ref, y_ref,
         pad_ref, k_ref, v_ref, q_ref, o_ref, gc_ref, kwq_s, h_s, au_s, gl_s) = refs
    else:
        (ka_ref, va_ref, wk_ref, wv_ref, grow_ref, sfin_ref,
         pad_ref, k_ref, v_ref, gc_ref, kwq_s, h_s, gl_s) = refs

    def conv_all(src_ref, w_ref, finish, dst_ref):
        _stage_padded(pad_ref, lambda r0: src_ref[pl.ds(r0, rb), :].astype(F32), t, rb)

        def body(i, _):
            r0 = pl.multiple_of(i * rb, rb)
            y = _silu(_conv_rows(pad_ref, w_ref, r0, rb, GDN_CONV))
            dst_ref[pl.ds(r0, rb), :] = finish(y, r0)
            return 0

        lax.fori_loop(0, t // rb, body, 0)

    def rope(x, r0):
        return (x * cos_ref[pl.ds(r0, rb), :]
                + pltpu.roll(x, HEAD_DIM // 2, axis=1) * sin_ref[pl.ds(r0, rb), :])

    if with_q:
        conv_all(ka_ref, wk_ref, lambda y, r0: rope(_l2norm(y), r0), k_ref)
        conv_all(qa_ref, wq_ref, lambda y, r0: rope(_l2norm(y), r0) * HEAD_DIM ** -0.5, q_ref)
    else:
        conv_all(ka_ref, wk_ref, lambda y, r0: _l2norm(y), k_ref)
    conv_all(va_ref, wv_ref, lambda y, r0: y, v_ref)

    ii = lax.broadcasted_iota(jnp.int32, (GROUP, GROUP), 0)
    jj = lax.broadcasted_iota(jnp.int32, (GROUP, GROUP), 1)
    same = (ii // CHUNK) == (jj // CHUNK)
    lower = same & (ii >= jj)
    upper = same & (ii <= jj)
    eye = ii == jj
    gc_ref[0] = _mm_exact(grow_ref[0], upper.astype(F32))
    gc_ref[1] = _mm_exact(grow_ref[1], lower.astype(F32))

    n_groups = t // GROUP
    gpi = min(PREP_GROUPS_PER_ITER, n_groups)
    n_sets = n_groups // gpi
    steps_per_set = 2 * gpi

    def prep_stages(i):
        chains = ([(i * gpi + k, 0) for k in range(gpi)]
                  + [(n_groups - 1 - (i * gpi + k), 1) for k in range(gpi)])
        ids = range(len(chains))
        st = {}

        def load():
            for ch, (g, d) in enumerate(chains):
                r0 = pl.multiple_of(g * GROUP, GROUP)
                st["k", ch] = k_ref[pl.ds(r0, GROUP), :]
                st["v", ch] = v_ref[pl.ds(r0, GROUP), :]
                kgb = st["k", ch].astype(BF16)
                if with_q:
                    st["q", ch] = q_ref[pl.ds(r0, GROUP), :]
                    both = _mm_nt(jnp.concatenate([st["k", ch], st["q", ch]], axis=0), kgb)
                    st["kk", ch], st["qk", ch] = both[:GROUP], both[GROUP:]
                else:
                    st["kk", ch] = _mm_nt(kgb, kgb)

        def masks():
            for ch, (g, d) in enumerate(chains):
                incl = lower if d == 0 else upper
                gc_r = jnp.broadcast_to(gc_ref[d, pl.ds(g, 1), :], (GROUP, GROUP))
                be_r = jnp.broadcast_to(grow_ref[2 + d, pl.ds(g, 1), :], (GROUP, GROUP))
                gc_c = gc_r.T
                decay = jnp.where(incl, jnp.exp(jnp.where(incl, gc_c - gc_r, 0.0)), 0.0)
                m = jnp.where(incl & jnp.logical_not(eye), st["kk", ch] * be_r.T * decay, 0.0)
                st["be_r", ch], st["gc_c", ch], st["decay", ch] = be_r, gc_c, decay
                st["p", ch] = jnp.where(eye, 1.0, -m)
                st["sq", ch] = _mm(m, m)

        def level():
            for ch in ids:
                p, sq = st["p", ch], st["sq", ch]
                st["p", ch] = p + _mm(p, sq)
                st["sq", ch] = _mm(sq, sq)

        def last_level():
            for ch in ids:
                st["p", ch] = st["p", ch] + _mm(st["p", ch], st["sq", ch])

        def solve():
            for ch in ids:
                rhs = jnp.concatenate([st["v", ch], st["k", ch] * jnp.exp(st["gc_c", ch])], axis=1)
                st["uw", ch] = _mm(st["p", ch] * st["be_r", ch], rhs)

        def fold():
            for ch, (g, d) in enumerate(chains):
                gc_c = st["gc_c", ch]
                last = [CHUNK - 1, GROUP - 1] if d == 0 else [0, CHUNK]
                gls = [gc_c[r:r + 1, :] for r in last]
                gl_c = jnp.concatenate([jnp.broadcast_to(x, (CHUNK, LANES)) for x in gls], axis=0)
                kdt = (st["k", ch] * jnp.exp(gl_c - gc_c)).T
                parts = [jnp.where(jj < CHUNK, kdt, 0.0), jnp.where(jj >= CHUNK, kdt, 0.0)]
                if with_q:
                    st["qd", ch] = st["q", ch] * jnp.exp(gc_c)
                    parts = [st["qk", ch] * st["decay", ch]] + parts
                st["gl", ch] = gls
                st["prod", ch] = _mm(jnp.concatenate(parts, axis=0), st["uw", ch])

        def store():
            for ch, (g, d) in enumerate(chains):
                r = st["prod", ch]
                off = GROUP if with_q else 0
                for half in range(2):
                    c = 2 * g + half
                    blk = r[off + half * HEAD_DIM:off + (half + 1) * HEAD_DIM]
                    h_s[d, c] = blk[:, :HEAD_DIM]
                    gl_s[d, c] = jnp.broadcast_to(jnp.exp(st["gl", ch][half]), (SUBLANES, LANES))
                    if with_q:
                        rs = slice(half * CHUNK, (half + 1) * CHUNK)
                        au_s[d, c] = r[rs, :HEAD_DIM]
                        qw = st["qd", ch][rs] - r[rs, HEAD_DIM:]
                        kwq_s[d, c] = jnp.concatenate([blk[:, HEAD_DIM:], qw], axis=0).astype(BF16)
                    else:
                        kwq_s[d, c] = blk[:, HEAD_DIM:].astype(BF16)

        return [load, masks] + [level] * (INVERSE_LEVELS - 2) + [last_level, solve, fold, store]

    def scan_step(s, carry):
        cs = (s, nc - 1 - s)
        rr = [jnp.dot(kwq_s[d, cs[d]], carry[d].astype(BF16), preferred_element_type=F32) for d in range(2)]
        new = []
        for d in range(2):
            c = cs[d]
            new.append(carry[d] * gl_s[d, c][0:1, :] + h_s[d, c] - rr[d][:HEAD_DIM])
            if with_q:
                r0 = pl.multiple_of(c * CHUNK, CHUNK)
                o_ref[pl.ds(r0, CHUNK), :] += rr[d][HEAD_DIM:] + au_s[d, c]
        return tuple(new)

    def run(stages, steps, carry):
        for idx in range(max(len(stages), len(steps))):
            if idx < len(stages):
                stages[idx]()
            if idx < len(steps):
                carry = scan_step(steps[idx], carry)
        return carry

    def set_steps(i):
        return [i * steps_per_set + k for k in range(steps_per_set)]

    if with_q:
        o_ref[...] = jnp.zeros_like(o_ref)
        carry = (s0_ref[0], s0_ref[1])
    else:
        zero = jnp.zeros((HEAD_DIM, HEAD_DIM), F32)
        carry = (zero, zero)
    run(prep_stages(0), [], carry)
    carry = lax.fori_loop(1, n_sets, lambda i, c: run(prep_stages(i), set_steps(i - 1), c), carry)
    s_f, s_b = run([], set_steps(n_sets - 1), carry)

    if not with_q:
        sfin_ref[0] = s_f
        sfin_ref[1] = s_b
        return

    def finish(i, _):
        r0 = pl.multiple_of(i * rb, rb)
        o = o_ref[pl.ds(r0, rb), :]
        y = o * lax.rsqrt(jnp.mean(o * o, axis=-1, keepdims=True) + NORM_EPS)
        y_ref[pl.ds(r0, rb), :] = ((y * gn_ref[...]) * _silu(za_ref[pl.ds(r0, rb), :].astype(F32))).astype(y_ref.dtype)
        return 0

    lax.fori_loop(0, t // rb, finish, 0)


def _single(block_shape, index_map):
    return pl.BlockSpec(block_shape, index_map, pipeline_mode=pl.Buffered(1))


def gdn_latent(p, conv_w, grow, cosf, sinf, s0, gnorm, col_k, col_v, col_q, col_z):
    bn, t, _ = p.shape
    nc = t // CHUNK
    ngp = grow.shape[3]
    hd = HEAD_DIM
    seq = lambda col: _single((None, t, hd), lambda b, h, col=col: (b, 0, col + h))
    cw = lambda col: pl.BlockSpec((GDN_CONV, hd), lambda b, h, col=col: (0, col + h))
    return pl.pallas_call(
        functools.partial(_gdn_kernel, t=t, with_q=True),
        grid=(bn, GDN_HEADS),
        in_specs=[
            seq(col_k), seq(col_v), seq(col_q), seq(col_z),
            cw(0), cw(GDN_HEADS), cw(2 * GDN_HEADS),
            pl.BlockSpec((None, None, 4, ngp, GROUP), lambda b, h: (b, h, 0, 0, 0)),
            _single((t, hd), lambda b, h: (0, 0)),
            _single((t, hd), lambda b, h: (0, 0)),
            pl.BlockSpec((None, None, 2, hd, hd), lambda b, h: (b, h, 0, 0, 0)),
            pl.BlockSpec((1, hd), lambda b, h: (0, 0)),
        ],
        out_specs=pl.BlockSpec((None, t, hd), lambda b, h: (b, 0, h)),
        out_shape=jax.ShapeDtypeStruct((bn, t, GDN_HEADS * hd), BF16),
        scratch_shapes=[
            pltpu.VMEM((t + 2 * CONV_HALO, hd), F32),
            pltpu.VMEM((t, hd), F32),
            pltpu.VMEM((t, hd), F32),
            pltpu.VMEM((t, hd), F32),
            pltpu.VMEM((t, hd), F32),
            pltpu.VMEM((2, ngp, GROUP), F32),
            pltpu.VMEM((2, nc, hd + CHUNK, hd), BF16),
            pltpu.VMEM((2, nc, hd, hd), F32),
            pltpu.VMEM((2, nc, CHUNK, hd), F32),
            pltpu.VMEM((2, nc, SUBLANES, LANES), F32),
        ],
        compiler_params=_cparams(("parallel", "parallel")),
        name="gdn_latent",
    )(p, p, p, p, conv_w, conv_w, conv_w, grow, cosf, sinf, s0, gnorm.reshape(1, hd))


def gdn_context(pc, conv_w, grow, col_k, col_v, casts=()):
    bn, t, _ = pc.shape
    nc = t // CHUNK
    ngp = grow.shape[3]
    hd = HEAD_DIM
    seq = lambda col: pl.BlockSpec((None, t, hd), lambda b, h, col=col: (b, 0, col + h))
    cw = lambda col: pl.BlockSpec((GDN_CONV, hd), lambda b, h, col=col: (0, col + h))
    (states,), rounded = _call_hosting(
        lambda *refs: _gdn_kernel(*refs, t=t, with_q=False), casts,
        grid=(bn, GDN_HEADS),
        in_specs=[
            seq(col_k), seq(col_v), cw(0), cw(GDN_HEADS),
            pl.BlockSpec((None, None, 4, ngp, GROUP), lambda b, h: (b, h, 0, 0, 0)),
        ],
        args=(pc, pc, conv_w, conv_w, grow),
        out_specs=[pl.BlockSpec((None, None, 2, hd, hd), lambda b, h: (b, h, 0, 0, 0))],
        out_shapes=[jax.ShapeDtypeStruct((bn, GDN_HEADS, 2, hd, hd), F32)],
        scratch_shapes=[
            pltpu.VMEM((t + 2 * CONV_HALO, hd), F32),
            pltpu.VMEM((t, hd), F32),
            pltpu.VMEM((t, hd), F32),
            pltpu.VMEM((2, ngp, GROUP), F32),
            pltpu.VMEM((2, nc, hd, hd), BF16),
            pltpu.VMEM((2, nc, hd, hd), F32),
            pltpu.VMEM((2, nc, SUBLANES, LANES), F32),
        ],
        compiler_params=_cparams(("parallel", "parallel")),
        name="gdn_context",
    )
    return states, rounded


def gate_rows(gates, t):
    bn = gates.shape[0]
    ng = t // GROUP
    g = gates[:, :, :4 * GDN_HEADS].reshape(bn, ng, GROUP, 4, GDN_HEADS)
    g = jnp.transpose(g, (0, 4, 3, 1, 2))
    if ng < SUBLANES:
        g = jnp.pad(g, ((0, 0), (0, 0), (0, 0), (0, SUBLANES - ng), (0, 0)))
    return g


def _natten_kernel(q_ref, k_ref, v_ref, kc_ref, vc_ref, bias_ref, o_ref, *, rows):
    kcb = kc_ref[...]
    vcb = vc_ref[...]
    scale = HEAD_DIM ** -0.5
    win = NA_KH * GRID_W
    per_iter = min(NA_ROWS_PER_ITER, rows)

    def body(it, _):
        rws = [it * per_iter + k for k in range(per_iter)]
        q0s, k0s, s_loc, s_ctx = [], [], [], []
        for r in rws:
            rs = jnp.clip(r - NA_KH // 2, 0, rows - NA_KH)
            q0s.append(pl.multiple_of(r * GRID_W, GRID_W))
            k0s.append(pl.multiple_of(rs * GRID_W, GRID_W))
            q = q_ref[pl.ds(q0s[-1], GRID_W), :]
            kw = k_ref[pl.ds(k0s[-1], win), :]
            s_loc.append(lax.dot_general(q, kw, NT_DIMS, preferred_element_type=F32) * scale + bias_ref[r - rs])
            s_ctx.append(lax.dot_general(q, kcb, NT_DIMS, preferred_element_type=F32) * scale)
        p_loc, p_ctx, den = [], [], []
        for a, b in zip(s_loc, s_ctx):
            m = jnp.maximum(jnp.max(a, axis=-1, keepdims=True), jnp.max(b, axis=-1, keepdims=True))
            p_loc.append(jnp.exp(a - m))
            p_ctx.append(jnp.exp(b - m))
            den.append(jnp.sum(p_loc[-1], axis=-1, keepdims=True) + jnp.sum(p_ctx[-1], axis=-1, keepdims=True))
        outs = []
        for k in range(per_iter):
            vw = v_ref[pl.ds(k0s[k], win), :]
            outs.append(jnp.dot(p_loc[k].astype(BF16), vw, preferred_element_type=F32)
                        + jnp.dot(p_ctx[k].astype(BF16), vcb, preferred_element_type=F32))
        for k in range(per_iter):
            o_ref[pl.ds(q0s[k], GRID_W), :] = (outs[k] / den[k]).astype(o_ref.dtype)
        return 0

    lax.fori_loop(0, rows // per_iter, body, 0)


def natten(p, pc, bias, col_q, col_k, col_v, ctx_col_k, ctx_col_v, casts=()):
    bn, t, _ = p.shape
    lc = pc.shape[1]
    hd = HEAD_DIM
    rows = t // GRID_W
    seq = lambda col: pl.BlockSpec((None, t, hd), lambda b, h, col=col: (b, 0, col + h))
    cseq = lambda col: pl.BlockSpec((None, lc, hd), lambda b, h, col=col: (b, 0, col + h))
    (y,), rounded = _call_hosting(
        lambda *refs: _natten_kernel(*refs, rows=rows), casts,
        grid=(bn, NA_HEADS),
        in_specs=[
            seq(col_q), seq(col_k), seq(col_v), cseq(ctx_col_k), cseq(ctx_col_v),
            pl.BlockSpec((None, NA_KH, GRID_W, NA_KH * GRID_W), lambda b, h: (h, 0, 0, 0)),
        ],
        args=(p, p, p, pc, pc, bias),
        out_specs=[pl.BlockSpec((None, t, hd), lambda b, h: (b, 0, h))],
        out_shapes=[jax.ShapeDtypeStruct((bn, t, NA_HEADS * hd), BF16)],
        compiler_params=_cparams(("parallel", "parallel")),
        name="natten",
    )
    return y, rounded


def natten_bias(rpb, rows):
    kh = min(NA_KH, rows)
    hn, n_dr, n_dc = rpb.shape
    col = np.arange(GRID_W)
    col_start = np.clip(col - NA_KW // 2, 0, GRID_W - NA_KW)
    in_win = (col[None, :] >= col_start[:, None]) & (col[None, :] < col_start[:, None] + NA_KW)
    dc = np.clip(col[None, :] - col[:, None], -(NA_KW - 1), NA_KW - 1) + (NA_KW - 1)
    onehot = (np.arange(n_dc)[:, None] == dc.reshape(-1)[None, :]).astype(np.float32)
    by_dr = jnp.dot(rpb.astype(F32).reshape(hn * n_dr, n_dc), jnp.asarray(onehot), precision=HIGHEST)
    by_dr = jnp.where(in_win[None, None], by_dr.reshape(hn, n_dr, GRID_W, GRID_W), NEG_BIG)
    b = jnp.stack([by_dr[:, NA_KH - 1 - cls:NA_KH - 1 - cls + kh] for cls in range(NA_KH)], axis=1)
    return jnp.transpose(b, (0, 1, 3, 2, 4)).reshape(hn, NA_KH, GRID_W, kh * GRID_W)


def _zmix_kernel(gb_ref, gc_ref, val_ref, w_ref, z_ref, pad_ref, *, t):
    rb = min(256, t)
    _stage_padded(pad_ref, lambda r0: gc_ref[pl.ds(r0, rb), :].astype(F32) * val_ref[pl.ds(r0, rb), :].astype(F32),
                  t, rb)

    def body(i, _):
        r0 = pl.multiple_of(i * rb, rb)
        y = _conv_rows(pad_ref, w_ref, r0, rb, SC_CONV)
        z_ref[pl.ds(r0, rb), :] = (gb_ref[pl.ds(r0, rb), :].astype(F32) * y).astype(z_ref.dtype)
        return 0

    lax.fori_loop(0, t // rb, body, 0)


def zmix(p, conv_w):
    bn, t, n3 = p.shape
    d = n3 // 3
    nb = d // LANES
    seq = lambda off: pl.BlockSpec((None, t, LANES), lambda b, j, off=off: (b, 0, off + j))
    return pl.pallas_call(
        functools.partial(_zmix_kernel, t=t),
        grid=(bn, nb),
        in_specs=[seq(0), seq(nb), seq(2 * nb), pl.BlockSpec((SC_CONV, LANES), lambda b, j: (0, j))],
        out_specs=pl.BlockSpec((None, t, LANES), lambda b, j: (b, 0, j)),
        out_shape=jax.ShapeDtypeStruct((bn, t, d), BF16),
        scratch_shapes=[pltpu.VMEM((t + 2 * CONV_HALO, LANES), F32)],
        compiler_params=_cparams(("parallel", "parallel")),
        name="zmix",
    )(p, p, p, conv_w)


def _outproj_kernel(*refs, n_in):
    x_ref, gate_ref, g_ref, sc_ref, sh_ref = refs[:5]
    a_refs = refs[5:5 + n_in]
    w_refs = refs[5 + n_in:5 + 2 * n_in]
    o_ref, hs_ref = refs[5 + 2 * n_in:]
    y = None
    for a_ref, w_ref in zip(a_refs, w_refs):
        part = jnp.dot(a_ref[...], w_ref[...], preferred_element_type=F32)
        y = part if y is None else y + part
    x1 = x_ref[...] + gate_ref[...] * y
    o_ref[...] = x1
    hs_ref[...] = _modulated_norm(x1, g_ref[...], sc_ref[...], sh_ref[...]).astype(BF16)


def outproj(x, gate, acts, weight, g_next, sc_next, sh_next, tm=512):
    bn, t, d = x.shape
    n_in = len(acts)
    kdim = acts[0].shape[-1]
    assert all(a.shape[-1] == kdim for a in acts) and weight.shape[0] == n_in * kdim
    weights = [weight] * n_in
    row = pl.BlockSpec((None, tm, d), lambda b, i: (b, i, 0))
    vec = pl.BlockSpec((None, 1, d), lambda b, i: (b, 0, 0))
    in_specs = [row, vec, pl.BlockSpec((1, d), lambda b, i: (0, 0)), vec, vec]
    in_specs += [pl.BlockSpec((None, tm, kdim), lambda b, i: (b, i, 0)) for a in acts]
    in_specs += [pl.BlockSpec((kdim, d), lambda b, i, k=k: (k, 0)) for k in range(n_in)]
    return pl.pallas_call(
        functools.partial(_outproj_kernel, n_in=n_in),
        grid=(bn, t // tm),
        in_specs=in_specs,
        out_specs=[row, row],
        out_shape=[jax.ShapeDtypeStruct((bn, t, d), F32), jax.ShapeDtypeStruct((bn, t, d), BF16)],
        compiler_params=_cparams(("parallel", "parallel")),
        name="outproj",
    )(x, gate, g_next.reshape(1, d), sc_next, sh_next, *acts, *weights)


def _ffn_kernel(x_ref, hs_ref, gate_ref, fn_ref, wg_ref, wu_ref, wd_ref, o_ref, *, final_norm):
    j = pl.program_id(2)

    @pl.when(j == 0)
    def _():
        o_ref[...] = jnp.zeros_like(o_ref)

    h = hs_ref[...]
    gate = jnp.dot(h, wg_ref[...], preferred_element_type=F32)
    up = jnp.dot(h, wu_ref[...], preferred_element_type=F32)
    a = (_silu(gate) * up).astype(BF16)
    o_ref[...] += jnp.dot(a, wd_ref[...], preferred_element_type=F32)

    @pl.when(j == pl.num_programs(2) - 1)
    def _():
        rb = min(NORM_ROW_BLOCK, o_ref.shape[0])

        def body(i, _):
            rows = pl.ds(pl.multiple_of(i * rb, rb), rb)
            y = x_ref[rows, :] + gate_ref[...] * o_ref[rows, :]
            if final_norm:
                y = (y * lax.rsqrt(jnp.mean(y * y, axis=-1, keepdims=True) + NORM_EPS)) * fn_ref[...]
            o_ref[rows, :] = y
            return 0

        lax.fori_loop(0, o_ref.shape[0] // rb, body, 0)


def ffn_grid(x, f, tm=1024, tf=512):
    bn, t, _ = x.shape
    tm = min(tm, t)
    return tm, tf, (bn, t // tm, f // tf)


def ffn_weight_jobs(w_gate, w_up, w_down, layer, grid, tf):
    bn, ni, nj = grid
    d = w_gate.shape[1]
    slab = d // (bn * ni)
    assert slab * bn * ni == d and slab % LANES == 0
    tile = lambda b, i: b * ni + i
    up_job = lambda w: CastJob(w, (None, slab, tf), lambda b, i, j: (layer, tile(b, i), j),
                               w.shape[1:], (slab, tf), lambda b, i, j: (tile(b, i), j))
    down_job = CastJob(w_down, (None, tf, slab), lambda b, i, j: (layer, j, tile(b, i)),
                       w_down.shape[1:], (tf, slab), lambda b, i, j: (j, tile(b, i)))
    return [up_job(w_gate), up_job(w_up), down_job]


def ffn(x, hs, gate, fnorm, w_gate, w_up, w_down, final_norm, casts=()):
    bn, t, d = x.shape
    f = w_gate.shape[1]
    tm, tf, grid = ffn_grid(x, f)
    vec = pl.BlockSpec((None, 1, d), lambda b, i, j: (b, 0, 0))
    one = pl.BlockSpec((1, d), lambda b, i, j: (0, 0))
    (y,), rounded = _call_hosting(
        functools.partial(_ffn_kernel, final_norm=final_norm), casts,
        grid=grid,
        in_specs=[
            _single((None, tm, d), lambda b, i, j: (b, i, 0)),
            pl.BlockSpec((None, tm, d), lambda b, i, j: (b, i, 0)),
            vec, one,
            pl.BlockSpec((d, tf), lambda b, i, j: (0, j)),
            pl.BlockSpec((d, tf), lambda b, i, j: (0, j)),
            pl.BlockSpec((tf, d), lambda b, i, j: (j, 0)),
        ],
        args=(x, hs, gate, fnorm.reshape(1, d), w_gate, w_up, w_down),
        out_specs=[pl.BlockSpec((None, tm, d), lambda b, i, j: (b, i, 0))],
        out_shapes=[jax.ShapeDtypeStruct((bn, t, d), F32)],
        compiler_params=_cparams(("parallel", "parallel", "arbitrary")),
        name="ffn",
    )
    return y, rounded


def rope_tables(t):
    pos = np.arange(t)
    row = (pos // GRID_W).astype(np.float32)
    col = (pos % GRID_W).astype(np.float32)
    n_freq = HEAD_DIM // 4
    inv_freq = jnp.asarray(ROPE_THETA, F32) ** (-jnp.arange(n_freq, dtype=F32) / n_freq)
    ang = jnp.concatenate([jnp.asarray(row)[:, None] * inv_freq, jnp.asarray(col)[:, None] * inv_freq], axis=-1)
    cos, sin = jnp.cos(ang), jnp.sin(ang)
    return jnp.concatenate([cos, cos], axis=-1), jnp.concatenate([-sin, sin], axis=-1)


def kernel(x, c, ctx, c_ctx, ada_w, ada_b, norm_mix, norm_ffn, ffn_w_gate, ffn_w_up, ffn_w_down, final_norm,
           ev_w_in, ev_conv, ev_a_log, ev_dt_bias, ev_gdn_norm, ev_rpb, ev_w_out, od_w_in, od_conv, od_w_out):
    bn, t, d = x.shape
    depth = ada_w.shape[0]
    assert depth == 2
    gw = GDN_HEADS * HEAD_DIM
    nw = NA_HEADS * HEAD_DIM
    nh = GDN_HEADS

    cv = jnp.zeros((SUBLANES, d), F32).at[:bn].set(c).at[bn].set(c_ctx)
    mods = ada_modulation(cv, ada_w, ada_b)

    def mod_vecs(l, rows):
        m = mods[l, rows].reshape(-1, 6, d)
        return [m[:, k][:, None, :] for k in range(6)]

    sh1, sc1, g1, sh2, sc2, g2 = mod_vecs(0, slice(0, bn))
    csh1, csc1 = [jnp.broadcast_to(v, (bn, 1, d)) for v in mod_vecs(0, slice(bn, bn + 1))[:2]]
    w_in = ev_w_in[0]
    n_gate = 4 * nh
    seg_gate = 2 * gw + 2 * nw
    w_tail32 = w_in[:, seg_gate + n_gate:][None]
    n_main = seg_gate + w_tail32.shape[2]
    w_ab = jnp.pad(w_in[:, seg_gate:seg_gate + n_gate], ((0, 0), (0, LANES - n_gate))).astype(BF16)
    gparams = jnp.zeros((SUBLANES, LANES), F32)
    gparams = gparams.at[0, :2 * nh].set(ev_a_log[0].reshape(-1)).at[1, :2 * nh].set(ev_dt_bias[0].reshape(-1))
    col_ka, col_va, col_kb, col_vb, col_qa, col_qb, col_za = [k * nh for k in range(7)]

    head_step = lambda b, h: b * nh + h
    n_head_steps = bn * nh
    pc, gates_c = proj(ctx, norm_mix[0], csc1, csh1, ev_w_in, 0, seg_gate, None, seg_gate, w_ab, gparams)
    s_ctx, (w_head, w_tail, w_out0, w_in1, w_out1) = gdn_context(
        pc, ev_conv[0], gate_rows(gates_c, ctx.shape[1]), col_ka, col_va,
        casts=[row_slab_job(ev_w_in, 0, n_head_steps, head_step, cols=seg_gate),
               row_slab_job(w_tail32, 0, n_head_steps, head_step),
               row_slab_job(ev_w_out, 0, n_head_steps, head_step),
               row_slab_job(od_w_in, 0, n_head_steps, head_step),
               row_slab_job(od_w_out, 0, n_head_steps, head_step)])

    p, gates = proj(x, norm_mix[0], sc1, sh1, w_head[None], 0, seg_gate, w_tail, n_main, w_ab, gparams)
    cosf, sinf = rope_tables(t)
    y_gdn = gdn_latent(p, ev_conv[0], gate_rows(gates, t), cosf, sinf, s_ctx, ev_gdn_norm[0],
                       col_ka, col_va, col_qa, col_za)
    y_na, ffn0 = natten(p, pc, natten_bias(ev_rpb[0], t // GRID_W), col_qb, col_kb, col_vb, col_kb, col_vb,
                        casts=[row_slab_job(w, 0, n_head_steps, head_step)
                               for w in (ffn_w_gate, ffn_w_up, ffn_w_down)])
    x_lat, hs = outproj(x, g1, [y_gdn, y_na], w_out0, norm_ffn[0], sc2, sh2)
    _, tf, grid = ffn_grid(x_lat, ffn_w_gate.shape[2])
    x_lat, ffn1 = ffn(x_lat, hs, g2, final_norm, *ffn0, final_norm=False,
                      casts=ffn_weight_jobs(ffn_w_gate, ffn_w_up, ffn_w_down, 1, grid, tf))

    sh1, sc1, g1, sh2, sc2, g2 = mod_vecs(1, slice(0, bn))
    p = proj(x_lat, norm_mix[1], sc1, sh1, w_in1[None], 0, 3 * d, None, 3 * d)
    z = zmix(p, od_conv[0])
    x_lat, hs = outproj(x_lat, g1, [z], w_out1, norm_ffn[1], sc2, sh2)
    x_lat, _ = ffn(x_lat, hs, g2, final_norm, *ffn1, final_norm=True)
    return x_lat
```

```python
import functools
import math

import jax
import jax.numpy as jnp
import numpy as np
from jax import lax
from jax.experimental import pallas as pl
from jax.experimental.pallas import tpu as pltpu

F32 = jnp.float32
BF16 = jnp.bfloat16
HIGHEST = lax.Precision.HIGHEST

LANES = 128
SUBLANES = 8
VMEM_LIMIT = 56 * 1024 * 1024

GRID_W = 64
HEAD_DIM = 128
GDN_HEADS = 8
NA_HEADS = 8
GDN_CONV = 5
CHUNK = 64
INVERSE_LEVELS = 6
GROUP = 2 * CHUNK
PREP_GROUPS_PER_ITER = 4
GDN_ROW_BLOCK = 512
NA_ROWS_PER_ITER = 8
NA_KH = 8
NA_KW = 16
SC_CONV = 3
ROPE_THETA = 10000.0
NORM_EPS = 1e-6
NEG_BIG = -1e30

NT_DIMS = (((1,), (1,)), ((), ()))


def _cparams(sem):
    return pltpu.CompilerParams(dimension_semantics=sem, vmem_limit_bytes=VMEM_LIMIT)


def _sigmoid(x):
    return 1.0 / (1.0 + jnp.exp(-x))


def _silu(x):
    return x * _sigmoid(x)


def _softplus(x):
    return jnp.maximum(x, 0.0) + jnp.log(1.0 + jnp.exp(-jnp.abs(x)))


def _mm(a, b):
    return jnp.dot(a.astype(BF16), b.astype(BF16), preferred_element_type=F32)


def _mm_nt(a, b):
    return lax.dot_general(a.astype(BF16), b.astype(BF16), NT_DIMS, preferred_element_type=F32)


def _mm_exact(a, b):
    return jnp.dot(a, b, precision=HIGHEST, preferred_element_type=F32)


class CastJob:
    def __init__(self, src, in_block, in_index, out_shape, out_block, out_index):
        self.src = src
        self.in_spec = pl.BlockSpec(in_block, in_index)
        self.out_spec = pl.BlockSpec(out_block, out_index)
        self.out_shape = jax.ShapeDtypeStruct(out_shape, BF16)


def _hosting_casts(body, n_in, n_out, n_cast):
    def kernel(*refs):
        ins, rest = refs[:n_in], refs[n_in:]
        cast_in, rest = rest[:n_cast], rest[n_cast:]
        outs, rest = rest[:n_out], rest[n_out:]
        cast_out, scratch = rest[:n_cast], rest[n_cast:]
        for src, dst in zip(cast_in, cast_out):
            dst[...] = src[...].astype(BF16)
        body(*ins, *outs, *scratch)

    return kernel


def row_slab_job(w, layer, n_steps, step_of, cols=None):
    rows = w.shape[1]
    cols = w.shape[2] if cols is None else cols
    slab = rows // n_steps
    assert slab * n_steps == rows and slab % (2 * SUBLANES) == 0
    return CastJob(w, (None, slab, cols), lambda *idx: (layer, step_of(*idx), 0),
                   (rows, cols), (slab, cols), lambda *idx: (step_of(*idx), 0))


def _call_hosting(body, casts, grid, in_specs, args, out_specs, out_shapes, **kwargs):
    n_in, n_out = len(in_specs), len(out_specs)
    casts = list(casts)
    res = pl.pallas_call(
        _hosting_casts(body, n_in, n_out, len(casts)),
        grid=grid,
        in_specs=list(in_specs) + [c.in_spec for c in casts],
        out_specs=list(out_specs) + [c.out_spec for c in casts],
        out_shape=list(out_shapes) + [c.out_shape for c in casts],
        **kwargs,
    )(*args, *[c.src for c in casts])
    return res[:n_out], res[n_out:]


def _ada_kernel(cv_ref, w_ref, b_ref, o_ref):
    s = _silu(cv_ref[...])
    o_ref[...] = _mm(s, w_ref[...]) + b_ref[...]


def ada_modulation(cv, ada_w, ada_b, tn=1024):
    depth, d, n = ada_w.shape
    return pl.pallas_call(
        _ada_kernel,
        grid=(depth, n // tn),
        in_specs=[
            pl.BlockSpec((SUBLANES, d), lambda l, j: (0, 0)),
            pl.BlockSpec((None, d, tn), lambda l, j: (l, 0, j)),
            pl.BlockSpec((None, 1, tn), lambda l, j: (l, 0, j)),
        ],
        out_specs=pl.BlockSpec((None, SUBLANES, tn), lambda l, j: (l, 0, j)),
        out_shape=jax.ShapeDtypeStruct((depth, SUBLANES, n), F32),
        compiler_params=_cparams(("parallel", "parallel")),
        name="ada",
    )(cv, ada_w, ada_b.reshape(depth, 1, n))


def _modulated_norm(x, g, sc, sh):
    y = x * lax.rsqrt(jnp.mean(x * x, axis=-1, keepdims=True) + NORM_EPS)
    return (y * g) * (1.0 + sc) + sh


NORM_ROW_BLOCK = 128


def _modulated_norm_rows(x_ref, g_ref, sc_ref, sh_ref, hs_ref):
    rb = min(NORM_ROW_BLOCK, x_ref.shape[0])

    def body(i, _):
        rows = pl.ds(pl.multiple_of(i * rb, rb), rb)
        hs_ref[rows, :] = _modulated_norm(x_ref[rows, :], g_ref[...], sc_ref[...], sh_ref[...]).astype(BF16)
        return 0

    lax.fori_loop(0, x_ref.shape[0] // rb, body, 0)


def _proj_kernel(*refs, with_gates, with_tail, head_blocks):
    x_ref, g_ref, sc_ref, sh_ref, w_ref = refs[:5]
    refs = refs[5:]
    wt_ref = None
    if with_tail:
        wt_ref, refs = refs[0], refs[1:]
    if with_gates:
        wab_ref, gp_ref, o_ref, gate_ref, hs_ref, wb_ref = refs
    else:
        o_ref, hs_ref, wb_ref = refs
    j = pl.program_id(2)

    @pl.when(j == 0)
    def _():
        if with_gates:
            hb = _modulated_norm(x_ref[...], g_ref[...], sc_ref[...], sh_ref[...]).astype(BF16)
            hs_ref[...] = hb
            a = jnp.dot(hb, wab_ref[...], preferred_element_type=F32)
            neg_decay_rate = -jnp.exp(gp_ref[0:1, :])
            g = neg_decay_rate * _softplus(a + gp_ref[1:2, :])
            lane = lax.broadcasted_iota(jnp.int32, a.shape, 1)
            gate_ref[...] = jnp.where(lane < 2 * GDN_HEADS, g, _sigmoid(a))
        else:
            _modulated_norm_rows(x_ref, g_ref, sc_ref, sh_ref, hs_ref)

    def emit(wref):
        if wref.dtype != BF16:
            wb_ref[...] = wref[...].astype(BF16)
            wref = wb_ref
        o_ref[...] = jnp.dot(hs_ref[...], wref[...], preferred_element_type=F32).astype(o_ref.dtype)

    pl.when(j < head_blocks)(lambda: emit(w_ref))
    if with_tail:
        pl.when(j >= head_blocks)(lambda: emit(wt_ref))


def proj(x, g, sc, sh, w, layer, head_cols, w_tail, n_out, wab=None, gparams=None, tm=1024, tn=512):
    bn, t, d = x.shape
    tm = min(tm, t)
    head_blocks = head_cols // tn
    with_gates = wab is not None
    vec = pl.BlockSpec((None, 1, d), lambda b, i, j: (b, 0, 0))
    in_specs = [
        pl.BlockSpec((None, tm, d), lambda b, i, j: (b, i, 0)),
        pl.BlockSpec((1, d), lambda b, i, j: (0, 0)),
        vec, vec,
        pl.BlockSpec((None, d, tn), lambda b, i, j: (layer, 0, jnp.minimum(j, head_blocks - 1))),
    ]
    args = [x, g.reshape(1, d), sc, sh, w]
    with_tail = w_tail is not None
    if with_tail:
        in_specs.append(pl.BlockSpec((d, tn), lambda b, i, j: (0, jnp.maximum(j - head_blocks, 0))))
        args.append(w_tail)
    out_specs = [pl.BlockSpec((None, tm, tn), lambda b, i, j: (b, i, j))]
    out_shape = [jax.ShapeDtypeStruct((bn, t, n_out), BF16)]
    if with_gates:
        in_specs += [pl.BlockSpec((d, LANES), lambda b, i, j: (0, 0)),
                     pl.BlockSpec((SUBLANES, LANES), lambda b, i, j: (0, 0))]
        args += [wab, gparams]
        out_specs.append(pl.BlockSpec((None, tm, LANES), lambda b, i, j: (b, i, 0)))
        out_shape.append(jax.ShapeDtypeStruct((bn, t, LANES), F32))
    res = pl.pallas_call(
        functools.partial(_proj_kernel, with_gates=with_gates, with_tail=with_tail, head_blocks=head_blocks),
        grid=(bn, t // tm, n_out // tn),
        in_specs=in_specs,
        out_specs=out_specs,
        out_shape=out_shape,
        scratch_shapes=[pltpu.VMEM((tm, d), BF16), pltpu.VMEM((d, tn), BF16)],
        compiler_params=_cparams(("parallel", "parallel", "arbitrary")),
        name="proj",
    )(*args)
    return res if with_gates else res[0]


CONV_HALO = SUBLANES


def _stage_padded(pad_ref, load_rows, t, rb):
    zeros = jnp.zeros((CONV_HALO, LANES), F32)
    pad_ref[0:CONV_HALO, :] = zeros
    pad_ref[CONV_HALO + t:2 * CONV_HALO + t, :] = zeros

    def body(i, _):
        r0 = pl.multiple_of(i * rb, rb)
        pad_ref[pl.ds(CONV_HALO + r0, rb), :] = load_rows(r0)
        return 0

    lax.fori_loop(0, t // rb, body, 0)


def _conv_rows(pad_ref, w_ref, r0, rb, taps):
    acc = None
    for j in range(taps):
        xj = pad_ref[pl.ds(r0 + (CONV_HALO + j - taps // 2), rb), :]
        term = xj * w_ref[j:j + 1, :]
        acc = term if acc is None else acc + term
    return acc


def _l2norm(x):
    return x * lax.rsqrt(jnp.sum(x * x, axis=-1, keepdims=True) + NORM_EPS)


def _gdn_kernel(*refs, t, with_q):
    nc = t // CHUNK
    rb = min(GDN_ROW_BLOCK, t)
    if with_q:
        (ka_ref, va_ref, qa_ref, za_ref, wk_ref, wv_ref, wq_ref, grow_ref, cos_ref, sin_ref, s0_ref,
         gn_ref, y_ref,
         pad_ref, k_ref, v_ref, q_ref, o_ref, gc_ref, kwq_s, h_s, au_s, gl_s) = refs
    else:
        (ka_ref, va_ref, wk_ref, wv_ref, grow_ref, sfin_ref,
         pad_ref, k_ref, v_ref, gc_ref, kwq_s, h_s, gl_s) = refs

    def conv_all(src_ref, w_ref, finish, dst_ref):
        _stage_padded(pad_ref, lambda r0: src_ref[pl.ds(r0, rb), :].astype(F32), t, rb)

        def body(i, _):
            r0 = pl.multiple_of(i * rb, rb)
            y = _silu(_conv_rows(pad_ref, w_ref, r0, rb, GDN_CONV))
            dst_ref[pl.ds(r0, rb), :] = finish(y, r0)
            return 0

        lax.fori_loop(0, t // rb, body, 0)

    def rope(x, r0):
        return (x * cos_ref[pl.ds(r0, rb), :]
                + pltpu.roll(x, HEAD_DIM // 2, axis=1) * sin_ref[pl.ds(r0, rb), :])

    if with_q:
        conv_all(ka_ref, wk_ref, lambda y, r0: rope(_l2norm(y), r0), k_ref)
        conv_all(qa_ref, wq_ref, lambda y, r0: rope(_l2norm(y), r0) * HEAD_DIM ** -0.5, q_ref)
    else:
        conv_all(ka_ref, wk_ref, lambda y, r0: _l2norm(y), k_ref)
    conv_all(va_ref, wv_ref, lambda y, r0: y, v_ref)

    ii = lax.broadcasted_iota(jnp.int32, (GROUP, GROUP), 0)
    jj = lax.broadcasted_iota(jnp.int32, (GROUP, GROUP), 1)
    same = (ii // CHUNK) == (jj // CHUNK)
    lower = same & (ii >= jj)
    upper = same & (ii <= jj)
    eye = ii == jj
    gc_ref[0] = _mm_exact(grow_ref[0], upper.astype(F32))
    gc_ref[1] = _mm_exact(grow_ref[1], lower.astype(F32))

    n_groups = t // GROUP
    gpi = min(PREP_GROUPS_PER_ITER, n_groups)
    n_sets = n_groups // gpi
    steps_per_set = 2 * gpi

    def prep_stages(i):
        chains = ([(i * gpi + k, 0) for k in range(gpi)]
                  + [(n_groups - 1 - (i * gpi + k), 1) for k in range(gpi)])
        ids = range(len(chains))
        st = {}

        def load():
            for ch, (g, d) in enumerate(chains):
                r0 = pl.multiple_of(g * GROUP, GROUP)
                st["k", ch] = k_ref[pl.ds(r0, GROUP), :]
                st["v", ch] = v_ref[pl.ds(r0, GROUP), :]
                kgb = st["k", ch].astype(BF16)
                if with_q:
                    st["q", ch] = q_ref[pl.ds(r0, GROUP), :]
                    both = _mm_nt(jnp.concatenate([st["k", ch], st["q", ch]], axis=0), kgb)
                    st["kk", ch], st["qk", ch] = both[:GROUP], both[GROUP:]
                else:
                    st["kk", ch] = _mm_nt(kgb, kgb)

        def masks():
            for ch, (g, d) in enumerate(chains):
                incl = lower if d == 0 else upper
                gc_r = jnp.broadcast_to(gc_ref[d, pl.ds(g, 1), :], (GROUP, GROUP))
                be_r = jnp.broadcast_to(grow_ref[2 + d, pl.ds(g, 1), :], (GROUP, GROUP))
                gc_c = gc_r.T
                decay = jnp.where(incl, jnp.exp(jnp.where(incl, gc_c - gc_r, 0.0)), 0.0)
                m = jnp.where(incl & jnp.logical_not(eye), st["kk", ch] * be_r.T * decay, 0.0)
                st["be_r", ch], st["gc_c", ch], st["decay", ch] = be_r, gc_c, decay
                st["p", ch] = jnp.where(eye, 1.0, -m)
                st["sq", ch] = _mm(m, m)

        def level():
            for ch in ids:
                p, sq = st["p", ch], st["sq", ch]
                st["p", ch] = p + _mm(p, sq)
                st["sq", ch] = _mm(sq, sq)

        def last_level():
            for ch in ids:
                st["p", ch] = st["p", ch] + _mm(st["p", ch], st["sq", ch])

        def solve():
            for ch in ids:
                rhs = jnp.concatenate([st["v", ch], st["k", ch] * jnp.exp(st["gc_c", ch])], axis=1)
                st["uw", ch] = _mm(st["p", ch] * st["be_r", ch], rhs)

        def fold():
            for ch, (g, d) in enumerate(chains):
                gc_c = st["gc_c", ch]
                last = [CHUNK - 1, GROUP - 1] if d == 0 else [0, CHUNK]
                gls = [gc_c[r:r + 1, :] for r in last]
                gl_c = jnp.concatenate([jnp.broadcast_to(x, (CHUNK, LANES)) for x in gls], axis=0)
                kdt = (st["k", ch] * jnp.exp(gl_c - gc_c)).T
                parts = [jnp.where(jj < CHUNK, kdt, 0.0), jnp.where(jj >= CHUNK, kdt, 0.0)]
                if with_q:
                    st["qd", ch] = st["q", ch] * jnp.exp(gc_c)
                    parts = [st["qk", ch] * st["decay", ch]] + parts
                st["gl", ch] = gls
                st["prod", ch] = _mm(jnp.concatenate(parts, axis=0), st["uw", ch])

        def store():
            for ch, (g, d) in enumerate(chains):
                r = st["prod", ch]
                off = GROUP if with_q else 0
                for half in range(2):
                    c = 2 * g + half
                    blk = r[off + half * HEAD_DIM:off + (half + 1) * HEAD_DIM]
                    h_s[d, c] = blk[:, :HEAD_DIM]
                    gl_s[d, c] = jnp.broadcast_to(jnp.exp(st["gl", ch][half]), (SUBLANES, LANES))
                    if with_q:
                        rs = slice(half * CHUNK, (half + 1) * CHUNK)
                        au_s[d, c] = r[rs, :HEAD_DIM]
                        qw = st["qd", ch][rs] - r[rs, HEAD_DIM:]
                        kwq_s[d, c] = jnp.concatenate([blk[:, HEAD_DIM:], qw], axis=0).astype(BF16)
                    else:
                        kwq_s[d, c] = blk[:, HEAD_DIM:].astype(BF16)

        return [load, masks] + [level] * (INVERSE_LEVELS - 2) + [last_level, solve, fold, store]

    def scan_step(s, carry):
        cs = (s, nc - 1 - s)
        rr = [jnp.dot(kwq_s[d, cs[d]], carry[d].astype(BF16), preferred_element_type=F32) for d in range(2)]
        new = []
        for d in range(2):
            c = cs[d]
            new.append(carry[d] * gl_s[d, c][0:1, :] + h_s[d, c] - rr[d][:HEAD_DIM])
            if with_q:
                r0 = pl.multiple_of(c * CHUNK, CHUNK)
                o_ref[pl.ds(r0, CHUNK), :] += rr[d][HEAD_DIM:] + au_s[d, c]
        return tuple(new)

    def run(stages, steps, carry):
        for idx in range(max(len(stages), len(steps))):
            if idx < len(stages):
                stages[idx]()
            if idx < len(steps):
                carry = scan_step(steps[idx], carry)
        return carry

    def set_steps(i):
        return [i * steps_per_set + k for k in range(steps_per_set)]

    if with_q:
        o_ref[...] = jnp.zeros_like(o_ref)
        carry = (s0_ref[0], s0_ref[1])
    else:
        zero = jnp.zeros((HEAD_DIM, HEAD_DIM), F32)
        carry = (zero, zero)
    run(prep_stages(0), [], carry)
    carry = lax.fori_loop(1, n_sets, lambda i, c: run(prep_stages(i), set_steps(i - 1), c), carry)
    s_f, s_b = run([], set_steps(n_sets - 1), carry)

    if not with_q:
        sfin_ref[0] = s_f
        sfin_ref[1] = s_b
        return

    def finish(i, _):
        r0 = pl.multiple_of(i * rb, rb)
        o = o_ref[pl.ds(r0, rb), :]
        y = o * lax.rsqrt(jnp.mean(o * o, axis=-1, keepdims=True) + NORM_EPS)
        y_ref[pl.ds(r0, rb), :] = ((y * gn_ref[...]) * _silu(za_ref[pl.ds(r0, rb), :].astype(F32))).astype(y_ref.dtype)
        return 0

    lax.fori_loop(0, t // rb, finish, 0)


def _single(block_shape, index_map):
    return pl.BlockSpec(block_shape, index_map, pipeline_mode=pl.Buffered(1))


def gdn_latent(p, conv_w, grow, cosf, sinf, s0, gnorm, col_k, col_v, col_q, col_z):
    bn, t, _ = p.shape
    nc = t // CHUNK
    ngp = grow.shape[3]
    hd = HEAD_DIM
    seq = lambda col: pl.BlockSpec((None, t, hd), lambda b, h, col=col: (b, 0, col + h))
    cw = lambda col: pl.BlockSpec((GDN_CONV, hd), lambda b, h, col=col: (0, col + h))
    return pl.pallas_call(
        functools.partial(_gdn_kernel, t=t, with_q=True),
        grid=(bn, GDN_HEADS),
        in_specs=[
            seq(col_k), seq(col_v), seq(col_q), seq(col_z),
            cw(0), cw(GDN_HEADS), cw(2 * GDN_HEADS),
            pl.BlockSpec((None, None, 4, ngp, GROUP), lambda b, h: (b, h, 0, 0, 0)),
            _single((t, hd), lambda b, h: (0, 0)),
            _single((t, hd), lambda b, h: (0, 0)),
            pl.BlockSpec((None, None, 2, hd, hd), lambda b, h: (b, h, 0, 0, 0)),
            pl.BlockSpec((1, hd), lambda b, h: (0, 0)),
        ],
        out_specs=pl.BlockSpec((None, t, hd), lambda b, h: (b, 0, h)),
        out_shape=jax.ShapeDtypeStruct((bn, t, GDN_HEADS * hd), BF16),
        scratch_shapes=[
            pltpu.VMEM((t + 2 * CONV_HALO, hd), F32),
            pltpu.VMEM((t, hd), F32),
            pltpu.VMEM((t, hd), F32),
            pltpu.VMEM((t, hd), F32),
            pltpu.VMEM((t, hd), F32),
            pltpu.VMEM((2, ngp, GROUP), F32),
            pltpu.VMEM((2, nc, hd + CHUNK, hd), BF16),
            pltpu.VMEM((2, nc, hd, hd), F32),
            pltpu.VMEM((2, nc, CHUNK, hd), F32),
            pltpu.VMEM((2, nc, SUBLANES, LANES), F32),
        ],
        compiler_params=_cparams(("parallel", "parallel")),
        name="gdn_latent",
    )(p, p, p, p, conv_w, conv_w, conv_w, grow, cosf, sinf, s0, gnorm.reshape(1, hd))


def gdn_context(pc, conv_w, grow, col_k, col_v, casts=()):
    bn, t, _ = pc.shape
    nc = t // CHUNK
    ngp = grow.shape[3]
    hd = HEAD_DIM
    seq = lambda col: pl.BlockSpec((None, t, hd), lambda b, h, col=col: (b, 0, col + h))
    cw = lambda col: pl.BlockSpec((GDN_CONV, hd), lambda b, h, col=col: (0, col + h))
    (states,), rounded = _call_hosting(
        lambda *refs: _gdn_kernel(*refs, t=t, with_q=False), casts,
        grid=(bn, GDN_HEADS),
        in_specs=[
            seq(col_k), seq(col_v), cw(0), cw(GDN_HEADS),
            pl.BlockSpec((None, None, 4, ngp, GROUP), lambda b, h: (b, h, 0, 0, 0)),
        ],
        args=(pc, pc, conv_w, conv_w, grow),
        out_specs=[pl.BlockSpec((None, None, 2, hd, hd), lambda b, h: (b, h, 0, 0, 0))],
        out_shapes=[jax.ShapeDtypeStruct((bn, GDN_HEADS, 2, hd, hd), F32)],
        scratch_shapes=[
            pltpu.VMEM((t + 2 * CONV_HALO, hd), F32),
            pltpu.VMEM((t, hd), F32),
            pltpu.VMEM((t, hd), F32),
            pltpu.VMEM((2, ngp, GROUP), F32),
            pltpu.VMEM((2, nc, hd, hd), BF16),
            pltpu.VMEM((2, nc, hd, hd), F32),
            pltpu.VMEM((2, nc, SUBLANES, LANES), F32),
        ],
        compiler_params=_cparams(("parallel", "parallel")),
        name="gdn_context",
    )
    return states, rounded


def gate_rows(gates, t):
    bn = gates.shape[0]
    ng = t // GROUP
    g = gates[:, :, :4 * GDN_HEADS].reshape(bn, ng, GROUP, 4, GDN_HEADS)
    g = jnp.transpose(g, (0, 4, 3, 1, 2))
    if ng < SUBLANES:
        g = jnp.pad(g, ((0, 0), (0, 0), (0, 0), (0, SUBLANES - ng), (0, 0)))
    return g


def _natten_kernel(q_ref, k_ref, v_ref, kc_ref, vc_ref, bias_ref, o_ref, *, rows):
    kcb = kc_ref[...]
    vcb = vc_ref[...]
    scale = HEAD_DIM ** -0.5
    win = NA_KH * GRID_W
    per_iter = min(NA_ROWS_PER_ITER, rows)

    def body(it, _):
        rws = [it * per_iter + k for k in range(per_iter)]
        q0s, k0s, s_loc, s_ctx = [], [], [], []
        for r in rws:
            rs = jnp.clip(r - NA_KH // 2, 0, rows - NA_KH)
            q0s.append(pl.multiple_of(r * GRID_W, GRID_W))
            k0s.append(pl.multiple_of(rs * GRID_W, GRID_W))
            q = q_ref[pl.ds(q0s[-1], GRID_W), :]
            kw = k_ref[pl.ds(k0s[-1], win), :]
            s_loc.append(lax.dot_general(q, kw, NT_DIMS, preferred_element_type=F32) * scale + bias_ref[r - rs])
            s_ctx.append(lax.dot_general(q, kcb, NT_DIMS, preferred_element_type=F32) * scale)
        p_loc, p_ctx, den = [], [], []
        for a, b in zip(s_loc, s_ctx):
            m = jnp.maximum(jnp.max(a, axis=-1, keepdims=True), jnp.max(b, axis=-1, keepdims=True))
            p_loc.append(jnp.exp(a - m))
            p_ctx.append(jnp.exp(b - m))
            den.append(jnp.sum(p_loc[-1], axis=-1, keepdims=True) + jnp.sum(p_ctx[-1], axis=-1, keepdims=True))
        outs = []
        for k in range(per_iter):
            vw = v_ref[pl.ds(k0s[k], win), :]
            outs.append(jnp.dot(p_loc[k].astype(BF16), vw, preferred_element_type=F32)
                        + jnp.dot(p_ctx[k].astype(BF16), vcb, preferred_element_type=F32))
        for k in range(per_iter):
            o_ref[pl.ds(q0s[k], GRID_W), :] = (outs[k] / den[k]).astype(o_ref.dtype)
        return 0

    lax.fori_loop(0, rows // per_iter, body, 0)


def natten(p, pc, bias, col_q, col_k, col_v, ctx_col_k, ctx_col_v, casts=()):
    bn, t, _ = p.shape
    lc = pc.shape[1]
    hd = HEAD_DIM
    rows = t // GRID_W
    seq = lambda col: pl.BlockSpec((None, t, hd), lambda b, h, col=col: (b, 0, col + h))
    cseq = lambda col: pl.BlockSpec((None, lc, hd), lambda b, h, col=col: (b, 0, col + h))
    (y,), rounded = _call_hosting(
        lambda *refs: _natten_kernel(*refs, rows=rows), casts,
        grid=(bn, NA_HEADS),
        in_specs=[
            seq(col_q), seq(col_k), seq(col_v), cseq(ctx_col_k), cseq(ctx_col_v),
            pl.BlockSpec((None, NA_KH, GRID_W, NA_KH * GRID_W), lambda b, h: (h, 0, 0, 0)),
        ],
        args=(p, p, p, pc, pc, bias),
        out_specs=[pl.BlockSpec((None, t, hd), lambda b, h: (b, 0, h))],
        out_shapes=[jax.ShapeDtypeStruct((bn, t, NA_HEADS * hd), BF16)],
        compiler_params=_cparams(("parallel", "parallel")),
        name="natten",
    )
    return y, rounded


def natten_bias(rpb, rows):
    kh = min(NA_KH, rows)
    hn, n_dr, n_dc = rpb.shape
    col = np.arange(GRID_W)
    col_start = np.clip(col - NA_KW // 2, 0, GRID_W - NA_KW)
    in_win = (col[None, :] >= col_start[:, None]) & (col[None, :] < col_start[:, None] + NA_KW)
    dc = np.clip(col[None, :] - col[:, None], -(NA_KW - 1), NA_KW - 1) + (NA_KW - 1)
    onehot = (np.arange(n_dc)[:, None] == dc.reshape(-1)[None, :]).astype(np.float32)
    by_dr = jnp.dot(rpb.astype(F32).reshape(hn * n_dr, n_dc), jnp.asarray(onehot), precision=HIGHEST)
    by_dr = jnp.where(in_win[None, None], by_dr.reshape(hn, n_dr, GRID_W, GRID_W), NEG_BIG)
    b = jnp.stack([by_dr[:, NA_KH - 1 - cls:NA_KH - 1 - cls + kh] for cls in range(NA_KH)], axis=1)
    return jnp.transpose(b, (0, 1, 3, 2, 4)).reshape(hn, NA_KH, GRID_W, kh * GRID_W)


def _outproj_kernel(*refs, n_in):
    x_ref, gate_ref, g_ref, sc_ref, sh_ref = refs[:5]
    a_refs = refs[5:5 + n_in]
    w_refs = refs[5 + n_in:5 + 2 * n_in]
    o_ref, hs_ref = refs[5 + 2 * n_in:]
    y = None
    for a_ref, w_ref in zip(a_refs, w_refs):
        part = jnp.dot(a_ref[...], w_ref[...], preferred_element_type=F32)
        y = part if y is None else y + part
    x1 = x_ref[...] + gate_ref[...] * y
    o_ref[...] = x1
    hs_ref[...] = _modulated_norm(x1, g_ref[...], sc_ref[...], sh_ref[...]).astype(BF16)


def outproj(x, gate, acts, weight, g_next, sc_next, sh_next, tm=512):
    bn, t, d = x.shape
    n_in = len(acts)
    kdim = acts[0].shape[-1]
    assert all(a.shape[-1] == kdim for a in acts) and weight.shape[0] == n_in * kdim
    weights = [weight] * n_in
    row = pl.BlockSpec((None, tm, d), lambda b, i: (b, i, 0))
    vec = pl.BlockSpec((None, 1, d), lambda b, i: (b, 0, 0))
    in_specs = [row, vec, pl.BlockSpec((1, d), lambda b, i: (0, 0)), vec, vec]
    in_specs += [pl.BlockSpec((None, tm, kdim), lambda b, i: (b, i, 0)) for a in acts]
    in_specs += [pl.BlockSpec((kdim, d), lambda b, i, k=k: (k, 0)) for k in range(n_in)]
    return pl.pallas_call(
        functools.partial(_outproj_kernel, n_in=n_in),
        grid=(bn, t // tm),
        in_specs=in_specs,
        out_specs=[row, row],
        out_shape=[jax.ShapeDtypeStruct((bn, t, d), F32), jax.ShapeDtypeStruct((bn, t, d), BF16)],
        compiler_params=_cparams(("parallel", "parallel")),
        name="outproj",
    )(x, gate, g_next.reshape(1, d), sc_next, sh_next, *acts, *weights)


def _outproj_conv_kernel(x_ref, gate_ref, g_ref, sc_ref, sh_ref, gb_ref, gc_ref, val_ref,
                         gcp_ref, valp_ref, gcn_ref, valn_ref, cw_ref, w_ref, o_ref, hs_ref):
    i = pl.program_id(1)
    tm = x_ref.shape[0]
    u = gc_ref[...].astype(F32) * val_ref[...].astype(F32)
    last = gcp_ref.shape[0] - 1
    u_prev = gcp_ref[last:last + 1, :].astype(F32) * valp_ref[last:last + 1, :].astype(F32)
    u_next = gcn_ref[0:1, :].astype(F32) * valn_ref[0:1, :].astype(F32)
    u_prev = jnp.where(i == 0, 0.0, u_prev)
    u_next = jnp.where(i == pl.num_programs(1) - 1, 0.0, u_next)
    row = lax.broadcasted_iota(jnp.int32, (tm, 1), 0)
    below = jnp.where(row == 0, u_prev, pltpu.roll(u, 1, axis=0))
    above = jnp.where(row == tm - 1, u_next, pltpu.roll(u, tm - 1, axis=0))
    conv = below * cw_ref[0:1, :] + u * cw_ref[1:2, :] + above * cw_ref[2:3, :]
    z = (gb_ref[...].astype(F32) * conv).astype(BF16)
    x1 = x_ref[...] + gate_ref[...] * jnp.dot(z, w_ref[...], preferred_element_type=F32)
    o_ref[...] = x1
    hs_ref[...] = _modulated_norm(x1, g_ref[...], sc_ref[...], sh_ref[...]).astype(BF16)


def outproj_conv(x, gate, p, conv_w, weight, g_next, sc_next, sh_next, tm=512):
    bn, t, d = x.shape
    halo = 2 * SUBLANES
    nb = t // halo
    row = pl.BlockSpec((None, tm, d), lambda b, i: (b, i, 0))
    vec = pl.BlockSpec((None, 1, d), lambda b, i: (b, 0, 0))
    seg = lambda k: pl.BlockSpec((None, tm, d), lambda b, i, k=k: (b, i, k))
    prev = lambda k: pl.BlockSpec((None, halo, d), lambda b, i, k=k: (b, jnp.maximum(i * (tm // halo) - 1, 0), k))
    nxt = lambda k: pl.BlockSpec((None, halo, d), lambda b, i, k=k: (b, jnp.minimum((i + 1) * (tm // halo), nb - 1), k))
    return pl.pallas_call(
        _outproj_conv_kernel,
        grid=(bn, t // tm),
        in_specs=[row, vec, pl.BlockSpec((1, d), lambda b, i: (0, 0)), vec, vec,
                  seg(0), seg(1), seg(2), prev(1), prev(2), nxt(1), nxt(2),
                  pl.BlockSpec((SC_CONV, d), lambda b, i: (0, 0)),
                  pl.BlockSpec((d, d), lambda b, i: (0, 0))],
        out_specs=[row, row],
        out_shape=[jax.ShapeDtypeStruct((bn, t, d), F32), jax.ShapeDtypeStruct((bn, t, d), BF16)],
        compiler_params=_cparams(("parallel", "parallel")),
        name="outproj_conv",
    )(x, gate, g_next.reshape(1, d), sc_next, sh_next, p, p, p, p, p, p, p, conv_w, weight)


def _ffn_kernel(x_ref, hs_ref, gate_ref, fn_ref, wg_ref, wu_ref, wd_ref, o_ref, *, final_norm):
    j = pl.program_id(2)

    @pl.when(j == 0)
    def _():
        o_ref[...] = jnp.zeros_like(o_ref)

    h = hs_ref[...]
    gate = jnp.dot(h, wg_ref[...], preferred_element_type=F32)
    up = jnp.dot(h, wu_ref[...], preferred_element_type=F32)
    a = (_silu(gate) * up).astype(BF16)
    o_ref[...] += jnp.dot(a, wd_ref[...], preferred_element_type=F32)

    @pl.when(j == pl.num_programs(2) - 1)
    def _():
        rb = min(NORM_ROW_BLOCK, o_ref.shape[0])

        def body(i, _):
            rows = pl.ds(pl.multiple_of(i * rb, rb), rb)
            y = x_ref[rows, :] + gate_ref[...] * o_ref[rows, :]
            if final_norm:
                y = (y * lax.rsqrt(jnp.mean(y * y, axis=-1, keepdims=True) + NORM_EPS)) * fn_ref[...]
            o_ref[rows, :] = y
            return 0

        lax.fori_loop(0, o_ref.shape[0] // rb, body, 0)


def ffn_grid(x, f, tm=1024, tf=512):
    bn, t, _ = x.shape
    tm = min(tm, t)
    return tm, tf, (bn, t // tm, f // tf)


def ffn_weight_jobs(w_gate, w_up, w_down, layer, grid, tf):
    bn, ni, nj = grid
    d = w_gate.shape[1]
    slab = d // (bn * ni)
    assert slab * bn * ni == d and slab % LANES == 0
    tile = lambda b, i: b * ni + i
    up_job = lambda w: CastJob(w, (None, slab, tf), lambda b, i, j: (layer, tile(b, i), j),
                               w.shape[1:], (slab, tf), lambda b, i, j: (tile(b, i), j))
    down_job = CastJob(w_down, (None, tf, slab), lambda b, i, j: (layer, j, tile(b, i)),
                       w_down.shape[1:], (tf, slab), lambda b, i, j: (j, tile(b, i)))
    return [up_job(w_gate), up_job(w_up), down_job]


def ffn(x, hs, gate, fnorm, w_gate, w_up, w_down, final_norm, casts=()):
    bn, t, d = x.shape
    f = w_gate.shape[1]
    tm, tf, grid = ffn_grid(x, f)
    vec = pl.BlockSpec((None, 1, d), lambda b, i, j: (b, 0, 0))
    one = pl.BlockSpec((1, d), lambda b, i, j: (0, 0))
    (y,), rounded = _call_hosting(
        functools.partial(_ffn_kernel, final_norm=final_norm), casts,
        grid=grid,
        in_specs=[
            _single((None, tm, d), lambda b, i, j: (b, i, 0)),
            pl.BlockSpec((None, tm, d), lambda b, i, j: (b, i, 0)),
            vec, one,
            pl.BlockSpec((d, tf), lambda b, i, j: (0, j)),
            pl.BlockSpec((d, tf), lambda b, i, j: (0, j)),
            pl.BlockSpec((tf, d), lambda b, i, j: (j, 0)),
        ],
        args=(x, hs, gate, fnorm.reshape(1, d), w_gate, w_up, w_down),
        out_specs=[pl.BlockSpec((None, tm, d), lambda b, i, j: (b, i, 0))],
        out_shapes=[jax.ShapeDtypeStruct((bn, t, d), F32)],
        compiler_params=_cparams(("parallel", "parallel", "arbitrary")),
        name="ffn",
    )
    return y, rounded


def rope_tables(t):
    pos = np.arange(t)
    row = (pos // GRID_W).astype(np.float32)
    col = (pos % GRID_W).astype(np.float32)
    n_freq = HEAD_DIM // 4
    inv_freq = jnp.asarray(ROPE_THETA, F32) ** (-jnp.arange(n_freq, dtype=F32) / n_freq)
    ang = jnp.concatenate([jnp.asarray(row)[:, None] * inv_freq, jnp.asarray(col)[:, None] * inv_freq], axis=-1)
    cos, sin = jnp.cos(ang), jnp.sin(ang)
    return jnp.concatenate([cos, cos], axis=-1), jnp.concatenate([-sin, sin], axis=-1)


def kernel(x, c, ctx, c_ctx, ada_w, ada_b, norm_mix, norm_ffn, ffn_w_gate, ffn_w_up, ffn_w_down, final_norm,
           ev_w_in, ev_conv, ev_a_log, ev_dt_bias, ev_gdn_norm, ev_rpb, ev_w_out, od_w_in, od_conv, od_w_out):
    bn, t, d = x.shape
    depth = ada_w.shape[0]
    assert depth == 2
    gw = GDN_HEADS * HEAD_DIM
    nw = NA_HEADS * HEAD_DIM
    nh = GDN_HEADS

    cv = jnp.zeros((SUBLANES, d), F32).at[:bn].set(c).at[bn].set(c_ctx)
    mods = ada_modulation(cv, ada_w, ada_b)

    def mod_vecs(l, rows):
        m = mods[l, rows].reshape(-1, 6, d)
        return [m[:, k][:, None, :] for k in range(6)]

    sh1, sc1, g1, sh2, sc2, g2 = mod_vecs(0, slice(0, bn))
    csh1, csc1 = [jnp.broadcast_to(v, (bn, 1, d)) for v in mod_vecs(0, slice(bn, bn + 1))[:2]]
    w_in = ev_w_in[0]
    n_gate = 4 * nh
    seg_gate = 2 * gw + 2 * nw
    w_head32 = w_in[:, :seg_gate][None]
    w_tail32 = w_in[:, seg_gate + n_gate:][None]
    n_main = seg_gate + w_tail32.shape[2]
    w_ab = jnp.pad(w_in[:, seg_gate:seg_gate + n_gate], ((0, 0), (0, LANES - n_gate))).astype(BF16)
    gparams = jnp.zeros((SUBLANES, LANES), F32)
    gparams = gparams.at[0, :2 * nh].set(ev_a_log[0].reshape(-1)).at[1, :2 * nh].set(ev_dt_bias[0].reshape(-1))
    col_ka, col_va, col_kb, col_vb, col_qa, col_qb, col_za = [k * nh for k in range(7)]

    head_step = lambda b, h: b * nh + h
    n_head_steps = bn * nh
    pc, gates_c = proj(ctx, norm_mix[0], csc1, csh1, w_head32, 0, seg_gate, None, seg_gate, w_ab, gparams)
    s_ctx, (w_head, w_tail, w_out0) = gdn_context(
        pc, ev_conv[0], gate_rows(gates_c, ctx.shape[1]), col_ka, col_va,
        casts=[row_slab_job(w_head32, 0, n_head_steps, head_step),
               row_slab_job(w_tail32, 0, n_head_steps, head_step),
               row_slab_job(ev_w_out, 0, n_head_steps, head_step)])

    p, gates = proj(x, norm_mix[0], sc1, sh1, w_head[None], 0, seg_gate, w_tail, n_main, w_ab, gparams)
    cosf, sinf = rope_tables(t)
    y_gdn = gdn_latent(p, ev_conv[0], gate_rows(gates, t), cosf, sinf, s_ctx, ev_gdn_norm[0],
                       col_ka, col_va, col_qa, col_za)
    y_na, rounded = natten(p, pc, natten_bias(ev_rpb[0], t // GRID_W), col_qb, col_kb, col_vb, col_kb, col_vb,
                           casts=[row_slab_job(w, 0, n_head_steps, head_step)
                                  for w in (ffn_w_gate, ffn_w_up, ffn_w_down, od_w_in, od_w_out)])
    ffn0, (w_in1, w_out1) = rounded[:3], rounded[3:]
    x_lat, hs = outproj(x, g1, [y_gdn, y_na], w_out0, norm_ffn[0], sc2, sh2)
    _, tf, grid = ffn_grid(x_lat, ffn_w_gate.shape[2])
    x_lat, ffn1 = ffn(x_lat, hs, g2, final_norm, *ffn0, final_norm=False,
                      casts=ffn_weight_jobs(ffn_w_gate, ffn_w_up, ffn_w_down, 1, grid, tf))

    sh1, sc1, g1, sh2, sc2, g2 = mod_vecs(1, slice(0, bn))
    p = proj(x_lat, norm_mix[1], sc1, sh1, w_in1[None], 0, 3 * d, None, 3 * d)
    x_lat, hs = outproj_conv(x_lat, g1, p, od_conv[0], w_out1, norm_ffn[1], sc2, sh2)
    x_lat, _ = ffn(x_lat, hs, g2, final_norm, *ffn1, final_norm=True)
    return x_lat
```

```python
import functools
import math

import jax
import jax.numpy as jnp
import numpy as np
from jax import lax
from jax.experimental import pallas as pl
from jax.experimental.pallas import tpu as pltpu

F32 = jnp.float32
BF16 = jnp.bfloat16
HIGHEST = lax.Precision.HIGHEST

LANES = 128
SUBLANES = 8
VMEM_LIMIT = 56 * 1024 * 1024

GRID_W = 64
HEAD_DIM = 128
GDN_HEADS = 8
NA_HEADS = 8
GDN_CONV = 5
CHUNK = 64
INVERSE_LEVELS = 6
GROUP = 2 * CHUNK
PREP_GROUPS_PER_ITER = 4
GDN_ROW_BLOCK = 512
NA_ROWS_PER_ITER = 8
NA_KH = 8
NA_KW = 16
SC_CONV = 3
ROPE_THETA = 10000.0
NORM_EPS = 1e-6
NEG_BIG = -1e30

NT_DIMS = (((1,), (1,)), ((), ()))


def _cparams(sem):
    return pltpu.CompilerParams(dimension_semantics=sem, vmem_limit_bytes=VMEM_LIMIT)


def _sigmoid(x):
    return 1.0 / (1.0 + jnp.exp(-x))


def _silu(x):
    return x * _sigmoid(x)


def _softplus(x):
    return jnp.maximum(x, 0.0) + jnp.log(1.0 + jnp.exp(-jnp.abs(x)))


def _mm(a, b):
    return jnp.dot(a.astype(BF16), b.astype(BF16), preferred_element_type=F32)


def _mm_nt(a, b):
    return lax.dot_general(a.astype(BF16), b.astype(BF16), NT_DIMS, preferred_element_type=F32)


def _mm_exact(a, b):
    return jnp.dot(a, b, precision=HIGHEST, preferred_element_type=F32)


class CastJob:
    def __init__(self, src, in_block, in_index, out_shape, out_block, out_index, lanes=None):
        self.src = src
        self.lanes = lanes
        self.in_spec = pl.BlockSpec(in_block, in_index)
        self.out_spec = pl.BlockSpec(out_block, out_index)
        self.out_shape = jax.ShapeDtypeStruct(out_shape, BF16)


def _hosting_casts(body, n_in, n_out, casts):
    n_cast = len(casts)

    def kernel(*refs):
        ins, rest = refs[:n_in], refs[n_in:]
        cast_in, rest = rest[:n_cast], rest[n_cast:]
        outs, rest = rest[:n_out], rest[n_out:]
        cast_out, scratch = rest[:n_cast], rest[n_cast:]
        for job, src, dst in zip(casts, cast_in, cast_out):
            block = src[...] if job.lanes is None else src[:, job.lanes[0]:job.lanes[1]]
            dst[...] = block.astype(BF16)
        body(*ins, *outs, *scratch)

    return kernel


def row_slab_job(w, layer, n_steps, step_of, cols=None, lanes=None):
    rows = w.shape[1]
    cols = w.shape[2] if cols is None else cols
    out_cols = cols if lanes is None else lanes[1] - lanes[0]
    slab = rows // n_steps
    assert slab * n_steps == rows and slab % (2 * SUBLANES) == 0
    return CastJob(w, (None, slab, cols), lambda *idx: (layer, step_of(*idx), 0),
                   (rows, out_cols), (slab, out_cols), lambda *idx: (step_of(*idx), 0), lanes=lanes)


def _call_hosting(body, casts, grid, in_specs, args, out_specs, out_shapes, **kwargs):
    n_in, n_out = len(in_specs), len(out_specs)
    casts = list(casts)
    res = pl.pallas_call(
        _hosting_casts(body, n_in, n_out, casts),
        grid=grid,
        in_specs=list(in_specs) + [c.in_spec for c in casts],
        out_specs=list(out_specs) + [c.out_spec for c in casts],
        out_shape=list(out_shapes) + [c.out_shape for c in casts],
        **kwargs,
    )(*args, *[c.src for c in casts])
    return res[:n_out], res[n_out:]


def _ada_kernel(cv_ref, w_ref, b_ref, o_ref):
    s = _silu(cv_ref[...])
    o_ref[...] = _mm(s, w_ref[...]) + b_ref[...]


def ada_modulation(cv, ada_w, ada_b, tn=1024):
    depth, d, n = ada_w.shape
    return pl.pallas_call(
        _ada_kernel,
        grid=(depth, n // tn),
        in_specs=[
            pl.BlockSpec((SUBLANES, d), lambda l, j: (0, 0)),
            pl.BlockSpec((None, d, tn), lambda l, j: (l, 0, j)),
            pl.BlockSpec((None, 1, tn), lambda l, j: (l, 0, j)),
        ],
        out_specs=pl.BlockSpec((None, SUBLANES, tn), lambda l, j: (l, 0, j)),
        out_shape=jax.ShapeDtypeStruct((depth, SUBLANES, n), F32),
        compiler_params=_cparams(("parallel", "parallel")),
        name="ada",
    )(cv, ada_w, ada_b.reshape(depth, 1, n))


def _modulated_norm(x, g, sc, sh):
    y = x * lax.rsqrt(jnp.mean(x * x, axis=-1, keepdims=True) + NORM_EPS)
    return (y * g) * (1.0 + sc) + sh


NORM_ROW_BLOCK = 128


def _modulated_norm_rows(x_ref, g_ref, sc_ref, sh_ref, hs_ref):
    rb = min(NORM_ROW_BLOCK, x_ref.shape[0])

    def body(i, _):
        rows = pl.ds(pl.multiple_of(i * rb, rb), rb)
        hs_ref[rows, :] = _modulated_norm(x_ref[rows, :], g_ref[...], sc_ref[...], sh_ref[...]).astype(BF16)
        return 0

    lax.fori_loop(0, x_ref.shape[0] // rb, body, 0)


def _proj_kernel(*refs, with_gates, with_tail, head_blocks):
    x_ref, g_ref, sc_ref, sh_ref, w_ref = refs[:5]
    refs = refs[5:]
    wt_ref = None
    if with_tail:
        wt_ref, refs = refs[0], refs[1:]
    if with_gates:
        wab_ref, gp_ref, o_ref, gate_ref, hs_ref, wb_ref = refs
    else:
        o_ref, hs_ref, wb_ref = refs
    j = pl.program_id(2)

    @pl.when(j == 0)
    def _():
        if with_gates:
            hb = _modulated_norm(x_ref[...], g_ref[...], sc_ref[...], sh_ref[...]).astype(BF16)
            hs_ref[...] = hb
            a = jnp.dot(hb, wab_ref[...], preferred_element_type=F32)
            neg_decay_rate = -jnp.exp(gp_ref[0:1, :])
            g = neg_decay_rate * _softplus(a + gp_ref[1:2, :])
            lane = lax.broadcasted_iota(jnp.int32, a.shape, 1)
            gate_ref[...] = jnp.where(lane < 2 * GDN_HEADS, g, _sigmoid(a))
        else:
            _modulated_norm_rows(x_ref, g_ref, sc_ref, sh_ref, hs_ref)

    def emit(wref):
        if wref.dtype != BF16:
            wb_ref[...] = wref[...].astype(BF16)
            wref = wb_ref
        o_ref[...] = jnp.dot(hs_ref[...], wref[...], preferred_element_type=F32).astype(o_ref.dtype)

    pl.when(j < head_blocks)(lambda: emit(w_ref))
    if with_tail:
        pl.when(j >= head_blocks)(lambda: emit(wt_ref))


def proj(x, g, sc, sh, w, layer, head_cols, w_tail, n_out, wab=None, gparams=None, tm=1024, tn=512):
    bn, t, d = x.shape
    tm = min(tm, t)
    head_blocks = head_cols // tn
    with_gates = wab is not None
    vec = pl.BlockSpec((None, 1, d), lambda b, i, j: (b, 0, 0))
    in_specs = [
        pl.BlockSpec((None, tm, d), lambda b, i, j: (b, i, 0)),
        pl.BlockSpec((1, d), lambda b, i, j: (0, 0)),
        vec, vec,
        pl.BlockSpec((None, d, tn), lambda b, i, j: (layer, 0, jnp.minimum(j, head_blocks - 1))),
    ]
    args = [x, g.reshape(1, d), sc, sh, w]
    with_tail = w_tail is not None
    if with_tail:
        in_specs.append(pl.BlockSpec((d, tn), lambda b, i, j: (0, jnp.maximum(j - head_blocks, 0))))
        args.append(w_tail)
    out_specs = [pl.BlockSpec((None, tm, tn), lambda b, i, j: (b, i, j))]
    out_shape = [jax.ShapeDtypeStruct((bn, t, n_out), BF16)]
    if with_gates:
        in_specs += [pl.BlockSpec((d, LANES), lambda b, i, j: (0, 0)),
                     pl.BlockSpec((SUBLANES, LANES), lambda b, i, j: (0, 0))]
        args += [wab, gparams]
        out_specs.append(pl.BlockSpec((None, tm, LANES), lambda b, i, j: (b, i, 0)))
        out_shape.append(jax.ShapeDtypeStruct((bn, t, LANES), F32))
    res = pl.pallas_call(
        functools.partial(_proj_kernel, with_gates=with_gates, with_tail=with_tail, head_blocks=head_blocks),
        grid=(bn, t // tm, n_out // tn),
        in_specs=in_specs,
        out_specs=out_specs,
        out_shape=out_shape,
        scratch_shapes=[pltpu.VMEM((tm, d), BF16), pltpu.VMEM((d, tn), BF16)],
        compiler_params=_cparams(("parallel", "parallel", "arbitrary")),
        name="proj",
    )(*args)
    return res if with_gates else res[0]


CONV_HALO = SUBLANES


def _stage_padded(pad_ref, load_rows, t, rb):
    zeros = jnp.zeros((CONV_HALO, LANES), F32)
    pad_ref[0:CONV_HALO, :] = zeros
    pad_ref[CONV_HALO + t:2 * CONV_HALO + t, :] = zeros

    def body(i, _):
        r0 = pl.multiple_of(i * rb, rb)
        pad_ref[pl.ds(CONV_HALO + r0, rb), :] = load_rows(r0)
        return 0

    lax.fori_loop(0, t // rb, body, 0)


def _conv_rows(pad_ref, w_ref, r0, rb, taps):
    acc = None
    for j in range(taps):
        xj = pad_ref[pl.ds(r0 + (CONV_HALO + j - taps // 2), rb), :]
        term = xj * w_ref[j:j + 1, :]
        acc = term if acc is None else acc + term
    return acc


def _l2norm(x):
    return x * lax.rsqrt(jnp.sum(x * x, axis=-1, keepdims=True) + NORM_EPS)


def _gdn_kernel(*refs, t, with_q):
    nc = t // CHUNK
    rb = min(GDN_ROW_BLOCK, t)
    if with_q:
        (ka_ref, va_ref, qa_ref, za_ref, wk_ref, wv_ref, wq_ref, grow_ref, cos_ref, sin_ref, s0_ref,
         gn_ref, y_ref,
         pad_ref, pad2_ref, k_ref, v_ref, q_ref, o_ref, gc_ref, kwq_s, h_s, au_s, gl_s) = refs
    else:
        (ka_ref, va_ref, wk_ref, wv_ref, grow_ref, sfin_ref,
         pad_ref, k_ref, v_ref, gc_ref, kwq_s, h_s, gl_s) = refs

    def conv_all(src_ref, w_ref, finish, dst_ref):
        _stage_padded(pad_ref, lambda r0: src_ref[pl.ds(r0, rb), :].astype(F32), t, rb)

        def body(i, _):
            r0 = pl.multiple_of(i * rb, rb)
            y = _silu(_conv_rows(pad_ref, w_ref, r0, rb, GDN_CONV))
            dst_ref[pl.ds(r0, rb), :] = finish(y, r0)
            return 0

        lax.fori_loop(0, t // rb, body, 0)

    def rope(x, r0):
        return (x * cos_ref[pl.ds(r0, rb), :]
                + pltpu.roll(x, HEAD_DIM // 2, axis=1) * sin_ref[pl.ds(r0, rb), :])

    if with_q:
        _stage_padded(pad_ref, lambda r0: ka_ref[pl.ds(r0, rb), :].astype(F32), t, rb)
        _stage_padded(pad2_ref, lambda r0: qa_ref[pl.ds(r0, rb), :].astype(F32), t, rb)

        def kq_body(i, _):
            r0 = pl.multiple_of(i * rb, rb)
            yk = _silu(_conv_rows(pad_ref, wk_ref, r0, rb, GDN_CONV))
            yq = _silu(_conv_rows(pad2_ref, wq_ref, r0, rb, GDN_CONV))
            k_ref[pl.ds(r0, rb), :] = rope(_l2norm(yk), r0)
            q_ref[pl.ds(r0, rb), :] = rope(_l2norm(yq), r0) * HEAD_DIM ** -0.5
            return 0

        lax.fori_loop(0, t // rb, kq_body, 0)
    else:
        conv_all(ka_ref, wk_ref, lambda y, r0: _l2norm(y), k_ref)
    conv_all(va_ref, wv_ref, lambda y, r0: y, v_ref)

    ii = lax.broadcasted_iota(jnp.int32, (GROUP, GROUP), 0)
    jj = lax.broadcasted_iota(jnp.int32, (GROUP, GROUP), 1)
    same = (ii // CHUNK) == (jj // CHUNK)
    lower = same & (ii >= jj)
    upper = same & (ii <= jj)
    eye = ii == jj
    gc_ref[0] = _mm_exact(grow_ref[0], upper.astype(F32))
    gc_ref[1] = _mm_exact(grow_ref[1], lower.astype(F32))

    n_groups = t // GROUP
    gpi = min(PREP_GROUPS_PER_ITER, n_groups)
    n_sets = n_groups // gpi
    steps_per_set = 2 * gpi

    def prep_stages(i):
        chains = ([(i * gpi + k, 0) for k in range(gpi)]
                  + [(n_groups - 1 - (i * gpi + k), 1) for k in range(gpi)])
        ids = range(len(chains))
        st = {}

        def load():
            for ch, (g, d) in enumerate(chains):
                r0 = pl.multiple_of(g * GROUP, GROUP)
                st["k", ch] = k_ref[pl.ds(r0, GROUP), :]
                st["v", ch] = v_ref[pl.ds(r0, GROUP), :]
                kgb = st["k", ch].astype(BF16)
                if with_q:
                    st["q", ch] = q_ref[pl.ds(r0, GROUP), :]
                    both = _mm_nt(jnp.concatenate([st["k", ch], st["q", ch]], axis=0), kgb)
                    st["kk", ch], st["qk", ch] = both[:GROUP], both[GROUP:]
                else:
                    st["kk", ch] = _mm_nt(kgb, kgb)

        def masks():
            for ch, (g, d) in enumerate(chains):
                incl = lower if d == 0 else upper
                gc_r = jnp.broadcast_to(gc_ref[d, pl.ds(g, 1), :], (GROUP, GROUP))
                be_r = jnp.broadcast_to(grow_ref[2 + d, pl.ds(g, 1), :], (GROUP, GROUP))
                gc_c = gc_r.T
                decay = jnp.where(incl, jnp.exp(jnp.where(incl, gc_c - gc_r, 0.0)), 0.0)
                m = jnp.where(incl & jnp.logical_not(eye), st["kk", ch] * be_r.T * decay, 0.0)
                st["be_r", ch], st["gc_c", ch], st["decay", ch] = be_r, gc_c, decay
                st["p", ch] = jnp.where(eye, 1.0, -m)
                st["sq", ch] = _mm(m, m)

        def level():
            for ch in ids:
                p, sq = st["p", ch], st["sq", ch]
                st["p", ch] = p + _mm(p, sq)
                st["sq", ch] = _mm(sq, sq)

        def last_level():
            for ch in ids:
                st["p", ch] = st["p", ch] + _mm(st["p", ch], st["sq", ch])

        def solve():
            for ch in ids:
                rhs = jnp.concatenate([st["v", ch], st["k", ch] * jnp.exp(st["gc_c", ch])], axis=1)
                st["uw", ch] = _mm(st["p", ch] * st["be_r", ch], rhs)

        def fold():
            for ch, (g, d) in enumerate(chains):
                gc_c = st["gc_c", ch]
                last = [CHUNK - 1, GROUP - 1] if d == 0 else [0, CHUNK]
                gls = [gc_c[r:r + 1, :] for r in last]
                gl_c = jnp.concatenate([jnp.broadcast_to(x, (CHUNK, LANES)) for x in gls], axis=0)
                kdt = (st["k", ch] * jnp.exp(gl_c - gc_c)).T
                parts = [jnp.where(jj < CHUNK, kdt, 0.0), jnp.where(jj >= CHUNK, kdt, 0.0)]
                if with_q:
                    st["qd", ch] = st["q", ch] * jnp.exp(gc_c)
                    parts = [st["qk", ch] * st["decay", ch]] + parts
                st["gl", ch] = gls
                st["prod", ch] = _mm(jnp.concatenate(parts, axis=0), st["uw", ch])

        def store():
            for ch, (g, d) in enumerate(chains):
                r = st["prod", ch]
                off = GROUP if with_q else 0
                for half in range(2):
                    c = 2 * g + half
                    blk = r[off + half * HEAD_DIM:off + (half + 1) * HEAD_DIM]
                    h_s[d, c] = blk[:, :HEAD_DIM]
                    gl_s[d, c] = jnp.broadcast_to(jnp.exp(st["gl", ch][half]), (SUBLANES, LANES))
                    if with_q:
                        rs = slice(half * CHUNK, (half + 1) * CHUNK)
                        au_s[d, c] = r[rs, :HEAD_DIM]
                        qw = st["qd", ch][rs] - r[rs, HEAD_DIM:]
                        kwq_s[d, c] = jnp.concatenate([blk[:, HEAD_DIM:], qw], axis=0).astype(BF16)
                    else:
                        kwq_s[d, c] = blk[:, HEAD_DIM:].astype(BF16)

        return [load, masks] + [level] * (INVERSE_LEVELS - 2) + [last_level, solve, fold, store]

    def scan_step(s, carry):
        cs = (s, nc - 1 - s)
        rr = [jnp.dot(kwq_s[d, cs[d]], carry[d].astype(BF16), preferred_element_type=F32) for d in range(2)]
        new = []
        for d in range(2):
            c = cs[d]
            new.append(carry[d] * gl_s[d, c][0:1, :] + h_s[d, c] - rr[d][:HEAD_DIM])
            if with_q:
                r0 = pl.multiple_of(c * CHUNK, CHUNK)
                o_ref[pl.ds(r0, CHUNK), :] += rr[d][HEAD_DIM:] + au_s[d, c]
        return tuple(new)

    def run(stages, steps, carry):
        for idx in range(max(len(stages), len(steps))):
            if idx < len(stages):
                stages[idx]()
            if idx < len(steps):
                carry = scan_step(steps[idx], carry)
        return carry

    def set_steps(i):
        return [i * steps_per_set + k for k in range(steps_per_set)]

    if with_q:
        o_ref[...] = jnp.zeros_like(o_ref)
        carry = (s0_ref[0], s0_ref[1])
    else:
        zero = jnp.zeros((HEAD_DIM, HEAD_DIM), F32)
        carry = (zero, zero)
    run(prep_stages(0), [], carry)
    carry = lax.fori_loop(1, n_sets, lambda i, c: run(prep_stages(i), set_steps(i - 1), c), carry)
    s_f, s_b = run([], set_steps(n_sets - 1), carry)

    if not with_q:
        sfin_ref[0] = s_f
        sfin_ref[1] = s_b
        return

    def finish(i, _):
        r0 = pl.multiple_of(i * rb, rb)
        o = o_ref[pl.ds(r0, rb), :]
        y = o * lax.rsqrt(jnp.mean(o * o, axis=-1, keepdims=True) + NORM_EPS)
        y_ref[pl.ds(r0, rb), :] = ((y * gn_ref[...]) * _silu(za_ref[pl.ds(r0, rb), :].astype(F32))).astype(y_ref.dtype)
        return 0

    lax.fori_loop(0, t // rb, finish, 0)


def _single(block_shape, index_map):
    return pl.BlockSpec(block_shape, index_map, pipeline_mode=pl.Buffered(1))


def gdn_latent(p, conv_w, grow, cosf, sinf, s0, gnorm, col_k, col_v, col_q, col_z):
    bn, t, _ = p.shape
    nc = t // CHUNK
    ngp = grow.shape[3]
    hd = HEAD_DIM
    seq = lambda col: pl.BlockSpec((None, t, hd), lambda b, h, col=col: (b, 0, col + h))
    cw = lambda col: pl.BlockSpec((GDN_CONV, hd), lambda b, h, col=col: (0, col + h))
    return pl.pallas_call(
        functools.partial(_gdn_kernel, t=t, with_q=True),
        grid=(bn, GDN_HEADS),
        in_specs=[
            seq(col_k), seq(col_v), seq(col_q), seq(col_z),
            cw(0), cw(GDN_HEADS), cw(2 * GDN_HEADS),
            pl.BlockSpec((None, None, 4, ngp, GROUP), lambda b, h: (b, h, 0, 0, 0)),
            _single((t, hd), lambda b, h: (0, 0)),
            _single((t, hd), lambda b, h: (0, 0)),
            pl.BlockSpec((None, None, 2, hd, hd), lambda b, h: (b, h, 0, 0, 0)),
            pl.BlockSpec((1, hd), lambda b, h: (0, 0)),
        ],
        out_specs=pl.BlockSpec((None, t, hd), lambda b, h: (b, 0, h)),
        out_shape=jax.ShapeDtypeStruct((bn, t, GDN_HEADS * hd), BF16),
        scratch_shapes=[
            pltpu.VMEM((t + 2 * CONV_HALO, hd), F32),
            pltpu.VMEM((t + 2 * CONV_HALO, hd), F32),
            pltpu.VMEM((t, hd), F32),
            pltpu.VMEM((t, hd), F32),
            pltpu.VMEM((t, hd), F32),
            pltpu.VMEM((t, hd), F32),
            pltpu.VMEM((2, ngp, GROUP), F32),
            pltpu.VMEM((2, nc, hd + CHUNK, hd), BF16),
            pltpu.VMEM((2, nc, hd, hd), F32),
            pltpu.VMEM((2, nc, CHUNK, hd), F32),
            pltpu.VMEM((2, nc, SUBLANES, LANES), F32),
        ],
        compiler_params=_cparams(("parallel", "parallel")),
        name="gdn_latent",
    )(p, p, p, p, conv_w, conv_w, conv_w, grow, cosf, sinf, s0, gnorm.reshape(1, hd))


def gdn_context(pc, conv_w, grow, col_k, col_v, casts=()):
    bn, t, _ = pc.shape
    nc = t // CHUNK
    ngp = grow.shape[3]
    hd = HEAD_DIM
    seq = lambda col: pl.BlockSpec((None, t, hd), lambda b, h, col=col: (b, 0, col + h))
    cw = lambda col: pl.BlockSpec((GDN_CONV, hd), lambda b, h, col=col: (0, col + h))
    (states,), rounded = _call_hosting(
        lambda *refs: _gdn_kernel(*refs, t=t, with_q=False), casts,
        grid=(bn, GDN_HEADS),
        in_specs=[
            seq(col_k), seq(col_v), cw(0), cw(GDN_HEADS),
            pl.BlockSpec((None, None, 4, ngp, GROUP), lambda b, h: (b, h, 0, 0, 0)),
        ],
        args=(pc, pc, conv_w, conv_w, grow),
        out_specs=[pl.BlockSpec((None, None, 2, hd, hd), lambda b, h: (b, h, 0, 0, 0))],
        out_shapes=[jax.ShapeDtypeStruct((bn, GDN_HEADS, 2, hd, hd), F32)],
        scratch_shapes=[
            pltpu.VMEM((t + 2 * CONV_HALO, hd), F32),
            pltpu.VMEM((t, hd), F32),
            pltpu.VMEM((t, hd), F32),
            pltpu.VMEM((2, ngp, GROUP), F32),
            pltpu.VMEM((2, nc, hd, hd), BF16),
            pltpu.VMEM((2, nc, hd, hd), F32),
            pltpu.VMEM((2, nc, SUBLANES, LANES), F32),
        ],
        compiler_params=_cparams(("parallel", "parallel")),
        name="gdn_context",
    )
    return states, rounded


def gate_rows(gates, t):
    bn = gates.shape[0]
    ng = t // GROUP
    g = gates[:, :, :4 * GDN_HEADS].reshape(bn, ng, GROUP, 4, GDN_HEADS)
    g = jnp.transpose(g, (0, 4, 3, 1, 2))
    if ng < SUBLANES:
        g = jnp.pad(g, ((0, 0), (0, 0), (0, 0), (0, SUBLANES - ng), (0, 0)))
    return g


def _natten_kernel(q_ref, k_ref, v_ref, kc_ref, vc_ref, by_dr_ref, o_ref, bias_ref, *, rows):
    kcb = kc_ref[...]
    vcb = vc_ref[...]
    scale = HEAD_DIM ** -0.5
    kh = min(NA_KH, rows)
    win = kh * GRID_W
    per_iter = min(NA_ROWS_PER_ITER, rows)

    for cls in range(NA_KH):
        for j in range(kh):
            bias_ref[cls, :, j * GRID_W:(j + 1) * GRID_W] = by_dr_ref[NA_KH - 1 - cls + j]

    def body(it, _):
        rws = [it * per_iter + k for k in range(per_iter)]
        q0s, k0s, s_loc, s_ctx = [], [], [], []
        for r in rws:
            rs = jnp.clip(r - NA_KH // 2, 0, rows - NA_KH)
            q0s.append(pl.multiple_of(r * GRID_W, GRID_W))
            k0s.append(pl.multiple_of(rs * GRID_W, GRID_W))
            q = q_ref[pl.ds(q0s[-1], GRID_W), :]
            kw = k_ref[pl.ds(k0s[-1], win), :]
            s_loc.append(lax.dot_general(q, kw, NT_DIMS, preferred_element_type=F32) * scale + bias_ref[r - rs])
            s_ctx.append(lax.dot_general(q, kcb, NT_DIMS, preferred_element_type=F32) * scale)
        p_loc, p_ctx, den = [], [], []
        for a, b in zip(s_loc, s_ctx):
            m = jnp.maximum(jnp.max(a, axis=-1, keepdims=True), jnp.max(b, axis=-1, keepdims=True))
            p_loc.append(jnp.exp(a - m))
            p_ctx.append(jnp.exp(b - m))
            den.append(jnp.sum(p_loc[-1], axis=-1, keepdims=True) + jnp.sum(p_ctx[-1], axis=-1, keepdims=True))
        outs = []
        for k in range(per_iter):
            vw = v_ref[pl.ds(k0s[k], win), :]
            outs.append(jnp.dot(p_loc[k].astype(BF16), vw, preferred_element_type=F32)
                        + jnp.dot(p_ctx[k].astype(BF16), vcb, preferred_element_type=F32))
        for k in range(per_iter):
            o_ref[pl.ds(q0s[k], GRID_W), :] = (outs[k] / den[k]).astype(o_ref.dtype)
        return 0

    lax.fori_loop(0, rows // per_iter, body, 0)


def natten(p, pc, by_dr, col_q, col_k, col_v, ctx_col_k, ctx_col_v, casts=()):
    bn, t, _ = p.shape
    lc = pc.shape[1]
    hd = HEAD_DIM
    rows = t // GRID_W
    kh = min(NA_KH, rows)
    seq = lambda col: pl.BlockSpec((None, t, hd), lambda b, h, col=col: (b, 0, col + h))
    cseq = lambda col: pl.BlockSpec((None, lc, hd), lambda b, h, col=col: (b, 0, col + h))
    (y,), rounded = _call_hosting(
        lambda *refs: _natten_kernel(*refs, rows=rows), casts,
        grid=(bn, NA_HEADS),
        in_specs=[
            seq(col_q), seq(col_k), seq(col_v), cseq(ctx_col_k), cseq(ctx_col_v),
            pl.BlockSpec((None,) + by_dr.shape[1:], lambda b, h: (h, 0, 0, 0)),
        ],
        args=(p, p, p, pc, pc, by_dr),
        out_specs=[pl.BlockSpec((None, t, hd), lambda b, h: (b, 0, h))],
        out_shapes=[jax.ShapeDtypeStruct((bn, t, NA_HEADS * hd), BF16)],
        scratch_shapes=[pltpu.VMEM((NA_KH, GRID_W, kh * GRID_W), F32)],
        compiler_params=_cparams(("parallel", "parallel")),
        name="natten",
    )
    return y, rounded


def natten_bias(rpb):
    hn, n_dr, n_dc = rpb.shape
    col = np.arange(GRID_W)
    col_start = np.clip(col - NA_KW // 2, 0, GRID_W - NA_KW)
    in_win = (col[None, :] >= col_start[:, None]) & (col[None, :] < col_start[:, None] + NA_KW)
    dc = np.clip(col[None, :] - col[:, None], -(NA_KW - 1), NA_KW - 1) + (NA_KW - 1)
    onehot = (np.arange(n_dc)[:, None] == dc.reshape(-1)[None, :]).astype(np.float32)
    by_dr = jnp.dot(rpb.astype(F32).reshape(hn * n_dr, n_dc), jnp.asarray(onehot), precision=HIGHEST)
    return jnp.where(in_win[None, None], by_dr.reshape(hn, n_dr, GRID_W, GRID_W), NEG_BIG)


def _outproj_kernel(*refs, n_in):
    x_ref, gate_ref, g_ref, sc_ref, sh_ref = refs[:5]
    a_refs = refs[5:5 + n_in]
    w_refs = refs[5 + n_in:5 + 2 * n_in]
    o_ref, hs_ref = refs[5 + 2 * n_in:]
    y = None
    for a_ref, w_ref in zip(a_refs, w_refs):
        part = jnp.dot(a_ref[...], w_ref[...], preferred_element_type=F32)
        y = part if y is None else y + part
    x1 = x_ref[...] + gate_ref[...] * y
    o_ref[...] = x1
    hs_ref[...] = _modulated_norm(x1, g_ref[...], sc_ref[...], sh_ref[...]).astype(BF16)


def outproj(x, gate, acts, weight, g_next, sc_next, sh_next, tm=512):
    bn, t, d = x.shape
    n_in = len(acts)
    kdim = acts[0].shape[-1]
    assert all(a.shape[-1] == kdim for a in acts) and weight.shape[0] == n_in * kdim
    weights = [weight] * n_in
    row = pl.BlockSpec((None, tm, d), lambda b, i: (b, i, 0))
    vec = pl.BlockSpec((None, 1, d), lambda b, i: (b, 0, 0))
    in_specs = [row, vec, pl.BlockSpec((1, d), lambda b, i: (0, 0)), vec, vec]
    in_specs += [pl.BlockSpec((None, tm, kdim), lambda b, i: (b, i, 0)) for a in acts]
    in_specs += [pl.BlockSpec((kdim, d), lambda b, i, k=k: (k, 0)) for k in range(n_in)]
    return pl.pallas_call(
        functools.partial(_outproj_kernel, n_in=n_in),
        grid=(bn, t // tm),
        in_specs=in_specs,
        out_specs=[row, row],
        out_shape=[jax.ShapeDtypeStruct((bn, t, d), F32), jax.ShapeDtypeStruct((bn, t, d), BF16)],
        compiler_params=_cparams(("parallel", "parallel")),
        name="outproj",
    )(x, gate, g_next.reshape(1, d), sc_next, sh_next, *acts, *weights)


def _outproj_conv_kernel(x_ref, gate_ref, g_ref, sc_ref, sh_ref, gb_ref, gc_ref, val_ref,
                         gcp_ref, valp_ref, gcn_ref, valn_ref, cw_ref, w_ref, o_ref, hs_ref):
    i = pl.program_id(1)
    tm = x_ref.shape[0]
    u = gc_ref[...].astype(F32) * val_ref[...].astype(F32)
    last = gcp_ref.shape[0] - 1
    u_prev = gcp_ref[last:last + 1, :].astype(F32) * valp_ref[last:last + 1, :].astype(F32)
    u_next = gcn_ref[0:1, :].astype(F32) * valn_ref[0:1, :].astype(F32)
    u_prev = jnp.where(i == 0, 0.0, u_prev)
    u_next = jnp.where(i == pl.num_programs(1) - 1, 0.0, u_next)
    row = lax.broadcasted_iota(jnp.int32, (tm, 1), 0)
    below = jnp.where(row == 0, u_prev, pltpu.roll(u, 1, axis=0))
    above = jnp.where(row == tm - 1, u_next, pltpu.roll(u, tm - 1, axis=0))
    conv = below * cw_ref[0:1, :] + u * cw_ref[1:2, :] + above * cw_ref[2:3, :]
    z = (gb_ref[...].astype(F32) * conv).astype(BF16)
    x1 = x_ref[...] + gate_ref[...] * jnp.dot(z, w_ref[...], preferred_element_type=F32)
    o_ref[...] = x1
    hs_ref[...] = _modulated_norm(x1, g_ref[...], sc_ref[...], sh_ref[...]).astype(BF16)


def outproj_conv(x, gate, p, conv_w, weight, g_next, sc_next, sh_next, tm=512):
    bn, t, d = x.shape
    halo = 2 * SUBLANES
    nb = t // halo
    row = pl.BlockSpec((None, tm, d), lambda b, i: (b, i, 0))
    vec = pl.BlockSpec((None, 1, d), lambda b, i: (b, 0, 0))
    seg = lambda k: pl.BlockSpec((None, tm, d), lambda b, i, k=k: (b, i, k))
    prev = lambda k: pl.BlockSpec((None, halo, d), lambda b, i, k=k: (b, jnp.maximum(i * (tm // halo) - 1, 0), k))
    nxt = lambda k: pl.BlockSpec((None, halo, d), lambda b, i, k=k: (b, jnp.minimum((i + 1) * (tm // halo), nb - 1), k))
    return pl.pallas_call(
        _outproj_conv_kernel,
        grid=(bn, t // tm),
        in_specs=[row, vec, pl.BlockSpec((1, d), lambda b, i: (0, 0)), vec, vec,
                  seg(0), seg(1), seg(2), prev(1), prev(2), nxt(1), nxt(2),
                  pl.BlockSpec((SC_CONV, d), lambda b, i: (0, 0)),
                  pl.BlockSpec((d, d), lambda b, i: (0, 0))],
        out_specs=[row, row],
        out_shape=[jax.ShapeDtypeStruct((bn, t, d), F32), jax.ShapeDtypeStruct((bn, t, d), BF16)],
        compiler_params=_cparams(("parallel", "parallel")),
        name="outproj_conv",
    )(x, gate, g_next.reshape(1, d), sc_next, sh_next, p, p, p, p, p, p, p, conv_w, weight)


def _ffn_kernel(x_ref, hs_ref, gate_ref, fn_ref, wg_ref, wu_ref, wd_ref, o_ref, *, final_norm):
    j = pl.program_id(2)

    @pl.when(j == 0)
    def _():
        o_ref[...] = jnp.zeros_like(o_ref)

    h = hs_ref[...]
    gate = jnp.dot(h, wg_ref[...], preferred_element_type=F32)
    up = jnp.dot(h, wu_ref[...], preferred_element_type=F32)
    a = (_silu(gate) * up).astype(BF16)
    o_ref[...] += jnp.dot(a, wd_ref[...], preferred_element_type=F32)

    @pl.when(j == pl.num_programs(2) - 1)
    def _():
        rb = min(NORM_ROW_BLOCK, o_ref.shape[0])

        def body(i, _):
            rows = pl.ds(pl.multiple_of(i * rb, rb), rb)
            y = x_ref[rows, :] + gate_ref[...] * o_ref[rows, :]
            if final_norm:
                y = (y * lax.rsqrt(jnp.mean(y * y, axis=-1, keepdims=True) + NORM_EPS)) * fn_ref[...]
            o_ref[rows, :] = y
            return 0

        lax.fori_loop(0, o_ref.shape[0] // rb, body, 0)


def ffn_grid(x, f, tm=1024, tf=512):
    bn, t, _ = x.shape
    tm = min(tm, t)
    return tm, tf, (bn, t // tm, f // tf)


def ffn_weight_jobs(w_gate, w_up, w_down, layer, grid, tf):
    bn, ni, nj = grid
    d = w_gate.shape[1]
    slab = d // (bn * ni)
    assert slab * bn * ni == d and slab % LANES == 0
    tile = lambda b, i: b * ni + i
    up_job = lambda w: CastJob(w, (None, slab, tf), lambda b, i, j: (layer, tile(b, i), j),
                               w.shape[1:], (slab, tf), lambda b, i, j: (tile(b, i), j))
    down_job = CastJob(w_down, (None, tf, slab), lambda b, i, j: (layer, j, tile(b, i)),
                       w_down.shape[1:], (tf, slab), lambda b, i, j: (j, tile(b, i)))
    return [up_job(w_gate), up_job(w_up), down_job]


def ffn(x, hs, gate, fnorm, w_gate, w_up, w_down, final_norm, casts=()):
    bn, t, d = x.shape
    f = w_gate.shape[1]
    tm, tf, grid = ffn_grid(x, f)
    vec = pl.BlockSpec((None, 1, d), lambda b, i, j: (b, 0, 0))
    one = pl.BlockSpec((1, d), lambda b, i, j: (0, 0))
    (y,), rounded = _call_hosting(
        functools.partial(_ffn_kernel, final_norm=final_norm), casts,
        grid=grid,
        in_specs=[
            _single((None, tm, d), lambda b, i, j: (b, i, 0)),
            pl.BlockSpec((None, tm, d), lambda b, i, j: (b, i, 0)),
            vec, one,
            pl.BlockSpec((d, tf), lambda b, i, j: (0, j)),
            pl.BlockSpec((d, tf), lambda b, i, j: (0, j)),
            pl.BlockSpec((tf, d), lambda b, i, j: (j, 0)),
        ],
        args=(x, hs, gate, fnorm.reshape(1, d), w_gate, w_up, w_down),
        out_specs=[pl.BlockSpec((None, tm, d), lambda b, i, j: (b, i, 0))],
        out_shapes=[jax.ShapeDtypeStruct((bn, t, d), F32)],
        compiler_params=_cparams(("parallel", "parallel", "arbitrary")),
        name="ffn",
    )
    return y, rounded


def rope_tables(t):
    pos = np.arange(t)
    row = (pos // GRID_W).astype(np.float32)
    col = (pos % GRID_W).astype(np.float32)
    n_freq = HEAD_DIM // 4
    inv_freq = jnp.asarray(ROPE_THETA, F32) ** (-jnp.arange(n_freq, dtype=F32) / n_freq)
    ang = jnp.concatenate([jnp.asarray(row)[:, None] * inv_freq, jnp.asarray(col)[:, None] * inv_freq], axis=-1)
    cos, sin = jnp.cos(ang), jnp.sin(ang)
    return jnp.concatenate([cos, cos], axis=-1), jnp.concatenate([-sin, sin], axis=-1)


def kernel(x, c, ctx, c_ctx, ada_w, ada_b, norm_mix, norm_ffn, ffn_w_gate, ffn_w_up, ffn_w_down, final_norm,
           ev_w_in, ev_conv, ev_a_log, ev_dt_bias, ev_gdn_norm, ev_rpb, ev_w_out, od_w_in, od_conv, od_w_out):
    bn, t, d = x.shape
    depth = ada_w.shape[0]
    assert depth == 2
    gw = GDN_HEADS * HEAD_DIM
    nw = NA_HEADS * HEAD_DIM
    nh = GDN_HEADS

    cv = jnp.zeros((SUBLANES, d), F32).at[:bn].set(c).at[bn].set(c_ctx)
    mods = ada_modulation(cv, ada_w, ada_b)

    def mod_vecs(l, rows):
        m = mods[l, rows].reshape(-1, 6, d)
        return [m[:, k][:, None, :] for k in range(6)]

    sh1, sc1, g1, sh2, sc2, g2 = mod_vecs(0, slice(0, bn))
    csh1, csc1 = [jnp.broadcast_to(v, (bn, 1, d)) for v in mod_vecs(0, slice(bn, bn + 1))[:2]]
    w_in = ev_w_in[0]
    n_gate = 4 * nh
    seg_gate = 2 * gw + 2 * nw
    n_w_in = w_in.shape[1]
    n_main = n_w_in - n_gate
    w_ab = jnp.pad(w_in[:, seg_gate:seg_gate + n_gate], ((0, 0), (0, LANES - n_gate))).astype(BF16)
    gparams = jnp.zeros((SUBLANES, LANES), F32)
    gparams = gparams.at[0, :2 * nh].set(ev_a_log[0].reshape(-1)).at[1, :2 * nh].set(ev_dt_bias[0].reshape(-1))
    col_ka, col_va, col_kb, col_vb, col_qa, col_qb, col_za = [k * nh for k in range(7)]

    head_step = lambda b, h: b * nh + h
    n_head_steps = bn * nh
    pc, gates_c = proj(ctx, norm_mix[0], csc1, csh1, ev_w_in, 0, seg_gate, None, seg_gate, w_ab, gparams)
    s_ctx, (w_head, w_tail, w_out0) = gdn_context(
        pc, ev_conv[0], gate_rows(gates_c, ctx.shape[1]), col_ka, col_va,
        casts=[row_slab_job(ev_w_in, 0, n_head_steps, head_step, cols=seg_gate),
               row_slab_job(ev_w_in, 0, n_head_steps, head_step, lanes=(seg_gate + n_gate, n_w_in)),
               row_slab_job(ev_w_out, 0, n_head_steps, head_step)])

    p, gates = proj(x, norm_mix[0], sc1, sh1, w_head[None], 0, seg_gate, w_tail, n_main, w_ab, gparams)
    cosf, sinf = rope_tables(t)
    y_gdn = gdn_latent(p, ev_conv[0], gate_rows(gates, t), cosf, sinf, s_ctx, ev_gdn_norm[0],
                       col_ka, col_va, col_qa, col_za)
    y_na, rounded = natten(p, pc, natten_bias(ev_rpb[0]), col_qb, col_kb, col_vb, col_kb, col_vb,
                           casts=[row_slab_job(w, 0, n_head_steps, head_step)
                                  for w in (ffn_w_gate, ffn_w_up, ffn_w_down, od_w_in, od_w_out)])
    ffn0, (w_in1, w_out1) = rounded[:3], rounded[3:]
    x_lat, hs = outproj(x, g1, [y_gdn, y_na], w_out0, norm_ffn[0], sc2, sh2)
    _, tf, grid = ffn_grid(x_lat, ffn_w_gate.shape[2])
    x_lat, ffn1 = ffn(x_lat, hs, g2, final_norm, *ffn0, final_norm=False,
                      casts=ffn_weight_jobs(ffn_w_gate, ffn_w_up, ffn_w_down, 1, grid, tf))

    sh1, sc1, g1, sh2, sc2, g2 = mod_vecs(1, slice(0, bn))
    p = proj(x_lat, norm_mix[1], sc1, sh1, w_in1[None], 0, 3 * d, None, 3 * d)
    x_lat, hs = outproj_conv(x_lat, g1, p, od_conv[0], w_out1, norm_ffn[1], sc2, sh2)
    x_lat, _ = ffn(x_lat, hs, g2, final_norm, *ffn1, final_norm=True)
    return x_lat
```

```python
import functools
import math

import jax
import jax.numpy as jnp
import numpy as np
from jax import lax
from jax.experimental import pallas as pl
from jax.experimental.pallas import tpu as pltpu

F32 = jnp.float32
BF16 = jnp.bfloat16
HIGHEST = lax.Precision.HIGHEST

LANES = 128
SUBLANES = 8
VMEM_LIMIT = 56 * 1024 * 1024

GRID_W = 64
HEAD_DIM = 128
GDN_HEADS = 8
NA_HEADS = 8
GDN_CONV = 5
CHUNK = 64
INVERSE_LEVELS = 6
GROUP = 2 * CHUNK
PREP_GROUPS_PER_ITER = 4
GDN_ROW_BLOCK = 512
NA_ROWS_PER_ITER = 8
NA_KH = 8
NA_KW = 16
SC_CONV = 3
ROPE_THETA = 10000.0
NORM_EPS = 1e-6
NEG_BIG = -1e30

NT_DIMS = (((1,), (1,)), ((), ()))


def _cparams(sem):
    return pltpu.CompilerParams(dimension_semantics=sem, vmem_limit_bytes=VMEM_LIMIT)


def _sigmoid(x):
    return 1.0 / (1.0 + jnp.exp(-x))


def _silu(x):
    return x * _sigmoid(x)


def _softplus(x):
    return jnp.maximum(x, 0.0) + jnp.log(1.0 + jnp.exp(-jnp.abs(x)))


def _mm(a, b):
    return jnp.dot(a.astype(BF16), b.astype(BF16), preferred_element_type=F32)


def _mm_nt(a, b):
    return lax.dot_general(a.astype(BF16), b.astype(BF16), NT_DIMS, preferred_element_type=F32)


def _mm_exact(a, b):
    return jnp.dot(a, b, precision=HIGHEST, preferred_element_type=F32)


class CastJob:
    def __init__(self, src, in_block, in_index, out_shape, out_block, out_index, transpose=False):
        self.src = src
        self.transpose = transpose
        self.in_spec = pl.BlockSpec(in_block, in_index)
        self.out_spec = pl.BlockSpec(out_block, out_index)
        self.out_shape = jax.ShapeDtypeStruct(out_shape, BF16)


def _hosting_casts(body, n_in, n_out, casts):
    n_cast = len(casts)

    def kernel(*refs):
        ins, rest = refs[:n_in], refs[n_in:]
        cast_in, rest = rest[:n_cast], rest[n_cast:]
        outs, rest = rest[:n_out], rest[n_out:]
        cast_out, scratch = rest[:n_cast], rest[n_cast:]
        for job, src, dst in zip(casts, cast_in, cast_out):
            dst[...] = (src[...].T if job.transpose else src[...]).astype(BF16)
        body(*ins, *outs, *scratch)

    return kernel


def row_slab_job(w, layer, n_steps, step_of):
    rows, cols = w.shape[1:]
    slab = rows // n_steps
    assert slab * n_steps == rows and slab % (2 * SUBLANES) == 0
    return CastJob(w, (None, slab, cols), lambda *idx: (layer, step_of(*idx), 0),
                   (rows, cols), (slab, cols), lambda *idx: (step_of(*idx), 0))


def transposing_job(w_rows, n_rows, n_steps, step_of):
    d = w_rows.shape[1]
    slab = n_rows // n_steps
    assert slab * n_steps == n_rows and slab % LANES == 0
    return CastJob(w_rows, (slab, d), lambda *idx: (step_of(*idx), 0),
                   (d, n_rows), (d, slab), lambda *idx: (0, step_of(*idx)), transpose=True)


def _call_hosting(body, casts, grid, in_specs, args, out_specs, out_shapes, **kwargs):
    n_in, n_out = len(in_specs), len(out_specs)
    casts = list(casts)
    res = pl.pallas_call(
        _hosting_casts(body, n_in, n_out, casts),
        grid=grid,
        in_specs=list(in_specs) + [c.in_spec for c in casts],
        out_specs=list(out_specs) + [c.out_spec for c in casts],
        out_shape=list(out_shapes) + [c.out_shape for c in casts],
        **kwargs,
    )(*args, *[c.src for c in casts])
    return res[:n_out], res[n_out:]


def _ada_kernel(cv_ref, w_ref, b_ref, o_ref):
    s = _silu(cv_ref[...])
    o_ref[...] = _mm(s, w_ref[...]) + b_ref[...]


def ada_modulation(cv, ada_w, ada_b, tn=1024):
    depth, d, n = ada_w.shape
    return pl.pallas_call(
        _ada_kernel,
        grid=(depth, n // tn),
        in_specs=[
            pl.BlockSpec((SUBLANES, d), lambda l, j: (0, 0)),
            pl.BlockSpec((None, d, tn), lambda l, j: (l, 0, j)),
            pl.BlockSpec((None, 1, tn), lambda l, j: (l, 0, j)),
        ],
        out_specs=pl.BlockSpec((None, SUBLANES, tn), lambda l, j: (l, 0, j)),
        out_shape=jax.ShapeDtypeStruct((depth, SUBLANES, n), F32),
        compiler_params=_cparams(("parallel", "parallel")),
        name="ada",
    )(cv, ada_w, ada_b.reshape(depth, 1, n))


def _modulated_norm(x, g, sc, sh):
    y = x * lax.rsqrt(jnp.mean(x * x, axis=-1, keepdims=True) + NORM_EPS)
    return (y * g) * (1.0 + sc) + sh


NORM_ROW_BLOCK = 128


def _modulated_norm_rows(x_ref, g_ref, sc_ref, sh_ref, hs_ref):
    rb = min(NORM_ROW_BLOCK, x_ref.shape[0])

    def body(i, _):
        rows = pl.ds(pl.multiple_of(i * rb, rb), rb)
        hs_ref[rows, :] = _modulated_norm(x_ref[rows, :], g_ref[...], sc_ref[...], sh_ref[...]).astype(BF16)
        return 0

    lax.fori_loop(0, x_ref.shape[0] // rb, body, 0)


def _proj_kernel(*refs, with_gates, with_tail, head_blocks, rows_major):
    x_ref, g_ref, sc_ref, sh_ref, w_ref = refs[:5]
    refs = refs[5:]
    wt_ref = None
    if with_tail:
        wt_ref, refs = refs[0], refs[1:]
    if with_gates:
        wab_ref, gp_ref, o_ref, gate_ref, hs_ref, wb_ref = refs
    else:
        o_ref, hs_ref, wb_ref = refs
    j = pl.program_id(2)

    @pl.when(j == 0)
    def _():
        if with_gates:
            hb = _modulated_norm(x_ref[...], g_ref[...], sc_ref[...], sh_ref[...]).astype(BF16)
            hs_ref[...] = hb
            a = _mm_nt(hb, wab_ref[...])
            neg_decay_rate = -jnp.exp(gp_ref[0:1, :])
            g = neg_decay_rate * _softplus(a + gp_ref[1:2, :])
            lane = lax.broadcasted_iota(jnp.int32, a.shape, 1)
            gate_ref[...] = jnp.where(lane < 2 * GDN_HEADS, g, _sigmoid(a))
        else:
            _modulated_norm_rows(x_ref, g_ref, sc_ref, sh_ref, hs_ref)

    def emit(wref):
        if wref.dtype != BF16:
            wb_ref[...] = wref[...].astype(BF16)
            wref = wb_ref
        if rows_major:
            y = lax.dot_general(hs_ref[...], wref[...], NT_DIMS, preferred_element_type=F32)
        else:
            y = jnp.dot(hs_ref[...], wref[...], preferred_element_type=F32)
        o_ref[...] = y.astype(o_ref.dtype)

    pl.when(j < head_blocks)(lambda: emit(w_ref))
    if with_tail:
        pl.when(j >= head_blocks)(lambda: emit(wt_ref))


def proj(x, g, sc, sh, w, layer, head_cols, w_tail, n_out, wab=None, gate_block=0, gparams=None,
         rows_major=False, tail_skip=0, tm=1024, tn=512):
    bn, t, d = x.shape
    tm = min(tm, t)
    head_blocks = head_cols // tn
    with_gates = wab is not None
    vec = pl.BlockSpec((None, 1, d), lambda b, i, j: (b, 0, 0))
    if rows_major:
        w_spec = pl.BlockSpec((tn, d), lambda b, i, j: (jnp.minimum(j, head_blocks - 1), 0))
    else:
        w_spec = pl.BlockSpec((None, d, tn), lambda b, i, j: (layer, 0, jnp.minimum(j, head_blocks - 1)))
    in_specs = [
        pl.BlockSpec((None, tm, d), lambda b, i, j: (b, i, 0)),
        pl.BlockSpec((1, d), lambda b, i, j: (0, 0)),
        vec, vec,
        w_spec,
    ]
    args = [x, g.reshape(1, d), sc, sh, w]
    with_tail = w_tail is not None
    if with_tail:
        skip = tail_skip // tn
        in_specs.append(pl.BlockSpec((d, tn), lambda b, i, j: (0, jnp.maximum(j - head_blocks, 0) + skip)))
        args.append(w_tail)
    out_specs = [pl.BlockSpec((None, tm, tn), lambda b, i, j: (b, i, j))]
    out_shape = [jax.ShapeDtypeStruct((bn, t, n_out), BF16)]
    if with_gates:
        in_specs += [pl.BlockSpec((LANES, d), lambda b, i, j: (gate_block, 0)),
                     pl.BlockSpec((SUBLANES, LANES), lambda b, i, j: (0, 0))]
        args += [wab, gparams]
        out_specs.append(pl.BlockSpec((None, tm, LANES), lambda b, i, j: (b, i, 0)))
        out_shape.append(jax.ShapeDtypeStruct((bn, t, LANES), F32))
    res = pl.pallas_call(
        functools.partial(_proj_kernel, with_gates=with_gates, with_tail=with_tail, head_blocks=head_blocks,
                          rows_major=rows_major),
        grid=(bn, t // tm, n_out // tn),
        in_specs=in_specs,
        out_specs=out_specs,
        out_shape=out_shape,
        scratch_shapes=[pltpu.VMEM((tm, d), BF16), pltpu.VMEM((tn, d) if rows_major else (d, tn), BF16)],
        compiler_params=_cparams(("parallel", "parallel", "arbitrary")),
        name="proj",
    )(*args)
    return res if with_gates else res[0]


CONV_HALO = SUBLANES


def _stage_padded(pad_ref, load_rows, t, rb):
    zeros = jnp.zeros((CONV_HALO, LANES), F32)
    pad_ref[0:CONV_HALO, :] = zeros
    pad_ref[CONV_HALO + t:2 * CONV_HALO + t, :] = zeros

    def body(i, _):
        r0 = pl.multiple_of(i * rb, rb)
        pad_ref[pl.ds(CONV_HALO + r0, rb), :] = load_rows(r0)
        return 0

    lax.fori_loop(0, t // rb, body, 0)


def _conv_rows(pad_ref, w_ref, r0, rb, taps):
    acc = None
    for j in range(taps):
        xj = pad_ref[pl.ds(r0 + (CONV_HALO + j - taps // 2), rb), :]
        term = xj * w_ref[j:j + 1, :]
        acc = term if acc is None else acc + term
    return acc


def _l2norm(x):
    return x * lax.rsqrt(jnp.sum(x * x, axis=-1, keepdims=True) + NORM_EPS)


def _gdn_kernel(*refs, t, with_q):
    nc = t // CHUNK
    rb = min(GDN_ROW_BLOCK, t)
    if with_q:
        (ka_ref, va_ref, qa_ref, za_ref, wk_ref, wv_ref, wq_ref, grow_ref, cos_ref, sin_ref, s0_ref,
         gn_ref, y_ref,
         pad_ref, pad2_ref, k_ref, v_ref, q_ref, o_ref, gc_ref, kwq_s, h_s, au_s, gl_s) = refs
    else:
        (ka_ref, va_ref, wk_ref, wv_ref, grow_ref, sfin_ref,
         pad_ref, k_ref, v_ref, gc_ref, kwq_s, h_s, gl_s) = refs

    def conv_all(src_ref, w_ref, finish, dst_ref):
        _stage_padded(pad_ref, lambda r0: src_ref[pl.ds(r0, rb), :].astype(F32), t, rb)

        def body(i, _):
            r0 = pl.multiple_of(i * rb, rb)
            y = _silu(_conv_rows(pad_ref, w_ref, r0, rb, GDN_CONV))
            dst_ref[pl.ds(r0, rb), :] = finish(y, r0)
            return 0

        lax.fori_loop(0, t // rb, body, 0)

    def rope(x, r0):
        return (x * cos_ref[pl.ds(r0, rb), :]
                + pltpu.roll(x, HEAD_DIM // 2, axis=1) * sin_ref[pl.ds(r0, rb), :])

    if with_q:
        _stage_padded(pad_ref, lambda r0: ka_ref[pl.ds(r0, rb), :].astype(F32), t, rb)
        _stage_padded(pad2_ref, lambda r0: qa_ref[pl.ds(r0, rb), :].astype(F32), t, rb)

        def kq_body(i, _):
            r0 = pl.multiple_of(i * rb, rb)
            yk = _silu(_conv_rows(pad_ref, wk_ref, r0, rb, GDN_CONV))
            yq = _silu(_conv_rows(pad2_ref, wq_ref, r0, rb, GDN_CONV))
            k_ref[pl.ds(r0, rb), :] = rope(_l2norm(yk), r0)
            q_ref[pl.ds(r0, rb), :] = rope(_l2norm(yq), r0) * HEAD_DIM ** -0.5
            return 0

        lax.fori_loop(0, t // rb, kq_body, 0)
    else:
        conv_all(ka_ref, wk_ref, lambda y, r0: _l2norm(y), k_ref)
    conv_all(va_ref, wv_ref, lambda y, r0: y, v_ref)

    ii = lax.broadcasted_iota(jnp.int32, (GROUP, GROUP), 0)
    jj = lax.broadcasted_iota(jnp.int32, (GROUP, GROUP), 1)
    same = (ii // CHUNK) == (jj // CHUNK)
    lower = same & (ii >= jj)
    upper = same & (ii <= jj)
    eye = ii == jj
    gc_ref[0] = _mm_exact(grow_ref[0], upper.astype(F32))
    gc_ref[1] = _mm_exact(grow_ref[1], lower.astype(F32))

    n_groups = t // GROUP
    gpi = min(PREP_GROUPS_PER_ITER, n_groups)
    n_sets = n_groups // gpi
    steps_per_set = 2 * gpi

    def prep_stages(i):
        chains = ([(i * gpi + k, 0) for k in range(gpi)]
                  + [(n_groups - 1 - (i * gpi + k), 1) for k in range(gpi)])
        ids = range(len(chains))
        st = {}

        def load():
            for ch, (g, d) in enumerate(chains):
                r0 = pl.multiple_of(g * GROUP, GROUP)
                st["k", ch] = k_ref[pl.ds(r0, GROUP), :]
                st["v", ch] = v_ref[pl.ds(r0, GROUP), :]
                kgb = st["k", ch].astype(BF16)
                if with_q:
                    st["q", ch] = q_ref[pl.ds(r0, GROUP), :]
                    both = _mm_nt(jnp.concatenate([st["k", ch], st["q", ch]], axis=0), kgb)
                    st["kk", ch], st["qk", ch] = both[:GROUP], both[GROUP:]
                else:
                    st["kk", ch] = _mm_nt(kgb, kgb)

        def masks():
            for ch, (g, d) in enumerate(chains):
                incl = lower if d == 0 else upper
                gc_r = jnp.broadcast_to(gc_ref[d, pl.ds(g, 1), :], (GROUP, GROUP))
                be_r = jnp.broadcast_to(grow_ref[2 + d, pl.ds(g, 1), :], (GROUP, GROUP))
                gc_c = gc_r.T
                decay = jnp.where(incl, jnp.exp(jnp.where(incl, gc_c - gc_r, 0.0)), 0.0)
                m = jnp.where(incl & jnp.logical_not(eye), st["kk", ch] * be_r.T * decay, 0.0)
                st["be_r", ch], st["gc_c", ch], st["decay", ch] = be_r, gc_c, decay
                st["p", ch] = jnp.where(eye, 1.0, -m)
                st["sq", ch] = _mm(m, m)

        def level():
            for ch in ids:
                p, sq = st["p", ch], st["sq", ch]
                st["p", ch] = p + _mm(p, sq)
                st["sq", ch] = _mm(sq, sq)

        def last_level():
            for ch in ids:
                st["p", ch] = st["p", ch] + _mm(st["p", ch], st["sq", ch])

        def solve():
            for ch in ids:
                rhs = jnp.concatenate([st["v", ch], st["k", ch] * jnp.exp(st["gc_c", ch])], axis=1)
                st["uw", ch] = _mm(st["p", ch] * st["be_r", ch], rhs)

        def fold():
            for ch, (g, d) in enumerate(chains):
                gc_c = st["gc_c", ch]
                last = [CHUNK - 1, GROUP - 1] if d == 0 else [0, CHUNK]
                gls = [gc_c[r:r + 1, :] for r in last]
                gl_c = jnp.concatenate([jnp.broadcast_to(x, (CHUNK, LANES)) for x in gls], axis=0)
                kdt = (st["k", ch] * jnp.exp(gl_c - gc_c)).T
                parts = [jnp.where(jj < CHUNK, kdt, 0.0), jnp.where(jj >= CHUNK, kdt, 0.0)]
                if with_q:
                    st["qd", ch] = st["q", ch] * jnp.exp(gc_c)
                    parts = [st["qk", ch] * st["decay", ch]] + parts
                st["gl", ch] = gls
                st["prod", ch] = _mm(jnp.concatenate(parts, axis=0), st["uw", ch])

        def store():
            for ch, (g, d) in enumerate(chains):
                r = st["prod", ch]
                off = GROUP if with_q else 0
                for half in range(2):
                    c = 2 * g + half
                    blk = r[off + half * HEAD_DIM:off + (half + 1) * HEAD_DIM]
                    h_s[d, c] = blk[:, :HEAD_DIM]
                    gl_s[d, c] = jnp.broadcast_to(jnp.exp(st["gl", ch][half]), (SUBLANES, LANES))
                    if with_q:
                        rs = slice(half * CHUNK, (half + 1) * CHUNK)
                        au_s[d, c] = r[rs, :HEAD_DIM]
                        qw = st["qd", ch][rs] - r[rs, HEAD_DIM:]
                        kwq_s[d, c] = jnp.concatenate([blk[:, HEAD_DIM:], qw], axis=0).astype(BF16)
                    else:
                        kwq_s[d, c] = blk[:, HEAD_DIM:].astype(BF16)

        return [load, masks] + [level] * (INVERSE_LEVELS - 2) + [last_level, solve, fold, store]

    def scan_step(s, carry):
        cs = (s, nc - 1 - s)
        rr = [jnp.dot(kwq_s[d, cs[d]], carry[d].astype(BF16), preferred_element_type=F32) for d in range(2)]
        new = []
        for d in range(2):
            c = cs[d]
            new.append(carry[d] * gl_s[d, c][0:1, :] + h_s[d, c] - rr[d][:HEAD_DIM])
            if with_q:
                r0 = pl.multiple_of(c * CHUNK, CHUNK)
                o_ref[pl.ds(r0, CHUNK), :] += rr[d][HEAD_DIM:] + au_s[d, c]
        return tuple(new)

    def run(stages, steps, carry):
        for idx in range(max(len(stages), len(steps))):
            if idx < len(stages):
                stages[idx]()
            if idx < len(steps):
                carry = scan_step(steps[idx], carry)
        return carry

    def set_steps(i):
        return [i * steps_per_set + k for k in range(steps_per_set)]

    if with_q:
        o_ref[...] = jnp.zeros_like(o_ref)
        carry = (s0_ref[0], s0_ref[1])
    else:
        zero = jnp.zeros((HEAD_DIM, HEAD_DIM), F32)
        carry = (zero, zero)
    run(prep_stages(0), [], carry)
    carry = lax.fori_loop(1, n_sets, lambda i, c: run(prep_stages(i), set_steps(i - 1), c), carry)
    s_f, s_b = run([], set_steps(n_sets - 1), carry)

    if not with_q:
        sfin_ref[0] = s_f
        sfin_ref[1] = s_b
        return

    def finish(i, _):
        r0 = pl.multiple_of(i * rb, rb)
        o = o_ref[pl.ds(r0, rb), :]
        y = o * lax.rsqrt(jnp.mean(o * o, axis=-1, keepdims=True) + NORM_EPS)
        y_ref[pl.ds(r0, rb), :] = ((y * gn_ref[...]) * _silu(za_ref[pl.ds(r0, rb), :].astype(F32))).astype(y_ref.dtype)
        return 0

    lax.fori_loop(0, t // rb, finish, 0)


def _single(block_shape, index_map):
    return pl.BlockSpec(block_shape, index_map, pipeline_mode=pl.Buffered(1))


def gdn_latent(p, conv_w, grow, cosf, sinf, s0, gnorm, col_k, col_v, col_q, col_z):
    bn, t, _ = p.shape
    nc = t // CHUNK
    ngp = grow.shape[3]
    hd = HEAD_DIM
    seq = lambda col: pl.BlockSpec((None, t, hd), lambda b, h, col=col: (b, 0, col + h))
    cw = lambda col: pl.BlockSpec((GDN_CONV, hd), lambda b, h, col=col: (0, col + h))
    return pl.pallas_call(
        functools.partial(_gdn_kernel, t=t, with_q=True),
        grid=(bn, GDN_HEADS),
        in_specs=[
            seq(col_k), seq(col_v), seq(col_q), seq(col_z),
            cw(0), cw(GDN_HEADS), cw(2 * GDN_HEADS),
            pl.BlockSpec((None, None, 4, ngp, GROUP), lambda b, h: (b, h, 0, 0, 0)),
            _single((t, hd), lambda b, h: (0, 0)),
            _single((t, hd), lambda b, h: (0, 0)),
            pl.BlockSpec((None, None, 2, hd, hd), lambda b, h: (b, h, 0, 0, 0)),
            pl.BlockSpec((1, hd), lambda b, h: (0, 0)),
        ],
        out_specs=pl.BlockSpec((None, t, hd), lambda b, h: (b, 0, h)),
        out_shape=jax.ShapeDtypeStruct((bn, t, GDN_HEADS * hd), BF16),
        scratch_shapes=[
            pltpu.VMEM((t + 2 * CONV_HALO, hd), F32),
            pltpu.VMEM((t + 2 * CONV_HALO, hd), F32),
            pltpu.VMEM((t, hd), F32),
            pltpu.VMEM((t, hd), F32),
            pltpu.VMEM((t, hd), F32),
            pltpu.VMEM((t, hd), F32),
            pltpu.VMEM((2, ngp, GROUP), F32),
            pltpu.VMEM((2, nc, hd + CHUNK, hd), BF16),
            pltpu.VMEM((2, nc, hd, hd), F32),
            pltpu.VMEM((2, nc, CHUNK, hd), F32),
            pltpu.VMEM((2, nc, SUBLANES, LANES), F32),
        ],
        compiler_params=_cparams(("parallel", "parallel")),
        name="gdn_latent",
    )(p, p, p, p, conv_w, conv_w, conv_w, grow, cosf, sinf, s0, gnorm.reshape(1, hd))


def gdn_context(pc, conv_w, grow, col_k, col_v, casts=()):
    bn, t, _ = pc.shape
    nc = t // CHUNK
    ngp = grow.shape[3]
    hd = HEAD_DIM
    seq = lambda col: pl.BlockSpec((None, t, hd), lambda b, h, col=col: (b, 0, col + h))
    cw = lambda col: pl.BlockSpec((GDN_CONV, hd), lambda b, h, col=col: (0, col + h))
    (states,), rounded = _call_hosting(
        lambda *refs: _gdn_kernel(*refs, t=t, with_q=False), casts,
        grid=(bn, GDN_HEADS),
        in_specs=[
            seq(col_k), seq(col_v), cw(0), cw(GDN_HEADS),
            pl.BlockSpec((None, None, 4, ngp, GROUP), lambda b, h: (b, h, 0, 0, 0)),
        ],
        args=(pc, pc, conv_w, conv_w, grow),
        out_specs=[pl.BlockSpec((None, None, 2, hd, hd), lambda b, h: (b, h, 0, 0, 0))],
        out_shapes=[jax.ShapeDtypeStruct((bn, GDN_HEADS, 2, hd, hd), F32)],
        scratch_shapes=[
            pltpu.VMEM((t + 2 * CONV_HALO, hd), F32),
            pltpu.VMEM((t, hd), F32),
            pltpu.VMEM((t, hd), F32),
            pltpu.VMEM((2, ngp, GROUP), F32),
            pltpu.VMEM((2, nc, hd, hd), BF16),
            pltpu.VMEM((2, nc, hd, hd), F32),
            pltpu.VMEM((2, nc, SUBLANES, LANES), F32),
        ],
        compiler_params=_cparams(("parallel", "parallel")),
        name="gdn_context",
    )
    return states, rounded


def gate_rows(gates, t):
    bn = gates.shape[0]
    ng = t // GROUP
    g = gates[:, :, :4 * GDN_HEADS].reshape(bn, ng, GROUP, 4, GDN_HEADS)
    g = jnp.transpose(g, (0, 4, 3, 1, 2))
    if ng < SUBLANES:
        g = jnp.pad(g, ((0, 0), (0, 0), (0, 0), (0, SUBLANES - ng), (0, 0)))
    return g


def _natten_kernel(q_ref, k_ref, v_ref, kc_ref, vc_ref, by_dr_ref, o_ref, bias_ref, *, rows):
    kcb = kc_ref[...]
    vcb = vc_ref[...]
    scale = HEAD_DIM ** -0.5
    kh = min(NA_KH, rows)
    win = kh * GRID_W
    per_iter = min(NA_ROWS_PER_ITER, rows)

    for cls in range(NA_KH):
        for j in range(kh):
            bias_ref[cls, :, j * GRID_W:(j + 1) * GRID_W] = by_dr_ref[NA_KH - 1 - cls + j]

    def body(it, _):
        rws = [it * per_iter + k for k in range(per_iter)]
        q0s, k0s, s_loc, s_ctx = [], [], [], []
        for r in rws:
            rs = jnp.clip(r - NA_KH // 2, 0, rows - NA_KH)
            q0s.append(pl.multiple_of(r * GRID_W, GRID_W))
            k0s.append(pl.multiple_of(rs * GRID_W, GRID_W))
            q = q_ref[pl.ds(q0s[-1], GRID_W), :]
            kw = k_ref[pl.ds(k0s[-1], win), :]
            s_loc.append(lax.dot_general(q, kw, NT_DIMS, preferred_element_type=F32) * scale + bias_ref[r - rs])
            s_ctx.append(lax.dot_general(q, kcb, NT_DIMS, preferred_element_type=F32) * scale)
        p_loc, p_ctx, den = [], [], []
        for a, b in zip(s_loc, s_ctx):
            m = jnp.maximum(jnp.max(a, axis=-1, keepdims=True), jnp.max(b, axis=-1, keepdims=True))
            p_loc.append(jnp.exp(a - m))
            p_ctx.append(jnp.exp(b - m))
            den.append(jnp.sum(p_loc[-1], axis=-1, keepdims=True) + jnp.sum(p_ctx[-1], axis=-1, keepdims=True))
        outs = []
        for k in range(per_iter):
            vw = v_ref[pl.ds(k0s[k], win), :]
            outs.append(jnp.dot(p_loc[k].astype(BF16), vw, preferred_element_type=F32)
                        + jnp.dot(p_ctx[k].astype(BF16), vcb, preferred_element_type=F32))
        for k in range(per_iter):
            o_ref[pl.ds(q0s[k], GRID_W), :] = (outs[k] / den[k]).astype(o_ref.dtype)
        return 0

    lax.fori_loop(0, rows // per_iter, body, 0)


def natten(p, pc, by_dr, col_q, col_k, col_v, ctx_col_k, ctx_col_v, casts=()):
    bn, t, _ = p.shape
    lc = pc.shape[1]
    hd = HEAD_DIM
    rows = t // GRID_W
    kh = min(NA_KH, rows)
    seq = lambda col: pl.BlockSpec((None, t, hd), lambda b, h, col=col: (b, 0, col + h))
    cseq = lambda col: pl.BlockSpec((None, lc, hd), lambda b, h, col=col: (b, 0, col + h))
    (y,), rounded = _call_hosting(
        lambda *refs: _natten_kernel(*refs, rows=rows), casts,
        grid=(bn, NA_HEADS),
        in_specs=[
            seq(col_q), seq(col_k), seq(col_v), cseq(ctx_col_k), cseq(ctx_col_v),
            pl.BlockSpec((None,) + by_dr.shape[1:], lambda b, h: (h, 0, 0, 0)),
        ],
        args=(p, p, p, pc, pc, by_dr),
        out_specs=[pl.BlockSpec((None, t, hd), lambda b, h: (b, 0, h))],
        out_shapes=[jax.ShapeDtypeStruct((bn, t, NA_HEADS * hd), BF16)],
        scratch_shapes=[pltpu.VMEM((NA_KH, GRID_W, kh * GRID_W), F32)],
        compiler_params=_cparams(("parallel", "parallel")),
        name="natten",
    )
    return y, rounded


def natten_bias(rpb):
    hn, n_dr, n_dc = rpb.shape
    col = np.arange(GRID_W)
    col_start = np.clip(col - NA_KW // 2, 0, GRID_W - NA_KW)
    in_win = (col[None, :] >= col_start[:, None]) & (col[None, :] < col_start[:, None] + NA_KW)
    dc = np.clip(col[None, :] - col[:, None], -(NA_KW - 1), NA_KW - 1) + (NA_KW - 1)
    onehot = (np.arange(n_dc)[:, None] == dc.reshape(-1)[None, :]).astype(np.float32)
    by_dr = jnp.dot(rpb.astype(F32).reshape(hn * n_dr, n_dc), jnp.asarray(onehot), precision=HIGHEST)
    return jnp.where(in_win[None, None], by_dr.reshape(hn, n_dr, GRID_W, GRID_W), NEG_BIG)


def _outproj_kernel(*refs, n_in):
    x_ref, gate_ref, g_ref, sc_ref, sh_ref = refs[:5]
    a_refs = refs[5:5 + n_in]
    w_refs = refs[5 + n_in:5 + 2 * n_in]
    o_ref, hs_ref = refs[5 + 2 * n_in:]
    y = None
    for a_ref, w_ref in zip(a_refs, w_refs):
        part = jnp.dot(a_ref[...], w_ref[...], preferred_element_type=F32)
        y = part if y is None else y + part
    x1 = x_ref[...] + gate_ref[...] * y
    o_ref[...] = x1
    hs_ref[...] = _modulated_norm(x1, g_ref[...], sc_ref[...], sh_ref[...]).astype(BF16)


def outproj(x, gate, acts, weight, g_next, sc_next, sh_next, tm=512):
    bn, t, d = x.shape
    n_in = len(acts)
    kdim = acts[0].shape[-1]
    assert all(a.shape[-1] == kdim for a in acts) and weight.shape[0] == n_in * kdim
    weights = [weight] * n_in
    row = pl.BlockSpec((None, tm, d), lambda b, i: (b, i, 0))
    vec = pl.BlockSpec((None, 1, d), lambda b, i: (b, 0, 0))
    in_specs = [row, vec, pl.BlockSpec((1, d), lambda b, i: (0, 0)), vec, vec]
    in_specs += [pl.BlockSpec((None, tm, kdim), lambda b, i: (b, i, 0)) for a in acts]
    in_specs += [pl.BlockSpec((kdim, d), lambda b, i, k=k: (k, 0)) for k in range(n_in)]
    return pl.pallas_call(
        functools.partial(_outproj_kernel, n_in=n_in),
        grid=(bn, t // tm),
        in_specs=in_specs,
        out_specs=[row, row],
        out_shape=[jax.ShapeDtypeStruct((bn, t, d), F32), jax.ShapeDtypeStruct((bn, t, d), BF16)],
        compiler_params=_cparams(("parallel", "parallel")),
        name="outproj",
    )(x, gate, g_next.reshape(1, d), sc_next, sh_next, *acts, *weights)


def _outproj_conv_kernel(x_ref, gate_ref, g_ref, sc_ref, sh_ref, gb_ref, gc_ref, val_ref,
                         gcp_ref, valp_ref, gcn_ref, valn_ref, cw_ref, w_ref, o_ref, hs_ref):
    i = pl.program_id(1)
    tm = x_ref.shape[0]
    u = gc_ref[...].astype(F32) * val_ref[...].astype(F32)
    last = gcp_ref.shape[0] - 1
    u_prev = gcp_ref[last:last + 1, :].astype(F32) * valp_ref[last:last + 1, :].astype(F32)
    u_next = gcn_ref[0:1, :].astype(F32) * valn_ref[0:1, :].astype(F32)
    u_prev = jnp.where(i == 0, 0.0, u_prev)
    u_next = jnp.where(i == pl.num_programs(1) - 1, 0.0, u_next)
    row = lax.broadcasted_iota(jnp.int32, (tm, 1), 0)
    below = jnp.where(row == 0, u_prev, pltpu.roll(u, 1, axis=0))
    above = jnp.where(row == tm - 1, u_next, pltpu.roll(u, tm - 1, axis=0))
    conv = below * cw_ref[0:1, :] + u * cw_ref[1:2, :] + above * cw_ref[2:3, :]
    z = (gb_ref[...].astype(F32) * conv).astype(BF16)
    x1 = x_ref[...] + gate_ref[...] * jnp.dot(z, w_ref[...], preferred_element_type=F32)
    o_ref[...] = x1
    hs_ref[...] = _modulated_norm(x1, g_ref[...], sc_ref[...], sh_ref[...]).astype(BF16)


def outproj_conv(x, gate, p, conv_w, weight, g_next, sc_next, sh_next, tm=512):
    bn, t, d = x.shape
    halo = 2 * SUBLANES
    nb = t // halo
    row = pl.BlockSpec((None, tm, d), lambda b, i: (b, i, 0))
    vec = pl.BlockSpec((None, 1, d), lambda b, i: (b, 0, 0))
    seg = lambda k: pl.BlockSpec((None, tm, d), lambda b, i, k=k: (b, i, k))
    prev = lambda k: pl.BlockSpec((None, halo, d), lambda b, i, k=k: (b, jnp.maximum(i * (tm // halo) - 1, 0), k))
    nxt = lambda k: pl.BlockSpec((None, halo, d), lambda b, i, k=k: (b, jnp.minimum((i + 1) * (tm // halo), nb - 1), k))
    return pl.pallas_call(
        _outproj_conv_kernel,
        grid=(bn, t // tm),
        in_specs=[row, vec, pl.BlockSpec((1, d), lambda b, i: (0, 0)), vec, vec,
                  seg(0), seg(1), seg(2), prev(1), prev(2), nxt(1), nxt(2),
                  pl.BlockSpec((SC_CONV, d), lambda b, i: (0, 0)),
                  pl.BlockSpec((d, d), lambda b, i: (0, 0))],
        out_specs=[row, row],
        out_shape=[jax.ShapeDtypeStruct((bn, t, d), F32), jax.ShapeDtypeStruct((bn, t, d), BF16)],
        compiler_params=_cparams(("parallel", "parallel")),
        name="outproj_conv",
    )(x, gate, g_next.reshape(1, d), sc_next, sh_next, p, p, p, p, p, p, p, conv_w, weight)


def _ffn_kernel(x_ref, hs_ref, gate_ref, fn_ref, wg_ref, wu_ref, wd_ref, o_ref, *, final_norm):
    j = pl.program_id(2)

    @pl.when(j == 0)
    def _():
        o_ref[...] = jnp.zeros_like(o_ref)

    h = hs_ref[...]
    gate = jnp.dot(h, wg_ref[...], preferred_element_type=F32)
    up = jnp.dot(h, wu_ref[...], preferred_element_type=F32)
    a = (_silu(gate) * up).astype(BF16)
    o_ref[...] += jnp.dot(a, wd_ref[...], preferred_element_type=F32)

    @pl.when(j == pl.num_programs(2) - 1)
    def _():
        rb = min(NORM_ROW_BLOCK, o_ref.shape[0])

        def body(i, _):
            rows = pl.ds(pl.multiple_of(i * rb, rb), rb)
            y = x_ref[rows, :] + gate_ref[...] * o_ref[rows, :]
            if final_norm:
                y = (y * lax.rsqrt(jnp.mean(y * y, axis=-1, keepdims=True) + NORM_EPS)) * fn_ref[...]
            o_ref[rows, :] = y
            return 0

        lax.fori_loop(0, o_ref.shape[0] // rb, body, 0)


def ffn_grid(x, f, tm=1024, tf=512):
    bn, t, _ = x.shape
    tm = min(tm, t)
    return tm, tf, (bn, t // tm, f // tf)


def ffn_weight_jobs(w_gate, w_up, w_down, layer, grid, tf):
    bn, ni, nj = grid
    d = w_gate.shape[1]
    slab = d // (bn * ni)
    assert slab * bn * ni == d and slab % LANES == 0
    tile = lambda b, i: b * ni + i
    up_job = lambda w: CastJob(w, (None, slab, tf), lambda b, i, j: (layer, tile(b, i), j),
                               w.shape[1:], (slab, tf), lambda b, i, j: (tile(b, i), j))
    down_job = CastJob(w_down, (None, tf, slab), lambda b, i, j: (layer, j, tile(b, i)),
                       w_down.shape[1:], (tf, slab), lambda b, i, j: (j, tile(b, i)))
    return [up_job(w_gate), up_job(w_up), down_job]


def ffn(x, hs, gate, fnorm, w_gate, w_up, w_down, final_norm, casts=()):
    bn, t, d = x.shape
    f = w_gate.shape[1]
    tm, tf, grid = ffn_grid(x, f)
    vec = pl.BlockSpec((None, 1, d), lambda b, i, j: (b, 0, 0))
    one = pl.BlockSpec((1, d), lambda b, i, j: (0, 0))
    (y,), rounded = _call_hosting(
        functools.partial(_ffn_kernel, final_norm=final_norm), casts,
        grid=grid,
        in_specs=[
            _single((None, tm, d), lambda b, i, j: (b, i, 0)),
            pl.BlockSpec((None, tm, d), lambda b, i, j: (b, i, 0)),
            vec, one,
            pl.BlockSpec((d, tf), lambda b, i, j: (0, j)),
            pl.BlockSpec((d, tf), lambda b, i, j: (0, j)),
            pl.BlockSpec((tf, d), lambda b, i, j: (j, 0)),
        ],
        args=(x, hs, gate, fnorm.reshape(1, d), w_gate, w_up, w_down),
        out_specs=[pl.BlockSpec((None, tm, d), lambda b, i, j: (b, i, 0))],
        out_shapes=[jax.ShapeDtypeStruct((bn, t, d), F32)],
        compiler_params=_cparams(("parallel", "parallel", "arbitrary")),
        name="ffn",
    )
    return y, rounded


def rope_tables(t):
    pos = np.arange(t)
    row = (pos // GRID_W).astype(np.float32)
    col = (pos % GRID_W).astype(np.float32)
    n_freq = HEAD_DIM // 4
    inv_freq = jnp.asarray(ROPE_THETA, F32) ** (-jnp.arange(n_freq, dtype=F32) / n_freq)
    ang = jnp.concatenate([jnp.asarray(row)[:, None] * inv_freq, jnp.asarray(col)[:, None] * inv_freq], axis=-1)
    cos, sin = jnp.cos(ang), jnp.sin(ang)
    return jnp.concatenate([cos, cos], axis=-1), jnp.concatenate([-sin, sin], axis=-1)


def kernel(x, c, ctx, c_ctx, ada_w, ada_b, norm_mix, norm_ffn, ffn_w_gate, ffn_w_up, ffn_w_down, final_norm,
           ev_w_in, ev_conv, ev_a_log, ev_dt_bias, ev_gdn_norm, ev_rpb, ev_w_out, od_w_in, od_conv, od_w_out):
    bn, t, d = x.shape
    depth = ada_w.shape[0]
    assert depth == 2
    gw = GDN_HEADS * HEAD_DIM
    nw = NA_HEADS * HEAD_DIM
    nh = GDN_HEADS

    cv = jnp.zeros((SUBLANES, d), F32).at[:bn].set(c).at[bn].set(c_ctx)
    mods = ada_modulation(cv, ada_w, ada_b)

    def mod_vecs(l, rows):
        m = mods[l, rows].reshape(-1, 6, d)
        return [m[:, k][:, None, :] for k in range(6)]

    sh1, sc1, g1, sh2, sc2, g2 = mod_vecs(0, slice(0, bn))
    csh1, csc1 = [jnp.broadcast_to(v, (bn, 1, d)) for v in mod_vecs(0, slice(bn, bn + 1))[:2]]
    n_gate = 4 * nh
    seg_gate = 2 * gw + 2 * nw
    w_rows = jnp.swapaxes(ev_w_in[0], 0, 1)
    n_tail = w_rows.shape[0] - seg_gate - n_gate
    n_main = seg_gate + n_tail
    gate_block = seg_gate // LANES
    n_head_steps = bn * nh
    tail_slab = -(-n_tail // (n_head_steps * LANES)) * LANES
    tail_skip = n_head_steps * tail_slab - n_tail
    w_tail_rows = w_rows[w_rows.shape[0] - n_head_steps * tail_slab:]
    gparams = jnp.zeros((SUBLANES, LANES), F32)
    gparams = gparams.at[0, :2 * nh].set(ev_a_log[0].reshape(-1)).at[1, :2 * nh].set(ev_dt_bias[0].reshape(-1))
    col_ka, col_va, col_kb, col_vb, col_qa, col_qb, col_za = [k * nh for k in range(7)]

    head_step = lambda b, h: b * nh + h
    pc, gates_c = proj(ctx, norm_mix[0], csc1, csh1, w_rows, 0, seg_gate, None, seg_gate, w_rows, gate_block,
                       gparams, rows_major=True)
    s_ctx, (w_head, w_tail, w_out0) = gdn_context(
        pc, ev_conv[0], gate_rows(gates_c, ctx.shape[1]), col_ka, col_va,
        casts=[transposing_job(w_rows, seg_gate, n_head_steps, head_step),
               transposing_job(w_tail_rows, w_tail_rows.shape[0], n_head_steps, head_step),
               row_slab_job(ev_w_out, 0, n_head_steps, head_step)])

    p, gates = proj(x, norm_mix[0], sc1, sh1, w_head[None], 0, seg_gate, w_tail, n_main, w_rows, gate_block,
                    gparams, tail_skip=tail_skip)
    cosf, sinf = rope_tables(t)
    y_gdn = gdn_latent(p, ev_conv[0], gate_rows(gates, t), cosf, sinf, s_ctx, ev_gdn_norm[0],
                       col_ka, col_va, col_qa, col_za)
    y_na, rounded = natten(p, pc, natten_bias(ev_rpb[0]), col_qb, col_kb, col_vb, col_kb, col_vb,
                           casts=[row_slab_job(w, 0, n_head_steps, head_step)
                                  for w in (ffn_w_gate, ffn_w_up, ffn_w_down, od_w_in, od_w_out)])
    ffn0, (w_in1, w_out1) = rounded[:3], rounded[3:]
    x_lat, hs = outproj(x, g1, [y_gdn, y_na], w_out0, norm_ffn[0], sc2, sh2)
    _, tf, grid = ffn_grid(x_lat, ffn_w_gate.shape[2])
    x_lat, ffn1 = ffn(x_lat, hs, g2, final_norm, *ffn0, final_norm=False,
                      casts=ffn_weight_jobs(ffn_w_gate, ffn_w_up, ffn_w_down, 1, grid, tf))

    sh1, sc1, g1, sh2, sc2, g2 = mod_vecs(1, slice(0, bn))
    p = proj(x_lat, norm_mix[1], sc1, sh1, w_in1[None], 0, 3 * d, None, 3 * d)
    x_lat, hs = outproj_conv(x_lat, g1, p, od_conv[0], w_out1, norm_ffn[1], sc2, sh2)
    x_lat, _ = ffn(x_lat, hs, g2, final_norm, *ffn1, final_norm=True)
    return x_lat
```

```python
import functools
import math

import jax
import jax.numpy as jnp
import numpy as np
from jax import lax
from jax.experimental import pallas as pl
from jax.experimental.pallas import tpu as pltpu

F32 = jnp.float32
BF16 = jnp.bfloat16
HIGHEST = lax.Precision.HIGHEST

LANES = 128
SUBLANES = 8
VMEM_LIMIT = 56 * 1024 * 1024

GRID_W = 64
HEAD_DIM = 128
GDN_HEADS = 8
NA_HEADS = 8
GDN_CONV = 5
CHUNK = 64
INVERSE_LEVELS = 6
GROUP = 2 * CHUNK
PREP_GROUPS_PER_ITER = 4
GDN_ROW_BLOCK = 512
NA_ROWS_PER_ITER = 8
NA_KH = 8
NA_KW = 16
SC_CONV = 3
ROPE_THETA = 10000.0
NORM_EPS = 1e-6
NEG_BIG = -1e30

NT_DIMS = (((1,), (1,)), ((), ()))


def _cparams(sem):
    return pltpu.CompilerParams(dimension_semantics=sem, vmem_limit_bytes=VMEM_LIMIT)


def _sigmoid(x):
    return 1.0 / (1.0 + jnp.exp(-x))


def _silu(x):
    return x * _sigmoid(x)


def _softplus(x):
    return jnp.maximum(x, 0.0) + jnp.log(1.0 + jnp.exp(-jnp.abs(x)))


def _mm(a, b):
    return jnp.dot(a.astype(BF16), b.astype(BF16), preferred_element_type=F32)


def _mm_nt(a, b):
    return lax.dot_general(a.astype(BF16), b.astype(BF16), NT_DIMS, preferred_element_type=F32)


def _mm_exact(a, b):
    return jnp.dot(a, b, precision=HIGHEST, preferred_element_type=F32)


class CastJob:
    def __init__(self, src, in_block, in_index, out_shape, out_block, out_index, transpose=False,
                 row_offset=0, next_rows=None):
        self.transpose = transpose
        self.row_offset = row_offset
        self.srcs = [src]
        self.in_specs = [pl.BlockSpec(in_block, in_index)]
        if row_offset:
            self.srcs.append(src)
            self.in_specs.append(pl.BlockSpec(*next_rows))
        self.out_spec = pl.BlockSpec(out_block, out_index)
        self.out_shape = jax.ShapeDtypeStruct(out_shape, BF16)


def _hosting_casts(body, n_in, n_out, casts):
    n_cast_in = sum(len(job.srcs) for job in casts)

    def kernel(*refs):
        ins, rest = refs[:n_in], refs[n_in:]
        cast_in, rest = list(rest[:n_cast_in]), rest[n_cast_in:]
        outs, rest = rest[:n_out], rest[n_out:]
        cast_out, scratch = rest[:len(casts)], rest[len(casts):]
        for job, dst in zip(casts, cast_out):
            src = cast_in.pop(0)
            block = src[...]
            if job.row_offset:
                block = jnp.concatenate([src[job.row_offset:, :], cast_in.pop(0)[...]], axis=0)
            dst[...] = (block.T if job.transpose else block).astype(BF16)
        body(*ins, *outs, *scratch)

    return kernel


def row_slab_job(w, layer, n_steps, step_of):
    rows, cols = w.shape[1:]
    slab = rows // n_steps
    assert slab * n_steps == rows and slab % (2 * SUBLANES) == 0
    return CastJob(w, (None, slab, cols), lambda *idx: (layer, step_of(*idx), 0),
                   (rows, cols), (slab, cols), lambda *idx: (step_of(*idx), 0))


def transposing_job(w_rows, first_row, n_rows, n_steps, step_of):
    d = w_rows.shape[1]
    slab = n_rows // n_steps
    assert slab * n_steps == n_rows and slab % LANES == 0
    first_block, offset = divmod(first_row, slab)
    assert offset % SUBLANES == 0
    next_rows = None
    if offset:
        assert slab % offset == 0
        per = slab // offset
        next_rows = ((offset, d), lambda *idx: ((first_block + step_of(*idx) + 1) * per, 0))
    return CastJob(w_rows, (slab, d), lambda *idx: (first_block + step_of(*idx), 0),
                   (d, n_rows), (d, slab), lambda *idx: (0, step_of(*idx)), transpose=True,
                   row_offset=offset, next_rows=next_rows)


def _call_hosting(body, casts, grid, in_specs, args, out_specs, out_shapes, **kwargs):
    n_in, n_out = len(in_specs), len(out_specs)
    casts = list(casts)
    res = pl.pallas_call(
        _hosting_casts(body, n_in, n_out, casts),
        grid=grid,
        in_specs=list(in_specs) + [s for c in casts for s in c.in_specs],
        out_specs=list(out_specs) + [c.out_spec for c in casts],
        out_shape=list(out_shapes) + [c.out_shape for c in casts],
        **kwargs,
    )(*args, *[s for c in casts for s in c.srcs])
    return res[:n_out], res[n_out:]


def _ada_kernel(cv_ref, w_ref, b_ref, o_ref):
    s = _silu(cv_ref[...])
    o_ref[...] = _mm(s, w_ref[...]) + b_ref[...]


def ada_modulation(cv, ada_w, ada_b, tn=1024):
    depth, d, n = ada_w.shape
    return pl.pallas_call(
        _ada_kernel,
        grid=(depth, n // tn),
        in_specs=[
            pl.BlockSpec((SUBLANES, d), lambda l, j: (0, 0)),
            pl.BlockSpec((None, d, tn), lambda l, j: (l, 0, j)),
            pl.BlockSpec((None, 1, tn), lambda l, j: (l, 0, j)),
        ],
        out_specs=pl.BlockSpec((None, SUBLANES, tn), lambda l, j: (l, 0, j)),
        out_shape=jax.ShapeDtypeStruct((depth, SUBLANES, n), F32),
        compiler_params=_cparams(("parallel", "parallel")),
        name="ada",
    )(cv, ada_w, ada_b.reshape(depth, 1, n))


def _modulated_norm(x, g, sc, sh):
    y = x * lax.rsqrt(jnp.mean(x * x, axis=-1, keepdims=True) + NORM_EPS)
    return (y * g) * (1.0 + sc) + sh


NORM_ROW_BLOCK = 128


def _modulated_norm_rows(x_ref, g_ref, sc_ref, sh_ref, hs_ref):
    rb = min(NORM_ROW_BLOCK, x_ref.shape[0])

    def body(i, _):
        rows = pl.ds(pl.multiple_of(i * rb, rb), rb)
        hs_ref[rows, :] = _modulated_norm(x_ref[rows, :], g_ref[...], sc_ref[...], sh_ref[...]).astype(BF16)
        return 0

    lax.fori_loop(0, x_ref.shape[0] // rb, body, 0)


def _proj_kernel(*refs, with_gates, with_tail, head_blocks, rows_major):
    x_ref, g_ref, sc_ref, sh_ref, w_ref = refs[:5]
    refs = refs[5:]
    wt_ref = None
    if with_tail:
        wt_ref, refs = refs[0], refs[1:]
    if with_gates:
        wab_ref, gp_ref, o_ref, gate_ref, hs_ref, wb_ref = refs
    else:
        o_ref, hs_ref, wb_ref = refs
    j = pl.program_id(2)

    @pl.when(j == 0)
    def _():
        if with_gates:
            hb = _modulated_norm(x_ref[...], g_ref[...], sc_ref[...], sh_ref[...]).astype(BF16)
            hs_ref[...] = hb
            a = _mm_nt(hb, wab_ref[...])
            neg_decay_rate = -jnp.exp(gp_ref[0:1, :])
            g = neg_decay_rate * _softplus(a + gp_ref[1:2, :])
            lane = lax.broadcasted_iota(jnp.int32, a.shape, 1)
            gate_ref[...] = jnp.where(lane < 2 * GDN_HEADS, g, _sigmoid(a))
        else:
            _modulated_norm_rows(x_ref, g_ref, sc_ref, sh_ref, hs_ref)

    def emit(wref):
        if wref.dtype != BF16:
            wb_ref[...] = wref[...].astype(BF16)
            wref = wb_ref
        if rows_major:
            y = lax.dot_general(hs_ref[...], wref[...], NT_DIMS, preferred_element_type=F32)
        else:
            y = jnp.dot(hs_ref[...], wref[...], preferred_element_type=F32)
        o_ref[...] = y.astype(o_ref.dtype)

    pl.when(j < head_blocks)(lambda: emit(w_ref))
    if with_tail:
        pl.when(j >= head_blocks)(lambda: emit(wt_ref))


def proj(x, g, sc, sh, w, layer, head_cols, w_tail, n_out, wab=None, gate_block=0, gparams=None,
         rows_major=False, tail_skip=0, tm=1024, tn=512):
    bn, t, d = x.shape
    tm = min(tm, t)
    head_blocks = head_cols // tn
    with_gates = wab is not None
    vec = pl.BlockSpec((None, 1, d), lambda b, i, j: (b, 0, 0))
    if rows_major:
        w_spec = pl.BlockSpec((tn, d), lambda b, i, j: (jnp.minimum(j, head_blocks - 1), 0))
    else:
        w_spec = pl.BlockSpec((None, d, tn), lambda b, i, j: (layer, 0, jnp.minimum(j, head_blocks - 1)))
    in_specs = [
        pl.BlockSpec((None, tm, d), lambda b, i, j: (b, i, 0)),
        pl.BlockSpec((1, d), lambda b, i, j: (0, 0)),
        vec, vec,
        w_spec,
    ]
    args = [x, g.reshape(1, d), sc, sh, w]
    with_tail = w_tail is not None
    if with_tail:
        skip = tail_skip // tn
        in_specs.append(pl.BlockSpec((d, tn), lambda b, i, j: (0, jnp.maximum(j - head_blocks, 0) + skip)))
        args.append(w_tail)
    out_specs = [pl.BlockSpec((None, tm, tn), lambda b, i, j: (b, i, j))]
    out_shape = [jax.ShapeDtypeStruct((bn, t, n_out), BF16)]
    if with_gates:
        in_specs += [pl.BlockSpec((LANES, d), lambda b, i, j: (gate_block, 0)),
                     pl.BlockSpec((SUBLANES, LANES), lambda b, i, j: (0, 0))]
        args += [wab, gparams]
        out_specs.append(pl.BlockSpec((None, tm, LANES), lambda b, i, j: (b, i, 0)))
        out_shape.append(jax.ShapeDtypeStruct((bn, t, LANES), F32))
    res = pl.pallas_call(
        functools.partial(_proj_kernel, with_gates=with_gates, with_tail=with_tail, head_blocks=head_blocks,
                          rows_major=rows_major),
        grid=(bn, t // tm, n_out // tn),
        in_specs=in_specs,
        out_specs=out_specs,
        out_shape=out_shape,
        scratch_shapes=[pltpu.VMEM((tm, d), BF16), pltpu.VMEM((tn, d) if rows_major else (d, tn), BF16)],
        compiler_params=_cparams(("parallel", "parallel", "arbitrary")),
        name="proj",
    )(*args)
    return res if with_gates else res[0]


CONV_HALO = SUBLANES


def _stage_padded(pad_ref, load_rows, t, rb):
    zeros = jnp.zeros((CONV_HALO, LANES), F32)
    pad_ref[0:CONV_HALO, :] = zeros
    pad_ref[CONV_HALO + t:2 * CONV_HALO + t, :] = zeros

    def body(i, _):
        r0 = pl.multiple_of(i * rb, rb)
        pad_ref[pl.ds(CONV_HALO + r0, rb), :] = load_rows(r0)
        return 0

    lax.fori_loop(0, t // rb, body, 0)


def _conv_rows(pad_ref, w_ref, r0, rb, taps):
    acc = None
    for j in range(taps):
        xj = pad_ref[pl.ds(r0 + (CONV_HALO + j - taps // 2), rb), :]
        term = xj * w_ref[j:j + 1, :]
        acc = term if acc is None else acc + term
    return acc


def _l2norm(x):
    return x * lax.rsqrt(jnp.sum(x * x, axis=-1, keepdims=True) + NORM_EPS)


def _gdn_kernel(*refs, t, with_q):
    nc = t // CHUNK
    rb = min(GDN_ROW_BLOCK, t)
    if with_q:
        (ka_ref, va_ref, qa_ref, za_ref, wk_ref, wv_ref, wq_ref, grow_ref, cos_ref, sin_ref, s0_ref,
         gn_ref, y_ref,
         pad_ref, pad2_ref, k_ref, v_ref, q_ref, o_ref, gc_ref, kwq_s, h_s, au_s, gl_s) = refs
    else:
        (ka_ref, va_ref, wk_ref, wv_ref, grow_ref, sfin_ref,
         pad_ref, k_ref, v_ref, gc_ref, kwq_s, h_s, gl_s) = refs

    def conv_all(src_ref, w_ref, finish, dst_ref):
        _stage_padded(pad_ref, lambda r0: src_ref[pl.ds(r0, rb), :].astype(F32), t, rb)

        def body(i, _):
            r0 = pl.multiple_of(i * rb, rb)
            y = _silu(_conv_rows(pad_ref, w_ref, r0, rb, GDN_CONV))
            dst_ref[pl.ds(r0, rb), :] = finish(y, r0)
            return 0

        lax.fori_loop(0, t // rb, body, 0)

    def rope(x, r0):
        return (x * cos_ref[pl.ds(r0, rb), :]
                + pltpu.roll(x, HEAD_DIM // 2, axis=1) * sin_ref[pl.ds(r0, rb), :])

    if with_q:
        _stage_padded(pad_ref, lambda r0: ka_ref[pl.ds(r0, rb), :].astype(F32), t, rb)
        _stage_padded(pad2_ref, lambda r0: qa_ref[pl.ds(r0, rb), :].astype(F32), t, rb)

        def kq_body(i, _):
            r0 = pl.multiple_of(i * rb, rb)
            yk = _silu(_conv_rows(pad_ref, wk_ref, r0, rb, GDN_CONV))
            yq = _silu(_conv_rows(pad2_ref, wq_ref, r0, rb, GDN_CONV))
            k_ref[pl.ds(r0, rb), :] = rope(_l2norm(yk), r0)
            q_ref[pl.ds(r0, rb), :] = rope(_l2norm(yq), r0) * HEAD_DIM ** -0.5
            return 0

        lax.fori_loop(0, t // rb, kq_body, 0)
    else:
        conv_all(ka_ref, wk_ref, lambda y, r0: _l2norm(y), k_ref)
    conv_all(va_ref, wv_ref, lambda y, r0: y, v_ref)

    ii = lax.broadcasted_iota(jnp.int32, (GROUP, GROUP), 0)
    jj = lax.broadcasted_iota(jnp.int32, (GROUP, GROUP), 1)
    same = (ii // CHUNK) == (jj // CHUNK)
    lower = same & (ii >= jj)
    upper = same & (ii <= jj)
    eye = ii == jj
    gc_ref[0] = _mm_exact(grow_ref[0], upper.astype(F32))
    gc_ref[1] = _mm_exact(grow_ref[1], lower.astype(F32))

    n_groups = t // GROUP
    gpi = min(PREP_GROUPS_PER_ITER, n_groups)
    n_sets = n_groups // gpi
    steps_per_set = 2 * gpi

    def prep_stages(i):
        chains = ([(i * gpi + k, 0) for k in range(gpi)]
                  + [(n_groups - 1 - (i * gpi + k), 1) for k in range(gpi)])
        ids = range(len(chains))
        st = {}

        def load():
            for ch, (g, d) in enumerate(chains):
                r0 = pl.multiple_of(g * GROUP, GROUP)
                st["k", ch] = k_ref[pl.ds(r0, GROUP), :]
                st["v", ch] = v_ref[pl.ds(r0, GROUP), :]
                kgb = st["k", ch].astype(BF16)
                if with_q:
                    st["q", ch] = q_ref[pl.ds(r0, GROUP), :]
                    both = _mm_nt(jnp.concatenate([st["k", ch], st["q", ch]], axis=0), kgb)
                    st["kk", ch], st["qk", ch] = both[:GROUP], both[GROUP:]
                else:
                    st["kk", ch] = _mm_nt(kgb, kgb)

        def masks():
            for ch, (g, d) in enumerate(chains):
                incl = lower if d == 0 else upper
                gc_r = jnp.broadcast_to(gc_ref[d, pl.ds(g, 1), :], (GROUP, GROUP))
                be_r = jnp.broadcast_to(grow_ref[2 + d, pl.ds(g, 1), :], (GROUP, GROUP))
                gc_c = gc_r.T
                decay = jnp.where(incl, jnp.exp(jnp.where(incl, gc_c - gc_r, 0.0)), 0.0)
                m = jnp.where(incl & jnp.logical_not(eye), st["kk", ch] * be_r.T * decay, 0.0)
                st["be_r", ch], st["gc_c", ch], st["decay", ch] = be_r, gc_c, decay
                st["p", ch] = jnp.where(eye, 1.0, -m)
                st["sq", ch] = _mm(m, m)

        def level():
            for ch in ids:
                p, sq = st["p", ch], st["sq", ch]
                st["p", ch] = p + _mm(p, sq)
                st["sq", ch] = _mm(sq, sq)

        def last_level():
            for ch in ids:
                st["p", ch] = st["p", ch] + _mm(st["p", ch], st["sq", ch])

        def solve():
            for ch in ids:
                rhs = jnp.concatenate([st["v", ch], st["k", ch] * jnp.exp(st["gc_c", ch])], axis=1)
                st["uw", ch] = _mm(st["p", ch] * st["be_r", ch], rhs)

        def fold():
            for ch, (g, d) in enumerate(chains):
                gc_c = st["gc_c", ch]
                last = [CHUNK - 1, GROUP - 1] if d == 0 else [0, CHUNK]
                gls = [gc_c[r:r + 1, :] for r in last]
                gl_c = jnp.concatenate([jnp.broadcast_to(x, (CHUNK, LANES)) for x in gls], axis=0)
                kdt = (st["k", ch] * jnp.exp(gl_c - gc_c)).T
                parts = [jnp.where(jj < CHUNK, kdt, 0.0), jnp.where(jj >= CHUNK, kdt, 0.0)]
                if with_q:
                    st["qd", ch] = st["q", ch] * jnp.exp(gc_c)
                    parts = [st["qk", ch] * st["decay", ch]] + parts
                st["gl", ch] = gls
                st["prod", ch] = _mm(jnp.concatenate(parts, axis=0), st["uw", ch])

        def store():
            for ch, (g, d) in enumerate(chains):
                r = st["prod", ch]
                off = GROUP if with_q else 0
                for half in range(2):
                    c = 2 * g + half
                    blk = r[off + half * HEAD_DIM:off + (half + 1) * HEAD_DIM]
                    h_s[d, c] = blk[:, :HEAD_DIM]
                    gl_s[d, c] = jnp.broadcast_to(jnp.exp(st["gl", ch][half]), (SUBLANES, LANES))
                    if with_q:
                        rs = slice(half * CHUNK, (half + 1) * CHUNK)
                        au_s[d, c] = r[rs, :HEAD_DIM]
                        qw = st["qd", ch][rs] - r[rs, HEAD_DIM:]
                        kwq_s[d, c] = jnp.concatenate([blk[:, HEAD_DIM:], qw], axis=0).astype(BF16)
                    else:
                        kwq_s[d, c] = blk[:, HEAD_DIM:].astype(BF16)

        return [load, masks] + [level] * (INVERSE_LEVELS - 2) + [last_level, solve, fold, store]

    def scan_step(s, carry):
        cs = (s, nc - 1 - s)
        rr = [jnp.dot(kwq_s[d, cs[d]], carry[d].astype(BF16), preferred_element_type=F32) for d in range(2)]
        new = []
        for d in range(2):
            c = cs[d]
            new.append(carry[d] * gl_s[d, c][0:1, :] + h_s[d, c] - rr[d][:HEAD_DIM])
            if with_q:
                r0 = pl.multiple_of(c * CHUNK, CHUNK)
                o_ref[pl.ds(r0, CHUNK), :] += rr[d][HEAD_DIM:] + au_s[d, c]
        return tuple(new)

    def run(stages, steps, carry):
        for idx in range(max(len(stages), len(steps))):
            if idx < len(stages):
                stages[idx]()
            if idx < len(steps):
                carry = scan_step(steps[idx], carry)
        return carry

    def set_steps(i):
        return [i * steps_per_set + k for k in range(steps_per_set)]

    if with_q:
        o_ref[...] = jnp.zeros_like(o_ref)
        carry = (s0_ref[0], s0_ref[1])
    else:
        zero = jnp.zeros((HEAD_DIM, HEAD_DIM), F32)
        carry = (zero, zero)
    run(prep_stages(0), [], carry)
    carry = lax.fori_loop(1, n_sets, lambda i, c: run(prep_stages(i), set_steps(i - 1), c), carry)
    s_f, s_b = run([], set_steps(n_sets - 1), carry)

    if not with_q:
        sfin_ref[0] = s_f
        sfin_ref[1] = s_b
        return

    def finish(i, _):
        r0 = pl.multiple_of(i * rb, rb)
        o = o_ref[pl.ds(r0, rb), :]
        y = o * lax.rsqrt(jnp.mean(o * o, axis=-1, keepdims=True) + NORM_EPS)
        y_ref[pl.ds(r0, rb), :] = ((y * gn_ref[...]) * _silu(za_ref[pl.ds(r0, rb), :].astype(F32))).astype(y_ref.dtype)
        return 0

    lax.fori_loop(0, t // rb, finish, 0)


def _single(block_shape, index_map):
    return pl.BlockSpec(block_shape, index_map, pipeline_mode=pl.Buffered(1))


def gdn_latent(p, conv_w, grow, cosf, sinf, s0, gnorm, col_k, col_v, col_q, col_z):
    bn, t, _ = p.shape
    nc = t // CHUNK
    ngp = grow.shape[3]
    hd = HEAD_DIM
    seq = lambda col: pl.BlockSpec((None, t, hd), lambda b, h, col=col: (b, 0, col + h))
    cw = lambda col: pl.BlockSpec((GDN_CONV, hd), lambda b, h, col=col: (0, col + h))
    return pl.pallas_call(
        functools.partial(_gdn_kernel, t=t, with_q=True),
        grid=(bn, GDN_HEADS),
        in_specs=[
            seq(col_k), seq(col_v), seq(col_q), seq(col_z),
            cw(0), cw(GDN_HEADS), cw(2 * GDN_HEADS),
            pl.BlockSpec((None, None, 4, ngp, GROUP), lambda b, h: (b, h, 0, 0, 0)),
            _single((t, hd), lambda b, h: (0, 0)),
            _single((t, hd), lambda b, h: (0, 0)),
            pl.BlockSpec((None, None, 2, hd, hd), lambda b, h: (b, h, 0, 0, 0)),
            pl.BlockSpec((1, hd), lambda b, h: (0, 0)),
        ],
        out_specs=pl.BlockSpec((None, t, hd), lambda b, h: (b, 0, h)),
        out_shape=jax.ShapeDtypeStruct((bn, t, GDN_HEADS * hd), BF16),
        scratch_shapes=[
            pltpu.VMEM((t + 2 * CONV_HALO, hd), F32),
            pltpu.VMEM((t + 2 * CONV_HALO, hd), F32),
            pltpu.VMEM((t, hd), F32),
            pltpu.VMEM((t, hd), F32),
            pltpu.VMEM((t, hd), F32),
            pltpu.VMEM((t, hd), F32),
            pltpu.VMEM((2, ngp, GROUP), F32),
            pltpu.VMEM((2, nc, hd + CHUNK, hd), BF16),
            pltpu.VMEM((2, nc, hd, hd), F32),
            pltpu.VMEM((2, nc, CHUNK, hd), F32),
            pltpu.VMEM((2, nc, SUBLANES, LANES), F32),
        ],
        compiler_params=_cparams(("parallel", "parallel")),
        name="gdn_latent",
    )(p, p, p, p, conv_w, conv_w, conv_w, grow, cosf, sinf, s0, gnorm.reshape(1, hd))


def gdn_context(pc, conv_w, grow, col_k, col_v, casts=()):
    bn, t, _ = pc.shape
    nc = t // CHUNK
    ngp = grow.shape[3]
    hd = HEAD_DIM
    seq = lambda col: pl.BlockSpec((None, t, hd), lambda b, h, col=col: (b, 0, col + h))
    cw = lambda col: pl.BlockSpec((GDN_CONV, hd), lambda b, h, col=col: (0, col + h))
    (states,), rounded = _call_hosting(
        lambda *refs: _gdn_kernel(*refs, t=t, with_q=False), casts,
        grid=(bn, GDN_HEADS),
        in_specs=[
            seq(col_k), seq(col_v), cw(0), cw(GDN_HEADS),
            pl.BlockSpec((None, None, 4, ngp, GROUP), lambda b, h: (b, h, 0, 0, 0)),
        ],
        args=(pc, pc, conv_w, conv_w, grow),
        out_specs=[pl.BlockSpec((None, None, 2, hd, hd), lambda b, h: (b, h, 0, 0, 0))],
        out_shapes=[jax.ShapeDtypeStruct((bn, GDN_HEADS, 2, hd, hd), F32)],
        scratch_shapes=[
            pltpu.VMEM((t + 2 * CONV_HALO, hd), F32),
            pltpu.VMEM((t, hd), F32),
            pltpu.VMEM((t, hd), F32),
            pltpu.VMEM((2, ngp, GROUP), F32),
            pltpu.VMEM((2, nc, hd, hd), BF16),
            pltpu.VMEM((2, nc, hd, hd), F32),
            pltpu.VMEM((2, nc, SUBLANES, LANES), F32),
        ],
        compiler_params=_cparams(("parallel", "parallel")),
        name="gdn_context",
    )
    return states, rounded


def gate_rows(gates, t):
    bn = gates.shape[0]
    ng = t // GROUP
    g = gates[:, :, :4 * GDN_HEADS].reshape(bn, ng, GROUP, 4, GDN_HEADS)
    g = jnp.transpose(g, (0, 4, 3, 1, 2))
    if ng < SUBLANES:
        g = jnp.pad(g, ((0, 0), (0, 0), (0, 0), (0, SUBLANES - ng), (0, 0)))
    return g


def _natten_kernel(q_ref, k_ref, v_ref, kc_ref, vc_ref, by_dr_ref, o_ref, bias_ref, *, rows):
    kcb = kc_ref[...]
    vcb = vc_ref[...]
    scale = HEAD_DIM ** -0.5
    kh = min(NA_KH, rows)
    win = kh * GRID_W
    per_iter = min(NA_ROWS_PER_ITER, rows)

    for cls in range(NA_KH):
        for j in range(kh):
            bias_ref[cls, :, j * GRID_W:(j + 1) * GRID_W] = by_dr_ref[NA_KH - 1 - cls + j]

    def body(it, _):
        rws = [it * per_iter + k for k in range(per_iter)]
        q0s, k0s, s_loc, s_ctx = [], [], [], []
        for r in rws:
            rs = jnp.clip(r - NA_KH // 2, 0, rows - NA_KH)
            q0s.append(pl.multiple_of(r * GRID_W, GRID_W))
            k0s.append(pl.multiple_of(rs * GRID_W, GRID_W))
            q = q_ref[pl.ds(q0s[-1], GRID_W), :]
            kw = k_ref[pl.ds(k0s[-1], win), :]
            s_loc.append(lax.dot_general(q, kw, NT_DIMS, preferred_element_type=F32) * scale + bias_ref[r - rs])
            s_ctx.append(lax.dot_general(q, kcb, NT_DIMS, preferred_element_type=F32) * scale)
        p_loc, p_ctx, den = [], [], []
        for a, b in zip(s_loc, s_ctx):
            m = jnp.maximum(jnp.max(a, axis=-1, keepdims=True), jnp.max(b, axis=-1, keepdims=True))
            p_loc.append(jnp.exp(a - m))
            p_ctx.append(jnp.exp(b - m))
            den.append(jnp.sum(p_loc[-1], axis=-1, keepdims=True) + jnp.sum(p_ctx[-1], axis=-1, keepdims=True))
        outs = []
        for k in range(per_iter):
            vw = v_ref[pl.ds(k0s[k], win), :]
            outs.append(jnp.dot(p_loc[k].astype(BF16), vw, preferred_element_type=F32)
                        + jnp.dot(p_ctx[k].astype(BF16), vcb, preferred_element_type=F32))
        for k in range(per_iter):
            o_ref[pl.ds(q0s[k], GRID_W), :] = (outs[k] / den[k]).astype(o_ref.dtype)
        return 0

    lax.fori_loop(0, rows // per_iter, body, 0)


def natten(p, pc, by_dr, col_q, col_k, col_v, ctx_col_k, ctx_col_v, casts=()):
    bn, t, _ = p.shape
    lc = pc.shape[1]
    hd = HEAD_DIM
    rows = t // GRID_W
    kh = min(NA_KH, rows)
    seq = lambda col: pl.BlockSpec((None, t, hd), lambda b, h, col=col: (b, 0, col + h))
    cseq = lambda col: pl.BlockSpec((None, lc, hd), lambda b, h, col=col: (b, 0, col + h))
    (y,), rounded = _call_hosting(
        lambda *refs: _natten_kernel(*refs, rows=rows), casts,
        grid=(bn, NA_HEADS),
        in_specs=[
            seq(col_q), seq(col_k), seq(col_v), cseq(ctx_col_k), cseq(ctx_col_v),
            pl.BlockSpec((None,) + by_dr.shape[1:], lambda b, h: (h, 0, 0, 0)),
        ],
        args=(p, p, p, pc, pc, by_dr),
        out_specs=[pl.BlockSpec((None, t, hd), lambda b, h: (b, 0, h))],
        out_shapes=[jax.ShapeDtypeStruct((bn, t, NA_HEADS * hd), BF16)],
        scratch_shapes=[pltpu.VMEM((NA_KH, GRID_W, kh * GRID_W), F32)],
        compiler_params=_cparams(("parallel", "parallel")),
        name="natten",
    )
    return y, rounded


def natten_bias(rpb):
    hn, n_dr, n_dc = rpb.shape
    col = np.arange(GRID_W)
    col_start = np.clip(col - NA_KW // 2, 0, GRID_W - NA_KW)
    in_win = (col[None, :] >= col_start[:, None]) & (col[None, :] < col_start[:, None] + NA_KW)
    dc = np.clip(col[None, :] - col[:, None], -(NA_KW - 1), NA_KW - 1) + (NA_KW - 1)
    onehot = (np.arange(n_dc)[:, None] == dc.reshape(-1)[None, :]).astype(np.float32)
    by_dr = jnp.dot(rpb.astype(F32).reshape(hn * n_dr, n_dc), jnp.asarray(onehot), precision=HIGHEST)
    return jnp.where(in_win[None, None], by_dr.reshape(hn, n_dr, GRID_W, GRID_W), NEG_BIG)


def _outproj_kernel(*refs, n_in):
    x_ref, gate_ref, g_ref, sc_ref, sh_ref = refs[:5]
    a_refs = refs[5:5 + n_in]
    w_refs = refs[5 + n_in:5 + 2 * n_in]
    o_ref, hs_ref = refs[5 + 2 * n_in:]
    y = None
    for a_ref, w_ref in zip(a_refs, w_refs):
        part = jnp.dot(a_ref[...], w_ref[...], preferred_element_type=F32)
        y = part if y is None else y + part
    x1 = x_ref[...] + gate_ref[...] * y
    o_ref[...] = x1
    hs_ref[...] = _modulated_norm(x1, g_ref[...], sc_ref[...], sh_ref[...]).astype(BF16)


def outproj(x, gate, acts, weight, g_next, sc_next, sh_next, tm=512):
    bn, t, d = x.shape
    n_in = len(acts)
    kdim = acts[0].shape[-1]
    assert all(a.shape[-1] == kdim for a in acts) and weight.shape[0] == n_in * kdim
    weights = [weight] * n_in
    row = pl.BlockSpec((None, tm, d), lambda b, i: (b, i, 0))
    vec = pl.BlockSpec((None, 1, d), lambda b, i: (b, 0, 0))
    in_specs = [row, vec, pl.BlockSpec((1, d), lambda b, i: (0, 0)), vec, vec]
    in_specs += [pl.BlockSpec((None, tm, kdim), lambda b, i: (b, i, 0)) for a in acts]
    in_specs += [pl.BlockSpec((kdim, d), lambda b, i, k=k: (k, 0)) for k in range(n_in)]
    return pl.pallas_call(
        functools.partial(_outproj_kernel, n_in=n_in),
        grid=(bn, t // tm),
        in_specs=in_specs,
        out_specs=[row, row],
        out_shape=[jax.ShapeDtypeStruct((bn, t, d), F32), jax.ShapeDtypeStruct((bn, t, d), BF16)],
        compiler_params=_cparams(("parallel", "parallel")),
        name="outproj",
    )(x, gate, g_next.reshape(1, d), sc_next, sh_next, *acts, *weights)


def _outproj_conv_kernel(x_ref, gate_ref, g_ref, sc_ref, sh_ref, gb_ref, gc_ref, val_ref,
                         gcp_ref, valp_ref, gcn_ref, valn_ref, cw_ref, w_ref, o_ref, hs_ref):
    i = pl.program_id(1)
    tm = x_ref.shape[0]
    u = gc_ref[...].astype(F32) * val_ref[...].astype(F32)
    last = gcp_ref.shape[0] - 1
    u_prev = gcp_ref[last:last + 1, :].astype(F32) * valp_ref[last:last + 1, :].astype(F32)
    u_next = gcn_ref[0:1, :].astype(F32) * valn_ref[0:1, :].astype(F32)
    u_prev = jnp.where(i == 0, 0.0, u_prev)
    u_next = jnp.where(i == pl.num_programs(1) - 1, 0.0, u_next)
    row = lax.broadcasted_iota(jnp.int32, (tm, 1), 0)
    below = jnp.where(row == 0, u_prev, pltpu.roll(u, 1, axis=0))
    above = jnp.where(row == tm - 1, u_next, pltpu.roll(u, tm - 1, axis=0))
    conv = below * cw_ref[0:1, :] + u * cw_ref[1:2, :] + above * cw_ref[2:3, :]
    z = (gb_ref[...].astype(F32) * conv).astype(BF16)
    x1 = x_ref[...] + gate_ref[...] * jnp.dot(z, w_ref[...], preferred_element_type=F32)
    o_ref[...] = x1
    hs_ref[...] = _modulated_norm(x1, g_ref[...], sc_ref[...], sh_ref[...]).astype(BF16)


def outproj_conv(x, gate, p, conv_w, weight, g_next, sc_next, sh_next, tm=512):
    bn, t, d = x.shape
    halo = 2 * SUBLANES
    nb = t // halo
    row = pl.BlockSpec((None, tm, d), lambda b, i: (b, i, 0))
    vec = pl.BlockSpec((None, 1, d), lambda b, i: (b, 0, 0))
    seg = lambda k: pl.BlockSpec((None, tm, d), lambda b, i, k=k: (b, i, k))
    prev = lambda k: pl.BlockSpec((None, halo, d), lambda b, i, k=k: (b, jnp.maximum(i * (tm // halo) - 1, 0), k))
    nxt = lambda k: pl.BlockSpec((None, halo, d), lambda b, i, k=k: (b, jnp.minimum((i + 1) * (tm // halo), nb - 1), k))
    return pl.pallas_call(
        _outproj_conv_kernel,
        grid=(bn, t // tm),
        in_specs=[row, vec, pl.BlockSpec((1, d), lambda b, i: (0, 0)), vec, vec,
                  seg(0), seg(1), seg(2), prev(1), prev(2), nxt(1), nxt(2),
                  pl.BlockSpec((SC_CONV, d), lambda b, i: (0, 0)),
                  pl.BlockSpec((d, d), lambda b, i: (0, 0))],
        out_specs=[row, row],
        out_shape=[jax.ShapeDtypeStruct((bn, t, d), F32), jax.ShapeDtypeStruct((bn, t, d), BF16)],
        compiler_params=_cparams(("parallel", "parallel")),
        name="outproj_conv",
    )(x, gate, g_next.reshape(1, d), sc_next, sh_next, p, p, p, p, p, p, p, conv_w, weight)


def _ffn_kernel(x_ref, hs_ref, gate_ref, fn_ref, wg_ref, wu_ref, wd_ref, o_ref, *, final_norm):
    j = pl.program_id(2)

    @pl.when(j == 0)
    def _():
        o_ref[...] = jnp.zeros_like(o_ref)

    h = hs_ref[...]
    gate = jnp.dot(h, wg_ref[...], preferred_element_type=F32)
    up = jnp.dot(h, wu_ref[...], preferred_element_type=F32)
    a = (_silu(gate) * up).astype(BF16)
    o_ref[...] += jnp.dot(a, wd_ref[...], preferred_element_type=F32)

    @pl.when(j == pl.num_programs(2) - 1)
    def _():
        rb = min(NORM_ROW_BLOCK, o_ref.shape[0])

        def body(i, _):
            rows = pl.ds(pl.multiple_of(i * rb, rb), rb)
            y = x_ref[rows, :] + gate_ref[...] * o_ref[rows, :]
            if final_norm:
                y = (y * lax.rsqrt(jnp.mean(y * y, axis=-1, keepdims=True) + NORM_EPS)) * fn_ref[...]
            o_ref[rows, :] = y
            return 0

        lax.fori_loop(0, o_ref.shape[0] // rb, body, 0)


def ffn_grid(x, f, tm=1024, tf=512):
    bn, t, _ = x.shape
    tm = min(tm, t)
    return tm, tf, (bn, t // tm, f // tf)


def ffn_weight_jobs(w_gate, w_up, w_down, layer, grid, tf):
    bn, ni, nj = grid
    d = w_gate.shape[1]
    slab = d // (bn * ni)
    assert slab * bn * ni == d and slab % LANES == 0
    tile = lambda b, i: b * ni + i
    up_job = lambda w: CastJob(w, (None, slab, tf), lambda b, i, j: (layer, tile(b, i), j),
                               w.shape[1:], (slab, tf), lambda b, i, j: (tile(b, i), j))
    down_job = CastJob(w_down, (None, tf, slab), lambda b, i, j: (layer, j, tile(b, i)),
                       w_down.shape[1:], (tf, slab), lambda b, i, j: (j, tile(b, i)))
    return [up_job(w_gate), up_job(w_up), down_job]


def ffn(x, hs, gate, fnorm, w_gate, w_up, w_down, final_norm, casts=()):
    bn, t, d = x.shape
    f = w_gate.shape[1]
    tm, tf, grid = ffn_grid(x, f)
    vec = pl.BlockSpec((None, 1, d), lambda b, i, j: (b, 0, 0))
    one = pl.BlockSpec((1, d), lambda b, i, j: (0, 0))
    (y,), rounded = _call_hosting(
        functools.partial(_ffn_kernel, final_norm=final_norm), casts,
        grid=grid,
        in_specs=[
            _single((None, tm, d), lambda b, i, j: (b, i, 0)),
            pl.BlockSpec((None, tm, d), lambda b, i, j: (b, i, 0)),
            vec, one,
            pl.BlockSpec((d, tf), lambda b, i, j: (0, j)),
            pl.BlockSpec((d, tf), lambda b, i, j: (0, j)),
            pl.BlockSpec((tf, d), lambda b, i, j: (j, 0)),
        ],
        args=(x, hs, gate, fnorm.reshape(1, d), w_gate, w_up, w_down),
        out_specs=[pl.BlockSpec((None, tm, d), lambda b, i, j: (b, i, 0))],
        out_shapes=[jax.ShapeDtypeStruct((bn, t, d), F32)],
        compiler_params=_cparams(("parallel", "parallel", "arbitrary")),
        name="ffn",
    )
    return y, rounded


def rope_tables(t):
    pos = np.arange(t)
    row = (pos // GRID_W).astype(np.float32)
    col = (pos % GRID_W).astype(np.float32)
    n_freq = HEAD_DIM // 4
    inv_freq = jnp.asarray(ROPE_THETA, F32) ** (-jnp.arange(n_freq, dtype=F32) / n_freq)
    ang = jnp.concatenate([jnp.asarray(row)[:, None] * inv_freq, jnp.asarray(col)[:, None] * inv_freq], axis=-1)
    cos, sin = jnp.cos(ang), jnp.sin(ang)
    return jnp.concatenate([cos, cos], axis=-1), jnp.concatenate([-sin, sin], axis=-1)


def kernel(x, c, ctx, c_ctx, ada_w, ada_b, norm_mix, norm_ffn, ffn_w_gate, ffn_w_up, ffn_w_down, final_norm,
           ev_w_in, ev_conv, ev_a_log, ev_dt_bias, ev_gdn_norm, ev_rpb, ev_w_out, od_w_in, od_conv, od_w_out):
    bn, t, d = x.shape
    depth = ada_w.shape[0]
    assert depth == 2
    gw = GDN_HEADS * HEAD_DIM
    nw = NA_HEADS * HEAD_DIM
    nh = GDN_HEADS

    cv = jnp.zeros((SUBLANES, d), F32).at[:bn].set(c).at[bn].set(c_ctx)
    mods = ada_modulation(cv, ada_w, ada_b)

    def mod_vecs(l, rows):
        m = mods[l, rows].reshape(-1, 6, d)
        return [m[:, k][:, None, :] for k in range(6)]

    sh1, sc1, g1, sh2, sc2, g2 = mod_vecs(0, slice(0, bn))
    csh1, csc1 = [jnp.broadcast_to(v, (bn, 1, d)) for v in mod_vecs(0, slice(bn, bn + 1))[:2]]
    n_gate = 4 * nh
    seg_gate = 2 * gw + 2 * nw
    w_rows = jnp.swapaxes(ev_w_in[0], 0, 1)
    n_tail = w_rows.shape[0] - seg_gate - n_gate
    n_main = seg_gate + n_tail
    gate_block = seg_gate // LANES
    n_head_steps = bn * nh
    tail_slab = -(-n_tail // (n_head_steps * LANES)) * LANES
    tail_skip = n_head_steps * tail_slab - n_tail
    tail_first = w_rows.shape[0] - n_head_steps * tail_slab
    gparams = jnp.zeros((SUBLANES, LANES), F32)
    gparams = gparams.at[0, :2 * nh].set(ev_a_log[0].reshape(-1)).at[1, :2 * nh].set(ev_dt_bias[0].reshape(-1))
    col_ka, col_va, col_kb, col_vb, col_qa, col_qb, col_za = [k * nh for k in range(7)]

    head_step = lambda b, h: b * nh + h
    pc, gates_c = proj(ctx, norm_mix[0], csc1, csh1, w_rows, 0, seg_gate, None, seg_gate, w_rows, gate_block,
                       gparams, rows_major=True)
    s_ctx, (w_head, w_tail, w_out0) = gdn_context(
        pc, ev_conv[0], gate_rows(gates_c, ctx.shape[1]), col_ka, col_va,
        casts=[transposing_job(w_rows, 0, seg_gate, n_head_steps, head_step),
               transposing_job(w_rows, tail_first, n_head_steps * tail_slab, n_head_steps, head_step),
               row_slab_job(ev_w_out, 0, n_head_steps, head_step)])

    p, gates = proj(x, norm_mix[0], sc1, sh1, w_head[None], 0, seg_gate, w_tail, n_main, w_rows, gate_block,
                    gparams, tail_skip=tail_skip)
    cosf, sinf = rope_tables(t)
    y_gdn = gdn_latent(p, ev_conv[0], gate_rows(gates, t), cosf, sinf, s_ctx, ev_gdn_norm[0],
                       col_ka, col_va, col_qa, col_za)
    y_na, rounded = natten(p, pc, natten_bias(ev_rpb[0]), col_qb, col_kb, col_vb, col_kb, col_vb,
                           casts=[row_slab_job(w, 0, n_head_steps, head_step)
                                  for w in (ffn_w_gate, ffn_w_up, ffn_w_down, od_w_in, od_w_out)])
    ffn0, (w_in1, w_out1) = rounded[:3], rounded[3:]
    x_lat, hs = outproj(x, g1, [y_gdn, y_na], w_out0, norm_ffn[0], sc2, sh2)
    _, tf, grid = ffn_grid(x_lat, ffn_w_gate.shape[2])
    x_lat, ffn1 = ffn(x_lat, hs, g2, final_norm, *ffn0, final_norm=False,
                      casts=ffn_weight_jobs(ffn_w_gate, ffn_w_up, ffn_w_down, 1, grid, tf))

    sh1, sc1, g1, sh2, sc2, g2 = mod_vecs(1, slice(0, bn))
    p = proj(x_lat, norm_mix[1], sc1, sh1, w_in1[None], 0, 3 * d, None, 3 * d)
    x_lat, hs = outproj_conv(x_lat, g1, p, od_conv[0], w_out1, norm_ffn[1], sc2, sh2)
    x_lat, _ = ffn(x_lat, hs, g2, final_norm, *ffn1, final_norm=True)
    return x_lat
```

```python
import functools
import math

import jax
import jax.numpy as jnp
import numpy as np
from jax import lax
from jax.experimental import pallas as pl
from jax.experimental.pallas import tpu as pltpu

F32 = jnp.float32
BF16 = jnp.bfloat16
HIGHEST = lax.Precision.HIGHEST

LANES = 128
SUBLANES = 8
VMEM_LIMIT = 56 * 1024 * 1024

GRID_W = 64
HEAD_DIM = 128
GDN_HEADS = 8
NA_HEADS = 8
GDN_CONV = 5
CHUNK = 64
INVERSE_LEVELS = 6
GROUP = 2 * CHUNK
PREP_GROUPS_PER_ITER = 4
GDN_ROW_BLOCK = 512
NA_ROWS_PER_ITER = 8
NA_KH = 8
NA_KW = 16
SC_CONV = 3
ROPE_THETA = 10000.0
NORM_EPS = 1e-6
NEG_BIG = -1e30

NT_DIMS = (((1,), (1,)), ((), ()))


def _cparams(sem):
    return pltpu.CompilerParams(dimension_semantics=sem, vmem_limit_bytes=VMEM_LIMIT)


def _sigmoid(x):
    return 1.0 / (1.0 + jnp.exp(-x))


def _silu(x):
    return x * _sigmoid(x)


def _softplus(x):
    return jnp.maximum(x, 0.0) + jnp.log(1.0 + jnp.exp(-jnp.abs(x)))


def _mm(a, b):
    return jnp.dot(a.astype(BF16), b.astype(BF16), preferred_element_type=F32)


def _mm_nt(a, b):
    return lax.dot_general(a.astype(BF16), b.astype(BF16), NT_DIMS, preferred_element_type=F32)


def _mm_exact(a, b):
    return jnp.dot(a, b, precision=HIGHEST, preferred_element_type=F32)


class CastJob:
    def __init__(self, src, in_block, in_index, out_shape, out_block, out_index, transpose=False,
                 row_offset=0, next_rows=None):
        self.transpose = transpose
        self.row_offset = row_offset
        self.srcs = [src]
        self.in_specs = [pl.BlockSpec(in_block, in_index)]
        if row_offset:
            self.srcs.append(src)
            self.in_specs.append(pl.BlockSpec(*next_rows))
        self.out_spec = pl.BlockSpec(out_block, out_index)
        self.out_shape = jax.ShapeDtypeStruct(out_shape, BF16)


def _hosting_casts(body, n_in, n_out, casts):
    n_cast_in = sum(len(job.srcs) for job in casts)

    def kernel(*refs):
        ins, rest = refs[:n_in], refs[n_in:]
        cast_in, rest = list(rest[:n_cast_in]), rest[n_cast_in:]
        outs, rest = rest[:n_out], rest[n_out:]
        cast_out, scratch = rest[:len(casts)], rest[len(casts):]
        for job, dst in zip(casts, cast_out):
            src = cast_in.pop(0)
            block = src[...]
            if job.row_offset:
                block = jnp.concatenate([src[job.row_offset:, :], cast_in.pop(0)[...]], axis=0)
            dst[...] = (block.T if job.transpose else block).astype(BF16)
        body(*ins, *outs, *scratch)

    return kernel


def row_slab_job(w, layer, n_steps, step_of):
    rows, cols = w.shape[1:]
    slab = rows // n_steps
    assert slab * n_steps == rows and slab % (2 * SUBLANES) == 0
    return CastJob(w, (None, slab, cols), lambda *idx: (layer, step_of(*idx), 0),
                   (rows, cols), (slab, cols), lambda *idx: (step_of(*idx), 0))


def transposing_job(w_rows, first_row, n_rows, n_steps, step_of):
    d = w_rows.shape[1]
    slab = n_rows // n_steps
    assert slab * n_steps == n_rows and slab % LANES == 0
    first_block, offset = divmod(first_row, slab)
    assert offset % SUBLANES == 0
    next_rows = None
    if offset:
        assert slab % offset == 0
        per = slab // offset
        next_rows = ((offset, d), lambda *idx: ((first_block + step_of(*idx) + 1) * per, 0))
    return CastJob(w_rows, (slab, d), lambda *idx: (first_block + step_of(*idx), 0),
                   (d, n_rows), (d, slab), lambda *idx: (0, step_of(*idx)), transpose=True,
                   row_offset=offset, next_rows=next_rows)


def _call_hosting(body, casts, grid, in_specs, args, out_specs, out_shapes, **kwargs):
    n_in, n_out = len(in_specs), len(out_specs)
    casts = list(casts)
    res = pl.pallas_call(
        _hosting_casts(body, n_in, n_out, casts),
        grid=grid,
        in_specs=list(in_specs) + [s for c in casts for s in c.in_specs],
        out_specs=list(out_specs) + [c.out_spec for c in casts],
        out_shape=list(out_shapes) + [c.out_shape for c in casts],
        **kwargs,
    )(*args, *[s for c in casts for s in c.srcs])
    return res[:n_out], res[n_out:]


def _ada_kernel(cv_ref, w_ref, b_ref, o_ref):
    s = _silu(cv_ref[...])
    o_ref[...] = _mm(s, w_ref[...]) + b_ref[...]


def ada_modulation(cv, ada_w, ada_b, tn=1024):
    depth, d, n = ada_w.shape
    return pl.pallas_call(
        _ada_kernel,
        grid=(depth, n // tn),
        in_specs=[
            pl.BlockSpec((SUBLANES, d), lambda l, j: (0, 0)),
            pl.BlockSpec((None, d, tn), lambda l, j: (l, 0, j)),
            pl.BlockSpec((None, 1, tn), lambda l, j: (l, 0, j)),
        ],
        out_specs=pl.BlockSpec((None, SUBLANES, tn), lambda l, j: (l, 0, j)),
        out_shape=jax.ShapeDtypeStruct((depth, SUBLANES, n), F32),
        compiler_params=_cparams(("parallel", "parallel")),
        name="ada",
    )(cv, ada_w, ada_b.reshape(depth, 1, n))


def _modulated_norm(x, g, sc, sh):
    y = x * lax.rsqrt(jnp.mean(x * x, axis=-1, keepdims=True) + NORM_EPS)
    return (y * g) * (1.0 + sc) + sh


NORM_ROW_BLOCK = 128


def _modulated_norm_rows(x_ref, g_ref, sc_ref, sh_ref, hs_ref):
    rb = min(NORM_ROW_BLOCK, x_ref.shape[0])

    def body(i, _):
        rows = pl.ds(pl.multiple_of(i * rb, rb), rb)
        hs_ref[rows, :] = _modulated_norm(x_ref[rows, :], g_ref[...], sc_ref[...], sh_ref[...]).astype(BF16)
        return 0

    lax.fori_loop(0, x_ref.shape[0] // rb, body, 0)


def _proj_kernel(*refs, with_gates, with_tail, head_blocks, rows_major):
    x_ref, g_ref, sc_ref, sh_ref, w_ref = refs[:5]
    refs = refs[5:]
    wt_ref = None
    if with_tail:
        wt_ref, refs = refs[0], refs[1:]
    if with_gates:
        wab_ref, gp_ref, o_ref, gate_ref, hs_ref, wb_ref = refs
    else:
        o_ref, hs_ref, wb_ref = refs
    j = pl.program_id(2)

    @pl.when(j == 0)
    def _():
        if with_gates:
            hb = _modulated_norm(x_ref[...], g_ref[...], sc_ref[...], sh_ref[...]).astype(BF16)
            hs_ref[...] = hb
            a = _mm_nt(hb, wab_ref[...])
            neg_decay_rate = -jnp.exp(gp_ref[0:1, :])
            g = neg_decay_rate * _softplus(a + gp_ref[1:2, :])
            lane = lax.broadcasted_iota(jnp.int32, a.shape, 1)
            gate_ref[...] = jnp.where(lane < 2 * GDN_HEADS, g, _sigmoid(a))
        else:
            _modulated_norm_rows(x_ref, g_ref, sc_ref, sh_ref, hs_ref)

    def emit(wref):
        if wref.dtype != BF16:
            wb_ref[...] = wref[...].astype(BF16)
            wref = wb_ref
        if rows_major:
            y = lax.dot_general(hs_ref[...], wref[...], NT_DIMS, preferred_element_type=F32)
        else:
            y = jnp.dot(hs_ref[...], wref[...], preferred_element_type=F32)
        o_ref[...] = y.astype(o_ref.dtype)

    pl.when(j < head_blocks)(lambda: emit(w_ref))
    if with_tail:
        pl.when(j >= head_blocks)(lambda: emit(wt_ref))


def proj(x, g, sc, sh, w, layer, head_cols, w_tail, n_out, wab=None, gate_block=0, gparams=None,
         rows_major=False, tail_skip=0, tm=1024, tn=1024):
    bn, t, d = x.shape
    tm = min(tm, t)
    while n_out % tn or head_cols % tn or tail_skip % tn:
        tn //= 2
    head_blocks = head_cols // tn
    with_gates = wab is not None
    vec = pl.BlockSpec((None, 1, d), lambda b, i, j: (b, 0, 0))
    if rows_major:
        w_spec = pl.BlockSpec((tn, d), lambda b, i, j: (jnp.minimum(j, head_blocks - 1), 0))
    else:
        w_spec = pl.BlockSpec((None, d, tn), lambda b, i, j: (layer, 0, jnp.minimum(j, head_blocks - 1)))
    in_specs = [
        pl.BlockSpec((None, tm, d), lambda b, i, j: (b, i, 0)),
        pl.BlockSpec((1, d), lambda b, i, j: (0, 0)),
        vec, vec,
        w_spec,
    ]
    args = [x, g.reshape(1, d), sc, sh, w]
    with_tail = w_tail is not None
    if with_tail:
        skip = tail_skip // tn
        in_specs.append(pl.BlockSpec((d, tn), lambda b, i, j: (0, jnp.maximum(j - head_blocks, 0) + skip)))
        args.append(w_tail)
    out_specs = [pl.BlockSpec((None, tm, tn), lambda b, i, j: (b, i, j))]
    out_shape = [jax.ShapeDtypeStruct((bn, t, n_out), BF16)]
    if with_gates:
        in_specs += [pl.BlockSpec((LANES, d), lambda b, i, j: (gate_block, 0)),
                     pl.BlockSpec((SUBLANES, LANES), lambda b, i, j: (0, 0))]
        args += [wab, gparams]
        out_specs.append(pl.BlockSpec((None, tm, LANES), lambda b, i, j: (b, i, 0)))
        out_shape.append(jax.ShapeDtypeStruct((bn, t, LANES), F32))
    res = pl.pallas_call(
        functools.partial(_proj_kernel, with_gates=with_gates, with_tail=with_tail, head_blocks=head_blocks,
                          rows_major=rows_major),
        grid=(bn, t // tm, n_out // tn),
        in_specs=in_specs,
        out_specs=out_specs,
        out_shape=out_shape,
        scratch_shapes=[pltpu.VMEM((tm, d), BF16), pltpu.VMEM((tn, d) if rows_major else (d, tn), BF16)],
        compiler_params=_cparams(("parallel", "parallel", "arbitrary")),
        name="proj",
    )(*args)
    return res if with_gates else res[0]


CONV_HALO = SUBLANES


def _stage_padded(pad_ref, load_rows, t, rb):
    zeros = jnp.zeros((CONV_HALO, LANES), F32)
    pad_ref[0:CONV_HALO, :] = zeros
    pad_ref[CONV_HALO + t:2 * CONV_HALO + t, :] = zeros

    def body(i, _):
        r0 = pl.multiple_of(i * rb, rb)
        pad_ref[pl.ds(CONV_HALO + r0, rb), :] = load_rows(r0)
        return 0

    lax.fori_loop(0, t // rb, body, 0)


def _conv_rows(pad_ref, w_ref, r0, rb, taps):
    acc = None
    for j in range(taps):
        xj = pad_ref[pl.ds(r0 + (CONV_HALO + j - taps // 2), rb), :]
        term = xj * w_ref[j:j + 1, :]
        acc = term if acc is None else acc + term
    return acc


def _l2norm(x):
    return x * lax.rsqrt(jnp.sum(x * x, axis=-1, keepdims=True) + NORM_EPS)


def _gdn_kernel(*refs, t, with_q):
    nc = t // CHUNK
    rb = min(GDN_ROW_BLOCK, t)
    if with_q:
        (ka_ref, va_ref, qa_ref, za_ref, wk_ref, wv_ref, wq_ref, grow_ref, cos_ref, sin_ref, s0_ref,
         gn_ref, y_ref,
         pad_ref, pad2_ref, k_ref, v_ref, q_ref, o_ref, gc_ref, kwq_s, h_s, au_s, gl_s) = refs
    else:
        (ka_ref, va_ref, wk_ref, wv_ref, grow_ref, sfin_ref,
         pad_ref, k_ref, v_ref, gc_ref, kwq_s, h_s, gl_s) = refs

    def conv_all(src_ref, w_ref, finish, dst_ref):
        _stage_padded(pad_ref, lambda r0: src_ref[pl.ds(r0, rb), :].astype(F32), t, rb)

        def body(i, _):
            r0 = pl.multiple_of(i * rb, rb)
            y = _silu(_conv_rows(pad_ref, w_ref, r0, rb, GDN_CONV))
            dst_ref[pl.ds(r0, rb), :] = finish(y, r0)
            return 0

        lax.fori_loop(0, t // rb, body, 0)

    def rope(x, r0):
        return (x * cos_ref[pl.ds(r0, rb), :]
                + pltpu.roll(x, HEAD_DIM // 2, axis=1) * sin_ref[pl.ds(r0, rb), :])

    if with_q:
        _stage_padded(pad_ref, lambda r0: ka_ref[pl.ds(r0, rb), :].astype(F32), t, rb)
        _stage_padded(pad2_ref, lambda r0: qa_ref[pl.ds(r0, rb), :].astype(F32), t, rb)

        def kq_body(i, _):
            r0 = pl.multiple_of(i * rb, rb)
            yk = _silu(_conv_rows(pad_ref, wk_ref, r0, rb, GDN_CONV))
            yq = _silu(_conv_rows(pad2_ref, wq_ref, r0, rb, GDN_CONV))
            k_ref[pl.ds(r0, rb), :] = rope(_l2norm(yk), r0)
            q_ref[pl.ds(r0, rb), :] = rope(_l2norm(yq), r0) * HEAD_DIM ** -0.5
            return 0

        lax.fori_loop(0, t // rb, kq_body, 0)
    else:
        conv_all(ka_ref, wk_ref, lambda y, r0: _l2norm(y), k_ref)
    conv_all(va_ref, wv_ref, lambda y, r0: y, v_ref)

    ii = lax.broadcasted_iota(jnp.int32, (GROUP, GROUP), 0)
    jj = lax.broadcasted_iota(jnp.int32, (GROUP, GROUP), 1)
    same = (ii // CHUNK) == (jj // CHUNK)
    lower = same & (ii >= jj)
    upper = same & (ii <= jj)
    eye = ii == jj
    gc_ref[0] = _mm_exact(grow_ref[0], upper.astype(F32))
    gc_ref[1] = _mm_exact(grow_ref[1], lower.astype(F32))

    n_groups = t // GROUP
    gpi = min(PREP_GROUPS_PER_ITER, n_groups)
    n_sets = n_groups // gpi
    steps_per_set = 2 * gpi

    def prep_stages(i):
        chains = ([(i * gpi + k, 0) for k in range(gpi)]
                  + [(n_groups - 1 - (i * gpi + k), 1) for k in range(gpi)])
        ids = range(len(chains))
        st = {}

        def load():
            for ch, (g, d) in enumerate(chains):
                r0 = pl.multiple_of(g * GROUP, GROUP)
                st["k", ch] = k_ref[pl.ds(r0, GROUP), :]
                st["v", ch] = v_ref[pl.ds(r0, GROUP), :]
                kgb = st["k", ch].astype(BF16)
                if with_q:
                    st["q", ch] = q_ref[pl.ds(r0, GROUP), :]
                    both = _mm_nt(jnp.concatenate([st["k", ch], st["q", ch]], axis=0), kgb)
                    st["kk", ch], st["qk", ch] = both[:GROUP], both[GROUP:]
                else:
                    st["kk", ch] = _mm_nt(kgb, kgb)

        def masks():
            for ch, (g, d) in enumerate(chains):
                incl = lower if d == 0 else upper
                gc_r = jnp.broadcast_to(gc_ref[d, pl.ds(g, 1), :], (GROUP, GROUP))
                be_r = jnp.broadcast_to(grow_ref[2 + d, pl.ds(g, 1), :], (GROUP, GROUP))
                gc_c = gc_r.T
                decay = jnp.where(incl, jnp.exp(jnp.where(incl, gc_c - gc_r, 0.0)), 0.0)
                m = jnp.where(incl & jnp.logical_not(eye), st["kk", ch] * be_r.T * decay, 0.0)
                st["be_r", ch], st["gc_c", ch], st["decay", ch] = be_r, gc_c, decay
                st["p", ch] = jnp.where(eye, 1.0, -m)
                st["sq", ch] = _mm(m, m)

        def level():
            for ch in ids:
                p, sq = st["p", ch], st["sq", ch]
                st["p", ch] = p + _mm(p, sq)
                st["sq", ch] = _mm(sq, sq)

        def last_level():
            for ch in ids:
                st["p", ch] = st["p", ch] + _mm(st["p", ch], st["sq", ch])

        def solve():
            for ch in ids:
                rhs = jnp.concatenate([st["v", ch], st["k", ch] * jnp.exp(st["gc_c", ch])], axis=1)
                st["uw", ch] = _mm(st["p", ch] * st["be_r", ch], rhs)

        def fold():
            for ch, (g, d) in enumerate(chains):
                gc_c = st["gc_c", ch]
                last = [CHUNK - 1, GROUP - 1] if d == 0 else [0, CHUNK]
                gls = [gc_c[r:r + 1, :] for r in last]
                gl_c = jnp.concatenate([jnp.broadcast_to(x, (CHUNK, LANES)) for x in gls], axis=0)
                kdt = (st["k", ch] * jnp.exp(gl_c - gc_c)).T
                parts = [jnp.where(jj < CHUNK, kdt, 0.0), jnp.where(jj >= CHUNK, kdt, 0.0)]
                if with_q:
                    st["qd", ch] = st["q", ch] * jnp.exp(gc_c)
                    parts = [st["qk", ch] * st["decay", ch]] + parts
                st["gl", ch] = gls
                st["prod", ch] = _mm(jnp.concatenate(parts, axis=0), st["uw", ch])

        def store():
            for ch, (g, d) in enumerate(chains):
                r = st["prod", ch]
                off = GROUP if with_q else 0
                for half in range(2):
                    c = 2 * g + half
                    blk = r[off + half * HEAD_DIM:off + (half + 1) * HEAD_DIM]
                    h_s[d, c] = blk[:, :HEAD_DIM]
                    gl_s[d, c] = jnp.broadcast_to(jnp.exp(st["gl", ch][half]), (SUBLANES, LANES))
                    if with_q:
                        rs = slice(half * CHUNK, (half + 1) * CHUNK)
                        au_s[d, c] = r[rs, :HEAD_DIM]
                        qw = st["qd", ch][rs] - r[rs, HEAD_DIM:]
                        kwq_s[d, c] = jnp.concatenate([blk[:, HEAD_DIM:], qw], axis=0).astype(BF16)
                    else:
                        kwq_s[d, c] = blk[:, HEAD_DIM:].astype(BF16)

        return [load, masks] + [level] * (INVERSE_LEVELS - 2) + [last_level, solve, fold, store]

    def scan_step(s, carry):
        cs = (s, nc - 1 - s)
        rr = [jnp.dot(kwq_s[d, cs[d]], carry[d].astype(BF16), preferred_element_type=F32) for d in range(2)]
        new = []
        for d in range(2):
            c = cs[d]
            new.append(carry[d] * gl_s[d, c][0:1, :] + h_s[d, c] - rr[d][:HEAD_DIM])
            if with_q:
                r0 = pl.multiple_of(c * CHUNK, CHUNK)
                o_ref[pl.ds(r0, CHUNK), :] += rr[d][HEAD_DIM:] + au_s[d, c]
        return tuple(new)

    def run(stages, steps, carry):
        for idx in range(max(len(stages), len(steps))):
            if idx < len(stages):
                stages[idx]()
            if idx < len(steps):
                carry = scan_step(steps[idx], carry)
        return carry

    def set_steps(i):
        return [i * steps_per_set + k for k in range(steps_per_set)]

    if with_q:
        o_ref[...] = jnp.zeros_like(o_ref)
        carry = (s0_ref[0], s0_ref[1])
    else:
        zero = jnp.zeros((HEAD_DIM, HEAD_DIM), F32)
        carry = (zero, zero)
    run(prep_stages(0), [], carry)
    carry = lax.fori_loop(1, n_sets, lambda i, c: run(prep_stages(i), set_steps(i - 1), c), carry)
    s_f, s_b = run([], set_steps(n_sets - 1), carry)

    if not with_q:
        sfin_ref[0] = s_f
        sfin_ref[1] = s_b
        return

    def finish(i, _):
        r0 = pl.multiple_of(i * rb, rb)
        o = o_ref[pl.ds(r0, rb), :]
        y = o * lax.rsqrt(jnp.mean(o * o, axis=-1, keepdims=True) + NORM_EPS)
        y_ref[pl.ds(r0, rb), :] = ((y * gn_ref[...]) * _silu(za_ref[pl.ds(r0, rb), :].astype(F32))).astype(y_ref.dtype)
        return 0

    lax.fori_loop(0, t // rb, finish, 0)


def _single(block_shape, index_map):
    return pl.BlockSpec(block_shape, index_map, pipeline_mode=pl.Buffered(1))


def gdn_latent(p, conv_w, grow, cosf, sinf, s0, gnorm, col_k, col_v, col_q, col_z):
    bn, t, _ = p.shape
    nc = t // CHUNK
    ngp = grow.shape[3]
    hd = HEAD_DIM
    seq = lambda col: pl.BlockSpec((None, t, hd), lambda b, h, col=col: (b, 0, col + h))
    cw = lambda col: pl.BlockSpec((GDN_CONV, hd), lambda b, h, col=col: (0, col + h))
    return pl.pallas_call(
        functools.partial(_gdn_kernel, t=t, with_q=True),
        grid=(bn, GDN_HEADS),
        in_specs=[
            seq(col_k), seq(col_v), seq(col_q), seq(col_z),
            cw(0), cw(GDN_HEADS), cw(2 * GDN_HEADS),
            pl.BlockSpec((None, None, 4, ngp, GROUP), lambda b, h: (b, h, 0, 0, 0)),
            _single((t, hd), lambda b, h: (0, 0)),
            _single((t, hd), lambda b, h: (0, 0)),
            pl.BlockSpec((None, None, 2, hd, hd), lambda b, h: (b, h, 0, 0, 0)),
            pl.BlockSpec((1, hd), lambda b, h: (0, 0)),
        ],
        out_specs=pl.BlockSpec((None, t, hd), lambda b, h: (b, 0, h)),
        out_shape=jax.ShapeDtypeStruct((bn, t, GDN_HEADS * hd), BF16),
        scratch_shapes=[
            pltpu.VMEM((t + 2 * CONV_HALO, hd), F32),
            pltpu.VMEM((t + 2 * CONV_HALO, hd), F32),
            pltpu.VMEM((t, hd), F32),
            pltpu.VMEM((t, hd), F32),
            pltpu.VMEM((t, hd), F32),
            pltpu.VMEM((t, hd), F32),
            pltpu.VMEM((2, ngp, GROUP), F32),
            pltpu.VMEM((2, nc, hd + CHUNK, hd), BF16),
            pltpu.VMEM((2, nc, hd, hd), F32),
            pltpu.VMEM((2, nc, CHUNK, hd), F32),
            pltpu.VMEM((2, nc, SUBLANES, LANES), F32),
        ],
        compiler_params=_cparams(("parallel", "parallel")),
        name="gdn_latent",
    )(p, p, p, p, conv_w, conv_w, conv_w, grow, cosf, sinf, s0, gnorm.reshape(1, hd))


def gdn_context(pc, conv_w, grow, col_k, col_v, casts=()):
    bn, t, _ = pc.shape
    nc = t // CHUNK
    ngp = grow.shape[3]
    hd = HEAD_DIM
    seq = lambda col: pl.BlockSpec((None, t, hd), lambda b, h, col=col: (b, 0, col + h))
    cw = lambda col: pl.BlockSpec((GDN_CONV, hd), lambda b, h, col=col: (0, col + h))
    (states,), rounded = _call_hosting(
        lambda *refs: _gdn_kernel(*refs, t=t, with_q=False), casts,
        grid=(bn, GDN_HEADS),
        in_specs=[
            seq(col_k), seq(col_v), cw(0), cw(GDN_HEADS),
            pl.BlockSpec((None, None, 4, ngp, GROUP), lambda b, h: (b, h, 0, 0, 0)),
        ],
        args=(pc, pc, conv_w, conv_w, grow),
        out_specs=[pl.BlockSpec((None, None, 2, hd, hd), lambda b, h: (b, h, 0, 0, 0))],
        out_shapes=[jax.ShapeDtypeStruct((bn, GDN_HEADS, 2, hd, hd), F32)],
        scratch_shapes=[
            pltpu.VMEM((t + 2 * CONV_HALO, hd), F32),
            pltpu.VMEM((t, hd), F32),
            pltpu.VMEM((t, hd), F32),
            pltpu.VMEM((2, ngp, GROUP), F32),
            pltpu.VMEM((2, nc, hd, hd), BF16),
            pltpu.VMEM((2, nc, hd, hd), F32),
            pltpu.VMEM((2, nc, SUBLANES, LANES), F32),
        ],
        compiler_params=_cparams(("parallel", "parallel")),
        name="gdn_context",
    )
    return states, rounded


def gate_rows(gates, t):
    bn = gates.shape[0]
    ng = t // GROUP
    g = gates[:, :, :4 * GDN_HEADS].reshape(bn, ng, GROUP, 4, GDN_HEADS)
    g = jnp.transpose(g, (0, 4, 3, 1, 2))
    if ng < SUBLANES:
        g = jnp.pad(g, ((0, 0), (0, 0), (0, 0), (0, SUBLANES - ng), (0, 0)))
    return g


def _natten_kernel(q_ref, k_ref, v_ref, kc_ref, vc_ref, by_dr_ref, o_ref, bias_ref, *, rows):
    kcb = kc_ref[...]
    vcb = vc_ref[...]
    scale = HEAD_DIM ** -0.5
    kh = min(NA_KH, rows)
    win = kh * GRID_W
    per_iter = min(NA_ROWS_PER_ITER, rows)

    for cls in range(NA_KH):
        for j in range(kh):
            bias_ref[cls, :, j * GRID_W:(j + 1) * GRID_W] = by_dr_ref[NA_KH - 1 - cls + j]

    def body(it, _):
        rws = [it * per_iter + k for k in range(per_iter)]
        q0s, k0s, s_loc, s_ctx = [], [], [], []
        for r in rws:
            rs = jnp.clip(r - NA_KH // 2, 0, rows - NA_KH)
            q0s.append(pl.multiple_of(r * GRID_W, GRID_W))
            k0s.append(pl.multiple_of(rs * GRID_W, GRID_W))
            q = q_ref[pl.ds(q0s[-1], GRID_W), :]
            kw = k_ref[pl.ds(k0s[-1], win), :]
            s_loc.append(lax.dot_general(q, kw, NT_DIMS, preferred_element_type=F32) * scale + bias_ref[r - rs])
            s_ctx.append(lax.dot_general(q, kcb, NT_DIMS, preferred_element_type=F32) * scale)
        p_loc, p_ctx, den = [], [], []
        for a, b in zip(s_loc, s_ctx):
            m = jnp.maximum(jnp.max(a, axis=-1, keepdims=True), jnp.max(b, axis=-1, keepdims=True))
            p_loc.append(jnp.exp(a - m))
            p_ctx.append(jnp.exp(b - m))
            den.append(jnp.sum(p_loc[-1], axis=-1, keepdims=True) + jnp.sum(p_ctx[-1], axis=-1, keepdims=True))
        outs = []
        for k in range(per_iter):
            vw = v_ref[pl.ds(k0s[k], win), :]
            outs.append(jnp.dot(p_loc[k].astype(BF16), vw, preferred_element_type=F32)
                        + jnp.dot(p_ctx[k].astype(BF16), vcb, preferred_element_type=F32))
        for k in range(per_iter):
            o_ref[pl.ds(q0s[k], GRID_W), :] = (outs[k] / den[k]).astype(o_ref.dtype)
        return 0

    lax.fori_loop(0, rows // per_iter, body, 0)


def natten(p, pc, by_dr, col_q, col_k, col_v, ctx_col_k, ctx_col_v, casts=()):
    bn, t, _ = p.shape
    lc = pc.shape[1]
    hd = HEAD_DIM
    rows = t // GRID_W
    kh = min(NA_KH, rows)
    seq = lambda col: pl.BlockSpec((None, t, hd), lambda b, h, col=col: (b, 0, col + h))
    cseq = lambda col: pl.BlockSpec((None, lc, hd), lambda b, h, col=col: (b, 0, col + h))
    (y,), rounded = _call_hosting(
        lambda *refs: _natten_kernel(*refs, rows=rows), casts,
        grid=(bn, NA_HEADS),
        in_specs=[
            seq(col_q), seq(col_k), seq(col_v), cseq(ctx_col_k), cseq(ctx_col_v),
            pl.BlockSpec((None,) + by_dr.shape[1:], lambda b, h: (h, 0, 0, 0)),
        ],
        args=(p, p, p, pc, pc, by_dr),
        out_specs=[pl.BlockSpec((None, t, hd), lambda b, h: (b, 0, h))],
        out_shapes=[jax.ShapeDtypeStruct((bn, t, NA_HEADS * hd), BF16)],
        scratch_shapes=[pltpu.VMEM((NA_KH, GRID_W, kh * GRID_W), F32)],
        compiler_params=_cparams(("parallel", "parallel")),
        name="natten",
    )
    return y, rounded


def natten_bias(rpb):
    hn, n_dr, n_dc = rpb.shape
    col = np.arange(GRID_W)
    col_start = np.clip(col - NA_KW // 2, 0, GRID_W - NA_KW)
    in_win = (col[None, :] >= col_start[:, None]) & (col[None, :] < col_start[:, None] + NA_KW)
    dc = np.clip(col[None, :] - col[:, None], -(NA_KW - 1), NA_KW - 1) + (NA_KW - 1)
    onehot = (np.arange(n_dc)[:, None] == dc.reshape(-1)[None, :]).astype(np.float32)
    by_dr = jnp.dot(rpb.astype(F32).reshape(hn * n_dr, n_dc), jnp.asarray(onehot), precision=HIGHEST)
    return jnp.where(in_win[None, None], by_dr.reshape(hn, n_dr, GRID_W, GRID_W), NEG_BIG)


def _outproj_kernel(*refs, n_in):
    x_ref, gate_ref, g_ref, sc_ref, sh_ref = refs[:5]
    a_refs = refs[5:5 + n_in]
    w_refs = refs[5 + n_in:5 + 2 * n_in]
    o_ref, hs_ref = refs[5 + 2 * n_in:]
    y = None
    for a_ref, w_ref in zip(a_refs, w_refs):
        part = jnp.dot(a_ref[...], w_ref[...], preferred_element_type=F32)
        y = part if y is None else y + part
    x1 = x_ref[...] + gate_ref[...] * y
    o_ref[...] = x1
    hs_ref[...] = _modulated_norm(x1, g_ref[...], sc_ref[...], sh_ref[...]).astype(BF16)


def outproj(x, gate, acts, weight, g_next, sc_next, sh_next, tm=512):
    bn, t, d = x.shape
    n_in = len(acts)
    kdim = acts[0].shape[-1]
    assert all(a.shape[-1] == kdim for a in acts) and weight.shape[0] == n_in * kdim
    weights = [weight] * n_in
    row = pl.BlockSpec((None, tm, d), lambda b, i: (b, i, 0))
    vec = pl.BlockSpec((None, 1, d), lambda b, i: (b, 0, 0))
    in_specs = [row, vec, pl.BlockSpec((1, d), lambda b, i: (0, 0)), vec, vec]
    in_specs += [pl.BlockSpec((None, tm, kdim), lambda b, i: (b, i, 0)) for a in acts]
    in_specs += [pl.BlockSpec((kdim, d), lambda b, i, k=k: (k, 0)) for k in range(n_in)]
    return pl.pallas_call(
        functools.partial(_outproj_kernel, n_in=n_in),
        grid=(bn, t // tm),
        in_specs=in_specs,
        out_specs=[row, row],
        out_shape=[jax.ShapeDtypeStruct((bn, t, d), F32), jax.ShapeDtypeStruct((bn, t, d), BF16)],
        compiler_params=_cparams(("parallel", "parallel")),
        name="outproj",
    )(x, gate, g_next.reshape(1, d), sc_next, sh_next, *acts, *weights)


def _outproj_conv_kernel(x_ref, gate_ref, g_ref, sc_ref, sh_ref, gb_ref, gc_ref, val_ref,
                         gcp_ref, valp_ref, gcn_ref, valn_ref, cw_ref, w_ref, o_ref, hs_ref):
    i = pl.program_id(1)
    tm = x_ref.shape[0]
    u = gc_ref[...].astype(F32) * val_ref[...].astype(F32)
    last = gcp_ref.shape[0] - 1
    u_prev = gcp_ref[last:last + 1, :].astype(F32) * valp_ref[last:last + 1, :].astype(F32)
    u_next = gcn_ref[0:1, :].astype(F32) * valn_ref[0:1, :].astype(F32)
    u_prev = jnp.where(i == 0, 0.0, u_prev)
    u_next = jnp.where(i == pl.num_programs(1) - 1, 0.0, u_next)
    row = lax.broadcasted_iota(jnp.int32, (tm, 1), 0)
    below = jnp.where(row == 0, u_prev, pltpu.roll(u, 1, axis=0))
    above = jnp.where(row == tm - 1, u_next, pltpu.roll(u, tm - 1, axis=0))
    conv = below * cw_ref[0:1, :] + u * cw_ref[1:2, :] + above * cw_ref[2:3, :]
    z = (gb_ref[...].astype(F32) * conv).astype(BF16)
    x1 = x_ref[...] + gate_ref[...] * jnp.dot(z, w_ref[...], preferred_element_type=F32)
    o_ref[...] = x1
    hs_ref[...] = _modulated_norm(x1, g_ref[...], sc_ref[...], sh_ref[...]).astype(BF16)


def outproj_conv(x, gate, p, conv_w, weight, g_next, sc_next, sh_next, tm=512):
    bn, t, d = x.shape
    halo = 2 * SUBLANES
    nb = t // halo
    row = pl.BlockSpec((None, tm, d), lambda b, i: (b, i, 0))
    vec = pl.BlockSpec((None, 1, d), lambda b, i: (b, 0, 0))
    seg = lambda k: pl.BlockSpec((None, tm, d), lambda b, i, k=k: (b, i, k))
    prev = lambda k: pl.BlockSpec((None, halo, d), lambda b, i, k=k: (b, jnp.maximum(i * (tm // halo) - 1, 0), k))
    nxt = lambda k: pl.BlockSpec((None, halo, d), lambda b, i, k=k: (b, jnp.minimum((i + 1) * (tm // halo), nb - 1), k))
    return pl.pallas_call(
        _outproj_conv_kernel,
        grid=(bn, t // tm),
        in_specs=[row, vec, pl.BlockSpec((1, d), lambda b, i: (0, 0)), vec, vec,
                  seg(0), seg(1), seg(2), prev(1), prev(2), nxt(1), nxt(2),
                  pl.BlockSpec((SC_CONV, d), lambda b, i: (0, 0)),
                  pl.BlockSpec((d, d), lambda b, i: (0, 0))],
        out_specs=[row, row],
        out_shape=[jax.ShapeDtypeStruct((bn, t, d), F32), jax.ShapeDtypeStruct((bn, t, d), BF16)],
        compiler_params=_cparams(("parallel", "parallel")),
        name="outproj_conv",
    )(x, gate, g_next.reshape(1, d), sc_next, sh_next, p, p, p, p, p, p, p, conv_w, weight)


def _ffn_kernel(x_ref, hs_ref, gate_ref, fn_ref, wg_ref, wu_ref, wd_ref, o_ref, *, final_norm):
    j = pl.program_id(2)

    def step(first):
        h = hs_ref[...]
        gate = jnp.dot(h, wg_ref[...], preferred_element_type=F32)
        up = jnp.dot(h, wu_ref[...], preferred_element_type=F32)
        a = (_silu(gate) * up).astype(BF16)
        part = jnp.dot(a, wd_ref[...], preferred_element_type=F32)
        if first:
            o_ref[...] = part
        else:
            o_ref[...] += part

    pl.when(j == 0)(lambda: step(True))
    pl.when(j > 0)(lambda: step(False))

    @pl.when(j == pl.num_programs(2) - 1)
    def _():
        rb = min(NORM_ROW_BLOCK, o_ref.shape[0])

        def body(i, _):
            rows = pl.ds(pl.multiple_of(i * rb, rb), rb)
            y = x_ref[rows, :] + gate_ref[...] * o_ref[rows, :]
            if final_norm:
                y = (y * lax.rsqrt(jnp.mean(y * y, axis=-1, keepdims=True) + NORM_EPS)) * fn_ref[...]
            o_ref[rows, :] = y
            return 0

        lax.fori_loop(0, o_ref.shape[0] // rb, body, 0)


def ffn_grid(x, f, tm=1024, tf=512):
    bn, t, _ = x.shape
    tm = min(tm, t)
    return tm, tf, (bn, t // tm, f // tf)


def ffn_weight_jobs(w_gate, w_up, w_down, layer, grid, tf):
    bn, ni, nj = grid
    d = w_gate.shape[1]
    slab = d // (bn * ni)
    assert slab * bn * ni == d and slab % LANES == 0
    tile = lambda b, i: b * ni + i
    up_job = lambda w: CastJob(w, (None, slab, tf), lambda b, i, j: (layer, tile(b, i), j),
                               w.shape[1:], (slab, tf), lambda b, i, j: (tile(b, i), j))
    down_job = CastJob(w_down, (None, tf, slab), lambda b, i, j: (layer, j, tile(b, i)),
                       w_down.shape[1:], (tf, slab), lambda b, i, j: (j, tile(b, i)))
    return [up_job(w_gate), up_job(w_up), down_job]


def ffn(x, hs, gate, fnorm, w_gate, w_up, w_down, final_norm, casts=()):
    bn, t, d = x.shape
    f = w_gate.shape[1]
    tm, tf, grid = ffn_grid(x, f)
    vec = pl.BlockSpec((None, 1, d), lambda b, i, j: (b, 0, 0))
    one = pl.BlockSpec((1, d), lambda b, i, j: (0, 0))
    (y,), rounded = _call_hosting(
        functools.partial(_ffn_kernel, final_norm=final_norm), casts,
        grid=grid,
        in_specs=[
            _single((None, tm, d), lambda b, i, j: (b, i, 0)),
            pl.BlockSpec((None, tm, d), lambda b, i, j: (b, i, 0)),
            vec, one,
            pl.BlockSpec((d, tf), lambda b, i, j: (0, j)),
            pl.BlockSpec((d, tf), lambda b, i, j: (0, j)),
            pl.BlockSpec((tf, d), lambda b, i, j: (j, 0)),
        ],
        args=(x, hs, gate, fnorm.reshape(1, d), w_gate, w_up, w_down),
        out_specs=[pl.BlockSpec((None, tm, d), lambda b, i, j: (b, i, 0))],
        out_shapes=[jax.ShapeDtypeStruct((bn, t, d), F32)],
        compiler_params=_cparams(("parallel", "parallel", "arbitrary")),
        name="ffn",
    )
    return y, rounded


def rope_tables(t):
    pos = np.arange(t)
    row = (pos // GRID_W).astype(np.float32)
    col = (pos % GRID_W).astype(np.float32)
    n_freq = HEAD_DIM // 4
    inv_freq = jnp.asarray(ROPE_THETA, F32) ** (-jnp.arange(n_freq, dtype=F32) / n_freq)
    ang = jnp.concatenate([jnp.asarray(row)[:, None] * inv_freq, jnp.asarray(col)[:, None] * inv_freq], axis=-1)
    cos, sin = jnp.cos(ang), jnp.sin(ang)
    return jnp.concatenate([cos, cos], axis=-1), jnp.concatenate([-sin, sin], axis=-1)


def kernel(x, c, ctx, c_ctx, ada_w, ada_b, norm_mix, norm_ffn, ffn_w_gate, ffn_w_up, ffn_w_down, final_norm,
           ev_w_in, ev_conv, ev_a_log, ev_dt_bias, ev_gdn_norm, ev_rpb, ev_w_out, od_w_in, od_conv, od_w_out):
    bn, t, d = x.shape
    depth = ada_w.shape[0]
    assert depth == 2
    gw = GDN_HEADS * HEAD_DIM
    nw = NA_HEADS * HEAD_DIM
    nh = GDN_HEADS

    cv = jnp.zeros((SUBLANES, d), F32).at[:bn].set(c).at[bn].set(c_ctx)
    mods = ada_modulation(cv, ada_w, ada_b)

    def mod_vecs(l, rows):
        m = mods[l, rows].reshape(-1, 6, d)
        return [m[:, k][:, None, :] for k in range(6)]

    sh1, sc1, g1, sh2, sc2, g2 = mod_vecs(0, slice(0, bn))
    csh1, csc1 = [jnp.broadcast_to(v, (bn, 1, d)) for v in mod_vecs(0, slice(bn, bn + 1))[:2]]
    n_gate = 4 * nh
    seg_gate = 2 * gw + 2 * nw
    w_rows = jnp.swapaxes(ev_w_in[0], 0, 1)
    n_tail = w_rows.shape[0] - seg_gate - n_gate
    n_main = seg_gate + n_tail
    gate_block = seg_gate // LANES
    n_head_steps = bn * nh
    tail_slab = -(-n_tail // (n_head_steps * LANES)) * LANES
    tail_skip = n_head_steps * tail_slab - n_tail
    tail_first = w_rows.shape[0] - n_head_steps * tail_slab
    gparams = jnp.zeros((SUBLANES, LANES), F32)
    gparams = gparams.at[0, :2 * nh].set(ev_a_log[0].reshape(-1)).at[1, :2 * nh].set(ev_dt_bias[0].reshape(-1))
    col_ka, col_va, col_kb, col_vb, col_qa, col_qb, col_za = [k * nh for k in range(7)]

    head_step = lambda b, h: b * nh + h
    pc, gates_c = proj(ctx, norm_mix[0], csc1, csh1, w_rows, 0, seg_gate, None, seg_gate, w_rows, gate_block,
                       gparams, rows_major=True)
    s_ctx, (w_head, w_tail, w_out0) = gdn_context(
        pc, ev_conv[0], gate_rows(gates_c, ctx.shape[1]), col_ka, col_va,
        casts=[transposing_job(w_rows, 0, seg_gate, n_head_steps, head_step),
               transposing_job(w_rows, tail_first, n_head_steps * tail_slab, n_head_steps, head_step),
               row_slab_job(ev_w_out, 0, n_head_steps, head_step)])

    p, gates = proj(x, norm_mix[0], sc1, sh1, w_head[None], 0, seg_gate, w_tail, n_main, w_rows, gate_block,
                    gparams, tail_skip=tail_skip)
    cosf, sinf = rope_tables(t)
    y_gdn = gdn_latent(p, ev_conv[0], gate_rows(gates, t), cosf, sinf, s_ctx, ev_gdn_norm[0],
                       col_ka, col_va, col_qa, col_za)
    y_na, rounded = natten(p, pc, natten_bias(ev_rpb[0]), col_qb, col_kb, col_vb, col_kb, col_vb,
                           casts=[row_slab_job(w, 0, n_head_steps, head_step)
                                  for w in (ffn_w_gate, ffn_w_up, ffn_w_down, od_w_in, od_w_out)])
    ffn0, (w_in1, w_out1) = rounded[:3], rounded[3:]
    x_lat, hs = outproj(x, g1, [y_gdn, y_na], w_out0, norm_ffn[0], sc2, sh2)
    _, tf, grid = ffn_grid(x_lat, ffn_w_gate.shape[2])
    x_lat, ffn1 = ffn(x_lat, hs, g2, final_norm, *ffn0, final_norm=False,
                      casts=ffn_weight_jobs(ffn_w_gate, ffn_w_up, ffn_w_down, 1, grid, tf))

    sh1, sc1, g1, sh2, sc2, g2 = mod_vecs(1, slice(0, bn))
    p = proj(x_lat, norm_mix[1], sc1, sh1, w_in1[None], 0, 3 * d, None, 3 * d)
    x_lat, hs = outproj_conv(x_lat, g1, p, od_conv[0], w_out1, norm_ffn[1], sc2, sh2)
    x_lat, _ = ffn(x_lat, hs, g2, final_norm, *ffn1, final_norm=True)
    return x_lat
```

```python
import functools
import math

import jax
import jax.numpy as jnp
import numpy as np
from jax import lax
from jax.experimental import pallas as pl
from jax.experimental.pallas import tpu as pltpu

F32 = jnp.float32
BF16 = jnp.bfloat16
HIGHEST = lax.Precision.HIGHEST

LANES = 128
SUBLANES = 8
VMEM_LIMIT = 56 * 1024 * 1024

GRID_W = 64
HEAD_DIM = 128
GDN_HEADS = 8
NA_HEADS = 8
GDN_CONV = 5
CHUNK = 64
INVERSE_LEVELS = 6
GROUP = 2 * CHUNK
PREP_GROUPS_PER_ITER = 4
GDN_ROW_BLOCK = 512
NA_ROWS_PER_ITER = 8
NA_KH = 8
NA_KW = 16
SC_CONV = 3
ROPE_THETA = 10000.0
NORM_EPS = 1e-6
NEG_BIG = -1e30

NT_DIMS = (((1,), (1,)), ((), ()))


def _cparams(sem):
    return pltpu.CompilerParams(dimension_semantics=sem, vmem_limit_bytes=VMEM_LIMIT)


def _sigmoid(x):
    return 1.0 / (1.0 + jnp.exp(-x))


def _silu(x):
    return x * _sigmoid(x)


def _softplus(x):
    return jnp.maximum(x, 0.0) + jnp.log(1.0 + jnp.exp(-jnp.abs(x)))


def _mm(a, b):
    return jnp.dot(a.astype(BF16), b.astype(BF16), preferred_element_type=F32)


def _mm_nt(a, b):
    return lax.dot_general(a.astype(BF16), b.astype(BF16), NT_DIMS, preferred_element_type=F32)


def _mm_exact(a, b):
    return jnp.dot(a, b, precision=HIGHEST, preferred_element_type=F32)


class CastJob:
    def __init__(self, src, in_block, in_index, out_shape, out_block, out_index, transpose=False,
                 row_offset=0, next_rows=None):
        self.transpose = transpose
        self.row_offset = row_offset
        self.srcs = [src]
        self.in_specs = [pl.BlockSpec(in_block, in_index)]
        if row_offset:
            self.srcs.append(src)
            self.in_specs.append(pl.BlockSpec(*next_rows))
        self.out_spec = pl.BlockSpec(out_block, out_index)
        self.out_shape = jax.ShapeDtypeStruct(out_shape, BF16)


def _hosting_casts(body, n_in, n_out, casts):
    n_cast_in = sum(len(job.srcs) for job in casts)

    def kernel(*refs):
        ins, rest = refs[:n_in], refs[n_in:]
        cast_in, rest = list(rest[:n_cast_in]), rest[n_cast_in:]
        outs, rest = rest[:n_out], rest[n_out:]
        cast_out, scratch = rest[:len(casts)], rest[len(casts):]
        for job, dst in zip(casts, cast_out):
            src = cast_in.pop(0)
            block = src[...]
            if job.row_offset:
                block = jnp.concatenate([src[job.row_offset:, :], cast_in.pop(0)[...]], axis=0)
            dst[...] = (block.T if job.transpose else block).astype(BF16)
        body(*ins, *outs, *scratch)

    return kernel


def row_slab_job(w, layer, n_steps, step_of):
    rows, cols = w.shape[1:]
    slab = rows // n_steps
    assert slab * n_steps == rows and slab % (2 * SUBLANES) == 0
    return CastJob(w, (None, slab, cols), lambda *idx: (layer, step_of(*idx), 0),
                   (rows, cols), (slab, cols), lambda *idx: (step_of(*idx), 0))


def transposing_job(w_rows, first_row, n_rows, n_steps, step_of):
    d = w_rows.shape[1]
    slab = n_rows // n_steps
    assert slab * n_steps == n_rows and slab % LANES == 0
    first_block, offset = divmod(first_row, slab)
    assert offset % SUBLANES == 0
    next_rows = None
    if offset:
        assert slab % offset == 0
        per = slab // offset
        next_rows = ((offset, d), lambda *idx: ((first_block + step_of(*idx) + 1) * per, 0))
    return CastJob(w_rows, (slab, d), lambda *idx: (first_block + step_of(*idx), 0),
                   (d, n_rows), (d, slab), lambda *idx: (0, step_of(*idx)), transpose=True,
                   row_offset=offset, next_rows=next_rows)


def _call_hosting(body, casts, grid, in_specs, args, out_specs, out_shapes, **kwargs):
    n_in, n_out = len(in_specs), len(out_specs)
    casts = list(casts)
    res = pl.pallas_call(
        _hosting_casts(body, n_in, n_out, casts),
        grid=grid,
        in_specs=list(in_specs) + [s for c in casts for s in c.in_specs],
        out_specs=list(out_specs) + [c.out_spec for c in casts],
        out_shape=list(out_shapes) + [c.out_shape for c in casts],
        **kwargs,
    )(*args, *[s for c in casts for s in c.srcs])
    return res[:n_out], res[n_out:]


def _ada_kernel(cv_ref, w_ref, b_ref, o_ref):
    s = _silu(cv_ref[...])
    o_ref[...] = _mm(s, w_ref[...]) + b_ref[...]


def ada_modulation(cv, ada_w, ada_b, tn=1024):
    depth, d, n = ada_w.shape
    return pl.pallas_call(
        _ada_kernel,
        grid=(depth, n // tn),
        in_specs=[
            pl.BlockSpec((SUBLANES, d), lambda l, j: (0, 0)),
            pl.BlockSpec((None, d, tn), lambda l, j: (l, 0, j)),
            pl.BlockSpec((None, 1, tn), lambda l, j: (l, 0, j)),
        ],
        out_specs=pl.BlockSpec((None, SUBLANES, tn), lambda l, j: (l, 0, j)),
        out_shape=jax.ShapeDtypeStruct((depth, SUBLANES, n), F32),
        compiler_params=_cparams(("parallel", "parallel")),
        name="ada",
    )(cv, ada_w, ada_b.reshape(depth, 1, n))


def _modulated_norm(x, g, sc, sh):
    y = x * lax.rsqrt(jnp.mean(x * x, axis=-1, keepdims=True) + NORM_EPS)
    return (y * g) * (1.0 + sc) + sh


NORM_ROW_BLOCK = 128


def _modulated_norm_rows(x_ref, g_ref, sc_ref, sh_ref, hs_ref):
    rb = min(NORM_ROW_BLOCK, x_ref.shape[0])

    def body(i, _):
        rows = pl.ds(pl.multiple_of(i * rb, rb), rb)
        hs_ref[rows, :] = _modulated_norm(x_ref[rows, :], g_ref[...], sc_ref[...], sh_ref[...]).astype(BF16)
        return 0

    lax.fori_loop(0, x_ref.shape[0] // rb, body, 0)


def _proj_kernel(*refs, with_gates, with_tail, head_blocks, rows_major):
    x_ref, g_ref, sc_ref, sh_ref, w_ref = refs[:5]
    refs = refs[5:]
    wt_ref = None
    if with_tail:
        wt_ref, refs = refs[0], refs[1:]
    if with_gates:
        wab_ref, gp_ref, o_ref, gate_ref, hs_ref, wb_ref = refs
    else:
        o_ref, hs_ref, wb_ref = refs
    j = pl.program_id(2)

    @pl.when(j == 0)
    def _():
        if with_gates:
            hb = _modulated_norm(x_ref[...], g_ref[...], sc_ref[...], sh_ref[...]).astype(BF16)
            hs_ref[...] = hb
            a = _mm_nt(hb, wab_ref[...])
            neg_decay_rate = -jnp.exp(gp_ref[0:1, :])
            g = neg_decay_rate * _softplus(a + gp_ref[1:2, :])
            lane = lax.broadcasted_iota(jnp.int32, a.shape, 1)
            gate_ref[...] = jnp.where(lane < 2 * GDN_HEADS, g, _sigmoid(a))
        else:
            _modulated_norm_rows(x_ref, g_ref, sc_ref, sh_ref, hs_ref)

    def emit(wref):
        if wref.dtype != BF16:
            wb_ref[...] = wref[...].astype(BF16)
            wref = wb_ref
        if rows_major:
            y = lax.dot_general(hs_ref[...], wref[...], NT_DIMS, preferred_element_type=F32)
        else:
            y = jnp.dot(hs_ref[...], wref[...], preferred_element_type=F32)
        o_ref[...] = y.astype(o_ref.dtype)

    pl.when(j < head_blocks)(lambda: emit(w_ref))
    if with_tail:
        pl.when(j >= head_blocks)(lambda: emit(wt_ref))


def proj(x, g, sc, sh, w, layer, head_cols, w_tail, n_out, wab=None, gate_block=0, gparams=None,
         rows_major=False, tail_skip=0, tm=1024, tn=2048):
    bn, t, d = x.shape
    tm = min(tm, t)
    while n_out % tn or head_cols % tn or tail_skip % tn:
        tn //= 2
    head_blocks = head_cols // tn
    with_gates = wab is not None
    vec = pl.BlockSpec((None, 1, d), lambda b, i, j: (b, 0, 0))
    if rows_major:
        w_spec = pl.BlockSpec((tn, d), lambda b, i, j: (jnp.minimum(j, head_blocks - 1), 0))
    else:
        w_spec = pl.BlockSpec((None, d, tn), lambda b, i, j: (layer, 0, jnp.minimum(j, head_blocks - 1)))
    in_specs = [
        pl.BlockSpec((None, tm, d), lambda b, i, j: (b, i, 0)),
        pl.BlockSpec((1, d), lambda b, i, j: (0, 0)),
        vec, vec,
        w_spec,
    ]
    args = [x, g.reshape(1, d), sc, sh, w]
    with_tail = w_tail is not None
    if with_tail:
        skip = tail_skip // tn
        in_specs.append(pl.BlockSpec((d, tn), lambda b, i, j: (0, jnp.maximum(j - head_blocks, 0) + skip)))
        args.append(w_tail)
    out_specs = [pl.BlockSpec((None, tm, tn), lambda b, i, j: (b, i, j))]
    out_shape = [jax.ShapeDtypeStruct((bn, t, n_out), BF16)]
    if with_gates:
        in_specs += [pl.BlockSpec((LANES, d), lambda b, i, j: (gate_block, 0)),
                     pl.BlockSpec((SUBLANES, LANES), lambda b, i, j: (0, 0))]
        args += [wab, gparams]
        out_specs.append(pl.BlockSpec((None, tm, LANES), lambda b, i, j: (b, i, 0)))
        out_shape.append(jax.ShapeDtypeStruct((bn, t, LANES), F32))
    res = pl.pallas_call(
        functools.partial(_proj_kernel, with_gates=with_gates, with_tail=with_tail, head_blocks=head_blocks,
                          rows_major=rows_major),
        grid=(bn, t // tm, n_out // tn),
        in_specs=in_specs,
        out_specs=out_specs,
        out_shape=out_shape,
        scratch_shapes=[pltpu.VMEM((tm, d), BF16), pltpu.VMEM((tn, d) if rows_major else (d, tn), BF16)],
        compiler_params=_cparams(("parallel", "parallel", "arbitrary")),
        name="proj",
    )(*args)
    return res if with_gates else res[0]


CONV_HALO = SUBLANES


def _stage_padded(pad_ref, load_rows, t, rb):
    zeros = jnp.zeros((CONV_HALO, LANES), F32)
    pad_ref[0:CONV_HALO, :] = zeros
    pad_ref[CONV_HALO + t:2 * CONV_HALO + t, :] = zeros

    def body(i, _):
        r0 = pl.multiple_of(i * rb, rb)
        pad_ref[pl.ds(CONV_HALO + r0, rb), :] = load_rows(r0)
        return 0

    lax.fori_loop(0, t // rb, body, 0)


def _conv_rows(pad_ref, w_ref, r0, rb, taps):
    acc = None
    for j in range(taps):
        xj = pad_ref[pl.ds(r0 + (CONV_HALO + j - taps // 2), rb), :]
        term = xj * w_ref[j:j + 1, :]
        acc = term if acc is None else acc + term
    return acc


def _l2norm(x):
    return x * lax.rsqrt(jnp.sum(x * x, axis=-1, keepdims=True) + NORM_EPS)


def _gdn_kernel(*refs, t, with_q):
    nc = t // CHUNK
    rb = min(GDN_ROW_BLOCK, t)
    if with_q:
        (ka_ref, va_ref, qa_ref, za_ref, wk_ref, wv_ref, wq_ref, grow_ref, cos_ref, sin_ref, s0_ref,
         gn_ref, y_ref,
         pad_ref, pad2_ref, k_ref, v_ref, q_ref, o_ref, gc_ref, kwq_s, h_s, au_s, gl_s) = refs
    else:
        (ka_ref, va_ref, wk_ref, wv_ref, grow_ref, sfin_ref,
         pad_ref, k_ref, v_ref, gc_ref, kwq_s, h_s, gl_s) = refs

    def conv_all(src_ref, w_ref, finish, dst_ref):
        _stage_padded(pad_ref, lambda r0: src_ref[pl.ds(r0, rb), :].astype(F32), t, rb)

        def body(i, _):
            r0 = pl.multiple_of(i * rb, rb)
            y = _silu(_conv_rows(pad_ref, w_ref, r0, rb, GDN_CONV))
            dst_ref[pl.ds(r0, rb), :] = finish(y, r0)
            return 0

        lax.fori_loop(0, t // rb, body, 0)

    def rope(x, r0):
        return (x * cos_ref[pl.ds(r0, rb), :]
                + pltpu.roll(x, HEAD_DIM // 2, axis=1) * sin_ref[pl.ds(r0, rb), :])

    if with_q:
        _stage_padded(pad_ref, lambda r0: ka_ref[pl.ds(r0, rb), :].astype(F32), t, rb)
        _stage_padded(pad2_ref, lambda r0: qa_ref[pl.ds(r0, rb), :].astype(F32), t, rb)

        def kq_body(i, _):
            r0 = pl.multiple_of(i * rb, rb)
            yk = _silu(_conv_rows(pad_ref, wk_ref, r0, rb, GDN_CONV))
            yq = _silu(_conv_rows(pad2_ref, wq_ref, r0, rb, GDN_CONV))
            k_ref[pl.ds(r0, rb), :] = rope(_l2norm(yk), r0)
            q_ref[pl.ds(r0, rb), :] = rope(_l2norm(yq), r0) * HEAD_DIM ** -0.5
            return 0

        lax.fori_loop(0, t // rb, kq_body, 0)
    else:
        conv_all(ka_ref, wk_ref, lambda y, r0: _l2norm(y), k_ref)
    conv_all(va_ref, wv_ref, lambda y, r0: y, v_ref)

    ii = lax.broadcasted_iota(jnp.int32, (GROUP, GROUP), 0)
    jj = lax.broadcasted_iota(jnp.int32, (GROUP, GROUP), 1)
    same = (ii // CHUNK) == (jj // CHUNK)
    lower = same & (ii >= jj)
    upper = same & (ii <= jj)
    eye = ii == jj
    gc_ref[0] = _mm_exact(grow_ref[0], upper.astype(F32))
    gc_ref[1] = _mm_exact(grow_ref[1], lower.astype(F32))

    n_groups = t // GROUP
    gpi = min(PREP_GROUPS_PER_ITER, n_groups)
    n_sets = n_groups // gpi
    steps_per_set = 2 * gpi

    def prep_stages(i):
        chains = ([(i * gpi + k, 0) for k in range(gpi)]
                  + [(n_groups - 1 - (i * gpi + k), 1) for k in range(gpi)])
        ids = range(len(chains))
        st = {}

        def load():
            for ch, (g, d) in enumerate(chains):
                r0 = pl.multiple_of(g * GROUP, GROUP)
                st["k", ch] = k_ref[pl.ds(r0, GROUP), :]
                st["v", ch] = v_ref[pl.ds(r0, GROUP), :]
                kgb = st["k", ch].astype(BF16)
                if with_q:
                    st["q", ch] = q_ref[pl.ds(r0, GROUP), :]
                    both = _mm_nt(jnp.concatenate([st["k", ch], st["q", ch]], axis=0), kgb)
                    st["kk", ch], st["qk", ch] = both[:GROUP], both[GROUP:]
                else:
                    st["kk", ch] = _mm_nt(kgb, kgb)

        def masks():
            for ch, (g, d) in enumerate(chains):
                incl = lower if d == 0 else upper
                gc_r = jnp.broadcast_to(gc_ref[d, pl.ds(g, 1), :], (GROUP, GROUP))
                be_r = jnp.broadcast_to(grow_ref[2 + d, pl.ds(g, 1), :], (GROUP, GROUP))
                gc_c = gc_r.T
                decay = jnp.where(incl, jnp.exp(jnp.where(incl, gc_c - gc_r, 0.0)), 0.0)
                m = jnp.where(incl & jnp.logical_not(eye), st["kk", ch] * be_r.T * decay, 0.0)
                st["be_r", ch], st["gc_c", ch], st["decay", ch] = be_r, gc_c, decay
                st["p", ch] = jnp.where(eye, 1.0, -m)
                st["sq", ch] = _mm(m, m)

        def level():
            for ch in ids:
                p, sq = st["p", ch], st["sq", ch]
                st["p", ch] = p + _mm(p, sq)
                st["sq", ch] = _mm(sq, sq)

        def last_level():
            for ch in ids:
                st["p", ch] = st["p", ch] + _mm(st["p", ch], st["sq", ch])

        def solve():
            for ch in ids:
                rhs = jnp.concatenate([st["v", ch], st["k", ch] * jnp.exp(st["gc_c", ch])], axis=1)
                st["uw", ch] = _mm(st["p", ch] * st["be_r", ch], rhs)

        def fold():
            for ch, (g, d) in enumerate(chains):
                gc_c = st["gc_c", ch]
                last = [CHUNK - 1, GROUP - 1] if d == 0 else [0, CHUNK]
                gls = [gc_c[r:r + 1, :] for r in last]
                gl_c = jnp.concatenate([jnp.broadcast_to(x, (CHUNK, LANES)) for x in gls], axis=0)
                kdt = (st["k", ch] * jnp.exp(gl_c - gc_c)).T
                parts = [jnp.where(jj < CHUNK, kdt, 0.0), jnp.where(jj >= CHUNK, kdt, 0.0)]
                if with_q:
                    st["qd", ch] = st["q", ch] * jnp.exp(gc_c)
                    parts = [st["qk", ch] * st["decay", ch]] + parts
                st["gl", ch] = gls
                st["prod", ch] = _mm(jnp.concatenate(parts, axis=0), st["uw", ch])

        def store():
            for ch, (g, d) in enumerate(chains):
                r = st["prod", ch]
                off = GROUP if with_q else 0
                for half in range(2):
                    c = 2 * g + half
                    blk = r[off + half * HEAD_DIM:off + (half + 1) * HEAD_DIM]
                    h_s[d, c] = blk[:, :HEAD_DIM]
                    gl_s[d, c] = jnp.broadcast_to(jnp.exp(st["gl", ch][half]), (SUBLANES, LANES))
                    if with_q:
                        rs = slice(half * CHUNK, (half + 1) * CHUNK)
                        au_s[d, c] = r[rs, :HEAD_DIM]
                        qw = st["qd", ch][rs] - r[rs, HEAD_DIM:]
                        kwq_s[d, c] = jnp.concatenate([blk[:, HEAD_DIM:], qw], axis=0).astype(BF16)
                    else:
                        kwq_s[d, c] = blk[:, HEAD_DIM:].astype(BF16)

        return [load, masks] + [level] * (INVERSE_LEVELS - 2) + [last_level, solve, fold, store]

    def scan_step(s, carry):
        cs = (s, nc - 1 - s)
        rr = [jnp.dot(kwq_s[d, cs[d]], carry[d].astype(BF16), preferred_element_type=F32) for d in range(2)]
        new = []
        for d in range(2):
            c = cs[d]
            new.append(carry[d] * gl_s[d, c][0:1, :] + h_s[d, c] - rr[d][:HEAD_DIM])
            if with_q:
                r0 = pl.multiple_of(c * CHUNK, CHUNK)
                o_ref[pl.ds(r0, CHUNK), :] += rr[d][HEAD_DIM:] + au_s[d, c]
        return tuple(new)

    def run(stages, steps, carry):
        for idx in range(max(len(stages), len(steps))):
            if idx < len(stages):
                stages[idx]()
            if idx < len(steps):
                carry = scan_step(steps[idx], carry)
        return carry

    def set_steps(i):
        return [i * steps_per_set + k for k in range(steps_per_set)]

    if with_q:
        o_ref[...] = jnp.zeros_like(o_ref)
        carry = (s0_ref[0], s0_ref[1])
    else:
        zero = jnp.zeros((HEAD_DIM, HEAD_DIM), F32)
        carry = (zero, zero)
    run(prep_stages(0), [], carry)
    carry = lax.fori_loop(1, n_sets, lambda i, c: run(prep_stages(i), set_steps(i - 1), c), carry)
    s_f, s_b = run([], set_steps(n_sets - 1), carry)

    if not with_q:
        sfin_ref[0] = s_f
        sfin_ref[1] = s_b
        return

    def finish(i, _):
        r0 = pl.multiple_of(i * rb, rb)
        o = o_ref[pl.ds(r0, rb), :]
        y = o * lax.rsqrt(jnp.mean(o * o, axis=-1, keepdims=True) + NORM_EPS)
        y_ref[pl.ds(r0, rb), :] = ((y * gn_ref[...]) * _silu(za_ref[pl.ds(r0, rb), :].astype(F32))).astype(y_ref.dtype)
        return 0

    lax.fori_loop(0, t // rb, finish, 0)


def _single(block_shape, index_map):
    return pl.BlockSpec(block_shape, index_map, pipeline_mode=pl.Buffered(1))


def gdn_latent(p, conv_w, grow, cosf, sinf, s0, gnorm, col_k, col_v, col_q, col_z, casts=()):
    bn, t, _ = p.shape
    nc = t // CHUNK
    ngp = grow.shape[3]
    hd = HEAD_DIM
    seq = lambda col: pl.BlockSpec((None, t, hd), lambda b, h, col=col: (b, 0, col + h))
    cw = lambda col: pl.BlockSpec((GDN_CONV, hd), lambda b, h, col=col: (0, col + h))
    (y,), rounded = _call_hosting(
        lambda *refs: _gdn_kernel(*refs, t=t, with_q=True), casts,
        grid=(bn, GDN_HEADS),
        in_specs=[
            seq(col_k), seq(col_v), seq(col_q), seq(col_z),
            cw(0), cw(GDN_HEADS), cw(2 * GDN_HEADS),
            pl.BlockSpec((None, None, 4, ngp, GROUP), lambda b, h: (b, h, 0, 0, 0)),
            _single((t, hd), lambda b, h: (0, 0)),
            _single((t, hd), lambda b, h: (0, 0)),
            pl.BlockSpec((None, None, 2, hd, hd), lambda b, h: (b, h, 0, 0, 0)),
            pl.BlockSpec((1, hd), lambda b, h: (0, 0)),
        ],
        args=(p, p, p, p, conv_w, conv_w, conv_w, grow, cosf, sinf, s0, gnorm.reshape(1, hd)),
        out_specs=[pl.BlockSpec((None, t, hd), lambda b, h: (b, 0, h))],
        out_shapes=[jax.ShapeDtypeStruct((bn, t, GDN_HEADS * hd), BF16)],
        scratch_shapes=[
            pltpu.VMEM((t + 2 * CONV_HALO, hd), F32),
            pltpu.VMEM((t + 2 * CONV_HALO, hd), F32),
            pltpu.VMEM((t, hd), F32),
            pltpu.VMEM((t, hd), F32),
            pltpu.VMEM((t, hd), F32),
            pltpu.VMEM((t, hd), F32),
            pltpu.VMEM((2, ngp, GROUP), F32),
            pltpu.VMEM((2, nc, hd + CHUNK, hd), BF16),
            pltpu.VMEM((2, nc, hd, hd), F32),
            pltpu.VMEM((2, nc, CHUNK, hd), F32),
            pltpu.VMEM((2, nc, SUBLANES, LANES), F32),
        ],
        compiler_params=_cparams(("parallel", "parallel")),
        name="gdn_latent",
    )
    return y, rounded


def gdn_context(pc, conv_w, grow, col_k, col_v, casts=()):
    bn, t, _ = pc.shape
    nc = t // CHUNK
    ngp = grow.shape[3]
    hd = HEAD_DIM
    seq = lambda col: pl.BlockSpec((None, t, hd), lambda b, h, col=col: (b, 0, col + h))
    cw = lambda col: pl.BlockSpec((GDN_CONV, hd), lambda b, h, col=col: (0, col + h))
    (states,), rounded = _call_hosting(
        lambda *refs: _gdn_kernel(*refs, t=t, with_q=False), casts,
        grid=(bn, GDN_HEADS),
        in_specs=[
            seq(col_k), seq(col_v), cw(0), cw(GDN_HEADS),
            pl.BlockSpec((None, None, 4, ngp, GROUP), lambda b, h: (b, h, 0, 0, 0)),
        ],
        args=(pc, pc, conv_w, conv_w, grow),
        out_specs=[pl.BlockSpec((None, None, 2, hd, hd), lambda b, h: (b, h, 0, 0, 0))],
        out_shapes=[jax.ShapeDtypeStruct((bn, GDN_HEADS, 2, hd, hd), F32)],
        scratch_shapes=[
            pltpu.VMEM((t + 2 * CONV_HALO, hd), F32),
            pltpu.VMEM((t, hd), F32),
            pltpu.VMEM((t, hd), F32),
            pltpu.VMEM((2, ngp, GROUP), F32),
            pltpu.VMEM((2, nc, hd, hd), BF16),
            pltpu.VMEM((2, nc, hd, hd), F32),
            pltpu.VMEM((2, nc, SUBLANES, LANES), F32),
        ],
        compiler_params=_cparams(("parallel", "parallel")),
        name="gdn_context",
    )
    return states, rounded


def gate_rows(gates, t):
    bn = gates.shape[0]
    ng = t // GROUP
    g = gates[:, :, :4 * GDN_HEADS].reshape(bn, ng, GROUP, 4, GDN_HEADS)
    g = jnp.transpose(g, (0, 4, 3, 1, 2))
    if ng < SUBLANES:
        g = jnp.pad(g, ((0, 0), (0, 0), (0, 0), (0, SUBLANES - ng), (0, 0)))
    return g


def _natten_kernel(q_ref, k_ref, v_ref, kc_ref, vc_ref, by_dr_ref, o_ref, bias_ref, *, rows):
    kcb = kc_ref[...]
    vcb = vc_ref[...]
    scale = HEAD_DIM ** -0.5
    kh = min(NA_KH, rows)
    win = kh * GRID_W
    per_iter = min(NA_ROWS_PER_ITER, rows)

    for cls in range(NA_KH):
        for j in range(kh):
            bias_ref[cls, :, j * GRID_W:(j + 1) * GRID_W] = by_dr_ref[NA_KH - 1 - cls + j]

    def body(it, _):
        rws = [it * per_iter + k for k in range(per_iter)]
        q0s, k0s, s_loc, s_ctx = [], [], [], []
        for r in rws:
            rs = jnp.clip(r - NA_KH // 2, 0, rows - NA_KH)
            q0s.append(pl.multiple_of(r * GRID_W, GRID_W))
            k0s.append(pl.multiple_of(rs * GRID_W, GRID_W))
            q = q_ref[pl.ds(q0s[-1], GRID_W), :]
            kw = k_ref[pl.ds(k0s[-1], win), :]
            s_loc.append(lax.dot_general(q, kw, NT_DIMS, preferred_element_type=F32) * scale + bias_ref[r - rs])
            s_ctx.append(lax.dot_general(q, kcb, NT_DIMS, preferred_element_type=F32) * scale)
        p_loc, p_ctx, den = [], [], []
        for a, b in zip(s_loc, s_ctx):
            m = jnp.maximum(jnp.max(a, axis=-1, keepdims=True), jnp.max(b, axis=-1, keepdims=True))
            p_loc.append(jnp.exp(a - m))
            p_ctx.append(jnp.exp(b - m))
            den.append(jnp.sum(p_loc[-1], axis=-1, keepdims=True) + jnp.sum(p_ctx[-1], axis=-1, keepdims=True))
        outs = []
        for k in range(per_iter):
            vw = v_ref[pl.ds(k0s[k], win), :]
            outs.append(jnp.dot(p_loc[k].astype(BF16), vw, preferred_element_type=F32)
                        + jnp.dot(p_ctx[k].astype(BF16), vcb, preferred_element_type=F32))
        for k in range(per_iter):
            o_ref[pl.ds(q0s[k], GRID_W), :] = (outs[k] / den[k]).astype(o_ref.dtype)
        return 0

    lax.fori_loop(0, rows // per_iter, body, 0)


def natten(p, pc, by_dr, col_q, col_k, col_v, ctx_col_k, ctx_col_v, casts=()):
    bn, t, _ = p.shape
    lc = pc.shape[1]
    hd = HEAD_DIM
    rows = t // GRID_W
    kh = min(NA_KH, rows)
    seq = lambda col: pl.BlockSpec((None, t, hd), lambda b, h, col=col: (b, 0, col + h))
    cseq = lambda col: pl.BlockSpec((None, lc, hd), lambda b, h, col=col: (b, 0, col + h))
    (y,), rounded = _call_hosting(
        lambda *refs: _natten_kernel(*refs, rows=rows), casts,
        grid=(bn, NA_HEADS),
        in_specs=[
            seq(col_q), seq(col_k), seq(col_v), cseq(ctx_col_k), cseq(ctx_col_v),
            pl.BlockSpec((None,) + by_dr.shape[1:], lambda b, h: (h, 0, 0, 0)),
        ],
        args=(p, p, p, pc, pc, by_dr),
        out_specs=[pl.BlockSpec((None, t, hd), lambda b, h: (b, 0, h))],
        out_shapes=[jax.ShapeDtypeStruct((bn, t, NA_HEADS * hd), BF16)],
        scratch_shapes=[pltpu.VMEM((NA_KH, GRID_W, kh * GRID_W), F32)],
        compiler_params=_cparams(("parallel", "parallel")),
        name="natten",
    )
    return y, rounded


def natten_bias(rpb):
    hn, n_dr, n_dc = rpb.shape
    col = np.arange(GRID_W)
    col_start = np.clip(col - NA_KW // 2, 0, GRID_W - NA_KW)
    in_win = (col[None, :] >= col_start[:, None]) & (col[None, :] < col_start[:, None] + NA_KW)
    dc = np.clip(col[None, :] - col[:, None], -(NA_KW - 1), NA_KW - 1) + (NA_KW - 1)
    onehot = (np.arange(n_dc)[:, None] == dc.reshape(-1)[None, :]).astype(np.float32)
    by_dr = jnp.dot(rpb.astype(F32).reshape(hn * n_dr, n_dc), jnp.asarray(onehot), precision=HIGHEST)
    return jnp.where(in_win[None, None], by_dr.reshape(hn, n_dr, GRID_W, GRID_W), NEG_BIG)


def _outproj_kernel(*refs, n_in):
    x_ref, gate_ref, g_ref, sc_ref, sh_ref = refs[:5]
    a_refs = refs[5:5 + n_in]
    w_refs = refs[5 + n_in:5 + 2 * n_in]
    o_ref, hs_ref = refs[5 + 2 * n_in:]
    y = None
    for a_ref, w_ref in zip(a_refs, w_refs):
        part = jnp.dot(a_ref[...], w_ref[...], preferred_element_type=F32)
        y = part if y is None else y + part
    x1 = x_ref[...] + gate_ref[...] * y
    o_ref[...] = x1
    hs_ref[...] = _modulated_norm(x1, g_ref[...], sc_ref[...], sh_ref[...]).astype(BF16)


def outproj(x, gate, acts, weight, g_next, sc_next, sh_next, tm=512):
    bn, t, d = x.shape
    n_in = len(acts)
    kdim = acts[0].shape[-1]
    assert all(a.shape[-1] == kdim for a in acts) and weight.shape[0] == n_in * kdim
    weights = [weight] * n_in
    row = pl.BlockSpec((None, tm, d), lambda b, i: (b, i, 0))
    vec = pl.BlockSpec((None, 1, d), lambda b, i: (b, 0, 0))
    in_specs = [row, vec, pl.BlockSpec((1, d), lambda b, i: (0, 0)), vec, vec]
    in_specs += [pl.BlockSpec((None, tm, kdim), lambda b, i: (b, i, 0)) for a in acts]
    in_specs += [pl.BlockSpec((kdim, d), lambda b, i, k=k: (k, 0)) for k in range(n_in)]
    return pl.pallas_call(
        functools.partial(_outproj_kernel, n_in=n_in),
        grid=(bn, t // tm),
        in_specs=in_specs,
        out_specs=[row, row],
        out_shape=[jax.ShapeDtypeStruct((bn, t, d), F32), jax.ShapeDtypeStruct((bn, t, d), BF16)],
        compiler_params=_cparams(("parallel", "parallel")),
        name="outproj",
    )(x, gate, g_next.reshape(1, d), sc_next, sh_next, *acts, *weights)


def _outproj_conv_kernel(x_ref, gate_ref, g_ref, sc_ref, sh_ref, gb_ref, gc_ref, val_ref,
                         gcp_ref, valp_ref, gcn_ref, valn_ref, cw_ref, w_ref, o_ref, hs_ref):
    i = pl.program_id(1)
    tm = x_ref.shape[0]
    u = gc_ref[...].astype(F32) * val_ref[...].astype(F32)
    last = gcp_ref.shape[0] - 1
    u_prev = gcp_ref[last:last + 1, :].astype(F32) * valp_ref[last:last + 1, :].astype(F32)
    u_next = gcn_ref[0:1, :].astype(F32) * valn_ref[0:1, :].astype(F32)
    u_prev = jnp.where(i == 0, 0.0, u_prev)
    u_next = jnp.where(i == pl.num_programs(1) - 1, 0.0, u_next)
    row = lax.broadcasted_iota(jnp.int32, (tm, 1), 0)
    below = jnp.where(row == 0, u_prev, pltpu.roll(u, 1, axis=0))
    above = jnp.where(row == tm - 1, u_next, pltpu.roll(u, tm - 1, axis=0))
    conv = below * cw_ref[0:1, :] + u * cw_ref[1:2, :] + above * cw_ref[2:3, :]
    z = (gb_ref[...].astype(F32) * conv).astype(BF16)
    x1 = x_ref[...] + gate_ref[...] * jnp.dot(z, w_ref[...], preferred_element_type=F32)
    o_ref[...] = x1
    hs_ref[...] = _modulated_norm(x1, g_ref[...], sc_ref[...], sh_ref[...]).astype(BF16)


def outproj_conv(x, gate, p, conv_w, weight, g_next, sc_next, sh_next, tm=512):
    bn, t, d = x.shape
    halo = 2 * SUBLANES
    nb = t // halo
    row = pl.BlockSpec((None, tm, d), lambda b, i: (b, i, 0))
    vec = pl.BlockSpec((None, 1, d), lambda b, i: (b, 0, 0))
    seg = lambda k: pl.BlockSpec((None, tm, d), lambda b, i, k=k: (b, i, k))
    prev = lambda k: pl.BlockSpec((None, halo, d), lambda b, i, k=k: (b, jnp.maximum(i * (tm // halo) - 1, 0), k))
    nxt = lambda k: pl.BlockSpec((None, halo, d), lambda b, i, k=k: (b, jnp.minimum((i + 1) * (tm // halo), nb - 1), k))
    return pl.pallas_call(
        _outproj_conv_kernel,
        grid=(bn, t // tm),
        in_specs=[row, vec, pl.BlockSpec((1, d), lambda b, i: (0, 0)), vec, vec,
                  seg(0), seg(1), seg(2), prev(1), prev(2), nxt(1), nxt(2),
                  pl.BlockSpec((SC_CONV, d), lambda b, i: (0, 0)),
                  pl.BlockSpec((d, d), lambda b, i: (0, 0))],
        out_specs=[row, row],
        out_shape=[jax.ShapeDtypeStruct((bn, t, d), F32), jax.ShapeDtypeStruct((bn, t, d), BF16)],
        compiler_params=_cparams(("parallel", "parallel")),
        name="outproj_conv",
    )(x, gate, g_next.reshape(1, d), sc_next, sh_next, p, p, p, p, p, p, p, conv_w, weight)


def _ffn_kernel(x_ref, hs_ref, gate_ref, fn_ref, wg_ref, wu_ref, wd_ref, o_ref, *, final_norm):
    j = pl.program_id(2)

    def step(first):
        h = hs_ref[...]
        gate = jnp.dot(h, wg_ref[...], preferred_element_type=F32)
        up = jnp.dot(h, wu_ref[...], preferred_element_type=F32)
        a = (_silu(gate) * up).astype(BF16)
        part = jnp.dot(a, wd_ref[...], preferred_element_type=F32)
        if first:
            o_ref[...] = part
        else:
            o_ref[...] += part

    pl.when(j == 0)(lambda: step(True))
    pl.when(j > 0)(lambda: step(False))

    @pl.when(j == pl.num_programs(2) - 1)
    def _():
        rb = min(NORM_ROW_BLOCK, o_ref.shape[0])

        def body(i, _):
            rows = pl.ds(pl.multiple_of(i * rb, rb), rb)
            y = x_ref[rows, :] + gate_ref[...] * o_ref[rows, :]
            if final_norm:
                y = (y * lax.rsqrt(jnp.mean(y * y, axis=-1, keepdims=True) + NORM_EPS)) * fn_ref[...]
            o_ref[rows, :] = y
            return 0

        lax.fori_loop(0, o_ref.shape[0] // rb, body, 0)


def ffn_grid(x, f, tm=1024, tf=512):
    bn, t, _ = x.shape
    tm = min(tm, t)
    return tm, tf, (bn, t // tm, f // tf)


def ffn_weight_jobs(w_gate, w_up, w_down, layer, grid, tf):
    bn, ni, nj = grid
    d = w_gate.shape[1]
    slab = d // (bn * ni)
    assert slab * bn * ni == d and slab % LANES == 0
    tile = lambda b, i: b * ni + i
    up_job = lambda w: CastJob(w, (None, slab, tf), lambda b, i, j: (layer, tile(b, i), j),
                               w.shape[1:], (slab, tf), lambda b, i, j: (tile(b, i), j))
    down_job = CastJob(w_down, (None, tf, slab), lambda b, i, j: (layer, j, tile(b, i)),
                       w_down.shape[1:], (tf, slab), lambda b, i, j: (j, tile(b, i)))
    return [up_job(w_gate), up_job(w_up), down_job]


def ffn(x, hs, gate, fnorm, w_gate, w_up, w_down, final_norm, casts=()):
    bn, t, d = x.shape
    f = w_gate.shape[1]
    tm, tf, grid = ffn_grid(x, f)
    vec = pl.BlockSpec((None, 1, d), lambda b, i, j: (b, 0, 0))
    one = pl.BlockSpec((1, d), lambda b, i, j: (0, 0))
    (y,), rounded = _call_hosting(
        functools.partial(_ffn_kernel, final_norm=final_norm), casts,
        grid=grid,
        in_specs=[
            _single((None, tm, d), lambda b, i, j: (b, i, 0)),
            pl.BlockSpec((None, tm, d), lambda b, i, j: (b, i, 0)),
            vec, one,
            pl.BlockSpec((d, tf), lambda b, i, j: (0, j)),
            pl.BlockSpec((d, tf), lambda b, i, j: (0, j)),
            pl.BlockSpec((tf, d), lambda b, i, j: (j, 0)),
        ],
        args=(x, hs, gate, fnorm.reshape(1, d), w_gate, w_up, w_down),
        out_specs=[pl.BlockSpec((None, tm, d), lambda b, i, j: (b, i, 0))],
        out_shapes=[jax.ShapeDtypeStruct((bn, t, d), F32)],
        compiler_params=_cparams(("parallel", "parallel", "arbitrary")),
        name="ffn",
    )
    return y, rounded


def rope_tables(t):
    pos = np.arange(t)
    row = (pos // GRID_W).astype(np.float32)
    col = (pos % GRID_W).astype(np.float32)
    n_freq = HEAD_DIM // 4
    inv_freq = jnp.asarray(ROPE_THETA, F32) ** (-jnp.arange(n_freq, dtype=F32) / n_freq)
    ang = jnp.concatenate([jnp.asarray(row)[:, None] * inv_freq, jnp.asarray(col)[:, None] * inv_freq], axis=-1)
    cos, sin = jnp.cos(ang), jnp.sin(ang)
    return jnp.concatenate([cos, cos], axis=-1), jnp.concatenate([-sin, sin], axis=-1)


def kernel(x, c, ctx, c_ctx, ada_w, ada_b, norm_mix, norm_ffn, ffn_w_gate, ffn_w_up, ffn_w_down, final_norm,
           ev_w_in, ev_conv, ev_a_log, ev_dt_bias, ev_gdn_norm, ev_rpb, ev_w_out, od_w_in, od_conv, od_w_out):
    bn, t, d = x.shape
    depth = ada_w.shape[0]
    assert depth == 2
    gw = GDN_HEADS * HEAD_DIM
    nw = NA_HEADS * HEAD_DIM
    nh = GDN_HEADS

    cv = jnp.zeros((SUBLANES, d), F32).at[:bn].set(c).at[bn].set(c_ctx)
    mods = ada_modulation(cv, ada_w, ada_b)

    def mod_vecs(l, rows):
        m = mods[l, rows].reshape(-1, 6, d)
        return [m[:, k][:, None, :] for k in range(6)]

    sh1, sc1, g1, sh2, sc2, g2 = mod_vecs(0, slice(0, bn))
    csh1, csc1 = [jnp.broadcast_to(v, (bn, 1, d)) for v in mod_vecs(0, slice(bn, bn + 1))[:2]]
    n_gate = 4 * nh
    seg_gate = 2 * gw + 2 * nw
    w_rows = jnp.swapaxes(ev_w_in[0], 0, 1)
    n_tail = w_rows.shape[0] - seg_gate - n_gate
    n_main = seg_gate + n_tail
    gate_block = seg_gate // LANES
    n_head_steps = bn * nh
    tail_slab = -(-n_tail // (n_head_steps * LANES)) * LANES
    tail_skip = n_head_steps * tail_slab - n_tail
    tail_first = w_rows.shape[0] - n_head_steps * tail_slab
    gparams = jnp.zeros((SUBLANES, LANES), F32)
    gparams = gparams.at[0, :2 * nh].set(ev_a_log[0].reshape(-1)).at[1, :2 * nh].set(ev_dt_bias[0].reshape(-1))
    col_ka, col_va, col_kb, col_vb, col_qa, col_qb, col_za = [k * nh for k in range(7)]

    head_step = lambda b, h: b * nh + h
    pc, gates_c = proj(ctx, norm_mix[0], csc1, csh1, w_rows, 0, seg_gate, None, seg_gate, w_rows, gate_block,
                       gparams, rows_major=True, tn=1024)
    s_ctx, (w_head, w_tail) = gdn_context(
        pc, ev_conv[0], gate_rows(gates_c, ctx.shape[1]), col_ka, col_va,
        casts=[transposing_job(w_rows, 0, seg_gate, n_head_steps, head_step),
               transposing_job(w_rows, tail_first, n_head_steps * tail_slab, n_head_steps, head_step)])

    p, gates = proj(x, norm_mix[0], sc1, sh1, w_head[None], 0, seg_gate, w_tail, n_main, w_rows, gate_block,
                    gparams, tail_skip=tail_skip)
    cosf, sinf = rope_tables(t)
    y_gdn, (w_out0,) = gdn_latent(p, ev_conv[0], gate_rows(gates, t), cosf, sinf, s_ctx, ev_gdn_norm[0],
                                  col_ka, col_va, col_qa, col_za,
                                  casts=[row_slab_job(ev_w_out, 0, n_head_steps, head_step)])
    y_na, rounded = natten(p, pc, natten_bias(ev_rpb[0]), col_qb, col_kb, col_vb, col_kb, col_vb,
                           casts=[row_slab_job(w, 0, n_head_steps, head_step)
                                  for w in (ffn_w_gate, ffn_w_up, ffn_w_down, od_w_in, od_w_out)])
    ffn0, (w_in1, w_out1) = rounded[:3], rounded[3:]
    x_lat, hs = outproj(x, g1, [y_gdn, y_na], w_out0, norm_ffn[0], sc2, sh2)
    _, tf, grid = ffn_grid(x_lat, ffn_w_gate.shape[2])
    x_lat, ffn1 = ffn(x_lat, hs, g2, final_norm, *ffn0, final_norm=False,
                      casts=ffn_weight_jobs(ffn_w_gate, ffn_w_up, ffn_w_down, 1, grid, tf))

    sh1, sc1, g1, sh2, sc2, g2 = mod_vecs(1, slice(0, bn))
    p = proj(x_lat, norm_mix[1], sc1, sh1, w_in1[None], 0, 3 * d, None, 3 * d)
    x_lat, hs = outproj_conv(x_lat, g1, p, od_conv[0], w_out1, norm_ffn[1], sc2, sh2)
    x_lat, _ = ffn(x_lat, hs, g2, final_norm, *ffn1, final_norm=True)
    return x_lat
```

```python
import functools
import math

import jax
import jax.numpy as jnp
import numpy as np
from jax import lax
from jax.experimental import pallas as pl
from jax.experimental.pallas import tpu as pltpu

F32 = jnp.float32
BF16 = jnp.bfloat16
HIGHEST = lax.Precision.HIGHEST

LANES = 128
SUBLANES = 8
VMEM_LIMIT = 56 * 1024 * 1024

GRID_W = 64
HEAD_DIM = 128
GDN_HEADS = 8
NA_HEADS = 8
GDN_CONV = 5
CHUNK = 64
INVERSE_LEVELS = 6
GROUP = 2 * CHUNK
PREP_GROUPS_PER_ITER = 4
GDN_ROW_BLOCK = 512
NA_ROWS_PER_ITER = 8
NA_KH = 8
NA_KW = 16
SC_CONV = 3
ROPE_THETA = 10000.0
NORM_EPS = 1e-6
NEG_BIG = -1e30

NT_DIMS = (((1,), (1,)), ((), ()))


def _cparams(sem):
    return pltpu.CompilerParams(dimension_semantics=sem, vmem_limit_bytes=VMEM_LIMIT)


def _sigmoid(x):
    return 1.0 / (1.0 + jnp.exp(-x))


def _silu(x):
    return x * _sigmoid(x)


def _softplus(x):
    return jnp.maximum(x, 0.0) + jnp.log(1.0 + jnp.exp(-jnp.abs(x)))


def _mm(a, b):
    return jnp.dot(a.astype(BF16), b.astype(BF16), preferred_element_type=F32)


def _mm_nt(a, b):
    return lax.dot_general(a.astype(BF16), b.astype(BF16), NT_DIMS, preferred_element_type=F32)


def _mm_exact(a, b):
    return jnp.dot(a, b, precision=HIGHEST, preferred_element_type=F32)


class CastJob:
    def __init__(self, src, in_block, in_index, out_shape, out_block, out_index, transpose=False,
                 row_offset=0, next_rows=None):
        self.transpose = transpose
        self.row_offset = row_offset
        self.srcs = [src]
        self.in_specs = [pl.BlockSpec(in_block, in_index)]
        if row_offset:
            self.srcs.append(src)
            self.in_specs.append(pl.BlockSpec(*next_rows))
        self.out_spec = pl.BlockSpec(out_block, out_index)
        self.out_shape = jax.ShapeDtypeStruct(out_shape, BF16)


def _hosting_casts(body, n_in, n_out, casts):
    n_cast_in = sum(len(job.srcs) for job in casts)

    def kernel(*refs):
        ins, rest = refs[:n_in], refs[n_in:]
        cast_in, rest = list(rest[:n_cast_in]), rest[n_cast_in:]
        outs, rest = rest[:n_out], rest[n_out:]
        cast_out, scratch = rest[:len(casts)], rest[len(casts):]
        for job, dst in zip(casts, cast_out):
            src = cast_in.pop(0)
            block = src[...]
            if job.row_offset:
                block = jnp.concatenate([src[job.row_offset:, :], cast_in.pop(0)[...]], axis=0)
            dst[...] = (block.T if job.transpose else block).astype(BF16)
        body(*ins, *outs, *scratch)

    return kernel


def row_slab_job(w, layer, n_steps, step_of):
    rows, cols = w.shape[1:]
    slab = rows // n_steps
    assert slab * n_steps == rows and slab % (2 * SUBLANES) == 0
    return CastJob(w, (None, slab, cols), lambda *idx: (layer, step_of(*idx), 0),
                   (rows, cols), (slab, cols), lambda *idx: (step_of(*idx), 0))


def transposing_job(w_rows, first_row, n_rows, n_steps, step_of):
    d = w_rows.shape[1]
    slab = n_rows // n_steps
    assert slab * n_steps == n_rows and slab % LANES == 0
    first_block, offset = divmod(first_row, slab)
    assert offset % SUBLANES == 0
    next_rows = None
    if offset:
        assert slab % offset == 0
        per = slab // offset
        next_rows = ((offset, d), lambda *idx: ((first_block + step_of(*idx) + 1) * per, 0))
    return CastJob(w_rows, (slab, d), lambda *idx: (first_block + step_of(*idx), 0),
                   (d, n_rows), (d, slab), lambda *idx: (0, step_of(*idx)), transpose=True,
                   row_offset=offset, next_rows=next_rows)


def _call_hosting(body, casts, grid, in_specs, args, out_specs, out_shapes, **kwargs):
    n_in, n_out = len(in_specs), len(out_specs)
    casts = list(casts)
    res = pl.pallas_call(
        _hosting_casts(body, n_in, n_out, casts),
        grid=grid,
        in_specs=list(in_specs) + [s for c in casts for s in c.in_specs],
        out_specs=list(out_specs) + [c.out_spec for c in casts],
        out_shape=list(out_shapes) + [c.out_shape for c in casts],
        **kwargs,
    )(*args, *[s for c in casts for s in c.srcs])
    return res[:n_out], res[n_out:]


def _ada_kernel(cv_ref, w_ref, b_ref, o_ref):
    s = _silu(cv_ref[...])
    o_ref[...] = _mm(s, w_ref[...]) + b_ref[...]


def ada_modulation(cv, ada_w, ada_b, tn=1024):
    depth, d, n = ada_w.shape
    return pl.pallas_call(
        _ada_kernel,
        grid=(depth, n // tn),
        in_specs=[
            pl.BlockSpec((SUBLANES, d), lambda l, j: (0, 0)),
            pl.BlockSpec((None, d, tn), lambda l, j: (l, 0, j)),
            pl.BlockSpec((None, 1, tn), lambda l, j: (l, 0, j)),
        ],
        out_specs=pl.BlockSpec((None, SUBLANES, tn), lambda l, j: (l, 0, j)),
        out_shape=jax.ShapeDtypeStruct((depth, SUBLANES, n), F32),
        compiler_params=_cparams(("parallel", "parallel")),
        name="ada",
    )(cv, ada_w, ada_b.reshape(depth, 1, n))


def _modulated_norm(x, g, sc, sh):
    y = x * lax.rsqrt(jnp.mean(x * x, axis=-1, keepdims=True) + NORM_EPS)
    return (y * g) * (1.0 + sc) + sh


NORM_ROW_BLOCK = 128


def _modulated_norm_rows(x_ref, g_ref, sc_ref, sh_ref, hs_ref):
    rb = min(NORM_ROW_BLOCK, x_ref.shape[0])

    def body(i, _):
        rows = pl.ds(pl.multiple_of(i * rb, rb), rb)
        hs_ref[rows, :] = _modulated_norm(x_ref[rows, :], g_ref[...], sc_ref[...], sh_ref[...]).astype(BF16)
        return 0

    lax.fori_loop(0, x_ref.shape[0] // rb, body, 0)


def _proj_kernel(*refs, with_gates, with_tail, head_blocks, rows_major):
    x_ref, g_ref, sc_ref, sh_ref, w_ref = refs[:5]
    refs = refs[5:]
    wt_ref = None
    if with_tail:
        wt_ref, refs = refs[0], refs[1:]
    if with_gates:
        wab_ref, gp_ref, o_ref, gate_ref, hs_ref, wb_ref = refs
    else:
        o_ref, hs_ref, wb_ref = refs
    j = pl.program_id(2)

    @pl.when(j == 0)
    def _():
        if with_gates:
            hb = _modulated_norm(x_ref[...], g_ref[...], sc_ref[...], sh_ref[...]).astype(BF16)
            hs_ref[...] = hb
            a = _mm_nt(hb, wab_ref[...])
            neg_decay_rate = -jnp.exp(gp_ref[0:1, :])
            g = neg_decay_rate * _softplus(a + gp_ref[1:2, :])
            lane = lax.broadcasted_iota(jnp.int32, a.shape, 1)
            gate_ref[...] = jnp.where(lane < 2 * GDN_HEADS, g, _sigmoid(a))
        else:
            _modulated_norm_rows(x_ref, g_ref, sc_ref, sh_ref, hs_ref)

    def emit(wref):
        if wref.dtype != BF16:
            wb_ref[...] = wref[...].astype(BF16)
            wref = wb_ref
        if rows_major:
            y = lax.dot_general(hs_ref[...], wref[...], NT_DIMS, preferred_element_type=F32)
        else:
            y = jnp.dot(hs_ref[...], wref[...], preferred_element_type=F32)
        o_ref[...] = y.astype(o_ref.dtype)

    pl.when(j < head_blocks)(lambda: emit(w_ref))
    if with_tail:
        pl.when(j >= head_blocks)(lambda: emit(wt_ref))


def proj(x, g, sc, sh, w, layer, head_cols, w_tail, n_out, wab=None, gate_block=0, gparams=None,
         rows_major=False, tail_skip=0, tm=1024, tn=2048):
    bn, t, d = x.shape
    tm = min(tm, t)
    while n_out % tn or head_cols % tn or tail_skip % tn:
        tn //= 2
    head_blocks = head_cols // tn
    with_gates = wab is not None
    vec = pl.BlockSpec((None, 1, d), lambda b, i, j: (b, 0, 0))
    if rows_major:
        w_spec = pl.BlockSpec((tn, d), lambda b, i, j: (jnp.minimum(j, head_blocks - 1), 0))
    else:
        w_spec = pl.BlockSpec((None, d, tn), lambda b, i, j: (layer, 0, jnp.minimum(j, head_blocks - 1)))
    in_specs = [
        pl.BlockSpec((None, tm, d), lambda b, i, j: (b, i, 0)),
        pl.BlockSpec((1, d), lambda b, i, j: (0, 0)),
        vec, vec,
        w_spec,
    ]
    args = [x, g.reshape(1, d), sc, sh, w]
    with_tail = w_tail is not None
    if with_tail:
        skip = tail_skip // tn
        in_specs.append(pl.BlockSpec((d, tn), lambda b, i, j: (0, jnp.maximum(j - head_blocks, 0) + skip)))
        args.append(w_tail)
    out_specs = [pl.BlockSpec((None, tm, tn), lambda b, i, j: (b, i, j))]
    out_shape = [jax.ShapeDtypeStruct((bn, t, n_out), BF16)]
    if with_gates:
        in_specs += [pl.BlockSpec((LANES, d), lambda b, i, j: (gate_block, 0)),
                     pl.BlockSpec((SUBLANES, LANES), lambda b, i, j: (0, 0))]
        args += [wab, gparams]
        out_specs.append(pl.BlockSpec((None, tm, LANES), lambda b, i, j: (b, i, 0)))
        out_shape.append(jax.ShapeDtypeStruct((bn, t, LANES), F32))
    res = pl.pallas_call(
        functools.partial(_proj_kernel, with_gates=with_gates, with_tail=with_tail, head_blocks=head_blocks,
                          rows_major=rows_major),
        grid=(bn, t // tm, n_out // tn),
        in_specs=in_specs,
        out_specs=out_specs,
        out_shape=out_shape,
        scratch_shapes=[pltpu.VMEM((tm, d), BF16), pltpu.VMEM((tn, d) if rows_major else (d, tn), BF16)],
        compiler_params=_cparams(("parallel", "parallel", "arbitrary")),
        name="proj",
    )(*args)
    return res if with_gates else res[0]


CONV_HALO = SUBLANES


def _stage_padded(pad_ref, load_rows, t, rb):
    zeros = jnp.zeros((CONV_HALO, LANES), F32)
    pad_ref[0:CONV_HALO, :] = zeros
    pad_ref[CONV_HALO + t:2 * CONV_HALO + t, :] = zeros

    def body(i, _):
        r0 = pl.multiple_of(i * rb, rb)
        pad_ref[pl.ds(CONV_HALO + r0, rb), :] = load_rows(r0)
        return 0

    lax.fori_loop(0, t // rb, body, 0)


def _conv_rows(pad_ref, w_ref, r0, rb, taps):
    acc = None
    for j in range(taps):
        xj = pad_ref[pl.ds(r0 + (CONV_HALO + j - taps // 2), rb), :]
        term = xj * w_ref[j:j + 1, :]
        acc = term if acc is None else acc + term
    return acc


def _l2norm(x):
    return x * lax.rsqrt(jnp.sum(x * x, axis=-1, keepdims=True) + NORM_EPS)


def _gdn_kernel(*refs, t, with_q):
    nc = t // CHUNK
    rb = min(GDN_ROW_BLOCK, t)
    if with_q:
        (ka_ref, va_ref, qa_ref, za_ref, wk_ref, wv_ref, wq_ref, grow_ref, cos_ref, sin_ref, s0_ref,
         gn_ref, y_ref,
         pad_ref, pad2_ref, k_ref, v_ref, q_ref, o_ref, gc_ref, kwq_s, h_s, au_s, gl_s) = refs
    else:
        (ka_ref, va_ref, wk_ref, wv_ref, grow_ref, sfin_ref,
         pad_ref, k_ref, v_ref, gc_ref, kwq_s, h_s, gl_s) = refs

    def conv_all(src_ref, w_ref, finish, dst_ref):
        _stage_padded(pad_ref, lambda r0: src_ref[pl.ds(r0, rb), :].astype(F32), t, rb)

        def body(i, _):
            r0 = pl.multiple_of(i * rb, rb)
            y = _silu(_conv_rows(pad_ref, w_ref, r0, rb, GDN_CONV))
            dst_ref[pl.ds(r0, rb), :] = finish(y, r0)
            return 0

        lax.fori_loop(0, t // rb, body, 0)

    def rope(x, r0):
        return (x * cos_ref[pl.ds(r0, rb), :]
                + pltpu.roll(x, HEAD_DIM // 2, axis=1) * sin_ref[pl.ds(r0, rb), :])

    if with_q:
        _stage_padded(pad_ref, lambda r0: ka_ref[pl.ds(r0, rb), :].astype(F32), t, rb)
        _stage_padded(pad2_ref, lambda r0: qa_ref[pl.ds(r0, rb), :].astype(F32), t, rb)

        def kq_body(i, _):
            r0 = pl.multiple_of(i * rb, rb)
            yk = _silu(_conv_rows(pad_ref, wk_ref, r0, rb, GDN_CONV))
            yq = _silu(_conv_rows(pad2_ref, wq_ref, r0, rb, GDN_CONV))
            k_ref[pl.ds(r0, rb), :] = rope(_l2norm(yk), r0)
            q_ref[pl.ds(r0, rb), :] = rope(_l2norm(yq), r0) * HEAD_DIM ** -0.5
            return 0

        lax.fori_loop(0, t // rb, kq_body, 0)
    else:
        conv_all(ka_ref, wk_ref, lambda y, r0: _l2norm(y), k_ref)
    conv_all(va_ref, wv_ref, lambda y, r0: y, v_ref)

    ii = lax.broadcasted_iota(jnp.int32, (GROUP, GROUP), 0)
    jj = lax.broadcasted_iota(jnp.int32, (GROUP, GROUP), 1)
    same = (ii // CHUNK) == (jj // CHUNK)
    lower = same & (ii >= jj)
    upper = same & (ii <= jj)
    eye = ii == jj
    gc_ref[0] = _mm_exact(grow_ref[0], upper.astype(F32))
    gc_ref[1] = _mm_exact(grow_ref[1], lower.astype(F32))

    n_groups = t // GROUP
    gpi = min(PREP_GROUPS_PER_ITER, n_groups)
    n_sets = n_groups // gpi
    steps_per_set = 2 * gpi

    def prep_stages(i):
        chains = ([(i * gpi + k, 0) for k in range(gpi)]
                  + [(n_groups - 1 - (i * gpi + k), 1) for k in range(gpi)])
        ids = range(len(chains))
        st = {}

        def load():
            for ch, (g, d) in enumerate(chains):
                r0 = pl.multiple_of(g * GROUP, GROUP)
                st["k", ch] = k_ref[pl.ds(r0, GROUP), :]
                st["v", ch] = v_ref[pl.ds(r0, GROUP), :]
                kgb = st["k", ch].astype(BF16)
                if with_q:
                    st["q", ch] = q_ref[pl.ds(r0, GROUP), :]
                    both = _mm_nt(jnp.concatenate([st["k", ch], st["q", ch]], axis=0), kgb)
                    st["kk", ch], st["qk", ch] = both[:GROUP], both[GROUP:]
                else:
                    st["kk", ch] = _mm_nt(kgb, kgb)

        def masks():
            for ch, (g, d) in enumerate(chains):
                incl = lower if d == 0 else upper
                gc_r = jnp.broadcast_to(gc_ref[d, pl.ds(g, 1), :], (GROUP, GROUP))
                be_r = jnp.broadcast_to(grow_ref[2 + d, pl.ds(g, 1), :], (GROUP, GROUP))
                gc_c = gc_r.T
                decay = jnp.where(incl, jnp.exp(jnp.where(incl, gc_c - gc_r, 0.0)), 0.0)
                m = jnp.where(incl & jnp.logical_not(eye), st["kk", ch] * be_r.T * decay, 0.0)
                st["be_r", ch], st["gc_c", ch], st["decay", ch] = be_r, gc_c, decay
                st["p", ch] = jnp.where(eye, 1.0, -m)
                st["sq", ch] = _mm(m, m)

        def level():
            for ch in ids:
                p, sq = st["p", ch], st["sq", ch]
                st["p", ch] = p + _mm(p, sq)
                st["sq", ch] = _mm(sq, sq)

        def last_level():
            for ch in ids:
                st["p", ch] = st["p", ch] + _mm(st["p", ch], st["sq", ch])

        def solve():
            for ch in ids:
                rhs = jnp.concatenate([st["v", ch], st["k", ch] * jnp.exp(st["gc_c", ch])], axis=1)
                st["uw", ch] = _mm(st["p", ch] * st["be_r", ch], rhs)

        def fold():
            for ch, (g, d) in enumerate(chains):
                gc_c = st["gc_c", ch]
                last = [CHUNK - 1, GROUP - 1] if d == 0 else [0, CHUNK]
                gls = [gc_c[r:r + 1, :] for r in last]
                gl_c = jnp.concatenate([jnp.broadcast_to(x, (CHUNK, LANES)) for x in gls], axis=0)
                kdt = (st["k", ch] * jnp.exp(gl_c - gc_c)).T
                parts = [jnp.where(jj < CHUNK, kdt, 0.0), jnp.where(jj >= CHUNK, kdt, 0.0)]
                if with_q:
                    st["qd", ch] = st["q", ch] * jnp.exp(gc_c)
                    parts = [st["qk", ch] * st["decay", ch]] + parts
                st["gl", ch] = gls
                st["prod", ch] = _mm(jnp.concatenate(parts, axis=0), st["uw", ch])

        def store():
            for ch, (g, d) in enumerate(chains):
                r = st["prod", ch]
                off = GROUP if with_q else 0
                for half in range(2):
                    c = 2 * g + half
                    blk = r[off + half * HEAD_DIM:off + (half + 1) * HEAD_DIM]
                    h_s[d, c] = blk[:, :HEAD_DIM]
                    gl_s[d, c] = jnp.broadcast_to(jnp.exp(st["gl", ch][half]), (SUBLANES, LANES))
                    if with_q:
                        rs = slice(half * CHUNK, (half + 1) * CHUNK)
                        au_s[d, c] = r[rs, :HEAD_DIM]
                        qw = st["qd", ch][rs] - r[rs, HEAD_DIM:]
                        kwq_s[d, c] = jnp.concatenate([blk[:, HEAD_DIM:], qw], axis=0).astype(BF16)
                    else:
                        kwq_s[d, c] = blk[:, HEAD_DIM:].astype(BF16)

        return [load, masks] + [level] * (INVERSE_LEVELS - 2) + [last_level, solve, fold, store]

    def scan_step(s, carry):
        cs = (s, nc - 1 - s)
        rr = [jnp.dot(kwq_s[d, cs[d]], carry[d].astype(BF16), preferred_element_type=F32) for d in range(2)]
        new = []
        for d in range(2):
            c = cs[d]
            new.append(carry[d] * gl_s[d, c][0:1, :] + h_s[d, c] - rr[d][:HEAD_DIM])
            if with_q:
                r0 = pl.multiple_of(c * CHUNK, CHUNK)
                o_ref[pl.ds(r0, CHUNK), :] += rr[d][HEAD_DIM:] + au_s[d, c]
        return tuple(new)

    def run(stages, steps, carry):
        for idx in range(max(len(stages), len(steps))):
            if idx < len(stages):
                stages[idx]()
            if idx < len(steps):
                carry = scan_step(steps[idx], carry)
        return carry

    def set_steps(i):
        return [i * steps_per_set + k for k in range(steps_per_set)]

    if with_q:
        o_ref[...] = jnp.zeros_like(o_ref)
        carry = (s0_ref[0], s0_ref[1])
    else:
        zero = jnp.zeros((HEAD_DIM, HEAD_DIM), F32)
        carry = (zero, zero)
    run(prep_stages(0), [], carry)
    carry = lax.fori_loop(1, n_sets, lambda i, c: run(prep_stages(i), set_steps(i - 1), c), carry)
    s_f, s_b = run([], set_steps(n_sets - 1), carry)

    if not with_q:
        sfin_ref[0] = s_f
        sfin_ref[1] = s_b
        return

    def finish(i, _):
        r0 = pl.multiple_of(i * rb, rb)
        o = o_ref[pl.ds(r0, rb), :]
        y = o * lax.rsqrt(jnp.mean(o * o, axis=-1, keepdims=True) + NORM_EPS)
        y_ref[pl.ds(r0, rb), :] = ((y * gn_ref[...]) * _silu(za_ref[pl.ds(r0, rb), :].astype(F32))).astype(y_ref.dtype)
        return 0

    lax.fori_loop(0, t // rb, finish, 0)


def _single(block_shape, index_map):
    return pl.BlockSpec(block_shape, index_map, pipeline_mode=pl.Buffered(1))


def gdn_latent(p, conv_w, grow, cosf, sinf, s0, gnorm, col_k, col_v, col_q, col_z, casts=()):
    bn, t, _ = p.shape
    nc = t // CHUNK
    ngp = grow.shape[3]
    hd = HEAD_DIM
    seq = lambda col: pl.BlockSpec((None, t, hd), lambda b, h, col=col: (b, 0, col + h))
    cw = lambda col: pl.BlockSpec((GDN_CONV, hd), lambda b, h, col=col: (0, col + h))
    (y,), rounded = _call_hosting(
        lambda *refs: _gdn_kernel(*refs, t=t, with_q=True), casts,
        grid=(bn, GDN_HEADS),
        in_specs=[
            seq(col_k), seq(col_v), seq(col_q), seq(col_z),
            cw(0), cw(GDN_HEADS), cw(2 * GDN_HEADS),
            pl.BlockSpec((None, None, 4, ngp, GROUP), lambda b, h: (b, h, 0, 0, 0)),
            _single((t, hd), lambda b, h: (0, 0)),
            _single((t, hd), lambda b, h: (0, 0)),
            pl.BlockSpec((None, None, 2, hd, hd), lambda b, h: (b, h, 0, 0, 0)),
            pl.BlockSpec((1, hd), lambda b, h: (0, 0)),
        ],
        args=(p, p, p, p, conv_w, conv_w, conv_w, grow, cosf, sinf, s0, gnorm.reshape(1, hd)),
        out_specs=[pl.BlockSpec((None, t, hd), lambda b, h: (b, 0, h))],
        out_shapes=[jax.ShapeDtypeStruct((bn, t, GDN_HEADS * hd), BF16)],
        scratch_shapes=[
            pltpu.VMEM((t + 2 * CONV_HALO, hd), F32),
            pltpu.VMEM((t + 2 * CONV_HALO, hd), F32),
            pltpu.VMEM((t, hd), F32),
            pltpu.VMEM((t, hd), F32),
            pltpu.VMEM((t, hd), F32),
            pltpu.VMEM((t, hd), F32),
            pltpu.VMEM((2, ngp, GROUP), F32),
            pltpu.VMEM((2, nc, hd + CHUNK, hd), BF16),
            pltpu.VMEM((2, nc, hd, hd), F32),
            pltpu.VMEM((2, nc, CHUNK, hd), F32),
            pltpu.VMEM((2, nc, SUBLANES, LANES), F32),
        ],
        compiler_params=_cparams(("parallel", "parallel")),
        name="gdn_latent",
    )
    return y, rounded


def gdn_context(pc, conv_w, grow, col_k, col_v, casts=()):
    bn, t, _ = pc.shape
    nc = t // CHUNK
    ngp = grow.shape[3]
    hd = HEAD_DIM
    seq = lambda col: pl.BlockSpec((None, t, hd), lambda b, h, col=col: (b, 0, col + h))
    cw = lambda col: pl.BlockSpec((GDN_CONV, hd), lambda b, h, col=col: (0, col + h))
    (states,), rounded = _call_hosting(
        lambda *refs: _gdn_kernel(*refs, t=t, with_q=False), casts,
        grid=(bn, GDN_HEADS),
        in_specs=[
            seq(col_k), seq(col_v), cw(0), cw(GDN_HEADS),
            pl.BlockSpec((None, None, 4, ngp, GROUP), lambda b, h: (b, h, 0, 0, 0)),
        ],
        args=(pc, pc, conv_w, conv_w, grow),
        out_specs=[pl.BlockSpec((None, None, 2, hd, hd), lambda b, h: (b, h, 0, 0, 0))],
        out_shapes=[jax.ShapeDtypeStruct((bn, GDN_HEADS, 2, hd, hd), F32)],
        scratch_shapes=[
            pltpu.VMEM((t + 2 * CONV_HALO, hd), F32),
            pltpu.VMEM((t, hd), F32),
            pltpu.VMEM((t, hd), F32),
            pltpu.VMEM((2, ngp, GROUP), F32),
            pltpu.VMEM((2, nc, hd, hd), BF16),
            pltpu.VMEM((2, nc, hd, hd), F32),
            pltpu.VMEM((2, nc, SUBLANES, LANES), F32),
        ],
        compiler_params=_cparams(("parallel", "parallel")),
        name="gdn_context",
    )
    return states, rounded


def gate_rows(gates, t):
    bn = gates.shape[0]
    ng = t // GROUP
    g = gates[:, :, :4 * GDN_HEADS].reshape(bn, ng, GROUP, 4, GDN_HEADS)
    g = jnp.transpose(g, (0, 4, 3, 1, 2))
    if ng < SUBLANES:
        g = jnp.pad(g, ((0, 0), (0, 0), (0, 0), (0, SUBLANES - ng), (0, 0)))
    return g


def _natten_kernel(q_ref, k_ref, v_ref, kc_ref, vc_ref, by_dr_ref, o_ref, bias_ref, *, rows):
    kcb = kc_ref[...]
    vcb = vc_ref[...]
    scale = HEAD_DIM ** -0.5
    kh = min(NA_KH, rows)
    win = kh * GRID_W
    per_iter = min(NA_ROWS_PER_ITER, rows)

    for cls in range(NA_KH):
        for j in range(kh):
            bias_ref[cls, :, j * GRID_W:(j + 1) * GRID_W] = by_dr_ref[NA_KH - 1 - cls + j]

    def body(it, _):
        rws = [it * per_iter + k for k in range(per_iter)]
        q0s, k0s, s_loc, s_ctx = [], [], [], []
        for r in rws:
            rs = jnp.clip(r - NA_KH // 2, 0, rows - NA_KH)
            q0s.append(pl.multiple_of(r * GRID_W, GRID_W))
            k0s.append(pl.multiple_of(rs * GRID_W, GRID_W))
            q = q_ref[pl.ds(q0s[-1], GRID_W), :]
            kw = k_ref[pl.ds(k0s[-1], win), :]
            s_loc.append(lax.dot_general(q, kw, NT_DIMS, preferred_element_type=F32) * scale + bias_ref[r - rs])
            s_ctx.append(lax.dot_general(q, kcb, NT_DIMS, preferred_element_type=F32) * scale)
        p_loc, p_ctx, den = [], [], []
        for a, b in zip(s_loc, s_ctx):
            m = jnp.maximum(jnp.max(a, axis=-1, keepdims=True), jnp.max(b, axis=-1, keepdims=True))
            p_loc.append(jnp.exp(a - m))
            p_ctx.append(jnp.exp(b - m))
            den.append(jnp.sum(p_loc[-1], axis=-1, keepdims=True) + jnp.sum(p_ctx[-1], axis=-1, keepdims=True))
        outs = []
        for k in range(per_iter):
            vw = v_ref[pl.ds(k0s[k], win), :]
            outs.append(jnp.dot(p_loc[k].astype(BF16), vw, preferred_element_type=F32)
                        + jnp.dot(p_ctx[k].astype(BF16), vcb, preferred_element_type=F32))
        for k in range(per_iter):
            o_ref[pl.ds(q0s[k], GRID_W), :] = (outs[k] / den[k]).astype(o_ref.dtype)
        return 0

    lax.fori_loop(0, rows // per_iter, body, 0)


def natten(p, pc, by_dr, col_q, col_k, col_v, ctx_col_k, ctx_col_v, casts=()):
    bn, t, _ = p.shape
    lc = pc.shape[1]
    hd = HEAD_DIM
    rows = t // GRID_W
    kh = min(NA_KH, rows)
    seq = lambda col: pl.BlockSpec((None, t, hd), lambda b, h, col=col: (b, 0, col + h))
    cseq = lambda col: pl.BlockSpec((None, lc, hd), lambda b, h, col=col: (b, 0, col + h))
    (y,), rounded = _call_hosting(
        lambda *refs: _natten_kernel(*refs, rows=rows), casts,
        grid=(bn, NA_HEADS),
        in_specs=[
            seq(col_q), seq(col_k), seq(col_v), cseq(ctx_col_k), cseq(ctx_col_v),
            pl.BlockSpec((None,) + by_dr.shape[1:], lambda b, h: (h, 0, 0, 0)),
        ],
        args=(p, p, p, pc, pc, by_dr),
        out_specs=[pl.BlockSpec((None, t, hd), lambda b, h: (b, 0, h))],
        out_shapes=[jax.ShapeDtypeStruct((bn, t, NA_HEADS * hd), BF16)],
        scratch_shapes=[pltpu.VMEM((NA_KH, GRID_W, kh * GRID_W), F32)],
        compiler_params=_cparams(("parallel", "parallel")),
        name="natten",
    )
    return y, rounded


def natten_bias(rpb):
    hn, n_dr, n_dc = rpb.shape
    col = np.arange(GRID_W)
    col_start = np.clip(col - NA_KW // 2, 0, GRID_W - NA_KW)
    in_win = (col[None, :] >= col_start[:, None]) & (col[None, :] < col_start[:, None] + NA_KW)
    dc = np.clip(col[None, :] - col[:, None], -(NA_KW - 1), NA_KW - 1) + (NA_KW - 1)
    onehot = (np.arange(n_dc)[:, None] == dc.reshape(-1)[None, :]).astype(np.float32)
    by_dr = jnp.dot(rpb.astype(F32).reshape(hn * n_dr, n_dc), jnp.asarray(onehot), precision=HIGHEST)
    return jnp.where(in_win[None, None], by_dr.reshape(hn, n_dr, GRID_W, GRID_W), NEG_BIG)


def _outproj_kernel(*refs, n_in):
    x_ref, gate_ref, g_ref, sc_ref, sh_ref = refs[:5]
    a_refs = refs[5:5 + n_in]
    w_refs = refs[5 + n_in:5 + 2 * n_in]
    o_ref, hs_ref = refs[5 + 2 * n_in:]
    y = None
    for a_ref, w_ref in zip(a_refs, w_refs):
        part = jnp.dot(a_ref[...], w_ref[...], preferred_element_type=F32)
        y = part if y is None else y + part
    x1 = x_ref[...] + gate_ref[...] * y
    o_ref[...] = x1
    hs_ref[...] = _modulated_norm(x1, g_ref[...], sc_ref[...], sh_ref[...]).astype(BF16)


def outproj(x, gate, acts, weight, g_next, sc_next, sh_next, tm=512):
    bn, t, d = x.shape
    n_in = len(acts)
    kdim = acts[0].shape[-1]
    assert all(a.shape[-1] == kdim for a in acts) and weight.shape[0] == n_in * kdim
    weights = [weight] * n_in
    row = pl.BlockSpec((None, tm, d), lambda b, i: (b, i, 0))
    vec = pl.BlockSpec((None, 1, d), lambda b, i: (b, 0, 0))
    in_specs = [row, vec, pl.BlockSpec((1, d), lambda b, i: (0, 0)), vec, vec]
    in_specs += [pl.BlockSpec((None, tm, kdim), lambda b, i: (b, i, 0)) for a in acts]
    in_specs += [pl.BlockSpec((kdim, d), lambda b, i, k=k: (k, 0)) for k in range(n_in)]
    return pl.pallas_call(
        functools.partial(_outproj_kernel, n_in=n_in),
        grid=(bn, t // tm),
        in_specs=in_specs,
        out_specs=[row, row],
        out_shape=[jax.ShapeDtypeStruct((bn, t, d), F32), jax.ShapeDtypeStruct((bn, t, d), BF16)],
        compiler_params=_cparams(("parallel", "parallel")),
        name="outproj",
    )(x, gate, g_next.reshape(1, d), sc_next, sh_next, *acts, *weights)


def _outproj_conv_kernel(x_ref, gate_ref, g_ref, sc_ref, sh_ref, gb_ref, gc_ref, val_ref,
                         gcp_ref, valp_ref, gcn_ref, valn_ref, cw_ref, w_ref, o_ref, hs_ref):
    i = pl.program_id(1)
    tm = x_ref.shape[0]
    u = gc_ref[...].astype(F32) * val_ref[...].astype(F32)
    last = gcp_ref.shape[0] - 1
    u_prev = gcp_ref[last:last + 1, :].astype(F32) * valp_ref[last:last + 1, :].astype(F32)
    u_next = gcn_ref[0:1, :].astype(F32) * valn_ref[0:1, :].astype(F32)
    u_prev = jnp.where(i == 0, 0.0, u_prev)
    u_next = jnp.where(i == pl.num_programs(1) - 1, 0.0, u_next)
    row = lax.broadcasted_iota(jnp.int32, (tm, 1), 0)
    below = jnp.where(row == 0, u_prev, pltpu.roll(u, 1, axis=0))
    above = jnp.where(row == tm - 1, u_next, pltpu.roll(u, tm - 1, axis=0))
    conv = below * cw_ref[0:1, :] + u * cw_ref[1:2, :] + above * cw_ref[2:3, :]
    z = (gb_ref[...].astype(F32) * conv).astype(BF16)
    x1 = x_ref[...] + gate_ref[...] * jnp.dot(z, w_ref[...], preferred_element_type=F32)
    o_ref[...] = x1
    hs_ref[...] = _modulated_norm(x1, g_ref[...], sc_ref[...], sh_ref[...]).astype(BF16)


def outproj_conv(x, gate, p, conv_w, weight, g_next, sc_next, sh_next, tm=512):
    bn, t, d = x.shape
    halo = 2 * SUBLANES
    nb = t // halo
    row = pl.BlockSpec((None, tm, d), lambda b, i: (b, i, 0))
    vec = pl.BlockSpec((None, 1, d), lambda b, i: (b, 0, 0))
    seg = lambda k: pl.BlockSpec((None, tm, d), lambda b, i, k=k: (b, i, k))
    prev = lambda k: pl.BlockSpec((None, halo, d), lambda b, i, k=k: (b, jnp.maximum(i * (tm // halo) - 1, 0), k))
    nxt = lambda k: pl.BlockSpec((None, halo, d), lambda b, i, k=k: (b, jnp.minimum((i + 1) * (tm // halo), nb - 1), k))
    return pl.pallas_call(
        _outproj_conv_kernel,
        grid=(bn, t // tm),
        in_specs=[row, vec, pl.BlockSpec((1, d), lambda b, i: (0, 0)), vec, vec,
                  seg(0), seg(1), seg(2), prev(1), prev(2), nxt(1), nxt(2),
                  pl.BlockSpec((SC_CONV, d), lambda b, i: (0, 0)),
                  pl.BlockSpec((d, d), lambda b, i: (0, 0))],
        out_specs=[row, row],
        out_shape=[jax.ShapeDtypeStruct((bn, t, d), F32), jax.ShapeDtypeStruct((bn, t, d), BF16)],
        compiler_params=_cparams(("parallel", "parallel")),
        name="outproj_conv",
    )(x, gate, g_next.reshape(1, d), sc_next, sh_next, p, p, p, p, p, p, p, conv_w, weight)


def _ffn_kernel(x_ref, hs_ref, gate_ref, fn_ref, wg_ref, wu_ref, wd_ref, o_ref, *, final_norm):
    j = pl.program_id(2)

    def step(first):
        h = hs_ref[...]
        gate = jnp.dot(h, wg_ref[...], preferred_element_type=F32)
        up = jnp.dot(h, wu_ref[...], preferred_element_type=F32)
        a = (_silu(gate) * up).astype(BF16)
        part = jnp.dot(a, wd_ref[...], preferred_element_type=F32)
        if first:
            o_ref[...] = part
        else:
            o_ref[...] += part

    pl.when(j == 0)(lambda: step(True))
    pl.when(j > 0)(lambda: step(False))

    @pl.when(j == pl.num_programs(2) - 1)
    def _():
        rb = min(NORM_ROW_BLOCK, o_ref.shape[0])

        def body(i, _):
            rows = pl.ds(pl.multiple_of(i * rb, rb), rb)
            y = x_ref[rows, :] + gate_ref[...] * o_ref[rows, :]
            if final_norm:
                y = (y * lax.rsqrt(jnp.mean(y * y, axis=-1, keepdims=True) + NORM_EPS)) * fn_ref[...]
            o_ref[rows, :] = y
            return 0

        lax.fori_loop(0, o_ref.shape[0] // rb, body, 0)


def ffn_grid(x, f, tm=1024, tf=512):
    bn, t, _ = x.shape
    tm = min(tm, t)
    return tm, tf, (bn, t // tm, f // tf)


def ffn_weight_jobs(w_gate, w_up, w_down, layer, grid, tf):
    bn, ni, nj = grid
    d = w_gate.shape[1]
    slab = d // (bn * ni)
    assert slab * bn * ni == d and slab % LANES == 0
    tile = lambda b, i: b * ni + i
    up_job = lambda w: CastJob(w, (None, slab, tf), lambda b, i, j: (layer, tile(b, i), j),
                               w.shape[1:], (slab, tf), lambda b, i, j: (tile(b, i), j))
    down_job = CastJob(w_down, (None, tf, slab), lambda b, i, j: (layer, j, tile(b, i)),
                       w_down.shape[1:], (tf, slab), lambda b, i, j: (j, tile(b, i)))
    return [up_job(w_gate), up_job(w_up), down_job]


def ffn(x, hs, gate, fnorm, w_gate, w_up, w_down, final_norm, casts=()):
    bn, t, d = x.shape
    f = w_gate.shape[1]
    tm, tf, grid = ffn_grid(x, f)
    vec = pl.BlockSpec((None, 1, d), lambda b, i, j: (b, 0, 0))
    one = pl.BlockSpec((1, d), lambda b, i, j: (0, 0))
    row_index = lambda b, i, j: (b, i, 0)
    if casts:
        x_spec, hs_spec = _single((None, tm, d), row_index), pl.BlockSpec((None, tm, d), row_index)
    else:
        x_spec, hs_spec = pl.BlockSpec((None, tm, d), row_index), _single((None, tm, d), row_index)
    (y,), rounded = _call_hosting(
        functools.partial(_ffn_kernel, final_norm=final_norm), casts,
        grid=grid,
        in_specs=[
            x_spec, hs_spec,
            vec, one,
            pl.BlockSpec((d, tf), lambda b, i, j: (0, j)),
            pl.BlockSpec((d, tf), lambda b, i, j: (0, j)),
            pl.BlockSpec((tf, d), lambda b, i, j: (j, 0)),
        ],
        args=(x, hs, gate, fnorm.reshape(1, d), w_gate, w_up, w_down),
        out_specs=[pl.BlockSpec((None, tm, d), lambda b, i, j: (b, i, 0))],
        out_shapes=[jax.ShapeDtypeStruct((bn, t, d), F32)],
        compiler_params=_cparams(("parallel", "parallel", "arbitrary")),
        name="ffn",
    )
    return y, rounded


def rope_tables(t):
    pos = np.arange(t)
    row = (pos // GRID_W).astype(np.float32)
    col = (pos % GRID_W).astype(np.float32)
    n_freq = HEAD_DIM // 4
    inv_freq = jnp.asarray(ROPE_THETA, F32) ** (-jnp.arange(n_freq, dtype=F32) / n_freq)
    ang = jnp.concatenate([jnp.asarray(row)[:, None] * inv_freq, jnp.asarray(col)[:, None] * inv_freq], axis=-1)
    cos, sin = jnp.cos(ang), jnp.sin(ang)
    return jnp.concatenate([cos, cos], axis=-1), jnp.concatenate([-sin, sin], axis=-1)


def kernel(x, c, ctx, c_ctx, ada_w, ada_b, norm_mix, norm_ffn, ffn_w_gate, ffn_w_up, ffn_w_down, final_norm,
           ev_w_in, ev_conv, ev_a_log, ev_dt_bias, ev_gdn_norm, ev_rpb, ev_w_out, od_w_in, od_conv, od_w_out):
    bn, t, d = x.shape
    depth = ada_w.shape[0]
    assert depth == 2
    gw = GDN_HEADS * HEAD_DIM
    nw = NA_HEADS * HEAD_DIM
    nh = GDN_HEADS

    cv = jnp.zeros((SUBLANES, d), F32).at[:bn].set(c).at[bn].set(c_ctx)
    mods = ada_modulation(cv, ada_w, ada_b)

    def mod_vecs(l, rows):
        m = mods[l, rows].reshape(-1, 6, d)
        return [m[:, k][:, None, :] for k in range(6)]

    sh1, sc1, g1, sh2, sc2, g2 = mod_vecs(0, slice(0, bn))
    csh1, csc1 = [jnp.broadcast_to(v, (bn, 1, d)) for v in mod_vecs(0, slice(bn, bn + 1))[:2]]
    n_gate = 4 * nh
    seg_gate = 2 * gw + 2 * nw
    w_rows = jnp.swapaxes(ev_w_in[0], 0, 1)
    n_tail = w_rows.shape[0] - seg_gate - n_gate
    n_main = seg_gate + n_tail
    gate_block = seg_gate // LANES
    n_head_steps = bn * nh
    tail_slab = -(-n_tail // (n_head_steps * LANES)) * LANES
    tail_skip = n_head_steps * tail_slab - n_tail
    tail_first = w_rows.shape[0] - n_head_steps * tail_slab
    gparams = jnp.zeros((SUBLANES, LANES), F32)
    gparams = gparams.at[0, :2 * nh].set(ev_a_log[0].reshape(-1)).at[1, :2 * nh].set(ev_dt_bias[0].reshape(-1))
    col_ka, col_va, col_kb, col_vb, col_qa, col_qb, col_za = [k * nh for k in range(7)]

    head_step = lambda b, h: b * nh + h
    pc, gates_c = proj(ctx, norm_mix[0], csc1, csh1, w_rows, 0, seg_gate, None, seg_gate, w_rows, gate_block,
                       gparams, rows_major=True, tn=1024)
    s_ctx, (w_head, w_tail) = gdn_context(
        pc, ev_conv[0], gate_rows(gates_c, ctx.shape[1]), col_ka, col_va,
        casts=[transposing_job(w_rows, 0, seg_gate, n_head_steps, head_step),
               transposing_job(w_rows, tail_first, n_head_steps * tail_slab, n_head_steps, head_step)])

    p, gates = proj(x, norm_mix[0], sc1, sh1, w_head[None], 0, seg_gate, w_tail, n_main, w_rows, gate_block,
                    gparams, tail_skip=tail_skip)
    cosf, sinf = rope_tables(t)
    y_gdn, (w_out0,) = gdn_latent(p, ev_conv[0], gate_rows(gates, t), cosf, sinf, s_ctx, ev_gdn_norm[0],
                                  col_ka, col_va, col_qa, col_za,
                                  casts=[row_slab_job(ev_w_out, 0, n_head_steps, head_step)])
    y_na, rounded = natten(p, pc, natten_bias(ev_rpb[0]), col_qb, col_kb, col_vb, col_kb, col_vb,
                           casts=[row_slab_job(w, 0, n_head_steps, head_step)
                                  for w in (ffn_w_gate, ffn_w_up, ffn_w_down, od_w_in, od_w_out)])
    ffn0, (w_in1, w_out1) = rounded[:3], rounded[3:]
    x_lat, hs = outproj(x, g1, [y_gdn, y_na], w_out0, norm_ffn[0], sc2, sh2)
    _, tf, grid = ffn_grid(x_lat, ffn_w_gate.shape[2])
    x_lat, ffn1 = ffn(x_lat, hs, g2, final_norm, *ffn0, final_norm=False,
                      casts=ffn_weight_jobs(ffn_w_gate, ffn_w_up, ffn_w_down, 1, grid, tf))

    sh1, sc1, g1, sh2, sc2, g2 = mod_vecs(1, slice(0, bn))
    p = proj(x_lat, norm_mix[1], sc1, sh1, w_in1[None], 0, 3 * d, None, 3 * d)
    x_lat, hs = outproj_conv(x_lat, g1, p, od_conv[0], w_out1, norm_ffn[1], sc2, sh2)
    x_lat, _ = ffn(x_lat, hs, g2, final_norm, *ffn1, final_norm=True)
    return x_lat
```

```python
import functools
import math

import jax
import jax.numpy as jnp
import numpy as np
from jax import lax
from jax.experimental import pallas as pl
from jax.experimental.pallas import tpu as pltpu

F32 = jnp.float32
BF16 = jnp.bfloat16
HIGHEST = lax.Precision.HIGHEST

LANES = 128
SUBLANES = 8
VMEM_LIMIT = 60 * 1024 * 1024

GRID_W = 64
HEAD_DIM = 128
GDN_HEADS = 8
NA_HEADS = 8
GDN_CONV = 5
CHUNK = 64
INVERSE_LEVELS = 6
GROUP = 2 * CHUNK
PREP_GROUPS_PER_ITER = 4
GDN_ROW_BLOCK = 512
NA_ROWS_PER_ITER = 8
NA_KH = 8
NA_KW = 16
SC_CONV = 3
ROPE_THETA = 10000.0
NORM_EPS = 1e-6
NEG_BIG = -1e30

NT_DIMS = (((1,), (1,)), ((), ()))


def _cparams(sem):
    return pltpu.CompilerParams(dimension_semantics=sem, vmem_limit_bytes=VMEM_LIMIT)


def _sigmoid(x):
    return 1.0 / (1.0 + jnp.exp(-x))


def _silu(x):
    return x * _sigmoid(x)


def _softplus(x):
    return jnp.maximum(x, 0.0) + jnp.log(1.0 + jnp.exp(-jnp.abs(x)))


def _mm(a, b):
    return jnp.dot(a.astype(BF16), b.astype(BF16), preferred_element_type=F32)


def _mm_nt(a, b):
    return lax.dot_general(a.astype(BF16), b.astype(BF16), NT_DIMS, preferred_element_type=F32)


def _mm_exact(a, b):
    return jnp.dot(a, b, precision=HIGHEST, preferred_element_type=F32)


class CastJob:
    def __init__(self, src, in_block, in_index, out_shape, out_block, out_index, transpose=False,
                 row_offset=0, next_rows=None):
        self.transpose = transpose
        self.row_offset = row_offset
        self.srcs = [src]
        self.in_specs = [pl.BlockSpec(in_block, in_index)]
        if row_offset:
            self.srcs.append(src)
            self.in_specs.append(pl.BlockSpec(*next_rows))
        self.out_spec = pl.BlockSpec(out_block, out_index)
        self.out_shape = jax.ShapeDtypeStruct(out_shape, BF16)


def _hosting_casts(body, n_in, n_out, casts):
    n_cast_in = sum(len(job.srcs) for job in casts)

    def kernel(*refs):
        ins, rest = refs[:n_in], refs[n_in:]
        cast_in, rest = list(rest[:n_cast_in]), rest[n_cast_in:]
        outs, rest = rest[:n_out], rest[n_out:]
        cast_out, scratch = rest[:len(casts)], rest[len(casts):]
        for job, dst in zip(casts, cast_out):
            src = cast_in.pop(0)
            block = src[...]
            if job.row_offset:
                block = jnp.concatenate([src[job.row_offset:, :], cast_in.pop(0)[...]], axis=0)
            dst[...] = (block.T if job.transpose else block).astype(BF16)
        body(*ins, *outs, *scratch)

    return kernel


def row_slab_job(w, layer, n_steps, step_of):
    rows, cols = w.shape[1:]
    slab = rows // n_steps
    assert slab * n_steps == rows and slab % (2 * SUBLANES) == 0
    return CastJob(w, (None, slab, cols), lambda *idx: (layer, step_of(*idx), 0),
                   (rows, cols), (slab, cols), lambda *idx: (step_of(*idx), 0))


def transposing_job(w_rows, first_row, n_rows, n_steps, step_of):
    d = w_rows.shape[1]
    slab = n_rows // n_steps
    assert slab * n_steps == n_rows and slab % LANES == 0
    first_block, offset = divmod(first_row, slab)
    assert offset % SUBLANES == 0
    next_rows = None
    if offset:
        assert slab % offset == 0
        per = slab // offset
        next_rows = ((offset, d), lambda *idx: ((first_block + step_of(*idx) + 1) * per, 0))
    return CastJob(w_rows, (slab, d), lambda *idx: (first_block + step_of(*idx), 0),
                   (d, n_rows), (d, slab), lambda *idx: (0, step_of(*idx)), transpose=True,
                   row_offset=offset, next_rows=next_rows)


def _call_hosting(body, casts, grid, in_specs, args, out_specs, out_shapes, **kwargs):
    n_in, n_out = len(in_specs), len(out_specs)
    casts = list(casts)
    res = pl.pallas_call(
        _hosting_casts(body, n_in, n_out, casts),
        grid=grid,
        in_specs=list(in_specs) + [s for c in casts for s in c.in_specs],
        out_specs=list(out_specs) + [c.out_spec for c in casts],
        out_shape=list(out_shapes) + [c.out_shape for c in casts],
        **kwargs,
    )(*args, *[s for c in casts for s in c.srcs])
    return res[:n_out], res[n_out:]


def _ada_kernel(cv_ref, w_ref, b_ref, o_ref):
    s = _silu(cv_ref[...])
    o_ref[...] = _mm(s, w_ref[...]) + b_ref[...]


def ada_modulation(cv, ada_w, ada_b, tn=1024):
    depth, d, n = ada_w.shape
    return pl.pallas_call(
        _ada_kernel,
        grid=(depth, n // tn),
        in_specs=[
            pl.BlockSpec((SUBLANES, d), lambda l, j: (0, 0)),
            pl.BlockSpec((None, d, tn), lambda l, j: (l, 0, j)),
            pl.BlockSpec((None, 1, tn), lambda l, j: (l, 0, j)),
        ],
        out_specs=pl.BlockSpec((None, SUBLANES, tn), lambda l, j: (l, 0, j)),
        out_shape=jax.ShapeDtypeStruct((depth, SUBLANES, n), F32),
        compiler_params=_cparams(("parallel", "parallel")),
        name="ada",
    )(cv, ada_w, ada_b.reshape(depth, 1, n))


def _modulated_norm(x, g, sc, sh):
    y = x * lax.rsqrt(jnp.mean(x * x, axis=-1, keepdims=True) + NORM_EPS)
    return (y * g) * (1.0 + sc) + sh


NORM_ROW_BLOCK = 128


def _modulated_norm_rows(x_ref, g_ref, sc_ref, sh_ref, hs_ref):
    rb = min(NORM_ROW_BLOCK, x_ref.shape[0])

    def body(i, _):
        rows = pl.ds(pl.multiple_of(i * rb, rb), rb)
        hs_ref[rows, :] = _modulated_norm(x_ref[rows, :], g_ref[...], sc_ref[...], sh_ref[...]).astype(BF16)
        return 0

    lax.fori_loop(0, x_ref.shape[0] // rb, body, 0)


def _proj_kernel(*refs, with_gates, with_tail, head_blocks, rows_major):
    x_ref, g_ref, sc_ref, sh_ref, w_ref = refs[:5]
    refs = refs[5:]
    wt_ref = None
    if with_tail:
        wt_ref, refs = refs[0], refs[1:]
    if with_gates:
        wab_ref, gp_ref, o_ref, gate_ref, hs_ref, wb_ref = refs
    else:
        o_ref, hs_ref, wb_ref = refs
    j = pl.program_id(2)

    @pl.when(j == 0)
    def _():
        if with_gates:
            hb = _modulated_norm(x_ref[...], g_ref[...], sc_ref[...], sh_ref[...]).astype(BF16)
            hs_ref[...] = hb
            a = _mm_nt(hb, wab_ref[...])
            neg_decay_rate = -jnp.exp(gp_ref[0:1, :])
            g = neg_decay_rate * _softplus(a + gp_ref[1:2, :])
            lane = lax.broadcasted_iota(jnp.int32, a.shape, 1)
            gate_ref[...] = jnp.where(lane < 2 * GDN_HEADS, g, _sigmoid(a))
        else:
            _modulated_norm_rows(x_ref, g_ref, sc_ref, sh_ref, hs_ref)

    def emit(wref):
        if wref.dtype != BF16:
            wb_ref[...] = wref[...].astype(BF16)
            wref = wb_ref
        if rows_major:
            y = lax.dot_general(hs_ref[...], wref[...], NT_DIMS, preferred_element_type=F32)
        else:
            y = jnp.dot(hs_ref[...], wref[...], preferred_element_type=F32)
        o_ref[...] = y.astype(o_ref.dtype)

    pl.when(j < head_blocks)(lambda: emit(w_ref))
    if with_tail:
        pl.when(j >= head_blocks)(lambda: emit(wt_ref))


def proj(x, g, sc, sh, w, layer, head_cols, w_tail, n_out, wab=None, gate_block=0, gparams=None,
         rows_major=False, tail_skip=0, tm=1024, tn=2048):
    bn, t, d = x.shape
    tm = min(tm, t)
    while n_out % tn or head_cols % tn or tail_skip % tn:
        tn //= 2
    head_blocks = head_cols // tn
    with_gates = wab is not None
    vec = pl.BlockSpec((None, 1, d), lambda b, i, j: (b, 0, 0))
    if rows_major:
        w_spec = pl.BlockSpec((tn, d), lambda b, i, j: (jnp.minimum(j, head_blocks - 1), 0))
    else:
        w_spec = pl.BlockSpec((None, d, tn), lambda b, i, j: (layer, 0, jnp.minimum(j, head_blocks - 1)))
    in_specs = [
        pl.BlockSpec((None, tm, d), lambda b, i, j: (b, i, 0)),
        pl.BlockSpec((1, d), lambda b, i, j: (0, 0)),
        vec, vec,
        w_spec,
    ]
    args = [x, g.reshape(1, d), sc, sh, w]
    with_tail = w_tail is not None
    if with_tail:
        skip = tail_skip // tn
        in_specs.append(pl.BlockSpec((d, tn), lambda b, i, j: (0, jnp.maximum(j - head_blocks, 0) + skip)))
        args.append(w_tail)
    out_specs = [pl.BlockSpec((None, tm, tn), lambda b, i, j: (b, i, j))]
    out_shape = [jax.ShapeDtypeStruct((bn, t, n_out), BF16)]
    if with_gates:
        in_specs += [pl.BlockSpec((LANES, d), lambda b, i, j: (gate_block, 0)),
                     pl.BlockSpec((SUBLANES, LANES), lambda b, i, j: (0, 0))]
        args += [wab, gparams]
        out_specs.append(pl.BlockSpec((None, tm, LANES), lambda b, i, j: (b, i, 0)))
        out_shape.append(jax.ShapeDtypeStruct((bn, t, LANES), F32))
    res = pl.pallas_call(
        functools.partial(_proj_kernel, with_gates=with_gates, with_tail=with_tail, head_blocks=head_blocks,
                          rows_major=rows_major),
        grid=(bn, t // tm, n_out // tn),
        in_specs=in_specs,
        out_specs=out_specs,
        out_shape=out_shape,
        scratch_shapes=[pltpu.VMEM((tm, d), BF16), pltpu.VMEM((tn, d) if rows_major else (d, tn), BF16)],
        compiler_params=_cparams(("parallel", "parallel", "arbitrary")),
        name="proj",
    )(*args)
    return res if with_gates else res[0]


CONV_HALO = SUBLANES


def _stage_padded(pad_ref, load_rows, t, rb):
    zeros = jnp.zeros((CONV_HALO, LANES), F32)
    pad_ref[0:CONV_HALO, :] = zeros
    pad_ref[CONV_HALO + t:2 * CONV_HALO + t, :] = zeros

    def body(i, _):
        r0 = pl.multiple_of(i * rb, rb)
        pad_ref[pl.ds(CONV_HALO + r0, rb), :] = load_rows(r0)
        return 0

    lax.fori_loop(0, t // rb, body, 0)


def _conv_rows(pad_ref, w_ref, r0, rb, taps):
    acc = None
    for j in range(taps):
        xj = pad_ref[pl.ds(r0 + (CONV_HALO + j - taps // 2), rb), :]
        term = xj * w_ref[j:j + 1, :]
        acc = term if acc is None else acc + term
    return acc


def _l2norm(x):
    return x * lax.rsqrt(jnp.sum(x * x, axis=-1, keepdims=True) + NORM_EPS)


def _gdn_kernel(*refs, t, with_q):
    nc = t // CHUNK
    rb = min(GDN_ROW_BLOCK, t)
    if with_q:
        (ka_ref, va_ref, qa_ref, za_ref, wk_ref, wv_ref, wq_ref, grow_ref, cos_ref, sin_ref, s0_ref,
         gn_ref, y_ref,
         pad_ref, pad2_ref, k_ref, v_ref, q_ref, o_ref, gc_ref, kwq_s, h_s, au_s, gl_s) = refs
    else:
        (ka_ref, va_ref, wk_ref, wv_ref, grow_ref, sfin_ref,
         pad_ref, k_ref, v_ref, gc_ref, kwq_s, h_s, gl_s) = refs

    def conv_all(src_ref, w_ref, finish, dst_ref):
        _stage_padded(pad_ref, lambda r0: src_ref[pl.ds(r0, rb), :].astype(F32), t, rb)

        def body(i, _):
            r0 = pl.multiple_of(i * rb, rb)
            y = _silu(_conv_rows(pad_ref, w_ref, r0, rb, GDN_CONV))
            dst_ref[pl.ds(r0, rb), :] = finish(y, r0)
            return 0

        lax.fori_loop(0, t // rb, body, 0)

    def rope(x, r0):
        return (x * cos_ref[pl.ds(r0, rb), :]
                + pltpu.roll(x, HEAD_DIM // 2, axis=1) * sin_ref[pl.ds(r0, rb), :])

    if with_q:
        _stage_padded(pad_ref, lambda r0: ka_ref[pl.ds(r0, rb), :].astype(F32), t, rb)
        _stage_padded(pad2_ref, lambda r0: qa_ref[pl.ds(r0, rb), :].astype(F32), t, rb)

        def kq_body(i, _):
            r0 = pl.multiple_of(i * rb, rb)
            yk = _silu(_conv_rows(pad_ref, wk_ref, r0, rb, GDN_CONV))
            yq = _silu(_conv_rows(pad2_ref, wq_ref, r0, rb, GDN_CONV))
            k_ref[pl.ds(r0, rb), :] = rope(_l2norm(yk), r0)
            q_ref[pl.ds(r0, rb), :] = rope(_l2norm(yq), r0) * HEAD_DIM ** -0.5
            return 0

        lax.fori_loop(0, t // rb, kq_body, 0)
    else:
        conv_all(ka_ref, wk_ref, lambda y, r0: _l2norm(y), k_ref)
    conv_all(va_ref, wv_ref, lambda y, r0: y, v_ref)

    ii = lax.broadcasted_iota(jnp.int32, (GROUP, GROUP), 0)
    jj = lax.broadcasted_iota(jnp.int32, (GROUP, GROUP), 1)
    same = (ii // CHUNK) == (jj // CHUNK)
    lower = same & (ii >= jj)
    upper = same & (ii <= jj)
    eye = ii == jj
    gc_ref[0] = _mm_exact(grow_ref[0], upper.astype(F32))
    gc_ref[1] = _mm_exact(grow_ref[1], lower.astype(F32))

    n_groups = t // GROUP
    gpi = min(PREP_GROUPS_PER_ITER, n_groups)
    n_sets = n_groups // gpi
    steps_per_set = 2 * gpi

    def prep_stages(i):
        chains = ([(i * gpi + k, 0) for k in range(gpi)]
                  + [(n_groups - 1 - (i * gpi + k), 1) for k in range(gpi)])
        ids = range(len(chains))
        st = {}

        def load():
            for ch, (g, d) in enumerate(chains):
                r0 = pl.multiple_of(g * GROUP, GROUP)
                st["k", ch] = k_ref[pl.ds(r0, GROUP), :]
                st["v", ch] = v_ref[pl.ds(r0, GROUP), :]
                kgb = st["k", ch].astype(BF16)
                if with_q:
                    st["q", ch] = q_ref[pl.ds(r0, GROUP), :]
                    both = _mm_nt(jnp.concatenate([st["k", ch], st["q", ch]], axis=0), kgb)
                    st["kk", ch], st["qk", ch] = both[:GROUP], both[GROUP:]
                else:
                    st["kk", ch] = _mm_nt(kgb, kgb)

        def masks():
            for ch, (g, d) in enumerate(chains):
                incl = lower if d == 0 else upper
                gc_r = jnp.broadcast_to(gc_ref[d, pl.ds(g, 1), :], (GROUP, GROUP))
                be_r = jnp.broadcast_to(grow_ref[2 + d, pl.ds(g, 1), :], (GROUP, GROUP))
                gc_c = gc_r.T
                decay = jnp.where(incl, jnp.exp(jnp.where(incl, gc_c - gc_r, 0.0)), 0.0)
                m = jnp.where(incl & jnp.logical_not(eye), st["kk", ch] * be_r.T * decay, 0.0)
                st["be_r", ch], st["gc_c", ch], st["decay", ch] = be_r, gc_c, decay
                st["p", ch] = jnp.where(eye, 1.0, -m)
                st["sq", ch] = _mm(m, m)

        def level():
            for ch in ids:
                p, sq = st["p", ch], st["sq", ch]
                st["p", ch] = p + _mm(p, sq)
                st["sq", ch] = _mm(sq, sq)

        def last_level():
            for ch in ids:
                st["p", ch] = st["p", ch] + _mm(st["p", ch], st["sq", ch])

        def solve():
            for ch in ids:
                rhs = jnp.concatenate([st["v", ch], st["k", ch] * jnp.exp(st["gc_c", ch])], axis=1)
                st["uw", ch] = _mm(st["p", ch] * st["be_r", ch], rhs)

        def fold():
            for ch, (g, d) in enumerate(chains):
                gc_c = st["gc_c", ch]
                last = [CHUNK - 1, GROUP - 1] if d == 0 else [0, CHUNK]
                gls = [gc_c[r:r + 1, :] for r in last]
                gl_c = jnp.concatenate([jnp.broadcast_to(x, (CHUNK, LANES)) for x in gls], axis=0)
                kdt = (st["k", ch] * jnp.exp(gl_c - gc_c)).T
                parts = [jnp.where(jj < CHUNK, kdt, 0.0), jnp.where(jj >= CHUNK, kdt, 0.0)]
                if with_q:
                    st["qd", ch] = st["q", ch] * jnp.exp(gc_c)
                    parts = [st["qk", ch] * st["decay", ch]] + parts
                st["gl", ch] = gls
                st["prod", ch] = _mm(jnp.concatenate(parts, axis=0), st["uw", ch])

        def store():
            for ch, (g, d) in enumerate(chains):
                r = st["prod", ch]
                off = GROUP if with_q else 0
                for half in range(2):
                    c = 2 * g + half
                    blk = r[off + half * HEAD_DIM:off + (half + 1) * HEAD_DIM]
                    h_s[d, c] = blk[:, :HEAD_DIM]
                    gl_s[d, c] = jnp.broadcast_to(jnp.exp(st["gl", ch][half]), (SUBLANES, LANES))
                    if with_q:
                        rs = slice(half * CHUNK, (half + 1) * CHUNK)
                        au_s[d, c] = r[rs, :HEAD_DIM]
                        qw = st["qd", ch][rs] - r[rs, HEAD_DIM:]
                        kwq_s[d, c] = jnp.concatenate([blk[:, HEAD_DIM:], qw], axis=0).astype(BF16)
                    else:
                        kwq_s[d, c] = blk[:, HEAD_DIM:].astype(BF16)

        return [load, masks] + [level] * (INVERSE_LEVELS - 2) + [last_level, solve, fold, store]

    def scan_step(s, carry):
        cs = (s, nc - 1 - s)
        rr = [jnp.dot(kwq_s[d, cs[d]], carry[d].astype(BF16), preferred_element_type=F32) for d in range(2)]
        new = []
        for d in range(2):
            c = cs[d]
            new.append(carry[d] * gl_s[d, c][0:1, :] + h_s[d, c] - rr[d][:HEAD_DIM])
            if with_q:
                r0 = pl.multiple_of(c * CHUNK, CHUNK)
                o_ref[pl.ds(r0, CHUNK), :] += rr[d][HEAD_DIM:] + au_s[d, c]
        return tuple(new)

    def run(stages, steps, carry):
        for idx in range(max(len(stages), len(steps))):
            if idx < len(stages):
                stages[idx]()
            if idx < len(steps):
                carry = scan_step(steps[idx], carry)
        return carry

    def set_steps(i):
        return [i * steps_per_set + k for k in range(steps_per_set)]

    if with_q:
        o_ref[...] = jnp.zeros_like(o_ref)
        carry = (s0_ref[0], s0_ref[1])
    else:
        zero = jnp.zeros((HEAD_DIM, HEAD_DIM), F32)
        carry = (zero, zero)
    run(prep_stages(0), [], carry)
    carry = lax.fori_loop(1, n_sets, lambda i, c: run(prep_stages(i), set_steps(i - 1), c), carry)
    s_f, s_b = run([], set_steps(n_sets - 1), carry)

    if not with_q:
        sfin_ref[0] = s_f
        sfin_ref[1] = s_b
        return

    def finish(i, _):
        r0 = pl.multiple_of(i * rb, rb)
        o = o_ref[pl.ds(r0, rb), :]
        y = o * lax.rsqrt(jnp.mean(o * o, axis=-1, keepdims=True) + NORM_EPS)
        y_ref[pl.ds(r0, rb), :] = ((y * gn_ref[...]) * _silu(za_ref[pl.ds(r0, rb), :].astype(F32))).astype(y_ref.dtype)
        return 0

    lax.fori_loop(0, t // rb, finish, 0)


def _single(block_shape, index_map):
    return pl.BlockSpec(block_shape, index_map, pipeline_mode=pl.Buffered(1))


def gdn_latent(p, conv_w, grow, cosf, sinf, s0, gnorm, col_k, col_v, col_q, col_z, casts=()):
    bn, t, _ = p.shape
    nc = t // CHUNK
    ngp = grow.shape[3]
    hd = HEAD_DIM
    seq = lambda col: pl.BlockSpec((None, t, hd), lambda b, h, col=col: (b, 0, col + h))
    cw = lambda col: pl.BlockSpec((GDN_CONV, hd), lambda b, h, col=col: (0, col + h))
    (y,), rounded = _call_hosting(
        lambda *refs: _gdn_kernel(*refs, t=t, with_q=True), casts,
        grid=(bn, GDN_HEADS),
        in_specs=[
            seq(col_k), seq(col_v), seq(col_q), seq(col_z),
            cw(0), cw(GDN_HEADS), cw(2 * GDN_HEADS),
            pl.BlockSpec((None, None, 4, ngp, GROUP), lambda b, h: (b, h, 0, 0, 0)),
            _single((t, hd), lambda b, h: (0, 0)),
            _single((t, hd), lambda b, h: (0, 0)),
            pl.BlockSpec((None, None, 2, hd, hd), lambda b, h: (b, h, 0, 0, 0)),
            pl.BlockSpec((1, hd), lambda b, h: (0, 0)),
        ],
        args=(p, p, p, p, conv_w, conv_w, conv_w, grow, cosf, sinf, s0, gnorm.reshape(1, hd)),
        out_specs=[pl.BlockSpec((None, t, hd), lambda b, h: (b, 0, h))],
        out_shapes=[jax.ShapeDtypeStruct((bn, t, GDN_HEADS * hd), BF16)],
        scratch_shapes=[
            pltpu.VMEM((t + 2 * CONV_HALO, hd), F32),
            pltpu.VMEM((t + 2 * CONV_HALO, hd), F32),
            pltpu.VMEM((t, hd), F32),
            pltpu.VMEM((t, hd), F32),
            pltpu.VMEM((t, hd), F32),
            pltpu.VMEM((t, hd), F32),
            pltpu.VMEM((2, ngp, GROUP), F32),
            pltpu.VMEM((2, nc, hd + CHUNK, hd), BF16),
            pltpu.VMEM((2, nc, hd, hd), F32),
            pltpu.VMEM((2, nc, CHUNK, hd), F32),
            pltpu.VMEM((2, nc, SUBLANES, LANES), F32),
        ],
        compiler_params=_cparams(("parallel", "parallel")),
        name="gdn_latent",
    )
    return y, rounded


def gdn_context(pc, conv_w, grow, col_k, col_v, casts=()):
    bn, t, _ = pc.shape
    nc = t // CHUNK
    ngp = grow.shape[3]
    hd = HEAD_DIM
    seq = lambda col: pl.BlockSpec((None, t, hd), lambda b, h, col=col: (b, 0, col + h))
    cw = lambda col: pl.BlockSpec((GDN_CONV, hd), lambda b, h, col=col: (0, col + h))
    (states,), rounded = _call_hosting(
        lambda *refs: _gdn_kernel(*refs, t=t, with_q=False), casts,
        grid=(bn, GDN_HEADS),
        in_specs=[
            seq(col_k), seq(col_v), cw(0), cw(GDN_HEADS),
            pl.BlockSpec((None, None, 4, ngp, GROUP), lambda b, h: (b, h, 0, 0, 0)),
        ],
        args=(pc, pc, conv_w, conv_w, grow),
        out_specs=[pl.BlockSpec((None, None, 2, hd, hd), lambda b, h: (b, h, 0, 0, 0))],
        out_shapes=[jax.ShapeDtypeStruct((bn, GDN_HEADS, 2, hd, hd), F32)],
        scratch_shapes=[
            pltpu.VMEM((t + 2 * CONV_HALO, hd), F32),
            pltpu.VMEM((t, hd), F32),
            pltpu.VMEM((t, hd), F32),
            pltpu.VMEM((2, ngp, GROUP), F32),
            pltpu.VMEM((2, nc, hd, hd), BF16),
            pltpu.VMEM((2, nc, hd, hd), F32),
            pltpu.VMEM((2, nc, SUBLANES, LANES), F32),
        ],
        compiler_params=_cparams(("parallel", "parallel")),
        name="gdn_context",
    )
    return states, rounded


def gate_rows(gates, t):
    bn = gates.shape[0]
    ng = t // GROUP
    g = gates[:, :, :4 * GDN_HEADS].reshape(bn, ng, GROUP, 4, GDN_HEADS)
    g = jnp.transpose(g, (0, 4, 3, 1, 2))
    if ng < SUBLANES:
        g = jnp.pad(g, ((0, 0), (0, 0), (0, 0), (0, SUBLANES - ng), (0, 0)))
    return g


def _natten_kernel(q_ref, k_ref, v_ref, kc_ref, vc_ref, by_dr_ref, o_ref, bias_ref, *, rows):
    kcb = kc_ref[...]
    vcb = vc_ref[...]
    scale = HEAD_DIM ** -0.5
    kh = min(NA_KH, rows)
    win = kh * GRID_W
    per_iter = min(NA_ROWS_PER_ITER, rows)

    for cls in range(NA_KH):
        for j in range(kh):
            bias_ref[cls, :, j * GRID_W:(j + 1) * GRID_W] = by_dr_ref[NA_KH - 1 - cls + j]

    def body(it, _):
        rws = [it * per_iter + k for k in range(per_iter)]
        q0s, k0s, s_loc, s_ctx = [], [], [], []
        for r in rws:
            rs = jnp.clip(r - NA_KH // 2, 0, rows - NA_KH)
            q0s.append(pl.multiple_of(r * GRID_W, GRID_W))
            k0s.append(pl.multiple_of(rs * GRID_W, GRID_W))
            q = q_ref[pl.ds(q0s[-1], GRID_W), :]
            kw = k_ref[pl.ds(k0s[-1], win), :]
            s_loc.append(lax.dot_general(q, kw, NT_DIMS, preferred_element_type=F32) * scale + bias_ref[r - rs])
            s_ctx.append(lax.dot_general(q, kcb, NT_DIMS, preferred_element_type=F32) * scale)
        p_loc, p_ctx, den = [], [], []
        for a, b in zip(s_loc, s_ctx):
            m = jnp.maximum(jnp.max(a, axis=-1, keepdims=True), jnp.max(b, axis=-1, keepdims=True))
            p_loc.append(jnp.exp(a - m))
            p_ctx.append(jnp.exp(b - m))
            den.append(jnp.sum(p_loc[-1], axis=-1, keepdims=True) + jnp.sum(p_ctx[-1], axis=-1, keepdims=True))
        outs = []
        for k in range(per_iter):
            vw = v_ref[pl.ds(k0s[k], win), :]
            outs.append(jnp.dot(p_loc[k].astype(BF16), vw, preferred_element_type=F32)
                        + jnp.dot(p_ctx[k].astype(BF16), vcb, preferred_element_type=F32))
        for k in range(per_iter):
            o_ref[pl.ds(q0s[k], GRID_W), :] = (outs[k] / den[k]).astype(o_ref.dtype)
        return 0

    lax.fori_loop(0, rows // per_iter, body, 0)


def natten(p, pc, by_dr, col_q, col_k, col_v, ctx_col_k, ctx_col_v, casts=()):
    bn, t, _ = p.shape
    lc = pc.shape[1]
    hd = HEAD_DIM
    rows = t // GRID_W
    kh = min(NA_KH, rows)
    seq = lambda col: pl.BlockSpec((None, t, hd), lambda b, h, col=col: (b, 0, col + h))
    cseq = lambda col: pl.BlockSpec((None, lc, hd), lambda b, h, col=col: (b, 0, col + h))
    (y,), rounded = _call_hosting(
        lambda *refs: _natten_kernel(*refs, rows=rows), casts,
        grid=(bn, NA_HEADS),
        in_specs=[
            seq(col_q), seq(col_k), seq(col_v), cseq(ctx_col_k), cseq(ctx_col_v),
            pl.BlockSpec((None,) + by_dr.shape[1:], lambda b, h: (h, 0, 0, 0)),
        ],
        args=(p, p, p, pc, pc, by_dr),
        out_specs=[pl.BlockSpec((None, t, hd), lambda b, h: (b, 0, h))],
        out_shapes=[jax.ShapeDtypeStruct((bn, t, NA_HEADS * hd), BF16)],
        scratch_shapes=[pltpu.VMEM((NA_KH, GRID_W, kh * GRID_W), F32)],
        compiler_params=_cparams(("parallel", "parallel")),
        name="natten",
    )
    return y, rounded


def natten_bias(rpb):
    hn, n_dr, n_dc = rpb.shape
    col = np.arange(GRID_W)
    col_start = np.clip(col - NA_KW // 2, 0, GRID_W - NA_KW)
    in_win = (col[None, :] >= col_start[:, None]) & (col[None, :] < col_start[:, None] + NA_KW)
    dc = np.clip(col[None, :] - col[:, None], -(NA_KW - 1), NA_KW - 1) + (NA_KW - 1)
    onehot = (np.arange(n_dc)[:, None] == dc.reshape(-1)[None, :]).astype(np.float32)
    by_dr = jnp.dot(rpb.astype(F32).reshape(hn * n_dr, n_dc), jnp.asarray(onehot), precision=HIGHEST)
    return jnp.where(in_win[None, None], by_dr.reshape(hn, n_dr, GRID_W, GRID_W), NEG_BIG)


def _outproj_kernel(*refs, n_in):
    x_ref, gate_ref, g_ref, sc_ref, sh_ref = refs[:5]
    a_refs = refs[5:5 + n_in]
    w_refs = refs[5 + n_in:5 + 2 * n_in]
    o_ref, hs_ref = refs[5 + 2 * n_in:]
    y = None
    for a_ref, w_ref in zip(a_refs, w_refs):
        part = jnp.dot(a_ref[...], w_ref[...], preferred_element_type=F32)
        y = part if y is None else y + part
    x1 = x_ref[...] + gate_ref[...] * y
    o_ref[...] = x1
    hs_ref[...] = _modulated_norm(x1, g_ref[...], sc_ref[...], sh_ref[...]).astype(BF16)


def outproj(x, gate, acts, weight, g_next, sc_next, sh_next, tm=512):
    bn, t, d = x.shape
    n_in = len(acts)
    kdim = acts[0].shape[-1]
    assert all(a.shape[-1] == kdim for a in acts) and weight.shape[0] == n_in * kdim
    weights = [weight] * n_in
    row = pl.BlockSpec((None, tm, d), lambda b, i: (b, i, 0))
    vec = pl.BlockSpec((None, 1, d), lambda b, i: (b, 0, 0))
    in_specs = [row, vec, pl.BlockSpec((1, d), lambda b, i: (0, 0)), vec, vec]
    in_specs += [pl.BlockSpec((None, tm, kdim), lambda b, i: (b, i, 0)) for a in acts]
    in_specs += [pl.BlockSpec((kdim, d), lambda b, i, k=k: (k, 0)) for k in range(n_in)]
    return pl.pallas_call(
        functools.partial(_outproj_kernel, n_in=n_in),
        grid=(bn, t // tm),
        in_specs=in_specs,
        out_specs=[row, row],
        out_shape=[jax.ShapeDtypeStruct((bn, t, d), F32), jax.ShapeDtypeStruct((bn, t, d), BF16)],
        compiler_params=_cparams(("parallel", "parallel")),
        name="outproj",
    )(x, gate, g_next.reshape(1, d), sc_next, sh_next, *acts, *weights)


def _outproj_conv_kernel(x_ref, gate_ref, g_ref, sc_ref, sh_ref, gb_ref, gc_ref, val_ref,
                         gcp_ref, valp_ref, gcn_ref, valn_ref, cw_ref, w_ref, o_ref, hs_ref):
    i = pl.program_id(1)
    tm = x_ref.shape[0]
    u = gc_ref[...].astype(F32) * val_ref[...].astype(F32)
    last = gcp_ref.shape[0] - 1
    u_prev = gcp_ref[last:last + 1, :].astype(F32) * valp_ref[last:last + 1, :].astype(F32)
    u_next = gcn_ref[0:1, :].astype(F32) * valn_ref[0:1, :].astype(F32)
    u_prev = jnp.where(i == 0, 0.0, u_prev)
    u_next = jnp.where(i == pl.num_programs(1) - 1, 0.0, u_next)
    row = lax.broadcasted_iota(jnp.int32, (tm, 1), 0)
    below = jnp.where(row == 0, u_prev, pltpu.roll(u, 1, axis=0))
    above = jnp.where(row == tm - 1, u_next, pltpu.roll(u, tm - 1, axis=0))
    conv = below * cw_ref[0:1, :] + u * cw_ref[1:2, :] + above * cw_ref[2:3, :]
    z = (gb_ref[...].astype(F32) * conv).astype(BF16)
    x1 = x_ref[...] + gate_ref[...] * jnp.dot(z, w_ref[...], preferred_element_type=F32)
    o_ref[...] = x1
    hs_ref[...] = _modulated_norm(x1, g_ref[...], sc_ref[...], sh_ref[...]).astype(BF16)


def outproj_conv(x, gate, p, conv_w, weight, g_next, sc_next, sh_next, tm=512):
    bn, t, d = x.shape
    halo = 2 * SUBLANES
    nb = t // halo
    row = pl.BlockSpec((None, tm, d), lambda b, i: (b, i, 0))
    vec = pl.BlockSpec((None, 1, d), lambda b, i: (b, 0, 0))
    seg = lambda k: pl.BlockSpec((None, tm, d), lambda b, i, k=k: (b, i, k))
    prev = lambda k: pl.BlockSpec((None, halo, d), lambda b, i, k=k: (b, jnp.maximum(i * (tm // halo) - 1, 0), k))
    nxt = lambda k: pl.BlockSpec((None, halo, d), lambda b, i, k=k: (b, jnp.minimum((i + 1) * (tm // halo), nb - 1), k))
    return pl.pallas_call(
        _outproj_conv_kernel,
        grid=(bn, t // tm),
        in_specs=[row, vec, pl.BlockSpec((1, d), lambda b, i: (0, 0)), vec, vec,
                  seg(0), seg(1), seg(2), prev(1), prev(2), nxt(1), nxt(2),
                  pl.BlockSpec((SC_CONV, d), lambda b, i: (0, 0)),
                  pl.BlockSpec((d, d), lambda b, i: (0, 0))],
        out_specs=[row, row],
        out_shape=[jax.ShapeDtypeStruct((bn, t, d), F32), jax.ShapeDtypeStruct((bn, t, d), BF16)],
        compiler_params=_cparams(("parallel", "parallel")),
        name="outproj_conv",
    )(x, gate, g_next.reshape(1, d), sc_next, sh_next, p, p, p, p, p, p, p, conv_w, weight)


def _ffn_kernel(x_ref, hs_ref, gate_ref, fn_ref, wg_ref, wu_ref, wd_ref, o_ref, *, final_norm):
    j = pl.program_id(2)

    def step(first):
        h = hs_ref[...]
        gate = jnp.dot(h, wg_ref[...], preferred_element_type=F32)
        up = jnp.dot(h, wu_ref[...], preferred_element_type=F32)
        a = (_silu(gate) * up).astype(BF16)
        part = jnp.dot(a, wd_ref[...], preferred_element_type=F32)
        if first:
            o_ref[...] = part
        else:
            o_ref[...] += part

    pl.when(j == 0)(lambda: step(True))
    pl.when(j > 0)(lambda: step(False))

    @pl.when(j == pl.num_programs(2) - 1)
    def _():
        rb = min(NORM_ROW_BLOCK, o_ref.shape[0])

        def body(i, _):
            rows = pl.ds(pl.multiple_of(i * rb, rb), rb)
            y = x_ref[rows, :] + gate_ref[...] * o_ref[rows, :]
            if final_norm:
                y = (y * lax.rsqrt(jnp.mean(y * y, axis=-1, keepdims=True) + NORM_EPS)) * fn_ref[...]
            o_ref[rows, :] = y
            return 0

        lax.fori_loop(0, o_ref.shape[0] // rb, body, 0)


def ffn_grid(x, f, tm=1024, tf=512):
    bn, t, _ = x.shape
    tm = min(tm, t)
    return tm, tf, (bn, t // tm, f // tf)


def ffn_weight_jobs(w_gate, w_up, w_down, layer, grid, tf):
    bn, ni, nj = grid
    d = w_gate.shape[1]
    slab = d // (bn * ni)
    assert slab * bn * ni == d and slab % LANES == 0
    tile = lambda b, i: b * ni + i
    up_job = lambda w: CastJob(w, (None, slab, tf), lambda b, i, j: (layer, tile(b, i), j),
                               w.shape[1:], (slab, tf), lambda b, i, j: (tile(b, i), j))
    down_job = CastJob(w_down, (None, tf, slab), lambda b, i, j: (layer, j, tile(b, i)),
                       w_down.shape[1:], (tf, slab), lambda b, i, j: (j, tile(b, i)))
    return [up_job(w_gate), up_job(w_up), down_job]


def ffn(x, hs, gate, fnorm, w_gate, w_up, w_down, final_norm, casts=()):
    bn, t, d = x.shape
    f = w_gate.shape[1]
    tm, tf, grid = ffn_grid(x, f)
    vec = pl.BlockSpec((None, 1, d), lambda b, i, j: (b, 0, 0))
    one = pl.BlockSpec((1, d), lambda b, i, j: (0, 0))
    row_index = lambda b, i, j: (b, i, 0)
    x_spec, hs_spec = pl.BlockSpec((None, tm, d), row_index), _single((None, tm, d), row_index)
    (y,), rounded = _call_hosting(
        functools.partial(_ffn_kernel, final_norm=final_norm), casts,
        grid=grid,
        in_specs=[
            x_spec, hs_spec,
            vec, one,
            pl.BlockSpec((d, tf), lambda b, i, j: (0, j)),
            pl.BlockSpec((d, tf), lambda b, i, j: (0, j)),
            pl.BlockSpec((tf, d), lambda b, i, j: (j, 0)),
        ],
        args=(x, hs, gate, fnorm.reshape(1, d), w_gate, w_up, w_down),
        out_specs=[pl.BlockSpec((None, tm, d), lambda b, i, j: (b, i, 0))],
        out_shapes=[jax.ShapeDtypeStruct((bn, t, d), F32)],
        compiler_params=_cparams(("parallel", "parallel", "arbitrary")),
        name="ffn",
    )
    return y, rounded


def rope_tables(t):
    pos = np.arange(t)
    row = (pos // GRID_W).astype(np.float32)
    col = (pos % GRID_W).astype(np.float32)
    n_freq = HEAD_DIM // 4
    inv_freq = jnp.asarray(ROPE_THETA, F32) ** (-jnp.arange(n_freq, dtype=F32) / n_freq)
    ang = jnp.concatenate([jnp.asarray(row)[:, None] * inv_freq, jnp.asarray(col)[:, None] * inv_freq], axis=-1)
    cos, sin = jnp.cos(ang), jnp.sin(ang)
    return jnp.concatenate([cos, cos], axis=-1), jnp.concatenate([-sin, sin], axis=-1)


def kernel(x, c, ctx, c_ctx, ada_w, ada_b, norm_mix, norm_ffn, ffn_w_gate, ffn_w_up, ffn_w_down, final_norm,
           ev_w_in, ev_conv, ev_a_log, ev_dt_bias, ev_gdn_norm, ev_rpb, ev_w_out, od_w_in, od_conv, od_w_out):
    bn, t, d = x.shape
    depth = ada_w.shape[0]
    assert depth == 2
    gw = GDN_HEADS * HEAD_DIM
    nw = NA_HEADS * HEAD_DIM
    nh = GDN_HEADS

    cv = jnp.zeros((SUBLANES, d), F32).at[:bn].set(c).at[bn].set(c_ctx)
    mods = ada_modulation(cv, ada_w, ada_b)

    def mod_vecs(l, rows):
        m = mods[l, rows].reshape(-1, 6, d)
        return [m[:, k][:, None, :] for k in range(6)]

    sh1, sc1, g1, sh2, sc2, g2 = mod_vecs(0, slice(0, bn))
    csh1, csc1 = [jnp.broadcast_to(v, (bn, 1, d)) for v in mod_vecs(0, slice(bn, bn + 1))[:2]]
    n_gate = 4 * nh
    seg_gate = 2 * gw + 2 * nw
    w_rows = jnp.swapaxes(ev_w_in[0], 0, 1)
    n_tail = w_rows.shape[0] - seg_gate - n_gate
    n_main = seg_gate + n_tail
    gate_block = seg_gate // LANES
    n_head_steps = bn * nh
    tail_slab = -(-n_tail // (n_head_steps * LANES)) * LANES
    tail_skip = n_head_steps * tail_slab - n_tail
    tail_first = w_rows.shape[0] - n_head_steps * tail_slab
    gparams = jnp.zeros((SUBLANES, LANES), F32)
    gparams = gparams.at[0, :2 * nh].set(ev_a_log[0].reshape(-1)).at[1, :2 * nh].set(ev_dt_bias[0].reshape(-1))
    col_ka, col_va, col_kb, col_vb, col_qa, col_qb, col_za = [k * nh for k in range(7)]

    head_step = lambda b, h: b * nh + h
    pc, gates_c = proj(ctx, norm_mix[0], csc1, csh1, w_rows, 0, seg_gate, None, seg_gate, w_rows, gate_block,
                       gparams, rows_major=True, tn=1024)
    s_ctx, (w_head, w_tail) = gdn_context(
        pc, ev_conv[0], gate_rows(gates_c, ctx.shape[1]), col_ka, col_va,
        casts=[transposing_job(w_rows, 0, seg_gate, n_head_steps, head_step),
               transposing_job(w_rows, tail_first, n_head_steps * tail_slab, n_head_steps, head_step)])

    p, gates = proj(x, norm_mix[0], sc1, sh1, w_head[None], 0, seg_gate, w_tail, n_main, w_rows, gate_block,
                    gparams, tail_skip=tail_skip)
    cosf, sinf = rope_tables(t)
    y_gdn, (w_out0,) = gdn_latent(p, ev_conv[0], gate_rows(gates, t), cosf, sinf, s_ctx, ev_gdn_norm[0],
                                  col_ka, col_va, col_qa, col_za,
                                  casts=[row_slab_job(ev_w_out, 0, n_head_steps, head_step)])
    y_na, rounded = natten(p, pc, natten_bias(ev_rpb[0]), col_qb, col_kb, col_vb, col_kb, col_vb,
                           casts=[row_slab_job(w, 0, n_head_steps, head_step)
                                  for w in (ffn_w_gate, ffn_w_up, ffn_w_down, od_w_in, od_w_out)])
    ffn0, (w_in1, w_out1) = rounded[:3], rounded[3:]
    x_lat, hs = outproj(x, g1, [y_gdn, y_na], w_out0, norm_ffn[0], sc2, sh2)
    _, tf, grid = ffn_grid(x_lat, ffn_w_gate.shape[2])
    x_lat, ffn1 = ffn(x_lat, hs, g2, final_norm, *ffn0, final_norm=False,
                      casts=ffn_weight_jobs(ffn_w_gate, ffn_w_up, ffn_w_down, 1, grid, tf))

    sh1, sc1, g1, sh2, sc2, g2 = mod_vecs(1, slice(0, bn))
    p = proj(x_lat, norm_mix[1], sc1, sh1, w_in1[None], 0, 3 * d, None, 3 * d)
    x_lat, hs = outproj_conv(x_lat, g1, p, od_conv[0], w_out1, norm_ffn[1], sc2, sh2)
    x_lat, _ = ffn(x_lat, hs, g2, final_norm, *ffn1, final_norm=True)
    return x_lat
```

```python
import functools

import jax
import jax.numpy as jnp
import numpy as np
from jax import lax
from jax.experimental import pallas as pl
from jax.experimental.pallas import tpu as pltpu

F32 = jnp.float32
BF16 = jnp.bfloat16
HIGHEST = lax.Precision.HIGHEST

LANES = 128
SUBLANES = 8
VMEM_LIMIT = 60 * 1024 * 1024

GRID_W = 64
HEAD_DIM = 128
GDN_HEADS = 8
NA_HEADS = 8
GDN_CONV = 5
CHUNK = 64
INVERSE_LEVELS = 6
GROUP = 2 * CHUNK
PREP_GROUPS_PER_ITER = 4
GDN_ROW_BLOCK = 1024
NA_ROWS_PER_ITER = 8
NA_KH = 8
NA_KW = 16
SC_CONV = 3
ROPE_THETA = 10000.0
NORM_EPS = 1e-6
NEG_BIG = -1e30

NT_DIMS = (((1,), (1,)), ((), ()))


def _cparams(sem):
    return pltpu.CompilerParams(dimension_semantics=sem, vmem_limit_bytes=VMEM_LIMIT)


def _sigmoid(x):
    return 1.0 / (1.0 + jnp.exp(-x))


def _silu(x):
    return x * _sigmoid(x)


def _softplus(x):
    return jnp.maximum(x, 0.0) + jnp.log(1.0 + jnp.exp(-jnp.abs(x)))


def _mm(a, b):
    return jnp.dot(a.astype(BF16), b.astype(BF16), preferred_element_type=F32)


def _mm_nt(a, b):
    return lax.dot_general(a.astype(BF16), b.astype(BF16), NT_DIMS, preferred_element_type=F32)


def _mm_exact(a, b):
    return jnp.dot(a, b, precision=HIGHEST, preferred_element_type=F32)


class CastJob:
    def __init__(self, src, in_block, in_index, out_shape, out_block, out_index, transpose=False,
                 row_offset=0, next_rows=None):
        self.transpose = transpose
        self.row_offset = row_offset
        self.srcs = [src]
        self.in_specs = [pl.BlockSpec(in_block, in_index)]
        if row_offset:
            self.srcs.append(src)
            self.in_specs.append(pl.BlockSpec(*next_rows))
        self.out_spec = pl.BlockSpec(out_block, out_index)
        self.out_shape = jax.ShapeDtypeStruct(out_shape, BF16)


def _hosting_casts(body, n_in, n_out, casts):
    n_cast_in = sum(len(job.srcs) for job in casts)

    def kernel(*refs):
        ins, rest = refs[:n_in], refs[n_in:]
        cast_in, rest = list(rest[:n_cast_in]), rest[n_cast_in:]
        outs, rest = rest[:n_out], rest[n_out:]
        cast_out, scratch = rest[:len(casts)], rest[len(casts):]
        for job, dst in zip(casts, cast_out):
            src = cast_in.pop(0)
            block = src[...]
            if job.row_offset:
                block = jnp.concatenate([src[job.row_offset:, :], cast_in.pop(0)[...]], axis=0)
            dst[...] = (block.T if job.transpose else block).astype(BF16)
        body(*ins, *outs, *scratch)

    return kernel


def row_slab_job(w, layer, n_steps, step_of):
    rows, cols = w.shape[1:]
    slab = rows // n_steps
    assert slab * n_steps == rows and slab % (2 * SUBLANES) == 0
    return CastJob(w, (None, slab, cols), lambda *idx: (layer, step_of(*idx), 0),
                   (rows, cols), (slab, cols), lambda *idx: (step_of(*idx), 0))


def transposing_job(w_rows, first_row, n_rows, n_steps, step_of):
    d = w_rows.shape[1]
    slab = n_rows // n_steps
    assert slab * n_steps == n_rows and slab % LANES == 0
    first_block, offset = divmod(first_row, slab)
    assert offset % SUBLANES == 0
    next_rows = None
    if offset:
        assert slab % offset == 0
        per = slab // offset
        next_rows = ((offset, d), lambda *idx: ((first_block + step_of(*idx) + 1) * per, 0))
    return CastJob(w_rows, (slab, d), lambda *idx: (first_block + step_of(*idx), 0),
                   (d, n_rows), (d, slab), lambda *idx: (0, step_of(*idx)), transpose=True,
                   row_offset=offset, next_rows=next_rows)


def _call_hosting(body, casts, grid, in_specs, args, out_specs, out_shapes, **kwargs):
    n_in, n_out = len(in_specs), len(out_specs)
    casts = list(casts)
    res = pl.pallas_call(
        _hosting_casts(body, n_in, n_out, casts),
        grid=grid,
        in_specs=list(in_specs) + [s for c in casts for s in c.in_specs],
        out_specs=list(out_specs) + [c.out_spec for c in casts],
        out_shape=list(out_shapes) + [c.out_shape for c in casts],
        **kwargs,
    )(*args, *[s for c in casts for s in c.srcs])
    return res[:n_out], res[n_out:]


def _ada_kernel(cv_ref, w_ref, b_ref, o_ref):
    s = _silu(cv_ref[...])
    o_ref[...] = _mm(s, w_ref[...]) + b_ref[...]


def ada_modulation(cv, ada_w, ada_b, tn=1024):
    depth, d, n = ada_w.shape
    return pl.pallas_call(
        _ada_kernel,
        grid=(depth, n // tn),
        in_specs=[
            pl.BlockSpec((SUBLANES, d), lambda l, j: (0, 0)),
            pl.BlockSpec((None, d, tn), lambda l, j: (l, 0, j)),
            pl.BlockSpec((None, 1, tn), lambda l, j: (l, 0, j)),
        ],
        out_specs=pl.BlockSpec((None, SUBLANES, tn), lambda l, j: (l, 0, j)),
        out_shape=jax.ShapeDtypeStruct((depth, SUBLANES, n), F32),
        compiler_params=_cparams(("parallel", "parallel")),
        name="ada",
    )(cv, ada_w, ada_b.reshape(depth, 1, n))


def _modulated_norm(x, g, sc, sh):
    y = x * lax.rsqrt(jnp.mean(x * x, axis=-1, keepdims=True) + NORM_EPS)
    return (y * g) * (1.0 + sc) + sh


NORM_ROW_BLOCK = 128


def _modulated_norm_rows(x_ref, g_ref, sc_ref, sh_ref, hs_ref):
    rb = min(NORM_ROW_BLOCK, x_ref.shape[0])

    def body(i, _):
        rows = pl.ds(pl.multiple_of(i * rb, rb), rb)
        hs_ref[rows, :] = _modulated_norm(x_ref[rows, :], g_ref[...], sc_ref[...], sh_ref[...]).astype(BF16)
        return 0

    lax.fori_loop(0, x_ref.shape[0] // rb, body, 0)


def _proj_kernel(*refs, with_gates, with_tail, head_blocks, rows_major):
    x_ref, g_ref, sc_ref, sh_ref, w_ref = refs[:5]
    refs = refs[5:]
    wt_ref = None
    if with_tail:
        wt_ref, refs = refs[0], refs[1:]
    if with_gates:
        wab_ref, gp_ref, o_ref, gate_ref, hs_ref, wb_ref = refs
    else:
        o_ref, hs_ref, wb_ref = refs
    j = pl.program_id(2)

    @pl.when(j == 0)
    def _():
        if with_gates:
            hb = _modulated_norm(x_ref[...], g_ref[...], sc_ref[...], sh_ref[...]).astype(BF16)
            hs_ref[...] = hb
            a = _mm_nt(hb, wab_ref[...])
            neg_decay_rate = -jnp.exp(gp_ref[0:1, :])
            g = neg_decay_rate * _softplus(a + gp_ref[1:2, :])
            lane = lax.broadcasted_iota(jnp.int32, a.shape, 1)
            gate_ref[...] = jnp.where(lane < 2 * GDN_HEADS, g, _sigmoid(a))
        else:
            _modulated_norm_rows(x_ref, g_ref, sc_ref, sh_ref, hs_ref)

    def emit(wref):
        if wref.dtype != BF16:
            wb_ref[...] = wref[...].astype(BF16)
            wref = wb_ref
        if rows_major:
            y = lax.dot_general(hs_ref[...], wref[...], NT_DIMS, preferred_element_type=F32)
        else:
            y = jnp.dot(hs_ref[...], wref[...], preferred_element_type=F32)
        o_ref[...] = y.astype(o_ref.dtype)

    pl.when(j < head_blocks)(lambda: emit(w_ref))
    if with_tail:
        pl.when(j >= head_blocks)(lambda: emit(wt_ref))


def proj(x, g, sc, sh, w, layer, head_cols, w_tail, n_out, wab=None, gate_block=0, gparams=None,
         rows_major=False, tail_skip=0, tm=1024, tn=2048):
    bn, t, d = x.shape
    tm = min(tm, t)
    while n_out % tn or head_cols % tn or tail_skip % tn:
        tn //= 2
    head_blocks = head_cols // tn
    with_gates = wab is not None
    vec = pl.BlockSpec((None, 1, d), lambda b, i, j: (b, 0, 0))
    if rows_major:
        w_spec = pl.BlockSpec((tn, d), lambda b, i, j: (jnp.minimum(j, head_blocks - 1), 0))
    else:
        w_spec = pl.BlockSpec((None, d, tn), lambda b, i, j: (layer, 0, jnp.minimum(j, head_blocks - 1)))
    in_specs = [
        pl.BlockSpec((None, tm, d), lambda b, i, j: (b, i, 0)),
        pl.BlockSpec((1, d), lambda b, i, j: (0, 0)),
        vec, vec,
        w_spec,
    ]
    args = [x, g.reshape(1, d), sc, sh, w]
    with_tail = w_tail is not None
    if with_tail:
        skip = tail_skip // tn
        in_specs.append(pl.BlockSpec((d, tn), lambda b, i, j: (0, jnp.maximum(j - head_blocks, 0) + skip)))
        args.append(w_tail)
    out_specs = [pl.BlockSpec((None, tm, tn), lambda b, i, j: (b, i, j))]
    out_shape = [jax.ShapeDtypeStruct((bn, t, n_out), BF16)]
    if with_gates:
        in_specs += [pl.BlockSpec((LANES, d), lambda b, i, j: (gate_block, 0)),
                     pl.BlockSpec((SUBLANES, LANES), lambda b, i, j: (0, 0))]
        args += [wab, gparams]
        out_specs.append(pl.BlockSpec((None, tm, LANES), lambda b, i, j: (b, i, 0)))
        out_shape.append(jax.ShapeDtypeStruct((bn, t, LANES), F32))
    res = pl.pallas_call(
        functools.partial(_proj_kernel, with_gates=with_gates, with_tail=with_tail, head_blocks=head_blocks,
                          rows_major=rows_major),
        grid=(bn, t // tm, n_out // tn),
        in_specs=in_specs,
        out_specs=out_specs,
        out_shape=out_shape,
        scratch_shapes=[pltpu.VMEM((tm, d), BF16), pltpu.VMEM((tn, d) if rows_major else (d, tn), BF16)],
        compiler_params=_cparams(("parallel", "parallel", "arbitrary")),
        name="proj",
    )(*args)
    return res if with_gates else res[0]


CONV_HALO = SUBLANES


def _stage_padded(pad_ref, load_rows, t, rb):
    zeros = jnp.zeros((CONV_HALO, LANES), F32)
    pad_ref[0:CONV_HALO, :] = zeros
    pad_ref[CONV_HALO + t:2 * CONV_HALO + t, :] = zeros

    def body(i, _):
        r0 = pl.multiple_of(i * rb, rb)
        pad_ref[pl.ds(CONV_HALO + r0, rb), :] = load_rows(r0)
        return 0

    lax.fori_loop(0, t // rb, body, 0)


def _conv_rows(pad_ref, w_ref, r0, rb, taps):
    acc = None
    for j in range(taps):
        xj = pad_ref[pl.ds(r0 + (CONV_HALO + j - taps // 2), rb), :]
        term = xj * w_ref[j:j + 1, :]
        acc = term if acc is None else acc + term
    return acc


def _l2norm(x):
    return x * lax.rsqrt(jnp.sum(x * x, axis=-1, keepdims=True) + NORM_EPS)


def _gdn_kernel(*refs, t, with_q):
    nc = t // CHUNK
    rb = min(GDN_ROW_BLOCK, t)
    if with_q:
        (ka_ref, va_ref, qa_ref, za_ref, wk_ref, wv_ref, wq_ref, grow_ref, cos_ref, sin_ref, s0_ref,
         gn_ref, y_ref,
         pad_ref, pad2_ref, k_ref, v_ref, q_ref, o_ref, gc_ref, kwq_s, h_s, au_s, gl_s) = refs
    else:
        (ka_ref, va_ref, wk_ref, wv_ref, grow_ref, sfin_ref,
         pad_ref, k_ref, v_ref, gc_ref, kwq_s, h_s, gl_s) = refs

    def conv_all(src_ref, w_ref, finish, dst_ref):
        _stage_padded(pad_ref, lambda r0: src_ref[pl.ds(r0, rb), :].astype(F32), t, rb)

        def body(i, _):
            r0 = pl.multiple_of(i * rb, rb)
            y = _silu(_conv_rows(pad_ref, w_ref, r0, rb, GDN_CONV))
            dst_ref[pl.ds(r0, rb), :] = finish(y, r0)
            return 0

        lax.fori_loop(0, t // rb, body, 0)

    def rope(x, r0):
        return (x * cos_ref[pl.ds(r0, rb), :]
                + pltpu.roll(x, HEAD_DIM // 2, axis=1) * sin_ref[pl.ds(r0, rb), :])

    if with_q:
        _stage_padded(pad_ref, lambda r0: ka_ref[pl.ds(r0, rb), :].astype(F32), t, rb)
        _stage_padded(pad2_ref, lambda r0: qa_ref[pl.ds(r0, rb), :].astype(F32), t, rb)

        def kq_body(i, _):
            r0 = pl.multiple_of(i * rb, rb)
            yk = _silu(_conv_rows(pad_ref, wk_ref, r0, rb, GDN_CONV))
            yq = _silu(_conv_rows(pad2_ref, wq_ref, r0, rb, GDN_CONV))
            k_ref[pl.ds(r0, rb), :] = rope(_l2norm(yk), r0)
            q_ref[pl.ds(r0, rb), :] = rope(_l2norm(yq), r0) * HEAD_DIM ** -0.5
            return 0

        lax.fori_loop(0, t // rb, kq_body, 0)
    else:
        conv_all(ka_ref, wk_ref, lambda y, r0: _l2norm(y), k_ref)
    conv_all(va_ref, wv_ref, lambda y, r0: y, v_ref)

    ii = lax.broadcasted_iota(jnp.int32, (GROUP, GROUP), 0)
    jj = lax.broadcasted_iota(jnp.int32, (GROUP, GROUP), 1)
    same = (ii // CHUNK) == (jj // CHUNK)
    lower = same & (ii >= jj)
    upper = same & (ii <= jj)
    eye = ii == jj
    gc_ref[0] = _mm_exact(grow_ref[0], upper.astype(F32))
    gc_ref[1] = _mm_exact(grow_ref[1], lower.astype(F32))

    n_groups = t // GROUP
    gpi = min(PREP_GROUPS_PER_ITER, n_groups)
    n_sets = n_groups // gpi
    steps_per_set = 2 * gpi

    def prep_stages(i):
        chains = ([(i * gpi + k, 0) for k in range(gpi)]
                  + [(n_groups - 1 - (i * gpi + k), 1) for k in range(gpi)])
        ids = range(len(chains))
        st = {}

        def load():
            for ch, (g, d) in enumerate(chains):
                r0 = pl.multiple_of(g * GROUP, GROUP)
                st["k", ch] = k_ref[pl.ds(r0, GROUP), :]
                st["v", ch] = v_ref[pl.ds(r0, GROUP), :]
                kgb = st["k", ch].astype(BF16)
                if with_q:
                    st["q", ch] = q_ref[pl.ds(r0, GROUP), :]
                    both = _mm_nt(jnp.concatenate([st["k", ch], st["q", ch]], axis=0), kgb)
                    st["kk", ch], st["qk", ch] = both[:GROUP], both[GROUP:]
                else:
                    st["kk", ch] = _mm_nt(kgb, kgb)

        def masks():
            for ch, (g, d) in enumerate(chains):
                incl = lower if d == 0 else upper
                gc_r = jnp.broadcast_to(gc_ref[d, pl.ds(g, 1), :], (GROUP, GROUP))
                be_r = jnp.broadcast_to(grow_ref[2 + d, pl.ds(g, 1), :], (GROUP, GROUP))
                gc_c = gc_r.T
                decay = jnp.where(incl, jnp.exp(jnp.where(incl, gc_c - gc_r, 0.0)), 0.0)
                m = jnp.where(incl & jnp.logical_not(eye), st["kk", ch] * be_r.T * decay, 0.0)
                st["be_r", ch], st["gc_c", ch], st["decay", ch] = be_r, gc_c, decay
                st["p", ch] = jnp.where(eye, 1.0, -m)
                st["sq", ch] = _mm(m, m)

        def level():
            for ch in ids:
                p, sq = st["p", ch], st["sq", ch]
                st["p", ch] = p + _mm(p, sq)
                st["sq", ch] = _mm(sq, sq)

        def last_level():
            for ch in ids:
                st["p", ch] = st["p", ch] + _mm(st["p", ch], st["sq", ch])

        def solve():
            for ch in ids:
                rhs = jnp.concatenate([st["v", ch], st["k", ch] * jnp.exp(st["gc_c", ch])], axis=1)
                st["uw", ch] = _mm(st["p", ch] * st["be_r", ch], rhs)

        def fold():
            for ch, (g, d) in enumerate(chains):
                gc_c = st["gc_c", ch]
                last = [CHUNK - 1, GROUP - 1] if d == 0 else [0, CHUNK]
                gls = [gc_c[r:r + 1, :] for r in last]
                gl_c = jnp.concatenate([jnp.broadcast_to(x, (CHUNK, LANES)) for x in gls], axis=0)
                kdt = (st["k", ch] * jnp.exp(gl_c - gc_c)).T
                parts = [jnp.where(jj < CHUNK, kdt, 0.0), jnp.where(jj >= CHUNK, kdt, 0.0)]
                if with_q:
                    st["qd", ch] = st["q", ch] * jnp.exp(gc_c)
                    parts = [st["qk", ch] * st["decay", ch]] + parts
                st["gl", ch] = gls
                st["prod", ch] = _mm(jnp.concatenate(parts, axis=0), st["uw", ch])

        def store():
            for ch, (g, d) in enumerate(chains):
                r = st["prod", ch]
                off = GROUP if with_q else 0
                for half in range(2):
                    c = 2 * g + half
                    blk = r[off + half * HEAD_DIM:off + (half + 1) * HEAD_DIM]
                    h_s[d, c] = blk[:, :HEAD_DIM]
                    gl_s[d, c] = jnp.broadcast_to(jnp.exp(st["gl", ch][half]), (SUBLANES, LANES))
                    if with_q:
                        rs = slice(half * CHUNK, (half + 1) * CHUNK)
                        au_s[d, c] = r[rs, :HEAD_DIM]
                        qw = st["qd", ch][rs] - r[rs, HEAD_DIM:]
                        kwq_s[d, c] = jnp.concatenate([blk[:, HEAD_DIM:], qw], axis=0).astype(BF16)
                    else:
                        kwq_s[d, c] = blk[:, HEAD_DIM:].astype(BF16)

        return [load, masks] + [level] * (INVERSE_LEVELS - 2) + [last_level, solve, fold, store]

    def scan_step(s, carry):
        cs = (s, nc - 1 - s)
        rr = [jnp.dot(kwq_s[d, cs[d]], carry[d].astype(BF16), preferred_element_type=F32) for d in range(2)]
        new = []
        for d in range(2):
            c = cs[d]
            new.append(carry[d] * gl_s[d, c][0:1, :] + h_s[d, c] - rr[d][:HEAD_DIM])
            if with_q:
                r0 = pl.multiple_of(c * CHUNK, CHUNK)
                o_ref[pl.ds(r0, CHUNK), :] += rr[d][HEAD_DIM:] + au_s[d, c]
        return tuple(new)

    def run(stages, steps, carry):
        for idx in range(max(len(stages), len(steps))):
            if idx < len(stages):
                stages[idx]()
            if idx < len(steps):
                carry = scan_step(steps[idx], carry)
        return carry

    def set_steps(i):
        return [i * steps_per_set + k for k in range(steps_per_set)]

    if with_q:
        o_ref[...] = jnp.zeros_like(o_ref)
        carry = (s0_ref[0], s0_ref[1])
    else:
        zero = jnp.zeros((HEAD_DIM, HEAD_DIM), F32)
        carry = (zero, zero)
    run(prep_stages(0), [], carry)
    carry = lax.fori_loop(1, n_sets, lambda i, c: run(prep_stages(i), set_steps(i - 1), c), carry)
    s_f, s_b = run([], set_steps(n_sets - 1), carry)

    if not with_q:
        sfin_ref[0] = s_f
        sfin_ref[1] = s_b
        return

    def finish(i, _):
        r0 = pl.multiple_of(i * rb, rb)
        o = o_ref[pl.ds(r0, rb), :]
        y = o * lax.rsqrt(jnp.mean(o * o, axis=-1, keepdims=True) + NORM_EPS)
        y_ref[pl.ds(r0, rb), :] = ((y * gn_ref[...]) * _silu(za_ref[pl.ds(r0, rb), :].astype(F32))).astype(y_ref.dtype)
        return 0

    lax.fori_loop(0, t // rb, finish, 0)


def _single(block_shape, index_map):
    return pl.BlockSpec(block_shape, index_map, pipeline_mode=pl.Buffered(1))


def gdn_latent(p, conv_w, grow, cosf, sinf, s0, gnorm, col_k, col_v, col_q, col_z, casts=()):
    bn, t, _ = p.shape
    nc = t // CHUNK
    ngp = grow.shape[3]
    hd = HEAD_DIM
    seq = lambda col: pl.BlockSpec((None, t, hd), lambda b, h, col=col: (b, 0, col + h))
    cw = lambda col: pl.BlockSpec((GDN_CONV, hd), lambda b, h, col=col: (0, col + h))
    (y,), rounded = _call_hosting(
        lambda *refs: _gdn_kernel(*refs, t=t, with_q=True), casts,
        grid=(bn, GDN_HEADS),
        in_specs=[
            seq(col_k), seq(col_v), seq(col_q), seq(col_z),
            cw(0), cw(GDN_HEADS), cw(2 * GDN_HEADS),
            pl.BlockSpec((None, None, 4, ngp, GROUP), lambda b, h: (b, h, 0, 0, 0)),
            _single((t, hd), lambda b, h: (0, 0)),
            _single((t, hd), lambda b, h: (0, 0)),
            pl.BlockSpec((None, None, 2, hd, hd), lambda b, h: (b, h, 0, 0, 0)),
            pl.BlockSpec((1, hd), lambda b, h: (0, 0)),
        ],
        args=(p, p, p, p, conv_w, conv_w, conv_w, grow, cosf, sinf, s0, gnorm.reshape(1, hd)),
        out_specs=[pl.BlockSpec((None, t, hd), lambda b, h: (b, 0, h))],
        out_shapes=[jax.ShapeDtypeStruct((bn, t, GDN_HEADS * hd), BF16)],
        scratch_shapes=[
            pltpu.VMEM((t + 2 * CONV_HALO, hd), F32),
            pltpu.VMEM((t + 2 * CONV_HALO, hd), F32),
            pltpu.VMEM((t, hd), F32),
            pltpu.VMEM((t, hd), F32),
            pltpu.VMEM((t, hd), F32),
            pltpu.VMEM((t, hd), F32),
            pltpu.VMEM((2, ngp, GROUP), F32),
            pltpu.VMEM((2, nc, hd + CHUNK, hd), BF16),
            pltpu.VMEM((2, nc, hd, hd), F32),
            pltpu.VMEM((2, nc, CHUNK, hd), F32),
            pltpu.VMEM((2, nc, SUBLANES, LANES), F32),
        ],
        compiler_params=_cparams(("parallel", "parallel")),
        name="gdn_latent",
    )
    return y, rounded


def gdn_context(pc, conv_w, grow, col_k, col_v, casts=()):
    bn, t, _ = pc.shape
    nc = t // CHUNK
    ngp = grow.shape[3]
    hd = HEAD_DIM
    seq = lambda col: pl.BlockSpec((None, t, hd), lambda b, h, col=col: (b, 0, col + h))
    cw = lambda col: pl.BlockSpec((GDN_CONV, hd), lambda b, h, col=col: (0, col + h))
    (states,), rounded = _call_hosting(
        lambda *refs: _gdn_kernel(*refs, t=t, with_q=False), casts,
        grid=(bn, GDN_HEADS),
        in_specs=[
            seq(col_k), seq(col_v), cw(0), cw(GDN_HEADS),
            pl.BlockSpec((None, None, 4, ngp, GROUP), lambda b, h: (b, h, 0, 0, 0)),
        ],
        args=(pc, pc, conv_w, conv_w, grow),
        out_specs=[pl.BlockSpec((None, None, 2, hd, hd), lambda b, h: (b, h, 0, 0, 0))],
        out_shapes=[jax.ShapeDtypeStruct((bn, GDN_HEADS, 2, hd, hd), F32)],
        scratch_shapes=[
            pltpu.VMEM((t + 2 * CONV_HALO, hd), F32),
            pltpu.VMEM((t, hd), F32),
            pltpu.VMEM((t, hd), F32),
            pltpu.VMEM((2, ngp, GROUP), F32),
            pltpu.VMEM((2, nc, hd, hd), BF16),
            pltpu.VMEM((2, nc, hd, hd), F32),
            pltpu.VMEM((2, nc, SUBLANES, LANES), F32),
        ],
        compiler_params=_cparams(("parallel", "parallel")),
        name="gdn_context",
    )
    return states, rounded


def gate_rows(gates, t):
    bn = gates.shape[0]
    ng = t // GROUP
    g = gates[:, :, :4 * GDN_HEADS].reshape(bn, ng, GROUP, 4, GDN_HEADS)
    g = jnp.transpose(g, (0, 4, 3, 1, 2))
    if ng < SUBLANES:
        g = jnp.pad(g, ((0, 0), (0, 0), (0, 0), (0, SUBLANES - ng), (0, 0)))
    return g


def _natten_kernel(q_ref, k_ref, v_ref, kc_ref, vc_ref, by_dr_ref, o_ref, bias_ref, *, rows):
    kcb = kc_ref[...]
    vcb = vc_ref[...]
    scale = HEAD_DIM ** -0.5
    kh = min(NA_KH, rows)
    win = kh * GRID_W
    per_iter = min(NA_ROWS_PER_ITER, rows)

    for cls in range(NA_KH):
        for j in range(kh):
            bias_ref[cls, :, j * GRID_W:(j + 1) * GRID_W] = by_dr_ref[NA_KH - 1 - cls + j]

    def body(it, _):
        rws = [it * per_iter + k for k in range(per_iter)]
        q0s, k0s, s_loc, s_ctx = [], [], [], []
        for r in rws:
            rs = jnp.clip(r - NA_KH // 2, 0, rows - NA_KH)
            q0s.append(pl.multiple_of(r * GRID_W, GRID_W))
            k0s.append(pl.multiple_of(rs * GRID_W, GRID_W))
            q = q_ref[pl.ds(q0s[-1], GRID_W), :]
            kw = k_ref[pl.ds(k0s[-1], win), :]
            s_loc.append(lax.dot_general(q, kw, NT_DIMS, preferred_element_type=F32) * scale + bias_ref[r - rs])
            s_ctx.append(lax.dot_general(q, kcb, NT_DIMS, preferred_element_type=F32) * scale)
        p_loc, p_ctx, den = [], [], []
        for a, b in zip(s_loc, s_ctx):
            m = jnp.maximum(jnp.max(a, axis=-1, keepdims=True), jnp.max(b, axis=-1, keepdims=True))
            p_loc.append(jnp.exp(a - m))
            p_ctx.append(jnp.exp(b - m))
            den.append(jnp.sum(p_loc[-1], axis=-1, keepdims=True) + jnp.sum(p_ctx[-1], axis=-1, keepdims=True))
        outs = []
        for k in range(per_iter):
            vw = v_ref[pl.ds(k0s[k], win), :]
            outs.append(jnp.dot(p_loc[k].astype(BF16), vw, preferred_element_type=F32)
                        + jnp.dot(p_ctx[k].astype(BF16), vcb, preferred_element_type=F32))
        for k in range(per_iter):
            o_ref[pl.ds(q0s[k], GRID_W), :] = (outs[k] / den[k]).astype(o_ref.dtype)
        return 0

    lax.fori_loop(0, rows // per_iter, body, 0)


def natten(p, pc, by_dr, col_q, col_k, col_v, ctx_col_k, ctx_col_v, casts=()):
    bn, t, _ = p.shape
    lc = pc.shape[1]
    hd = HEAD_DIM
    rows = t // GRID_W
    kh = min(NA_KH, rows)
    seq = lambda col: pl.BlockSpec((None, t, hd), lambda b, h, col=col: (b, 0, col + h))
    cseq = lambda col: pl.BlockSpec((None, lc, hd), lambda b, h, col=col: (b, 0, col + h))
    (y,), rounded = _call_hosting(
        lambda *refs: _natten_kernel(*refs, rows=rows), casts,
        grid=(bn, NA_HEADS),
        in_specs=[
            seq(col_q), seq(col_k), seq(col_v), cseq(ctx_col_k), cseq(ctx_col_v),
            pl.BlockSpec((None,) + by_dr.shape[1:], lambda b, h: (h, 0, 0, 0)),
        ],
        args=(p, p, p, pc, pc, by_dr),
        out_specs=[pl.BlockSpec((None, t, hd), lambda b, h: (b, 0, h))],
        out_shapes=[jax.ShapeDtypeStruct((bn, t, NA_HEADS * hd), BF16)],
        scratch_shapes=[pltpu.VMEM((NA_KH, GRID_W, kh * GRID_W), F32)],
        compiler_params=_cparams(("parallel", "parallel")),
        name="natten",
    )
    return y, rounded


def natten_bias(rpb):
    hn, n_dr, n_dc = rpb.shape
    col = np.arange(GRID_W)
    col_start = np.clip(col - NA_KW // 2, 0, GRID_W - NA_KW)
    in_win = (col[None, :] >= col_start[:, None]) & (col[None, :] < col_start[:, None] + NA_KW)
    dc = np.clip(col[None, :] - col[:, None], -(NA_KW - 1), NA_KW - 1) + (NA_KW - 1)
    onehot = (np.arange(n_dc)[:, None] == dc.reshape(-1)[None, :]).astype(np.float32)
    by_dr = jnp.dot(rpb.astype(F32).reshape(hn * n_dr, n_dc), jnp.asarray(onehot), precision=HIGHEST)
    return jnp.where(in_win[None, None], by_dr.reshape(hn, n_dr, GRID_W, GRID_W), NEG_BIG)


def _outproj_kernel(*refs, n_in):
    x_ref, gate_ref, g_ref, sc_ref, sh_ref = refs[:5]
    a_refs = refs[5:5 + n_in]
    w_refs = refs[5 + n_in:5 + 2 * n_in]
    o_ref, hs_ref = refs[5 + 2 * n_in:]
    y = None
    for a_ref, w_ref in zip(a_refs, w_refs):
        part = jnp.dot(a_ref[...], w_ref[...], preferred_element_type=F32)
        y = part if y is None else y + part
    x1 = x_ref[...] + gate_ref[...] * y
    o_ref[...] = x1
    hs_ref[...] = _modulated_norm(x1, g_ref[...], sc_ref[...], sh_ref[...]).astype(BF16)


def outproj(x, gate, acts, weight, g_next, sc_next, sh_next, tm=512):
    bn, t, d = x.shape
    n_in = len(acts)
    kdim = acts[0].shape[-1]
    assert all(a.shape[-1] == kdim for a in acts) and weight.shape[0] == n_in * kdim
    weights = [weight] * n_in
    row = pl.BlockSpec((None, tm, d), lambda b, i: (b, i, 0))
    vec = pl.BlockSpec((None, 1, d), lambda b, i: (b, 0, 0))
    in_specs = [row, vec, pl.BlockSpec((1, d), lambda b, i: (0, 0)), vec, vec]
    in_specs += [pl.BlockSpec((None, tm, kdim), lambda b, i: (b, i, 0)) for a in acts]
    in_specs += [pl.BlockSpec((kdim, d), lambda b, i, k=k: (k, 0)) for k in range(n_in)]
    return pl.pallas_call(
        functools.partial(_outproj_kernel, n_in=n_in),
        grid=(bn, t // tm),
        in_specs=in_specs,
        out_specs=[row, row],
        out_shape=[jax.ShapeDtypeStruct((bn, t, d), F32), jax.ShapeDtypeStruct((bn, t, d), BF16)],
        compiler_params=_cparams(("parallel", "parallel")),
        name="outproj",
    )(x, gate, g_next.reshape(1, d), sc_next, sh_next, *acts, *weights)


def _outproj_conv_kernel(x_ref, gate_ref, g_ref, sc_ref, sh_ref, gb_ref, gc_ref, val_ref,
                         gcp_ref, valp_ref, gcn_ref, valn_ref, cw_ref, w_ref, o_ref, hs_ref):
    i = pl.program_id(1)
    tm = x_ref.shape[0]
    u = gc_ref[...].astype(F32) * val_ref[...].astype(F32)
    last = gcp_ref.shape[0] - 1
    u_prev = gcp_ref[last:last + 1, :].astype(F32) * valp_ref[last:last + 1, :].astype(F32)
    u_next = gcn_ref[0:1, :].astype(F32) * valn_ref[0:1, :].astype(F32)
    u_prev = jnp.where(i == 0, 0.0, u_prev)
    u_next = jnp.where(i == pl.num_programs(1) - 1, 0.0, u_next)
    row = lax.broadcasted_iota(jnp.int32, (tm, 1), 0)
    below = jnp.where(row == 0, u_prev, pltpu.roll(u, 1, axis=0))
    above = jnp.where(row == tm - 1, u_next, pltpu.roll(u, tm - 1, axis=0))
    conv = below * cw_ref[0:1, :] + u * cw_ref[1:2, :] + above * cw_ref[2:3, :]
    z = (gb_ref[...].astype(F32) * conv).astype(BF16)
    x1 = x_ref[...] + gate_ref[...] * jnp.dot(z, w_ref[...], preferred_element_type=F32)
    o_ref[...] = x1
    hs_ref[...] = _modulated_norm(x1, g_ref[...], sc_ref[...], sh_ref[...]).astype(BF16)


def outproj_conv(x, gate, p, conv_w, weight, g_next, sc_next, sh_next, tm=512):
    bn, t, d = x.shape
    halo = 2 * SUBLANES
    nb = t // halo
    row = pl.BlockSpec((None, tm, d), lambda b, i: (b, i, 0))
    vec = pl.BlockSpec((None, 1, d), lambda b, i: (b, 0, 0))
    seg = lambda k: pl.BlockSpec((None, tm, d), lambda b, i, k=k: (b, i, k))
    prev = lambda k: pl.BlockSpec((None, halo, d), lambda b, i, k=k: (b, jnp.maximum(i * (tm // halo) - 1, 0), k))
    nxt = lambda k: pl.BlockSpec((None, halo, d), lambda b, i, k=k: (b, jnp.minimum((i + 1) * (tm // halo), nb - 1), k))
    return pl.pallas_call(
        _outproj_conv_kernel,
        grid=(bn, t // tm),
        in_specs=[row, vec, pl.BlockSpec((1, d), lambda b, i: (0, 0)), vec, vec,
                  seg(0), seg(1), seg(2), prev(1), prev(2), nxt(1), nxt(2),
                  pl.BlockSpec((SC_CONV, d), lambda b, i: (0, 0)),
                  pl.BlockSpec((d, d), lambda b, i: (0, 0))],
        out_specs=[row, row],
        out_shape=[jax.ShapeDtypeStruct((bn, t, d), F32), jax.ShapeDtypeStruct((bn, t, d), BF16)],
        compiler_params=_cparams(("parallel", "parallel")),
        name="outproj_conv",
    )(x, gate, g_next.reshape(1, d), sc_next, sh_next, p, p, p, p, p, p, p, conv_w, weight)


def _ffn_kernel(x_ref, hs_ref, gate_ref, fn_ref, wg_ref, wu_ref, wd_ref, o_ref, *, final_norm):
    j = pl.program_id(2)

    def step(first):
        h = hs_ref[...]
        gate = jnp.dot(h, wg_ref[...], preferred_element_type=F32)
        up = jnp.dot(h, wu_ref[...], preferred_element_type=F32)
        a = (_silu(gate) * up).astype(BF16)
        part = jnp.dot(a, wd_ref[...], preferred_element_type=F32)
        if first:
            o_ref[...] = part
        else:
            o_ref[...] += part

    pl.when(j == 0)(lambda: step(True))
    pl.when(j > 0)(lambda: step(False))

    @pl.when(j == pl.num_programs(2) - 1)
    def _():
        rb = min(NORM_ROW_BLOCK, o_ref.shape[0])

        def body(i, _):
            rows = pl.ds(pl.multiple_of(i * rb, rb), rb)
            y = x_ref[rows, :] + gate_ref[...] * o_ref[rows, :]
            if final_norm:
                y = (y * lax.rsqrt(jnp.mean(y * y, axis=-1, keepdims=True) + NORM_EPS)) * fn_ref[...]
            o_ref[rows, :] = y
            return 0

        lax.fori_loop(0, o_ref.shape[0] // rb, body, 0)


def ffn_grid(x, f, tm=1024, tf=512):
    bn, t, _ = x.shape
    tm = min(tm, t)
    return tm, tf, (bn, t // tm, f // tf)


def ffn_weight_jobs(w_gate, w_up, w_down, layer, grid, tf):
    bn, ni, nj = grid
    d = w_gate.shape[1]
    slab = d // (bn * ni)
    assert slab * bn * ni == d and slab % LANES == 0
    tile = lambda b, i: b * ni + i
    up_job = lambda w: CastJob(w, (None, slab, tf), lambda b, i, j: (layer, tile(b, i), j),
                               w.shape[1:], (slab, tf), lambda b, i, j: (tile(b, i), j))
    down_job = CastJob(w_down, (None, tf, slab), lambda b, i, j: (layer, j, tile(b, i)),
                       w_down.shape[1:], (tf, slab), lambda b, i, j: (j, tile(b, i)))
    return [up_job(w_gate), up_job(w_up), down_job]


def ffn(x, hs, gate, fnorm, w_gate, w_up, w_down, final_norm, casts=()):
    bn, t, d = x.shape
    f = w_gate.shape[1]
    tm, tf, grid = ffn_grid(x, f)
    vec = pl.BlockSpec((None, 1, d), lambda b, i, j: (b, 0, 0))
    one = pl.BlockSpec((1, d), lambda b, i, j: (0, 0))
    row_index = lambda b, i, j: (b, i, 0)
    x_spec, hs_spec = pl.BlockSpec((None, tm, d), row_index), _single((None, tm, d), row_index)
    (y,), rounded = _call_hosting(
        functools.partial(_ffn_kernel, final_norm=final_norm), casts,
        grid=grid,
        in_specs=[
            x_spec, hs_spec,
            vec, one,
            pl.BlockSpec((d, tf), lambda b, i, j: (0, j)),
            pl.BlockSpec((d, tf), lambda b, i, j: (0, j)),
            pl.BlockSpec((tf, d), lambda b, i, j: (j, 0)),
        ],
        args=(x, hs, gate, fnorm.reshape(1, d), w_gate, w_up, w_down),
        out_specs=[pl.BlockSpec((None, tm, d), lambda b, i, j: (b, i, 0))],
        out_shapes=[jax.ShapeDtypeStruct((bn, t, d), F32)],
        compiler_params=_cparams(("parallel", "parallel", "arbitrary")),
        name="ffn",
    )
    return y, rounded


def rope_tables(t):
    pos = np.arange(t)
    row = (pos // GRID_W).astype(np.float32)
    col = (pos % GRID_W).astype(np.float32)
    n_freq = HEAD_DIM // 4
    inv_freq = jnp.asarray(ROPE_THETA, F32) ** (-jnp.arange(n_freq, dtype=F32) / n_freq)
    ang = jnp.concatenate([jnp.asarray(row)[:, None] * inv_freq, jnp.asarray(col)[:, None] * inv_freq], axis=-1)
    cos, sin = jnp.cos(ang), jnp.sin(ang)
    return jnp.concatenate([cos, cos], axis=-1), jnp.concatenate([-sin, sin], axis=-1)


def kernel(x, c, ctx, c_ctx, ada_w, ada_b, norm_mix, norm_ffn, ffn_w_gate, ffn_w_up, ffn_w_down, final_norm,
           ev_w_in, ev_conv, ev_a_log, ev_dt_bias, ev_gdn_norm, ev_rpb, ev_w_out, od_w_in, od_conv, od_w_out):
    bn, t, d = x.shape
    depth = ada_w.shape[0]
    assert depth == 2
    gw = GDN_HEADS * HEAD_DIM
    nw = NA_HEADS * HEAD_DIM
    nh = GDN_HEADS

    cv = jnp.zeros((SUBLANES, d), F32).at[:bn].set(c).at[bn].set(c_ctx)
    mods = ada_modulation(cv, ada_w, ada_b)

    def mod_vecs(l, rows):
        m = mods[l, rows].reshape(-1, 6, d)
        return [m[:, k][:, None, :] for k in range(6)]

    sh1, sc1, g1, sh2, sc2, g2 = mod_vecs(0, slice(0, bn))
    csh1, csc1 = [jnp.broadcast_to(v, (bn, 1, d)) for v in mod_vecs(0, slice(bn, bn + 1))[:2]]
    n_gate = 4 * nh
    seg_gate = 2 * gw + 2 * nw
    w_rows = jnp.swapaxes(ev_w_in[0], 0, 1)
    n_tail = w_rows.shape[0] - seg_gate - n_gate
    n_main = seg_gate + n_tail
    gate_block = seg_gate // LANES
    n_head_steps = bn * nh
    tail_slab = -(-n_tail // (n_head_steps * LANES)) * LANES
    tail_skip = n_head_steps * tail_slab - n_tail
    tail_first = w_rows.shape[0] - n_head_steps * tail_slab
    gparams = jnp.zeros((SUBLANES, LANES), F32)
    gparams = gparams.at[0, :2 * nh].set(ev_a_log[0].reshape(-1)).at[1, :2 * nh].set(ev_dt_bias[0].reshape(-1))
    col_ka, col_va, col_kb, col_vb, col_qa, col_qb, col_za = [k * nh for k in range(7)]

    head_step = lambda b, h: b * nh + h
    pc, gates_c = proj(ctx, norm_mix[0], csc1, csh1, w_rows, 0, seg_gate, None, seg_gate, w_rows, gate_block,
                       gparams, rows_major=True, tn=1024)
    s_ctx, (w_head, w_tail) = gdn_context(
        pc, ev_conv[0], gate_rows(gates_c, ctx.shape[1]), col_ka, col_va,
        casts=[transposing_job(w_rows, 0, seg_gate, n_head_steps, head_step),
               transposing_job(w_rows, tail_first, n_head_steps * tail_slab, n_head_steps, head_step)])

    p, gates = proj(x, norm_mix[0], sc1, sh1, w_head[None], 0, seg_gate, w_tail, n_main, w_rows, gate_block,
                    gparams, tail_skip=tail_skip)
    cosf, sinf = rope_tables(t)
    y_gdn, (w_out0,) = gdn_latent(p, ev_conv[0], gate_rows(gates, t), cosf, sinf, s_ctx, ev_gdn_norm[0],
                                  col_ka, col_va, col_qa, col_za,
                                  casts=[row_slab_job(ev_w_out, 0, n_head_steps, head_step)])
    y_na, rounded = natten(p, pc, natten_bias(ev_rpb[0]), col_qb, col_kb, col_vb, col_kb, col_vb,
                           casts=[row_slab_job(w, 0, n_head_steps, head_step)
                                  for w in (ffn_w_gate, ffn_w_up, ffn_w_down, od_w_in, od_w_out)])
    ffn0, (w_in1, w_out1) = rounded[:3], rounded[3:]
    x_lat, hs = outproj(x, g1, [y_gdn, y_na], w_out0, norm_ffn[0], sc2, sh2)
    _, tf, grid = ffn_grid(x_lat, ffn_w_gate.shape[2])
    x_lat, ffn1 = ffn(x_lat, hs, g2, final_norm, *ffn0, final_norm=False,
                      casts=ffn_weight_jobs(ffn_w_gate, ffn_w_up, ffn_w_down, 1, grid, tf))

    sh1, sc1, g1, sh2, sc2, g2 = mod_vecs(1, slice(0, bn))
    p = proj(x_lat, norm_mix[1], sc1, sh1, w_in1[None], 0, 3 * d, None, 3 * d)
    x_lat, hs = outproj_conv(x_lat, g1, p, od_conv[0], w_out1, norm_ffn[1], sc2, sh2)
    x_lat, _ = ffn(x_lat, hs, g2, final_norm, *ffn1, final_norm=True)
    return x_lat
```

```python
import functools

import jax
import jax.numpy as jnp
import numpy as np
from jax import lax
from jax.experimental import pallas as pl
from jax.experimental.pallas import tpu as pltpu

F32 = jnp.float32
BF16 = jnp.bfloat16
HIGHEST = lax.Precision.HIGHEST

LANES = 128
SUBLANES = 8
VMEM_LIMIT = 60 * 1024 * 1024

GRID_W = 64
HEAD_DIM = 128
GDN_HEADS = 8
NA_HEADS = 8
GDN_CONV = 5
CHUNK = 64
INVERSE_LEVELS = 6
GROUP = 2 * CHUNK
PREP_GROUPS_PER_ITER = 4
GDN_ROW_BLOCK = 1024
NA_ROWS_PER_ITER = 16
NA_KH = 8
NA_KW = 16
SC_CONV = 3
ROPE_THETA = 10000.0
NORM_EPS = 1e-6
NEG_BIG = -1e30

NT_DIMS = (((1,), (1,)), ((), ()))


def _cparams(sem):
    return pltpu.CompilerParams(dimension_semantics=sem, vmem_limit_bytes=VMEM_LIMIT)


def _sigmoid(x):
    return 1.0 / (1.0 + jnp.exp(-x))


def _silu(x):
    return x * _sigmoid(x)


def _softplus(x):
    return jnp.maximum(x, 0.0) + jnp.log(1.0 + jnp.exp(-jnp.abs(x)))


def _mm(a, b):
    return jnp.dot(a.astype(BF16), b.astype(BF16), preferred_element_type=F32)


def _mm_nt(a, b):
    return lax.dot_general(a.astype(BF16), b.astype(BF16), NT_DIMS, preferred_element_type=F32)


def _mm_exact(a, b):
    return jnp.dot(a, b, precision=HIGHEST, preferred_element_type=F32)


class CastJob:
    def __init__(self, src, in_block, in_index, out_shape, out_block, out_index, transpose=False,
                 row_offset=0, next_rows=None):
        self.transpose = transpose
        self.row_offset = row_offset
        self.srcs = [src]
        self.in_specs = [pl.BlockSpec(in_block, in_index)]
        if row_offset:
            self.srcs.append(src)
            self.in_specs.append(pl.BlockSpec(*next_rows))
        self.out_spec = pl.BlockSpec(out_block, out_index)
        self.out_shape = jax.ShapeDtypeStruct(out_shape, BF16)


def _hosting_casts(body, n_in, n_out, casts):
    n_cast_in = sum(len(job.srcs) for job in casts)

    def kernel(*refs):
        ins, rest = refs[:n_in], refs[n_in:]
        cast_in, rest = list(rest[:n_cast_in]), rest[n_cast_in:]
        outs, rest = rest[:n_out], rest[n_out:]
        cast_out, scratch = rest[:len(casts)], rest[len(casts):]
        for job, dst in zip(casts, cast_out):
            src = cast_in.pop(0)
            block = src[...]
            if job.row_offset:
                block = jnp.concatenate([src[job.row_offset:, :], cast_in.pop(0)[...]], axis=0)
            dst[...] = (block.T if job.transpose else block).astype(BF16)
        body(*ins, *outs, *scratch)

    return kernel


def row_slab_job(w, layer, n_steps, step_of):
    rows, cols = w.shape[1:]
    slab = rows // n_steps
    assert slab * n_steps == rows and slab % (2 * SUBLANES) == 0
    return CastJob(w, (None, slab, cols), lambda *idx: (layer, step_of(*idx), 0),
                   (rows, cols), (slab, cols), lambda *idx: (step_of(*idx), 0))


def transposing_job(w_rows, first_row, n_rows, n_steps, step_of):
    d = w_rows.shape[1]
    slab = n_rows // n_steps
    assert slab * n_steps == n_rows and slab % LANES == 0
    first_block, offset = divmod(first_row, slab)
    assert offset % SUBLANES == 0
    next_rows = None
    if offset:
        assert slab % offset == 0
        per = slab // offset
        next_rows = ((offset, d), lambda *idx: ((first_block + step_of(*idx) + 1) * per, 0))
    return CastJob(w_rows, (slab, d), lambda *idx: (first_block + step_of(*idx), 0),
                   (d, n_rows), (d, slab), lambda *idx: (0, step_of(*idx)), transpose=True,
                   row_offset=offset, next_rows=next_rows)


def _call_hosting(body, casts, grid, in_specs, args, out_specs, out_shapes, **kwargs):
    n_in, n_out = len(in_specs), len(out_specs)
    casts = list(casts)
    res = pl.pallas_call(
        _hosting_casts(body, n_in, n_out, casts),
        grid=grid,
        in_specs=list(in_specs) + [s for c in casts for s in c.in_specs],
        out_specs=list(out_specs) + [c.out_spec for c in casts],
        out_shape=list(out_shapes) + [c.out_shape for c in casts],
        **kwargs,
    )(*args, *[s for c in casts for s in c.srcs])
    return res[:n_out], res[n_out:]


def _ada_kernel(cv_ref, w_ref, b_ref, o_ref):
    s = _silu(cv_ref[...])
    o_ref[...] = _mm(s, w_ref[...]) + b_ref[...]


def ada_modulation(cv, ada_w, ada_b, tn=1024):
    depth, d, n = ada_w.shape
    return pl.pallas_call(
        _ada_kernel,
        grid=(depth, n // tn),
        in_specs=[
            pl.BlockSpec((SUBLANES, d), lambda l, j: (0, 0)),
            pl.BlockSpec((None, d, tn), lambda l, j: (l, 0, j)),
            pl.BlockSpec((None, 1, tn), lambda l, j: (l, 0, j)),
        ],
        out_specs=pl.BlockSpec((None, SUBLANES, tn), lambda l, j: (l, 0, j)),
        out_shape=jax.ShapeDtypeStruct((depth, SUBLANES, n), F32),
        compiler_params=_cparams(("parallel", "parallel")),
        name="ada",
    )(cv, ada_w, ada_b.reshape(depth, 1, n))


def _modulated_norm(x, g, sc, sh):
    y = x * lax.rsqrt(jnp.mean(x * x, axis=-1, keepdims=True) + NORM_EPS)
    return (y * g) * (1.0 + sc) + sh


NORM_ROW_BLOCK = 128


def _modulated_norm_rows(x_ref, g_ref, sc_ref, sh_ref, hs_ref):
    rb = min(NORM_ROW_BLOCK, x_ref.shape[0])

    def body(i, _):
        rows = pl.ds(pl.multiple_of(i * rb, rb), rb)
        hs_ref[rows, :] = _modulated_norm(x_ref[rows, :], g_ref[...], sc_ref[...], sh_ref[...]).astype(BF16)
        return 0

    lax.fori_loop(0, x_ref.shape[0] // rb, body, 0)


def _proj_kernel(*refs, with_gates, with_tail, head_blocks, rows_major):
    x_ref, g_ref, sc_ref, sh_ref, w_ref = refs[:5]
    refs = refs[5:]
    wt_ref = None
    if with_tail:
        wt_ref, refs = refs[0], refs[1:]
    if with_gates:
        wab_ref, gp_ref, o_ref, gate_ref, hs_ref, wb_ref = refs
    else:
        o_ref, hs_ref, wb_ref = refs
    j = pl.program_id(2)

    @pl.when(j == 0)
    def _():
        if with_gates:
            hb = _modulated_norm(x_ref[...], g_ref[...], sc_ref[...], sh_ref[...]).astype(BF16)
            hs_ref[...] = hb
            a = _mm_nt(hb, wab_ref[...])
            neg_decay_rate = -jnp.exp(gp_ref[0:1, :])
            g = neg_decay_rate * _softplus(a + gp_ref[1:2, :])
            lane = lax.broadcasted_iota(jnp.int32, a.shape, 1)
            gate_ref[...] = jnp.where(lane < 2 * GDN_HEADS, g, _sigmoid(a))
        else:
            _modulated_norm_rows(x_ref, g_ref, sc_ref, sh_ref, hs_ref)

    def emit(wref):
        if wref.dtype != BF16:
            wb_ref[...] = wref[...].astype(BF16)
            wref = wb_ref
        if rows_major:
            y = lax.dot_general(hs_ref[...], wref[...], NT_DIMS, preferred_element_type=F32)
        else:
            y = jnp.dot(hs_ref[...], wref[...], preferred_element_type=F32)
        o_ref[...] = y.astype(o_ref.dtype)

    pl.when(j < head_blocks)(lambda: emit(w_ref))
    if with_tail:
        pl.when(j >= head_blocks)(lambda: emit(wt_ref))


def proj(x, g, sc, sh, w, layer, head_cols, w_tail, n_out, wab=None, gate_block=0, gparams=None,
         rows_major=False, tail_skip=0, tm=1024, tn=2048):
    bn, t, d = x.shape
    tm = min(tm, t)
    while n_out % tn or head_cols % tn or tail_skip % tn:
        tn //= 2
    head_blocks = head_cols // tn
    with_gates = wab is not None
    vec = pl.BlockSpec((None, 1, d), lambda b, i, j: (b, 0, 0))
    if rows_major:
        w_spec = pl.BlockSpec((tn, d), lambda b, i, j: (jnp.minimum(j, head_blocks - 1), 0))
    else:
        w_spec = pl.BlockSpec((None, d, tn), lambda b, i, j: (layer, 0, jnp.minimum(j, head_blocks - 1)))
    in_specs = [
        pl.BlockSpec((None, tm, d), lambda b, i, j: (b, i, 0)),
        pl.BlockSpec((1, d), lambda b, i, j: (0, 0)),
        vec, vec,
        w_spec,
    ]
    args = [x, g.reshape(1, d), sc, sh, w]
    with_tail = w_tail is not None
    if with_tail:
        skip = tail_skip // tn
        in_specs.append(pl.BlockSpec((d, tn), lambda b, i, j: (0, jnp.maximum(j - head_blocks, 0) + skip)))
        args.append(w_tail)
    out_specs = [pl.BlockSpec((None, tm, tn), lambda b, i, j: (b, i, j))]
    out_shape = [jax.ShapeDtypeStruct((bn, t, n_out), BF16)]
    if with_gates:
        in_specs += [pl.BlockSpec((LANES, d), lambda b, i, j: (gate_block, 0)),
                     pl.BlockSpec((SUBLANES, LANES), lambda b, i, j: (0, 0))]
        args += [wab, gparams]
        out_specs.append(pl.BlockSpec((None, tm, LANES), lambda b, i, j: (b, i, 0)))
        out_shape.append(jax.ShapeDtypeStruct((bn, t, LANES), F32))
    res = pl.pallas_call(
        functools.partial(_proj_kernel, with_gates=with_gates, with_tail=with_tail, head_blocks=head_blocks,
                          rows_major=rows_major),
        grid=(bn, t // tm, n_out // tn),
        in_specs=in_specs,
        out_specs=out_specs,
        out_shape=out_shape,
        scratch_shapes=[pltpu.VMEM((tm, d), BF16), pltpu.VMEM((tn, d) if rows_major else (d, tn), BF16)],
        compiler_params=_cparams(("parallel", "parallel", "arbitrary")),
        name="proj",
    )(*args)
    return res if with_gates else res[0]


CONV_HALO = SUBLANES


def _stage_padded(pad_ref, load_rows, t, rb):
    zeros = jnp.zeros((CONV_HALO, LANES), F32)
    pad_ref[0:CONV_HALO, :] = zeros
    pad_ref[CONV_HALO + t:2 * CONV_HALO + t, :] = zeros

    def body(i, _):
        r0 = pl.multiple_of(i * rb, rb)
        pad_ref[pl.ds(CONV_HALO + r0, rb), :] = load_rows(r0)
        return 0

    lax.fori_loop(0, t // rb, body, 0)


def _conv_rows(pad_ref, w_ref, r0, rb, taps):
    acc = None
    for j in range(taps):
        xj = pad_ref[pl.ds(r0 + (CONV_HALO + j - taps // 2), rb), :]
        term = xj * w_ref[j:j + 1, :]
        acc = term if acc is None else acc + term
    return acc


def _l2norm(x):
    return x * lax.rsqrt(jnp.sum(x * x, axis=-1, keepdims=True) + NORM_EPS)


def _gdn_kernel(*refs, t, with_q):
    nc = t // CHUNK
    rb = min(GDN_ROW_BLOCK, t)
    if with_q:
        (ka_ref, va_ref, qa_ref, za_ref, wk_ref, wv_ref, wq_ref, grow_ref, cos_ref, sin_ref, s0_ref,
         gn_ref, y_ref,
         pad_ref, pad2_ref, k_ref, v_ref, q_ref, o_ref, gc_ref, kwq_s, h_s, au_s, gl_s) = refs
    else:
        (ka_ref, va_ref, wk_ref, wv_ref, grow_ref, sfin_ref,
         pad_ref, k_ref, v_ref, gc_ref, kwq_s, h_s, gl_s) = refs

    def conv_all(src_ref, w_ref, finish, dst_ref):
        _stage_padded(pad_ref, lambda r0: src_ref[pl.ds(r0, rb), :].astype(F32), t, rb)

        def body(i, _):
            r0 = pl.multiple_of(i * rb, rb)
            y = _silu(_conv_rows(pad_ref, w_ref, r0, rb, GDN_CONV))
            dst_ref[pl.ds(r0, rb), :] = finish(y, r0)
            return 0

        lax.fori_loop(0, t // rb, body, 0)

    def rope(x, r0):
        return (x * cos_ref[pl.ds(r0, rb), :]
                + pltpu.roll(x, HEAD_DIM // 2, axis=1) * sin_ref[pl.ds(r0, rb), :])

    if with_q:
        _stage_padded(pad_ref, lambda r0: ka_ref[pl.ds(r0, rb), :].astype(F32), t, rb)
        _stage_padded(pad2_ref, lambda r0: qa_ref[pl.ds(r0, rb), :].astype(F32), t, rb)

        def kq_body(i, _):
            r0 = pl.multiple_of(i * rb, rb)
            yk = _silu(_conv_rows(pad_ref, wk_ref, r0, rb, GDN_CONV))
            yq = _silu(_conv_rows(pad2_ref, wq_ref, r0, rb, GDN_CONV))
            k_ref[pl.ds(r0, rb), :] = rope(_l2norm(yk), r0)
            q_ref[pl.ds(r0, rb), :] = rope(_l2norm(yq), r0) * HEAD_DIM ** -0.5
            return 0

        lax.fori_loop(0, t // rb, kq_body, 0)
    else:
        conv_all(ka_ref, wk_ref, lambda y, r0: _l2norm(y), k_ref)
    conv_all(va_ref, wv_ref, lambda y, r0: y, v_ref)

    ii = lax.broadcasted_iota(jnp.int32, (GROUP, GROUP), 0)
    jj = lax.broadcasted_iota(jnp.int32, (GROUP, GROUP), 1)
    same = (ii // CHUNK) == (jj // CHUNK)
    lower = same & (ii >= jj)
    upper = same & (ii <= jj)
    eye = ii == jj
    gc_ref[0] = _mm_exact(grow_ref[0], upper.astype(F32))
    gc_ref[1] = _mm_exact(grow_ref[1], lower.astype(F32))

    n_groups = t // GROUP
    gpi = min(PREP_GROUPS_PER_ITER, n_groups)
    n_sets = n_groups // gpi
    steps_per_set = 2 * gpi

    def prep_stages(i):
        chains = ([(i * gpi + k, 0) for k in range(gpi)]
                  + [(n_groups - 1 - (i * gpi + k), 1) for k in range(gpi)])
        ids = range(len(chains))
        st = {}

        def load():
            for ch, (g, d) in enumerate(chains):
                r0 = pl.multiple_of(g * GROUP, GROUP)
                st["k", ch] = k_ref[pl.ds(r0, GROUP), :]
                st["v", ch] = v_ref[pl.ds(r0, GROUP), :]
                kgb = st["k", ch].astype(BF16)
                if with_q:
                    st["q", ch] = q_ref[pl.ds(r0, GROUP), :]
                    both = _mm_nt(jnp.concatenate([st["k", ch], st["q", ch]], axis=0), kgb)
                    st["kk", ch], st["qk", ch] = both[:GROUP], both[GROUP:]
                else:
                    st["kk", ch] = _mm_nt(kgb, kgb)

        def masks():
            for ch, (g, d) in enumerate(chains):
                incl = lower if d == 0 else upper
                gc_r = jnp.broadcast_to(gc_ref[d, pl.ds(g, 1), :], (GROUP, GROUP))
                be_r = jnp.broadcast_to(grow_ref[2 + d, pl.ds(g, 1), :], (GROUP, GROUP))
                gc_c = gc_r.T
                decay = jnp.where(incl, jnp.exp(jnp.where(incl, gc_c - gc_r, 0.0)), 0.0)
                m = jnp.where(incl & jnp.logical_not(eye), st["kk", ch] * be_r.T * decay, 0.0)
                st["be_r", ch], st["gc_c", ch], st["decay", ch] = be_r, gc_c, decay
                st["p", ch] = jnp.where(eye, 1.0, -m)
                st["sq", ch] = _mm(m, m)

        def level():
            for ch in ids:
                p, sq = st["p", ch], st["sq", ch]
                st["p", ch] = p + _mm(p, sq)
                st["sq", ch] = _mm(sq, sq)

        def last_level():
            for ch in ids:
                st["p", ch] = st["p", ch] + _mm(st["p", ch], st["sq", ch])

        def solve():
            for ch in ids:
                rhs = jnp.concatenate([st["v", ch], st["k", ch] * jnp.exp(st["gc_c", ch])], axis=1)
                st["uw", ch] = _mm(st["p", ch] * st["be_r", ch], rhs)

        def fold():
            for ch, (g, d) in enumerate(chains):
                gc_c = st["gc_c", ch]
                last = [CHUNK - 1, GROUP - 1] if d == 0 else [0, CHUNK]
                gls = [gc_c[r:r + 1, :] for r in last]
                gl_c = jnp.concatenate([jnp.broadcast_to(x, (CHUNK, LANES)) for x in gls], axis=0)
                kdt = (st["k", ch] * jnp.exp(gl_c - gc_c)).T
                parts = [jnp.where(jj < CHUNK, kdt, 0.0), jnp.where(jj >= CHUNK, kdt, 0.0)]
                if with_q:
                    st["qd", ch] = st["q", ch] * jnp.exp(gc_c)
                    parts = [st["qk", ch] * st["decay", ch]] + parts
                st["gl", ch] = gls
                st["prod", ch] = _mm(jnp.concatenate(parts, axis=0), st["uw", ch])

        def store():
            for ch, (g, d) in enumerate(chains):
                r = st["prod", ch]
                off = GROUP if with_q else 0
                for half in range(2):
                    c = 2 * g + half
                    blk = r[off + half * HEAD_DIM:off + (half + 1) * HEAD_DIM]
                    h_s[d, c] = blk[:, :HEAD_DIM]
                    gl_s[d, c] = jnp.broadcast_to(jnp.exp(st["gl", ch][half]), (SUBLANES, LANES))
                    if with_q:
                        rs = slice(half * CHUNK, (half + 1) * CHUNK)
                        au_s[d, c] = r[rs, :HEAD_DIM]
                        qw = st["qd", ch][rs] - r[rs, HEAD_DIM:]
                        kwq_s[d, c] = jnp.concatenate([blk[:, HEAD_DIM:], qw], axis=0).astype(BF16)
                    else:
                        kwq_s[d, c] = blk[:, HEAD_DIM:].astype(BF16)

        return [load, masks] + [level] * (INVERSE_LEVELS - 2) + [last_level, solve, fold, store]

    def scan_step(s, carry):
        cs = (s, nc - 1 - s)
        rr = [jnp.dot(kwq_s[d, cs[d]], carry[d].astype(BF16), preferred_element_type=F32) for d in range(2)]
        new = []
        for d in range(2):
            c = cs[d]
            new.append(carry[d] * gl_s[d, c][0:1, :] + h_s[d, c] - rr[d][:HEAD_DIM])
            if with_q:
                r0 = pl.multiple_of(c * CHUNK, CHUNK)
                o_ref[pl.ds(r0, CHUNK), :] += rr[d][HEAD_DIM:] + au_s[d, c]
        return tuple(new)

    def run(stages, steps, carry):
        for idx in range(max(len(stages), len(steps))):
            if idx < len(stages):
                stages[idx]()
            if idx < len(steps):
                carry = scan_step(steps[idx], carry)
        return carry

    def set_steps(i):
        return [i * steps_per_set + k for k in range(steps_per_set)]

    if with_q:
        o_ref[...] = jnp.zeros_like(o_ref)
        carry = (s0_ref[0], s0_ref[1])
    else:
        zero = jnp.zeros((HEAD_DIM, HEAD_DIM), F32)
        carry = (zero, zero)
    run(prep_stages(0), [], carry)
    carry = lax.fori_loop(1, n_sets, lambda i, c: run(prep_stages(i), set_steps(i - 1), c), carry)
    s_f, s_b = run([], set_steps(n_sets - 1), carry)

    if not with_q:
        sfin_ref[0] = s_f
        sfin_ref[1] = s_b
        return

    def finish(i, _):
        r0 = pl.multiple_of(i * rb, rb)
        o = o_ref[pl.ds(r0, rb), :]
        y = o * lax.rsqrt(jnp.mean(o * o, axis=-1, keepdims=True) + NORM_EPS)
        y_ref[pl.ds(r0, rb), :] = ((y * gn_ref[...]) * _silu(za_ref[pl.ds(r0, rb), :].astype(F32))).astype(y_ref.dtype)
        return 0

    lax.fori_loop(0, t // rb, finish, 0)


def _single(block_shape, index_map):
    return pl.BlockSpec(block_shape, index_map, pipeline_mode=pl.Buffered(1))


def gdn_latent(p, conv_w, grow, cosf, sinf, s0, gnorm, col_k, col_v, col_q, col_z, casts=()):
    bn, t, _ = p.shape
    nc = t // CHUNK
    ngp = grow.shape[3]
    hd = HEAD_DIM
    seq = lambda col: pl.BlockSpec((None, t, hd), lambda b, h, col=col: (b, 0, col + h))
    cw = lambda col: pl.BlockSpec((GDN_CONV, hd), lambda b, h, col=col: (0, col + h))
    (y,), rounded = _call_hosting(
        lambda *refs: _gdn_kernel(*refs, t=t, with_q=True), casts,
        grid=(bn, GDN_HEADS),
        in_specs=[
            seq(col_k), seq(col_v), seq(col_q), seq(col_z),
            cw(0), cw(GDN_HEADS), cw(2 * GDN_HEADS),
            pl.BlockSpec((None, None, 4, ngp, GROUP), lambda b, h: (b, h, 0, 0, 0)),
            _single((t, hd), lambda b, h: (0, 0)),
            _single((t, hd), lambda b, h: (0, 0)),
            pl.BlockSpec((None, None, 2, hd, hd), lambda b, h: (b, h, 0, 0, 0)),
            pl.BlockSpec((1, hd), lambda b, h: (0, 0)),
        ],
        args=(p, p, p, p, conv_w, conv_w, conv_w, grow, cosf, sinf, s0, gnorm.reshape(1, hd)),
        out_specs=[pl.BlockSpec((None, t, hd), lambda b, h: (b, 0, h))],
        out_shapes=[jax.ShapeDtypeStruct((bn, t, GDN_HEADS * hd), BF16)],
        scratch_shapes=[
            pltpu.VMEM((t + 2 * CONV_HALO, hd), F32),
            pltpu.VMEM((t + 2 * CONV_HALO, hd), F32),
            pltpu.VMEM((t, hd), F32),
            pltpu.VMEM((t, hd), F32),
            pltpu.VMEM((t, hd), F32),
            pltpu.VMEM((t, hd), F32),
            pltpu.VMEM((2, ngp, GROUP), F32),
            pltpu.VMEM((2, nc, hd + CHUNK, hd), BF16),
            pltpu.VMEM((2, nc, hd, hd), F32),
            pltpu.VMEM((2, nc, CHUNK, hd), F32),
            pltpu.VMEM((2, nc, SUBLANES, LANES), F32),
        ],
        compiler_params=_cparams(("parallel", "parallel")),
        name="gdn_latent",
    )
    return y, rounded


def gdn_context(pc, conv_w, grow, col_k, col_v, casts=()):
    bn, t, _ = pc.shape
    nc = t // CHUNK
    ngp = grow.shape[3]
    hd = HEAD_DIM
    seq = lambda col: pl.BlockSpec((None, t, hd), lambda b, h, col=col: (b, 0, col + h))
    cw = lambda col: pl.BlockSpec((GDN_CONV, hd), lambda b, h, col=col: (0, col + h))
    (states,), rounded = _call_hosting(
        lambda *refs: _gdn_kernel(*refs, t=t, with_q=False), casts,
        grid=(bn, GDN_HEADS),
        in_specs=[
            seq(col_k), seq(col_v), cw(0), cw(GDN_HEADS),
            pl.BlockSpec((None, None, 4, ngp, GROUP), lambda b, h: (b, h, 0, 0, 0)),
        ],
        args=(pc, pc, conv_w, conv_w, grow),
        out_specs=[pl.BlockSpec((None, None, 2, hd, hd), lambda b, h: (b, h, 0, 0, 0))],
        out_shapes=[jax.ShapeDtypeStruct((bn, GDN_HEADS, 2, hd, hd), F32)],
        scratch_shapes=[
            pltpu.VMEM((t + 2 * CONV_HALO, hd), F32),
            pltpu.VMEM((t, hd), F32),
            pltpu.VMEM((t, hd), F32),
            pltpu.VMEM((2, ngp, GROUP), F32),
            pltpu.VMEM((2, nc, hd, hd), BF16),
            pltpu.VMEM((2, nc, hd, hd), F32),
            pltpu.VMEM((2, nc, SUBLANES, LANES), F32),
        ],
        compiler_params=_cparams(("parallel", "parallel")),
        name="gdn_context",
    )
    return states, rounded


def gate_rows(gates, t):
    bn = gates.shape[0]
    ng = t // GROUP
    g = gates[:, :, :4 * GDN_HEADS].reshape(bn, ng, GROUP, 4, GDN_HEADS)
    g = jnp.transpose(g, (0, 4, 3, 1, 2))
    if ng < SUBLANES:
        g = jnp.pad(g, ((0, 0), (0, 0), (0, 0), (0, SUBLANES - ng), (0, 0)))
    return g


def _natten_kernel(q_ref, k_ref, v_ref, kc_ref, vc_ref, by_dr_ref, o_ref, bias_ref, *, rows):
    kcb = kc_ref[...]
    vcb = vc_ref[...]
    scale = HEAD_DIM ** -0.5
    kh = min(NA_KH, rows)
    win = kh * GRID_W
    per_iter = min(NA_ROWS_PER_ITER, rows)

    for cls in range(NA_KH):
        for j in range(kh):
            bias_ref[cls, :, j * GRID_W:(j + 1) * GRID_W] = by_dr_ref[NA_KH - 1 - cls + j]

    def body(it, _):
        rws = [it * per_iter + k for k in range(per_iter)]
        q0s, k0s, s_loc, s_ctx = [], [], [], []
        for r in rws:
            rs = jnp.clip(r - NA_KH // 2, 0, rows - NA_KH)
            q0s.append(pl.multiple_of(r * GRID_W, GRID_W))
            k0s.append(pl.multiple_of(rs * GRID_W, GRID_W))
            q = q_ref[pl.ds(q0s[-1], GRID_W), :]
            kw = k_ref[pl.ds(k0s[-1], win), :]
            s_loc.append(lax.dot_general(q, kw, NT_DIMS, preferred_element_type=F32) * scale + bias_ref[r - rs])
            s_ctx.append(lax.dot_general(q, kcb, NT_DIMS, preferred_element_type=F32) * scale)
        p_loc, p_ctx, den = [], [], []
        for a, b in zip(s_loc, s_ctx):
            m = jnp.maximum(jnp.max(a, axis=-1, keepdims=True), jnp.max(b, axis=-1, keepdims=True))
            p_loc.append(jnp.exp(a - m))
            p_ctx.append(jnp.exp(b - m))
            den.append(jnp.sum(p_loc[-1], axis=-1, keepdims=True) + jnp.sum(p_ctx[-1], axis=-1, keepdims=True))
        outs = []
        for k in range(per_iter):
            vw = v_ref[pl.ds(k0s[k], win), :]
            outs.append(jnp.dot(p_loc[k].astype(BF16), vw, preferred_element_type=F32)
                        + jnp.dot(p_ctx[k].astype(BF16), vcb, preferred_element_type=F32))
        for k in range(per_iter):
            o_ref[pl.ds(q0s[k], GRID_W), :] = (outs[k] / den[k]).astype(o_ref.dtype)
        return 0

    lax.fori_loop(0, rows // per_iter, body, 0)


def natten(p, pc, by_dr, col_q, col_k, col_v, ctx_col_k, ctx_col_v, casts=()):
    bn, t, _ = p.shape
    lc = pc.shape[1]
    hd = HEAD_DIM
    rows = t // GRID_W
    kh = min(NA_KH, rows)
    seq = lambda col: pl.BlockSpec((None, t, hd), lambda b, h, col=col: (b, 0, col + h))
    cseq = lambda col: pl.BlockSpec((None, lc, hd), lambda b, h, col=col: (b, 0, col + h))
    (y,), rounded = _call_hosting(
        lambda *refs: _natten_kernel(*refs, rows=rows), casts,
        grid=(bn, NA_HEADS),
        in_specs=[
            seq(col_q), seq(col_k), seq(col_v), cseq(ctx_col_k), cseq(ctx_col_v),
            pl.BlockSpec((None,) + by_dr.shape[1:], lambda b, h: (h, 0, 0, 0)),
        ],
        args=(p, p, p, pc, pc, by_dr),
        out_specs=[pl.BlockSpec((None, t, hd), lambda b, h: (b, 0, h))],
        out_shapes=[jax.ShapeDtypeStruct((bn, t, NA_HEADS * hd), BF16)],
        scratch_shapes=[pltpu.VMEM((NA_KH, GRID_W, kh * GRID_W), F32)],
        compiler_params=_cparams(("parallel", "parallel")),
        name="natten",
    )
    return y, rounded


def natten_bias(rpb):
    hn, n_dr, n_dc = rpb.shape
    col = np.arange(GRID_W)
    col_start = np.clip(col - NA_KW // 2, 0, GRID_W - NA_KW)
    in_win = (col[None, :] >= col_start[:, None]) & (col[None, :] < col_start[:, None] + NA_KW)
    dc = np.clip(col[None, :] - col[:, None], -(NA_KW - 1), NA_KW - 1) + (NA_KW - 1)
    onehot = (np.arange(n_dc)[:, None] == dc.reshape(-1)[None, :]).astype(np.float32)
    by_dr = jnp.dot(rpb.astype(F32).reshape(hn * n_dr, n_dc), jnp.asarray(onehot), precision=HIGHEST)
    return jnp.where(in_win[None, None], by_dr.reshape(hn, n_dr, GRID_W, GRID_W), NEG_BIG)


def _outproj_kernel(*refs, n_in):
    x_ref, gate_ref, g_ref, sc_ref, sh_ref = refs[:5]
    a_refs = refs[5:5 + n_in]
    w_refs = refs[5 + n_in:5 + 2 * n_in]
    o_ref, hs_ref = refs[5 + 2 * n_in:]
    y = None
    for a_ref, w_ref in zip(a_refs, w_refs):
        part = jnp.dot(a_ref[...], w_ref[...], preferred_element_type=F32)
        y = part if y is None else y + part
    x1 = x_ref[...] + gate_ref[...] * y
    o_ref[...] = x1
    hs_ref[...] = _modulated_norm(x1, g_ref[...], sc_ref[...], sh_ref[...]).astype(BF16)


def outproj(x, gate, acts, weight, g_next, sc_next, sh_next, tm=512):
    bn, t, d = x.shape
    n_in = len(acts)
    kdim = acts[0].shape[-1]
    assert all(a.shape[-1] == kdim for a in acts) and weight.shape[0] == n_in * kdim
    weights = [weight] * n_in
    row = pl.BlockSpec((None, tm, d), lambda b, i: (b, i, 0))
    vec = pl.BlockSpec((None, 1, d), lambda b, i: (b, 0, 0))
    in_specs = [row, vec, pl.BlockSpec((1, d), lambda b, i: (0, 0)), vec, vec]
    in_specs += [pl.BlockSpec((None, tm, kdim), lambda b, i: (b, i, 0)) for a in acts]
    in_specs += [pl.BlockSpec((kdim, d), lambda b, i, k=k: (k, 0)) for k in range(n_in)]
    return pl.pallas_call(
        functools.partial(_outproj_kernel, n_in=n_in),
        grid=(bn, t // tm),
        in_specs=in_specs,
        out_specs=[row, row],
        out_shape=[jax.ShapeDtypeStruct((bn, t, d), F32), jax.ShapeDtypeStruct((bn, t, d), BF16)],
        compiler_params=_cparams(("parallel", "parallel")),
        name="outproj",
    )(x, gate, g_next.reshape(1, d), sc_next, sh_next, *acts, *weights)


def _outproj_conv_kernel(x_ref, gate_ref, g_ref, sc_ref, sh_ref, gb_ref, gc_ref, val_ref,
                         gcp_ref, valp_ref, gcn_ref, valn_ref, cw_ref, w_ref, o_ref, hs_ref):
    i = pl.program_id(1)
    tm = x_ref.shape[0]
    u = gc_ref[...].astype(F32) * val_ref[...].astype(F32)
    last = gcp_ref.shape[0] - 1
    u_prev = gcp_ref[last:last + 1, :].astype(F32) * valp_ref[last:last + 1, :].astype(F32)
    u_next = gcn_ref[0:1, :].astype(F32) * valn_ref[0:1, :].astype(F32)
    u_prev = jnp.where(i == 0, 0.0, u_prev)
    u_next = jnp.where(i == pl.num_programs(1) - 1, 0.0, u_next)
    row = lax.broadcasted_iota(jnp.int32, (tm, 1), 0)
    below = jnp.where(row == 0, u_prev, pltpu.roll(u, 1, axis=0))
    above = jnp.where(row == tm - 1, u_next, pltpu.roll(u, tm - 1, axis=0))
    conv = below * cw_ref[0:1, :] + u * cw_ref[1:2, :] + above * cw_ref[2:3, :]
    z = (gb_ref[...].astype(F32) * conv).astype(BF16)
    x1 = x_ref[...] + gate_ref[...] * jnp.dot(z, w_ref[...], preferred_element_type=F32)
    o_ref[...] = x1
    hs_ref[...] = _modulated_norm(x1, g_ref[...], sc_ref[...], sh_ref[...]).astype(BF16)


def outproj_conv(x, gate, p, conv_w, weight, g_next, sc_next, sh_next, tm=512):
    bn, t, d = x.shape
    halo = 2 * SUBLANES
    nb = t // halo
    row = pl.BlockSpec((None, tm, d), lambda b, i: (b, i, 0))
    vec = pl.BlockSpec((None, 1, d), lambda b, i: (b, 0, 0))
    seg = lambda k: pl.BlockSpec((None, tm, d), lambda b, i, k=k: (b, i, k))
    prev = lambda k: pl.BlockSpec((None, halo, d), lambda b, i, k=k: (b, jnp.maximum(i * (tm // halo) - 1, 0), k))
    nxt = lambda k: pl.BlockSpec((None, halo, d), lambda b, i, k=k: (b, jnp.minimum((i + 1) * (tm // halo), nb - 1), k))
    return pl.pallas_call(
        _outproj_conv_kernel,
        grid=(bn, t // tm),
        in_specs=[row, vec, pl.BlockSpec((1, d), lambda b, i: (0, 0)), vec, vec,
                  seg(0), seg(1), seg(2), prev(1), prev(2), nxt(1), nxt(2),
                  pl.BlockSpec((SC_CONV, d), lambda b, i: (0, 0)),
                  pl.BlockSpec((d, d), lambda b, i: (0, 0))],
        out_specs=[row, row],
        out_shape=[jax.ShapeDtypeStruct((bn, t, d), F32), jax.ShapeDtypeStruct((bn, t, d), BF16)],
        compiler_params=_cparams(("parallel", "parallel")),
        name="outproj_conv",
    )(x, gate, g_next.reshape(1, d), sc_next, sh_next, p, p, p, p, p, p, p, conv_w, weight)


def _ffn_kernel(x_ref, hs_ref, gate_ref, fn_ref, wg_ref, wu_ref, wd_ref, o_ref, *, final_norm):
    j = pl.program_id(2)

    def step(first):
        h = hs_ref[...]
        gate = jnp.dot(h, wg_ref[...], preferred_element_type=F32)
        up = jnp.dot(h, wu_ref[...], preferred_element_type=F32)
        a = (_silu(gate) * up).astype(BF16)
        part = jnp.dot(a, wd_ref[...], preferred_element_type=F32)
        if first:
            o_ref[...] = part
        else:
            o_ref[...] += part

    pl.when(j == 0)(lambda: step(True))
    pl.when(j > 0)(lambda: step(False))

    @pl.when(j == pl.num_programs(2) - 1)
    def _():
        rb = min(NORM_ROW_BLOCK, o_ref.shape[0])

        def body(i, _):
            rows = pl.ds(pl.multiple_of(i * rb, rb), rb)
            y = x_ref[rows, :] + gate_ref[...] * o_ref[rows, :]
            if final_norm:
                y = (y * lax.rsqrt(jnp.mean(y * y, axis=-1, keepdims=True) + NORM_EPS)) * fn_ref[...]
            o_ref[rows, :] = y
            return 0

        lax.fori_loop(0, o_ref.shape[0] // rb, body, 0)


def ffn_grid(x, f, tm=1024, tf=512):
    bn, t, _ = x.shape
    tm = min(tm, t)
    return tm, tf, (bn, t // tm, f // tf)


def ffn_weight_jobs(w_gate, w_up, w_down, layer, grid, tf):
    bn, ni, nj = grid
    d = w_gate.shape[1]
    slab = d // (bn * ni)
    assert slab * bn * ni == d and slab % LANES == 0
    tile = lambda b, i: b * ni + i
    up_job = lambda w: CastJob(w, (None, slab, tf), lambda b, i, j: (layer, tile(b, i), j),
                               w.shape[1:], (slab, tf), lambda b, i, j: (tile(b, i), j))
    down_job = CastJob(w_down, (None, tf, slab), lambda b, i, j: (layer, j, tile(b, i)),
                       w_down.shape[1:], (tf, slab), lambda b, i, j: (j, tile(b, i)))
    return [up_job(w_gate), up_job(w_up), down_job]


def ffn(x, hs, gate, fnorm, w_gate, w_up, w_down, final_norm, casts=()):
    bn, t, d = x.shape
    f = w_gate.shape[1]
    tm, tf, grid = ffn_grid(x, f)
    vec = pl.BlockSpec((None, 1, d), lambda b, i, j: (b, 0, 0))
    one = pl.BlockSpec((1, d), lambda b, i, j: (0, 0))
    row_index = lambda b, i, j: (b, i, 0)
    x_spec, hs_spec = pl.BlockSpec((None, tm, d), row_index), _single((None, tm, d), row_index)
    (y,), rounded = _call_hosting(
        functools.partial(_ffn_kernel, final_norm=final_norm), casts,
        grid=grid,
        in_specs=[
            x_spec, hs_spec,
            vec, one,
            pl.BlockSpec((d, tf), lambda b, i, j: (0, j)),
            pl.BlockSpec((d, tf), lambda b, i, j: (0, j)),
            pl.BlockSpec((tf, d), lambda b, i, j: (j, 0)),
        ],
        args=(x, hs, gate, fnorm.reshape(1, d), w_gate, w_up, w_down),
        out_specs=[pl.BlockSpec((None, tm, d), lambda b, i, j: (b, i, 0))],
        out_shapes=[jax.ShapeDtypeStruct((bn, t, d), F32)],
        compiler_params=_cparams(("parallel", "parallel", "arbitrary")),
        name="ffn",
    )
    return y, rounded


def rope_tables(t):
    pos = np.arange(t)
    row = (pos // GRID_W).astype(np.float32)
    col = (pos % GRID_W).astype(np.float32)
    n_freq = HEAD_DIM // 4
    inv_freq = jnp.asarray(ROPE_THETA, F32) ** (-jnp.arange(n_freq, dtype=F32) / n_freq)
    ang = jnp.concatenate([jnp.asarray(row)[:, None] * inv_freq, jnp.asarray(col)[:, None] * inv_freq], axis=-1)
    cos, sin = jnp.cos(ang), jnp.sin(ang)
    return jnp.concatenate([cos, cos], axis=-1), jnp.concatenate([-sin, sin], axis=-1)


def kernel(x, c, ctx, c_ctx, ada_w, ada_b, norm_mix, norm_ffn, ffn_w_gate, ffn_w_up, ffn_w_down, final_norm,
           ev_w_in, ev_conv, ev_a_log, ev_dt_bias, ev_gdn_norm, ev_rpb, ev_w_out, od_w_in, od_conv, od_w_out):
    bn, t, d = x.shape
    depth = ada_w.shape[0]
    assert depth == 2
    gw = GDN_HEADS * HEAD_DIM
    nw = NA_HEADS * HEAD_DIM
    nh = GDN_HEADS

    cv = jnp.zeros((SUBLANES, d), F32).at[:bn].set(c).at[bn].set(c_ctx)
    mods = ada_modulation(cv, ada_w, ada_b)

    def mod_vecs(l, rows):
        m = mods[l, rows].reshape(-1, 6, d)
        return [m[:, k][:, None, :] for k in range(6)]

    sh1, sc1, g1, sh2, sc2, g2 = mod_vecs(0, slice(0, bn))
    csh1, csc1 = [jnp.broadcast_to(v, (bn, 1, d)) for v in mod_vecs(0, slice(bn, bn + 1))[:2]]
    n_gate = 4 * nh
    seg_gate = 2 * gw + 2 * nw
    w_rows = jnp.swapaxes(ev_w_in[0], 0, 1)
    n_tail = w_rows.shape[0] - seg_gate - n_gate
    n_main = seg_gate + n_tail
    gate_block = seg_gate // LANES
    n_head_steps = bn * nh
    tail_slab = -(-n_tail // (n_head_steps * LANES)) * LANES
    tail_skip = n_head_steps * tail_slab - n_tail
    tail_first = w_rows.shape[0] - n_head_steps * tail_slab
    gparams = jnp.zeros((SUBLANES, LANES), F32)
    gparams = gparams.at[0, :2 * nh].set(ev_a_log[0].reshape(-1)).at[1, :2 * nh].set(ev_dt_bias[0].reshape(-1))
    col_ka, col_va, col_kb, col_vb, col_qa, col_qb, col_za = [k * nh for k in range(7)]

    head_step = lambda b, h: b * nh + h
    pc, gates_c = proj(ctx, norm_mix[0], csc1, csh1, w_rows, 0, seg_gate, None, seg_gate, w_rows, gate_block,
                       gparams, rows_major=True, tn=1024)
    s_ctx, (w_head, w_tail) = gdn_context(
        pc, ev_conv[0], gate_rows(gates_c, ctx.shape[1]), col_ka, col_va,
        casts=[transposing_job(w_rows, 0, seg_gate, n_head_steps, head_step),
               transposing_job(w_rows, tail_first, n_head_steps * tail_slab, n_head_steps, head_step)])

    p, gates = proj(x, norm_mix[0], sc1, sh1, w_head[None], 0, seg_gate, w_tail, n_main, w_rows, gate_block,
                    gparams, tail_skip=tail_skip)
    cosf, sinf = rope_tables(t)
    y_gdn, (w_out0,) = gdn_latent(p, ev_conv[0], gate_rows(gates, t), cosf, sinf, s_ctx, ev_gdn_norm[0],
                                  col_ka, col_va, col_qa, col_za,
                                  casts=[row_slab_job(ev_w_out, 0, n_head_steps, head_step)])
    y_na, rounded = natten(p, pc, natten_bias(ev_rpb[0]), col_qb, col_kb, col_vb, col_kb, col_vb,
                           casts=[row_slab_job(w, 0, n_head_steps, head_step)
                                  for w in (ffn_w_gate, ffn_w_up, ffn_w_down, od_w_in, od_w_out)])
    ffn0, (w_in1, w_out1) = rounded[:3], rounded[3:]
    x_lat, hs = outproj(x, g1, [y_gdn, y_na], w_out0, norm_ffn[0], sc2, sh2)
    _, tf, grid = ffn_grid(x_lat, ffn_w_gate.shape[2])
    x_lat, ffn1 = ffn(x_lat, hs, g2, final_norm, *ffn0, final_norm=False,
                      casts=ffn_weight_jobs(ffn_w_gate, ffn_w_up, ffn_w_down, 1, grid, tf))

    sh1, sc1, g1, sh2, sc2, g2 = mod_vecs(1, slice(0, bn))
    p = proj(x_lat, norm_mix[1], sc1, sh1, w_in1[None], 0, 3 * d, None, 3 * d)
    x_lat, hs = outproj_conv(x_lat, g1, p, od_conv[0], w_out1, norm_ffn[1], sc2, sh2)
    x_lat, _ = ffn(x_lat, hs, g2, final_norm, *ffn1, final_norm=True)
    return x_lat
```
